```python
import math
import jax, jax.numpy as jnp
from jax import lax
import numpy as np

D_MODEL = 1024
BATCH = 16
SEQ = 2048
DEPTH = 1

PLE_DIM = 256
GRID_W = 64
MIX_WIDTH = D_MODEL
HY_WIDTH = MIX_WIDTH // 2
NA_WIDTH = MIX_WIDTH - HY_WIDTH
NA_HEADS = 8
NA_HEAD_DIM = NA_WIDTH // NA_HEADS
HY_ORDER = 2
SHORT_CONV = 3
FILTER_EMB = 33
FILTER_BANDS = (FILTER_EMB - 1) // 2
FILTER_HIDDEN = 64
DECAY_TARGET = 1e-2
FAST_DECAY_PCT = 0.3
SLOW_DECAY_PCT = 1.5
WIN_ROWS = 8
WIN_COLS = 16
Q_ROWS = 2
Q_COLS = 16
N_GROUPS = 4
EXPERTS_PER_GROUP = 8
N_EXPERTS = N_GROUPS * EXPERTS_PER_GROUP
TOP_K = 2
D_EXPERT = D_MODEL // 2
MOE_BLOCK = 256
EPS = 1e-6
NEG_INF = -1e30
HY_COLS = (HY_ORDER + 1) * HY_WIDTH
IN_COLS = HY_COLS + 3 * NA_WIDTH

kernel_name = 'hybrid_hyena_natten_hiermoe_block'


def _rmsnorm(x, g):
    xf = x.astype(jnp.float32)
    y = xf * lax.rsqrt(jnp.mean(xf * xf, axis=-1, keepdims=True) + EPS)
    return (y * g.astype(jnp.float32)).astype(x.dtype)


def _short_conv(u, w, b):
    L = u.shape[1]
    pad = SHORT_CONV // 2
    up = jnp.pad(u, ((0, 0), (pad, SHORT_CONV - 1 - pad), (0, 0)))
    out = b
    for j in range(SHORT_CONV):
        out = out + up[:, j:j + L] * w[j]
    return out


def _hyena_filters(L, w1, b1, f1, w2, b2, f2, w3):
    f32 = jnp.float32
    t = jnp.linspace(0.0, 1.0, L, dtype=f32)[:, None]
    w = 2.0 * math.pi * jnp.arange(L, dtype=f32)[:, None] / L
    bands = jnp.linspace(1e-4, FILTER_BANDS - 1, FILTER_BANDS, dtype=f32)[None, :]
    z = jnp.concatenate([t, jnp.cos(bands * w), -jnp.sin(bands * w)], axis=-1)
    hid = jnp.sin(f1.astype(f32) * (z @ w1.astype(f32) + b1.astype(f32)))
    hid = jnp.sin(f2.astype(f32) * (hid @ w2.astype(f32) + b2.astype(f32)))
    hf = (hid @ w3.astype(f32)).reshape(L, 2, HY_ORDER, HY_WIDTH)
    max_decay = math.log(DECAY_TARGET) / FAST_DECAY_PCT
    min_decay = math.log(DECAY_TARGET) / SLOW_DECAY_PCT
    deltas = jnp.linspace(min_decay, max_decay, HY_WIDTH, dtype=f32)
    decay = jnp.exp(-t * jnp.abs(deltas)[None, :])
    hf = hf * decay[:, None, None, :]
    fwd, bwd = hf[:, 0], hf[:, 1]
    k = jnp.concatenate([fwd, jnp.zeros_like(fwd[:1]), bwd[:0:-1]], axis=0)
    return jnp.fft.rfft(k, axis=0)


def _hyena(u, conv_w, conv_b, filt_fft, skip):
    L = u.shape[1]
    u = _short_conv(u, conv_w, conv_b).astype(jnp.float32)
    x1, x2, v = jnp.split(u, 3, axis=-1)
    z = v
    for o, gate in enumerate((x1, x2)):
        zf = jnp.fft.rfft(z, n=2 * L, axis=1)
        conv = jnp.fft.irfft(zf * filt_fft[:, o][None], n=2 * L, axis=1)[:, :L]
        z = gate * (conv + z * skip[o].astype(jnp.float32))
    return z


def _natten_tables(rows):
    kr = min(WIN_ROWS, rows)
    krb = min(Q_ROWS + kr - 1, rows)
    kcb = min(Q_COLS + WIN_COLS - 1, GRID_W)
    npr, ncb = rows // Q_ROWS, GRID_W // Q_COLS
    rs = np.clip(np.arange(rows) - kr // 2, 0, rows - kr)
    cs = np.clip(np.arange(GRID_W) - WIN_COLS // 2, 0, GRID_W - WIN_COLS)
    q_r = np.arange(npr)[:, None] * Q_ROWS + np.arange(Q_ROWS)
    q_c = np.arange(ncb)[:, None] * Q_COLS + np.arange(Q_COLS)
    k_r = np.minimum(rs[q_r[:, 0]], rows - krb)[:, None] + np.arange(krb)
    k_c = np.minimum(cs[q_c[:, 0]], GRID_W - kcb)[:, None] + np.arange(kcb)
    nq, nk = Q_ROWS * Q_COLS, krb * kcb
    qr = np.broadcast_to(q_r[:, None, :, None], (npr, ncb, Q_ROWS, Q_COLS)).reshape(npr, ncb, nq, 1)
    qc = np.broadcast_to(q_c[None, :, None, :], (npr, ncb, Q_ROWS, Q_COLS)).reshape(npr, ncb, nq, 1)
    kr_ = np.broadcast_to(k_r[:, None, :, None], (npr, ncb, krb, kcb)).reshape(npr, ncb, 1, nk)
    kc_ = np.broadcast_to(k_c[None, :, None, :], (npr, ncb, krb, kcb)).reshape(npr, ncb, 1, nk)
    key_idx = (kr_ * GRID_W + kc_)[:, :, 0, :].astype(np.int32)
    valid = ((kr_ >= rs[qr]) & (kr_ < rs[qr] + kr)
             & (kc_ >= cs[qc]) & (kc_ < cs[qc] + WIN_COLS))
    dr = np.clip(kr_ - qr + WIN_ROWS - 1, 0, 2 * WIN_ROWS - 2).astype(np.int32)
    dc = np.clip(kc_ - qc + WIN_COLS - 1, 0, 2 * WIN_COLS - 2).astype(np.int32)
    return key_idx, valid, dr, dc


def _natten(qkv, rpb):
    B, L, _ = qkv.shape
    rows = L // GRID_W
    npr, ncb = rows // Q_ROWS, GRID_W // Q_COLS
    key_idx, valid, dr, dc = _natten_tables(rows)
    q, k, v = jnp.split(qkv, 3, axis=-1)
    q = q.reshape(B, npr, Q_ROWS, ncb, Q_COLS, NA_HEADS, NA_HEAD_DIM)
    q = q.transpose(0, 1, 3, 2, 4, 5, 6).reshape(B, npr, ncb, Q_ROWS * Q_COLS, NA_HEADS, NA_HEAD_DIM)
    k = k.reshape(B, L, NA_HEADS, NA_HEAD_DIM)
    v = v.reshape(B, L, NA_HEADS, NA_HEAD_DIM)
    kb = jnp.take(k, key_idx, axis=1)
    vb = jnp.take(v, key_idx, axis=1)
    s = jnp.einsum('bpjqhd,bpjkhd->bhpjqk', q, kb).astype(jnp.float32) * (NA_HEAD_DIM ** -0.5)
    bias = rpb.astype(jnp.float32)[:, dr, dc]
    s = jnp.where(valid, s + bias, NEG_INF)
    a = jax.nn.softmax(s, axis=-1).astype(vb.dtype)
    o = jnp.einsum('bhpjqk,bpjkhd->bpjqhd', a, vb)
    o = o.reshape(B, npr, ncb, Q_ROWS, Q_COLS, NA_WIDTH).transpose(0, 1, 3, 2, 4, 5)
    return o.reshape(B, L, NA_WIDTH)


def _moe(x, wg, bg, we, be, w1, w3, w2):
    B, L, D = x.shape
    T = B * L
    f32 = jnp.float32
    xf = x.reshape(T, D)
    xr = xf.astype(f32)
    g_prob = jax.nn.softmax(xr @ wg.astype(f32) + bg.astype(f32), axis=-1)
    g_w, g_idx = lax.top_k(g_prob, 1)
    e_logits = (xr @ we.astype(f32) + be.astype(f32)).reshape(T, N_GROUPS, EXPERTS_PER_GROUP)
    e_logits = jnp.take_along_axis(e_logits, g_idx[:, :, None], axis=1)[:, 0]
    e_w, e_idx = lax.top_k(jax.nn.softmax(e_logits, axis=-1), TOP_K)
    gates = g_w * (e_w / jnp.sum(e_w, axis=-1, keepdims=True))
    ids = (g_idx * EXPERTS_PER_GROUP + e_idx).astype(jnp.int32)
    A = T * TOP_K
    flat = ids.reshape(-1)
    order = jnp.argsort(flat)
    sid = flat[order]
    tok = (order // TOP_K).astype(jnp.int32)
    gate_s = gates.reshape(-1)[order]
    counts = jnp.bincount(flat, length=N_EXPERTS)
    padded = (counts + MOE_BLOCK - 1) // MOE_BLOCK * MOE_BLOCK
    pad_end = jnp.cumsum(padded)
    pad_start = pad_end - padded
    seg_start = jnp.cumsum(counts) - counts
    dest = pad_start[sid] + jnp.arange(A, dtype=jnp.int32) - seg_start[sid]
    n_blocks = -(-(A + N_EXPERTS * (MOE_BLOCK - 1)) // MOE_BLOCK)
    slot_tok = jnp.full((n_blocks * MOE_BLOCK,), T, jnp.int32).at[dest].set(tok)
    block_e = jnp.minimum(jnp.searchsorted(pad_end, jnp.arange(n_blocks) * MOE_BLOCK, side='right'), N_EXPERTS - 1)
    xpad = jnp.concatenate([xf, jnp.zeros((1, D), xf.dtype)], axis=0)
    xb = xpad[slot_tok].reshape(n_blocks, MOE_BLOCK, D)

    def expert_block(args):
        xblk, e = args
        hid = jax.nn.silu(xblk @ w1[e]) * (xblk @ w3[e])
        return hid @ w2[e]

    yb = lax.map(expert_block, (xb, block_e)).reshape(-1, D)
    y = jnp.zeros((T, D), x.dtype).at[tok].add(yb[dest] * gate_s[:, None].astype(x.dtype))
    return y.reshape(B, L, D)


def setup_inputs(seed: int = 0) -> dict:
    key = jax.random.key(seed)
    ks = jax.random.split(key, 32)
    n = jax.random.normal
    f32 = jnp.float32

    def gain(k, shape):
        return 1.0 + 0.01 * n(k, shape, f32)

    return {
        'x': n(ks[0], (BATCH, SEQ, D_MODEL), f32),
        'p': n(ks[1], (DEPTH, BATCH, SEQ, PLE_DIM), f32),
        'g_mix': gain(ks[2], (DEPTH, D_MODEL)),
        'w_in': n(ks[3], (DEPTH, D_MODEL, IN_COLS), f32) * D_MODEL ** -0.5,
        'hy_conv_w': n(ks[4], (DEPTH, SHORT_CONV, HY_COLS), f32) * SHORT_CONV ** -0.5,
        'hy_conv_b': 0.01 * n(ks[5], (DEPTH, HY_COLS), f32),
        'hy_f_w1': n(ks[6], (DEPTH, FILTER_EMB, FILTER_HIDDEN), f32) * FILTER_EMB ** -0.5,
        'hy_f_b1': 0.01 * n(ks[7], (DEPTH, FILTER_HIDDEN), f32),
        'hy_f_freq1': gain(ks[8], (DEPTH, FILTER_HIDDEN)),
        'hy_f_w2': n(ks[9], (DEPTH, FILTER_HIDDEN, FILTER_HIDDEN), f32) * FILTER_HIDDEN ** -0.5,
        'hy_f_b2': 0.01 * n(ks[10], (DEPTH, FILTER_HIDDEN), f32),
        'hy_f_freq2': gain(ks[11], (DEPTH, FILTER_HIDDEN)),
        'hy_f_w3': n(ks[12], (DEPTH, FILTER_HIDDEN, 2 * HY_ORDER * HY_WIDTH), f32) * (0.1 * FILTER_HIDDEN ** -0.5),
        'hy_skip': n(ks[13], (DEPTH, HY_ORDER, HY_WIDTH), f32),
        'na_rpb': 0.02 * n(ks[14], (DEPTH, NA_HEADS, 2 * WIN_ROWS - 1, 2 * WIN_COLS - 1), f32),
        'g_out_hy': gain(ks[15], (DEPTH, HY_WIDTH)),
        'g_out_na': gain(ks[16], (DEPTH, NA_WIDTH)),
        'w_out': n(ks[17], (DEPTH, MIX_WIDTH, D_MODEL), f32) * MIX_WIDTH ** -0.5,
        'g_ffn': gain(ks[18], (DEPTH, D_MODEL)),
        'router_wg': n(ks[19], (DEPTH, D_MODEL, N_GROUPS), f32) * D_MODEL ** -0.5,
        'router_bg': 0.01 * n(ks[20], (DEPTH, N_GROUPS), f32),
        'router_we': n(ks[21], (DEPTH, D_MODEL, N_EXPERTS), f32) * D_MODEL ** -0.5,
        'router_be': 0.01 * n(ks[22], (DEPTH, N_EXPERTS), f32),
        'exp_w1': n(ks[23], (DEPTH, N_EXPERTS, D_MODEL, D_EXPERT), f32) * D_MODEL ** -0.5,
        'exp_w3': n(ks[24], (DEPTH, N_EXPERTS, D_MODEL, D_EXPERT), f32) * D_MODEL ** -0.5,
        'exp_w2': n(ks[25], (DEPTH, N_EXPERTS, D_EXPERT, D_MODEL), f32) * D_EXPERT ** -0.5,
        'g_ple': gain(ks[26], (DEPTH, D_MODEL)),
        'w_ple_gate': n(ks[27], (DEPTH, D_MODEL, D_MODEL), f32) * D_MODEL ** -0.5,
        'w_ple_proj': n(ks[28], (DEPTH, PLE_DIM, D_MODEL), f32) * PLE_DIM ** -0.5,
        'g_final': gain(ks[29], (D_MODEL,)),
    }


def reference(x, p, g_mix, w_in, hy_conv_w, hy_conv_b, hy_f_w1, hy_f_b1, hy_f_freq1, hy_f_w2, hy_f_b2,
              hy_f_freq2, hy_f_w3, hy_skip, na_rpb, g_out_hy, g_out_na, w_out, g_ffn, router_wg, router_bg,
              router_we, router_be, exp_w1, exp_w3, exp_w2, g_ple, w_ple_gate, w_ple_proj, g_final):
    L = x.shape[1]
    h = x
    for i in range(DEPTH):
        u = _rmsnorm(h, g_mix[i]) @ w_in[i]
        filt = _hyena_filters(L, hy_f_w1[i], hy_f_b1[i], hy_f_freq1[i], hy_f_w2[i], hy_f_b2[i],
                              hy_f_freq2[i], hy_f_w3[i])
        y_hy = _hyena(u[..., :HY_COLS], hy_conv_w[i], hy_conv_b[i], filt, hy_skip[i])
        y_na = _natten(u[..., HY_COLS:], na_rpb[i])
        mixed = jnp.concatenate([_rmsnorm(y_hy, g_out_hy[i]).astype(h.dtype),
                                 _rmsnorm(y_na, g_out_na[i]).astype(h.dtype)], axis=-1)
        h = h + mixed @ w_out[i]
        h = h + _moe(_rmsnorm(h, g_ffn[i]), router_wg[i], router_bg[i], router_we[i], router_be[i],
                     exp_w1[i], exp_w3[i], exp_w2[i])
        gate = jax.nn.sigmoid(_rmsnorm(h, g_ple[i]) @ w_ple_gate[i])
        h = h + (p[i] @ w_ple_proj[i]) * gate
    return _rmsnorm(h, g_final)
```

```python
import functools
import math

import numpy as np
import jax
import jax.numpy as jnp
from jax import lax
from jax.experimental import pallas as pl
from jax.experimental.pallas import tpu as pltpu

_F32 = jnp.float32
_BF16 = jnp.bfloat16

GRID_W = 64
HY_WIDTH = 512
NA_WIDTH = 512
NA_HEADS = 8
NA_HEAD_DIM = 64
HY_ORDER = 2
SHORT_CONV = 3
FILTER_EMB = 33
FILTER_BANDS = (FILTER_EMB - 1) // 2
DECAY_TARGET = 1e-2
FAST_DECAY_PCT = 0.3
SLOW_DECAY_PCT = 1.5
WIN_ROWS = 8
WIN_COLS = 16
Q_ROWS = 2
N_GROUPS = 4
EXPERTS_PER_GROUP = 8
N_EXPERTS = N_GROUPS * EXPERTS_PER_GROUP
TOP_K = 2
MOE_BLOCK = 256
EPS = 1e-6
NEG_INF = -1e30
HY_COLS = (HY_ORDER + 1) * HY_WIDTH

V7X_LANES = 128
V7X_SUBLANES = 8
V7X_VMEM_LIMIT_BYTES = 56 * 2 ** 20

ROW_TILE = 512
FREQ_CHUNK = 512
HY_CH_TILE = 256
ROUTER_LANES = V7X_LANES
TOKEN_TILE = (V7X_SUBLANES, V7X_LANES)


def _cparams(*sem):
    return pltpu.CompilerParams(dimension_semantics=sem, vmem_limit_bytes=V7X_VMEM_LIMIT_BYTES)


def _resident(shape, index_map):
    return pl.BlockSpec(shape, index_map, pipeline_mode=pl.Buffered(1))


def _rmsnorm(x, g):
    return x * lax.rsqrt(jnp.mean(x * x, axis=-1, keepdims=True) + EPS) * g


def _dot(a, b):
    return jnp.dot(a, b, preferred_element_type=_F32)


def _dot_f32(a, b):
    return jnp.dot(a, b, preferred_element_type=_F32, precision=lax.Precision.HIGHEST)


def _in_proj_kernel(x_ref, g_ref, w_ref, uhy_ref, qkv_ref):
    xn = _rmsnorm(x_ref[...], g_ref[...]).astype(_BF16)
    n_hy = uhy_ref.shape[1]
    for c0 in range(0, n_hy, 512):
        uhy_ref[:, c0:c0 + 512] = _dot(xn, w_ref[:, c0:c0 + 512]).astype(_BF16)
    for c0 in range(0, qkv_ref.shape[1], 512):
        qkv_ref[:, c0:c0 + 512] = _dot(xn, w_ref[:, n_hy + c0:n_hy + c0 + 512]).astype(_BF16)


def _in_proj(x2, g_mix, w_in):
    T, D = x2.shape
    n_in = w_in.shape[1]
    n_qkv = n_in - HY_COLS
    return pl.pallas_call(
        _in_proj_kernel,
        grid=(T // ROW_TILE,),
        in_specs=[pl.BlockSpec((ROW_TILE, D), lambda i: (i, 0)),
                  pl.BlockSpec((1, D), lambda i: (0, 0)),
                  _resident((D, n_in), lambda i: (0, 0))],
        out_specs=[pl.BlockSpec((ROW_TILE, HY_COLS), lambda i: (i, 0)),
                   pl.BlockSpec((ROW_TILE, n_qkv), lambda i: (i, 0))],
        out_shape=[jax.ShapeDtypeStruct((T, HY_COLS), _BF16),
                   jax.ShapeDtypeStruct((T, n_qkv), _BF16)],
        compiler_params=_cparams("arbitrary"),
        name="in_proj",
    )(x2, g_mix.reshape(1, D), w_in.astype(_BF16))


def _filter_mlp_kernel(z_ref, w1_ref, b1_ref, f1_ref, w2_ref, b2_ref, f2_ref, w3_ref, delta_ref, hf_ref):
    z = z_ref[...]
    hid = jnp.sin(f1_ref[...] * (_dot_f32(z, w1_ref[...]) + b1_ref[...]))
    hid = jnp.sin(f2_ref[...] * (_dot_f32(hid, w2_ref[...]) + b2_ref[...]))
    hf = _dot_f32(hid, w3_ref[...])
    decay = jnp.exp(-z[:, 0:1] * delta_ref[...])
    tl = z.shape[0]
    row = lax.broadcasted_iota(jnp.int32, (tl, HY_WIDTH), 0) + pl.program_id(0) * tl
    for k in range(2 * HY_ORDER):
        blk = hf[:, k * HY_WIDTH:(k + 1) * HY_WIDTH] * decay
        if k >= HY_ORDER:
            blk = jnp.where(row == 0, 0.0, blk)
        hf_ref[:, k * HY_WIDTH:(k + 1) * HY_WIDTH] = blk


def _filter_mlp(L, w1, b1, f1, w2, b2, f2, w3):
    t = jnp.linspace(0.0, 1.0, L, dtype=_F32)[:, None]
    w = 2.0 * math.pi * jnp.arange(L, dtype=_F32)[:, None] / L
    bands = jnp.linspace(1e-4, FILTER_BANDS - 1, FILTER_BANDS, dtype=_F32)[None, :]
    z = jnp.concatenate([t, jnp.cos(bands * w), -jnp.sin(bands * w)], axis=-1)
    z = jnp.pad(z, ((0, 0), (0, V7X_LANES - FILTER_EMB)))
    w1p = jnp.pad(w1.astype(_F32), ((0, V7X_LANES - FILTER_EMB), (0, 0)))
    max_decay = math.log(DECAY_TARGET) / FAST_DECAY_PCT
    min_decay = math.log(DECAY_TARGET) / SLOW_DECAY_PCT
    deltas = jnp.abs(jnp.linspace(min_decay, max_decay, HY_WIDTH, dtype=_F32))[None, :]
    hid = w1.shape[1]
    n_out = w3.shape[1]
    tl = min(L, ROW_TILE)
    full = lambda shape: pl.BlockSpec(shape, lambda i: (0, 0))
    return pl.pallas_call(
        _filter_mlp_kernel,
        grid=(L // tl,),
        in_specs=[pl.BlockSpec((tl, V7X_LANES), lambda i: (i, 0)),
                  full((V7X_LANES, hid)), full((1, hid)), full((1, hid)),
                  full((hid, hid)), full((1, hid)), full((1, hid)),
                  full((hid, n_out)), full((1, HY_WIDTH))],
        out_specs=pl.BlockSpec((tl, n_out), lambda i: (i, 0)),
        out_shape=jax.ShapeDtypeStruct((L, n_out), _F32),
        compiler_params=_cparams("arbitrary"),
        name="hyena_filter_mlp",
    )(z, w1p, b1.reshape(1, hid), f1.reshape(1, hid), w2.astype(_F32), b2.reshape(1, hid),
      f2.reshape(1, hid), w3.astype(_F32), deltas)


def _dft_matrix(L):
    n = 2 * L
    f = jnp.arange(L, dtype=jnp.int32)[:, None]
    t = jnp.arange(L, dtype=jnp.int32)[None, :]
    ang = ((f * t) % n).astype(_F32) * (2.0 * math.pi / n)
    top = jnp.cos(ang)
    bot = jnp.where(f == 0, jnp.where(t % 2 == 0, 1.0, -1.0), -jnp.sin(ang))
    return jnp.concatenate([top, bot], axis=0)


def _filter_spec_kernel(f_ref, xf_ref, xb_ref, a_ref, b_ref, d_ref):
    L = xf_ref.shape[0]
    n = 2 * L
    ct = xf_ref.shape[1]
    x = jnp.concatenate([xf_ref[...], xb_ref[...]], axis=1).astype(_BF16)
    for r0 in range(0, L, FREQ_CHUNK):
        top = _dot(f_ref[r0:r0 + FREQ_CHUNK, :], x)
        bot = _dot(f_ref[L + r0:L + r0 + FREQ_CHUNK, :], x)
        kre = top[:, :ct] + top[:, ct:]
        kim = bot[:, :ct] - bot[:, ct:]
        knyq = bot[:, :ct] + bot[:, ct:]
        if r0 == 0:
            row0 = lax.broadcasted_iota(jnp.int32, kre.shape, 0) == 0
            a = jnp.where(row0, kre * (1.0 / n), kre * (2.0 / n))
            b = jnp.where(row0, 0.0, kim * (2.0 / n))
            d = jnp.where(row0, knyq * (1.0 / n), kre * (2.0 / n))
        else:
            a = kre * (2.0 / n)
            b = kim * (2.0 / n)
            d = a
        a_ref[0, r0:r0 + FREQ_CHUNK, :] = a
        b_ref[0, r0:r0 + FREQ_CHUNK, :] = b
        d_ref[0, r0:r0 + FREQ_CHUNK, :] = d


def _filter_spec(fmat, hf):
    L = hf.shape[0]
    n_ct = HY_WIDTH // HY_CH_TILE
    out = jax.ShapeDtypeStruct((HY_ORDER, L, HY_WIDTH), _F32)
    ospec = pl.BlockSpec((1, L, HY_CH_TILE), lambda j: (j // n_ct, 0, j % n_ct))
    return pl.pallas_call(
        _filter_spec_kernel,
        grid=(HY_ORDER * n_ct,),
        in_specs=[_resident((2 * L, L), lambda j: (0, 0)),
                  pl.BlockSpec((L, HY_CH_TILE), lambda j: (0, j)),
                  pl.BlockSpec((L, HY_CH_TILE), lambda j: (0, HY_ORDER * n_ct + j))],
        out_specs=[ospec, ospec, ospec],
        out_shape=[out, out, out],
        compiler_params=_cparams("arbitrary"),
        name="hyena_filter_spectrum",
    )(fmat, hf, hf)


def _short_conv(z, w_ref, b_ref):
    L = z.shape[0]
    row = lax.broadcasted_iota(jnp.int32, z.shape, 0)
    zm = jnp.where(row == 0, 0.0, pltpu.roll(z, 1, axis=0))
    zp = jnp.where(row == L - 1, 0.0, pltpu.roll(z, L - 1, axis=0))
    return ((b_ref[...] + zm * w_ref[0:1, :]) + z * w_ref[1:2, :]) + zp * w_ref[2:3, :]


def _hy_fwd_kernel(z_ref, cw_ref, cb_ref, f_ref, a_ref, b_ref, d_ref, y_ref, *, conv_input):
    z = z_ref[0].astype(_F32)
    if conv_input:
        z = _short_conv(z, cw_ref, cb_ref)
    zb = z.astype(_BF16)
    L = z.shape[0]
    for r0 in range(0, L, FREQ_CHUNK):
        top = _dot(f_ref[r0:r0 + FREQ_CHUNK, :], zb)
        bot = _dot(f_ref[L + r0:L + r0 + FREQ_CHUNK, :], zb)
        a = a_ref[0, r0:r0 + FREQ_CHUNK, :]
        b = b_ref[0, r0:r0 + FREQ_CHUNK, :]
        d = d_ref[0, r0:r0 + FREQ_CHUNK, :]
        y_ref[0, r0:r0 + FREQ_CHUNK, :] = (top * a - bot * b).astype(_BF16)
        y_ref[0, L + r0:L + r0 + FREQ_CHUNK, :] = (top * b + bot * d).astype(_BF16)


def _hy_inv_kernel(y_ref, ft_ref, gate_ref, gw_ref, gb_ref, zp_ref, zw_ref, zb_ref, skip_ref, o_ref, *, conv_prev):
    gate = _short_conv(gate_ref[0].astype(_F32), gw_ref, gb_ref)
    zprev = zp_ref[0].astype(_F32)
    if conv_prev:
        zprev = _short_conv(zprev, zw_ref, zb_ref)
    res = gate * (zprev * skip_ref[0])
    L = gate.shape[0]
    y = y_ref[0]
    for r0 in range(0, L, FREQ_CHUNK):
        conv = _dot(ft_ref[r0:r0 + FREQ_CHUNK, :], y)
        o_ref[0, r0:r0 + FREQ_CHUNK, :] = gate[r0:r0 + FREQ_CHUNK] * conv + res[r0:r0 + FREQ_CHUNK]


def _hyena(u_hy, conv_w, conv_b, fmat, fmat_t, coef_a, coef_b, coef_d, skip):
    B, L, _ = u_hy.shape
    n_ct = HY_WIDTH // HY_CH_TILE
    grid = (n_ct, B)
    conv_b2 = conv_b.reshape(1, HY_COLS)
    col = lambda blk0: pl.BlockSpec((1, L, HY_CH_TILE), lambda c, b: (b, 0, blk0 + c))
    cw = lambda blk0: pl.BlockSpec((SHORT_CONV, HY_CH_TILE), lambda c, b: (0, blk0 + c))
    cb = lambda blk0: pl.BlockSpec((1, HY_CH_TILE), lambda c, b: (0, blk0 + c))
    z = u_hy
    z_blk0 = 2 * n_ct
    for o in range(HY_ORDER):
        first = o == 0
        coef = lambda: pl.BlockSpec((1, L, HY_CH_TILE), lambda c, b, o=o: (o, 0, c),
                                    pipeline_mode=pl.Buffered(1))
        y = pl.pallas_call(
            functools.partial(_hy_fwd_kernel, conv_input=first),
            grid=grid,
            in_specs=[col(z_blk0), cw(2 * n_ct), cb(2 * n_ct),
                      _resident((2 * L, L), lambda c, b: (0, 0)),
                      coef(), coef(), coef()],
            out_specs=pl.BlockSpec((1, 2 * L, HY_CH_TILE), lambda c, b: (b, 0, c)),
            out_shape=jax.ShapeDtypeStruct((B, 2 * L, HY_WIDTH), _BF16),
            compiler_params=_cparams("arbitrary", "arbitrary"),
            name=f"hyena_fwd_dft_{o}",
        )(z, conv_w, conv_b2, fmat, coef_a, coef_b, coef_d)
        z = pl.pallas_call(
            functools.partial(_hy_inv_kernel, conv_prev=first),
            grid=grid,
            in_specs=[pl.BlockSpec((1, 2 * L, HY_CH_TILE), lambda c, b: (b, 0, c)),
                      _resident((L, 2 * L), lambda c, b: (0, 0)),
                      col(o * n_ct), cw(o * n_ct), cb(o * n_ct),
                      col(z_blk0), cw(2 * n_ct), cb(2 * n_ct),
                      pl.BlockSpec((1, 1, HY_CH_TILE), lambda c, b, o=o: (o, 0, c))],
            out_specs=pl.BlockSpec((1, L, HY_CH_TILE), lambda c, b: (b, 0, c)),
            out_shape=jax.ShapeDtypeStruct((B, L, HY_WIDTH), _F32),
            compiler_params=_cparams("arbitrary", "arbitrary"),
            name=f"hyena_inv_dft_{o}",
        )(y, fmat_t, u_hy, conv_w, conv_b2, z, conv_w, conv_b2, skip.reshape(HY_ORDER, 1, HY_WIDTH))
        z_blk0 = 0
    return z


def _natten_tables(rows):
    kr = min(WIN_ROWS, rows)
    krb = min(Q_ROWS + kr - 1, rows)
    rs = np.clip(np.arange(rows) - kr // 2, 0, rows - kr)
    cs = np.clip(np.arange(GRID_W) - WIN_COLS // 2, 0, GRID_W - WIN_COLS)
    qc = np.tile(np.arange(GRID_W), Q_ROWS)[:, None]
    kc = np.tile(np.arange(GRID_W), krb)[None, :]
    cases, case_of, kstart = [], [], []
    for p in range(rows // Q_ROWS):
        k_r0 = min(rs[p * Q_ROWS], rows - krb)
        qr = (p * Q_ROWS + np.repeat(np.arange(Q_ROWS), GRID_W))[:, None]
        kr_ = (k_r0 + np.repeat(np.arange(krb), GRID_W))[None, :]
        valid = ((kr_ >= rs[qr]) & (kr_ < rs[qr] + kr) & (kc >= cs[qc]) & (kc < cs[qc] + WIN_COLS))
        dr = np.clip(kr_ - qr + WIN_ROWS - 1, 0, 2 * WIN_ROWS - 2)
        dc = np.clip(kc - qc + WIN_COLS - 1, 0, 2 * WIN_COLS - 2)
        idx = np.where(valid, dr * (2 * WIN_COLS - 1) + dc, -1).astype(np.int32)
        for ci, c in enumerate(cases):
            if np.array_equal(c, idx):
                break
        else:
            ci = len(cases)
            cases.append(idx)
        case_of.append(ci)
        kstart.append(k_r0 * GRID_W)
    return np.stack(cases), np.asarray(case_of, np.int32), np.asarray(kstart, np.int32), krb * GRID_W


def _natten_kernel(case_ref, kstart_ref, qkv_ref, bias_ref, o_ref, *, n_pairs, n_keys):
    nq = Q_ROWS * GRID_W
    pair_w = 2 * NA_HEAD_DIM
    lane = lax.broadcasted_iota(jnp.int32, (nq, pair_w), 1)
    lo_half = lane < NA_HEAD_DIM
    scale = NA_HEAD_DIM ** -0.5

    def body(p, carry):
        q0 = pl.multiple_of(p * nq, nq)
        k0 = pl.multiple_of(kstart_ref[p], GRID_W)
        case = case_ref[p]
        for hp in range(NA_HEADS // 2):
            c0 = hp * pair_w
            q2 = qkv_ref[0, pl.ds(q0, nq), c0:c0 + pair_w]
            k2 = qkv_ref[0, pl.ds(k0, n_keys), NA_WIDTH + c0:NA_WIDTH + c0 + pair_w]
            v2 = qkv_ref[0, pl.ds(k0, n_keys), 2 * NA_WIDTH + c0:2 * NA_WIDTH + c0 + pair_w]
            outs = []
            for half in range(2):
                qm = jnp.where(lo_half if half == 0 else jnp.logical_not(lo_half), q2, jnp.zeros_like(q2))
                s = lax.dot_general(qm, k2, (((1,), (1,)), ((), ())), preferred_element_type=_F32)
                s = s * scale + bias_ref[case, 2 * hp + half]
                m = jnp.max(s, axis=-1, keepdims=True)
                e = jnp.exp(s - m)
                l = jnp.sum(e, axis=-1, keepdims=True)
                outs.append(_dot(e.astype(_BF16), v2) / l)
            o_ref[0, pl.ds(q0, nq), c0:c0 + pair_w] = jnp.where(lo_half, outs[0], outs[1])
        return carry

    lax.fori_loop(0, n_pairs, body, 0)


def _natten(qkv, rpb):
    B, L, _ = qkv.shape
    rows = L // GRID_W
    assert rows % Q_ROWS == 0 and rows >= Q_ROWS + WIN_ROWS - 1
    cases, case_of, kstart, n_keys = _natten_tables(rows)
    n_case = cases.shape[0]
    nq = Q_ROWS * GRID_W
    rpb_flat = rpb.astype(_F32).reshape(NA_HEADS, -1)
    bias = jnp.where(cases[None] >= 0, rpb_flat[:, np.maximum(cases, 0)], NEG_INF)
    bias = bias.transpose(1, 0, 2, 3)
    grid_spec = pltpu.PrefetchScalarGridSpec(
        num_scalar_prefetch=2,
        grid=(B,),
        in_specs=[pl.BlockSpec((1, L, 3 * NA_WIDTH), lambda b, *_: (b, 0, 0)),
                  _resident((n_case, NA_HEADS, nq, n_keys), lambda b, *_: (0, 0, 0, 0))],
        out_specs=pl.BlockSpec((1, L, NA_WIDTH), lambda b, *_: (b, 0, 0)),
    )
    return pl.pallas_call(
        functools.partial(_natten_kernel, n_pairs=rows // Q_ROWS, n_keys=n_keys),
        grid_spec=grid_spec,
        out_shape=jax.ShapeDtypeStruct((B, L, NA_WIDTH), _F32),
        compiler_params=_cparams("arbitrary"),
        name="natten",
    )(jnp.asarray(case_of), jnp.asarray(kstart), qkv, bias)


_META_ID, _META_GATE, _META_RANK = 0, 2, 4


def _out_router_kernel(yhy_ref, yna_ref, ghy_ref, gna_ref, wtop_ref, wbot_ref, x_ref, gffn_ref, rw_ref, rb_ref,
                       h1_ref, xn_ref, meta_ref, cnt_ref, carry_ref):
    @pl.when(pl.program_id(0) == 0)
    def _():
        carry_ref[...] = jnp.zeros_like(carry_ref)

    nh = _rmsnorm(yhy_ref[...], ghy_ref[...]).astype(_BF16)
    nn = _rmsnorm(yna_ref[...], gna_ref[...]).astype(_BF16)
    h1 = x_ref[...] + (_dot(nh, wtop_ref[...]) + _dot(nn, wbot_ref[...]))
    h1_ref[...] = h1
    xn = _rmsnorm(h1, gffn_ref[...])
    for j in range(V7X_SUBLANES):
        xn_ref[:, j, :] = xn[:, j * V7X_LANES:(j + 1) * V7X_LANES]

    logits = _dot_f32(xn, rw_ref[...]) + rb_ref[...]
    tm = logits.shape[0]
    lane = lax.broadcasted_iota(jnp.int32, logits.shape, 1).astype(_F32)
    first_max = lambda v, m: jnp.min(jnp.where(v == m, lane, float(ROUTER_LANES)), axis=-1, keepdims=True)

    gl = jnp.where(lane < N_GROUPS, logits, -jnp.inf)
    gmax = jnp.max(gl, axis=-1, keepdims=True)
    g_w = 1.0 / jnp.sum(jnp.exp(gl - gmax), axis=-1, keepdims=True)
    g_idx = first_max(gl, gmax)
    lo = N_GROUPS + EXPERTS_PER_GROUP * g_idx
    el = jnp.where((lane >= lo) & (lane < lo + EXPERTS_PER_GROUP), logits, -jnp.inf)
    m1 = jnp.max(el, axis=-1, keepdims=True)
    i1 = first_max(el, m1)
    el2 = jnp.where(lane == i1, -jnp.inf, el)
    m2 = jnp.max(el2, axis=-1, keepdims=True)
    i2 = first_max(el2, m2)
    r = jnp.exp(m2 - m1)
    gate1 = g_w / (1.0 + r)
    gate2 = g_w * r / (1.0 + r)

    sel1 = lane == i1
    sel2 = lane == i2
    onehot = jnp.where(sel1 | sel2, 1.0, 0.0)
    tri_r = lax.broadcasted_iota(jnp.int32, (tm, tm), 0)
    tri_c = lax.broadcasted_iota(jnp.int32, (tm, tm), 1)
    tri = jnp.where(tri_r > tri_c, 1.0, 0.0).astype(_BF16)
    before = carry_ref[0:1, :] + _dot(tri, onehot.astype(_BF16))
    rank1 = jnp.sum(jnp.where(sel1, before, 0.0), axis=-1, keepdims=True)
    rank2 = jnp.sum(jnp.where(sel2, before, 0.0), axis=-1, keepdims=True)
    new_carry = carry_ref[0:1, :] + jnp.sum(onehot, axis=0, keepdims=True)
    carry_ref[...] = jnp.broadcast_to(new_carry, carry_ref.shape)
    cnt_ref[...] = jnp.broadcast_to(new_carry, cnt_ref.shape)

    fields = (i1 - N_GROUPS, i2 - N_GROUPS, gate1, gate2, rank1, rank2)
    meta = jnp.zeros(logits.shape, _F32)
    for k, v in enumerate(fields):
        meta = jnp.where(lane == k, v, meta)
    meta_ref[...] = meta


def _out_router(y_hy, y_na, g_hy, g_na, w_out, x2, g_ffn, wg, bg, we, be):
    T, D = x2.shape
    n_r = N_GROUPS + N_EXPERTS
    rw = jnp.pad(jnp.concatenate([wg, we], axis=1).astype(_F32), ((0, 0), (0, ROUTER_LANES - n_r)))
    rb = jnp.pad(jnp.concatenate([bg, be]).astype(_F32), (0, ROUTER_LANES - n_r)).reshape(1, ROUTER_LANES)
    w_bf = w_out.astype(_BF16)
    row = lambda n: pl.BlockSpec((ROW_TILE, n), lambda i: (i, 0))
    vec = lambda n: pl.BlockSpec((1, n), lambda i: (0, 0))
    mat = lambda r, c: pl.BlockSpec((r, c), lambda i: (0, 0))
    return pl.pallas_call(
        _out_router_kernel,
        grid=(T // ROW_TILE,),
        in_specs=[row(HY_WIDTH), row(NA_WIDTH), vec(HY_WIDTH), vec(NA_WIDTH),
                  mat(HY_WIDTH, D), mat(NA_WIDTH, D), row(D), vec(D),
                  mat(D, ROUTER_LANES), vec(ROUTER_LANES)],
        out_specs=[row(D),
                   pl.BlockSpec((ROW_TILE,) + TOKEN_TILE, lambda i: (i, 0, 0)),
                   row(ROUTER_LANES),
                   pl.BlockSpec((V7X_SUBLANES, ROUTER_LANES), lambda i: (0, 0))],
        out_shape=[jax.ShapeDtypeStruct((T, D), _F32),
                   jax.ShapeDtypeStruct((T,) + TOKEN_TILE, _F32),
                   jax.ShapeDtypeStruct((T, ROUTER_LANES), _F32),
                   jax.ShapeDtypeStruct((V7X_SUBLANES, ROUTER_LANES), _F32)],
        scratch_shapes=[pltpu.VMEM((V7X_SUBLANES, ROUTER_LANES), _F32)],
        compiler_params=_cparams("arbitrary"),
        name="out_proj_router",
    )(y_hy, y_na, g_hy.reshape(1, -1), g_na.reshape(1, -1), w_bf[:HY_WIDTH], w_bf[HY_WIDTH:],
      x2, g_ffn.reshape(1, D), rw, rb)


def _scatter_kernel(padlo_ref, padhi_ref, dest_ref, xn_ref, xb_ref, zero_ref, sem):
    tm = xn_ref.shape[0]
    row_copy = lambda src, d: pltpu.make_async_copy(src, xb_ref.at[d], sem)

    @pl.when(pl.program_id(0) == 0)
    def _():
        zero_ref[...] = jnp.zeros_like(zero_ref)

        def per_range(fn):
            def outer(e, c):
                return lax.fori_loop(padlo_ref[e], padhi_ref[e], fn, c)
            lax.fori_loop(0, N_EXPERTS + 1, outer, 0)

        def start(r, c):
            row_copy(zero_ref, r).start()
            return c

        def wait(r, c):
            row_copy(zero_ref, r).wait()
            return c

        per_range(start)
        per_range(wait)

    def start(t, c):
        for k in range(TOP_K):
            row_copy(xn_ref.at[t], dest_ref[0, 0, TOP_K * t + k]).start()
        return c

    def wait(t, c):
        for k in range(TOP_K):
            row_copy(xn_ref.at[t], dest_ref[0, 0, TOP_K * t + k]).wait()
        return c

    lax.fori_loop(0, tm, start, 0)
    lax.fori_loop(0, tm, wait, 0)


def _moe_scatter(xn, dest, pad_lo, pad_hi, n_slots):
    T = xn.shape[0]
    nt = T // ROW_TILE
    pad_lo = jnp.concatenate([pad_lo, pad_hi[-1:]])
    pad_hi = jnp.concatenate([pad_hi, jnp.full((1,), n_slots, jnp.int32)])
    grid_spec = pltpu.PrefetchScalarGridSpec(
        num_scalar_prefetch=2,
        grid=(nt,),
        in_specs=[pl.BlockSpec((1, 1, TOP_K * ROW_TILE), lambda i, *_: (i, 0, 0), memory_space=pltpu.SMEM),
                  pl.BlockSpec((ROW_TILE,) + TOKEN_TILE, lambda i, *_: (i, 0, 0))],
        out_specs=pl.BlockSpec(memory_space=pl.ANY),
        scratch_shapes=[pltpu.VMEM(TOKEN_TILE, _F32), pltpu.SemaphoreType.DMA(())],
    )
    return pl.pallas_call(
        _scatter_kernel,
        grid_spec=grid_spec,
        out_shape=jax.ShapeDtypeStruct((n_slots,) + TOKEN_TILE, _F32),
        compiler_params=_cparams("arbitrary"),
        name="moe_scatter",
    )(pad_lo, pad_hi, dest.reshape(nt, 1, TOP_K * ROW_TILE), xn)


def _token_rows(ref, *lead):
    return jnp.concatenate([ref[lead + (slice(None), j, slice(None))] for j in range(V7X_SUBLANES)], axis=-1)


def _expert_kernel(be_ref, nblk_ref, xb_ref, w1_ref, w3_ref, w2_ref, yb_ref, wb1_ref, wb3_ref, wb2_ref):
    i = pl.program_id(0)

    @pl.when(i < nblk_ref[0])
    def _():
        e = be_ref[i]
        e_prev = be_ref[jnp.maximum(i - 1, 0)]

        @pl.when((i == 0) | (e != e_prev))
        def _():
            wb1_ref[...] = w1_ref[0].astype(_BF16)
            wb3_ref[...] = w3_ref[0].astype(_BF16)
            wb2_ref[...] = w2_ref[0].astype(_BF16)

        x = _token_rows(xb_ref).astype(_BF16)
        a = _dot(x, wb1_ref[...])
        hid = (a * jax.nn.sigmoid(a) * _dot(x, wb3_ref[...])).astype(_BF16)
        y = _dot(hid, wb2_ref[...])
        for j in range(V7X_SUBLANES):
            yb_ref[:, j, :] = y[:, j * V7X_LANES:(j + 1) * V7X_LANES]

    @pl.when(i >= nblk_ref[0])
    def _():
        yb_ref[...] = jnp.zeros_like(yb_ref)


def _moe_experts(xb, block_e, n_blk, w1, w3, w2):
    n_blocks = xb.shape[0] // MOE_BLOCK
    _, D, DE = w1.shape
    live = lambda i, nb: jnp.minimum(i, nb[0] - 1)
    grid_spec = pltpu.PrefetchScalarGridSpec(
        num_scalar_prefetch=2,
        grid=(n_blocks,),
        in_specs=[pl.BlockSpec((MOE_BLOCK,) + TOKEN_TILE, lambda i, be, nb: (live(i, nb), 0, 0)),
                  pl.BlockSpec((1, D, DE), lambda i, be, nb: (be[live(i, nb)], 0, 0)),
                  pl.BlockSpec((1, D, DE), lambda i, be, nb: (be[live(i, nb)], 0, 0)),
                  pl.BlockSpec((1, DE, D), lambda i, be, nb: (be[live(i, nb)], 0, 0))],
        out_specs=pl.BlockSpec((MOE_BLOCK,) + TOKEN_TILE, lambda i, be, nb: (i, 0, 0)),
        scratch_shapes=[pltpu.VMEM((D, DE), _BF16), pltpu.VMEM((D, DE), _BF16), pltpu.VMEM((DE, D), _BF16)],
    )
    return pl.pallas_call(
        _expert_kernel,
        grid_spec=grid_spec,
        out_shape=jax.ShapeDtypeStruct(xb.shape, _F32),
        compiler_params=_cparams("arbitrary"),
        name="moe_experts",
    )(block_e, n_blk, xb, w1, w3, w2)


def _final_kernel(dest_ref, h1_ref, meta_ref, yb_ref, p_ref, wg_ref, wp_ref, gple_ref, gfin_ref, o_ref, ybuf_ref, sem):
    tm = h1_ref.shape[0]
    row_copy = lambda t, k: pltpu.make_async_copy(yb_ref.at[dest_ref[0, 0, TOP_K * t + k]], ybuf_ref.at[k, t], sem)

    def start(t, c):
        for k in range(TOP_K):
            row_copy(t, k).start()
        return c

    def wait(t, c):
        for k in range(TOP_K):
            row_copy(t, k).wait()
        return c

    lax.fori_loop(0, tm, start, 0)
    lax.fori_loop(0, tm, wait, 0)

    meta = meta_ref[...]
    moe = (_token_rows(ybuf_ref, 0) * meta[:, _META_GATE:_META_GATE + 1]
           + _token_rows(ybuf_ref, 1) * meta[:, _META_GATE + 1:_META_GATE + 2])
    h2 = h1_ref[...] + moe
    gate = jax.nn.sigmoid(_dot(_rmsnorm(h2, gple_ref[...]).astype(_BF16), wg_ref[...]))
    h3 = h2 + _dot(p_ref[...].astype(_BF16), wp_ref[...]) * gate
    o_ref[...] = _rmsnorm(h3, gfin_ref[...])


def _final(dest, h1, meta, yb, p2, w_gate, w_proj, g_ple, g_final):
    T, D = h1.shape
    nt = T // ROW_TILE
    PD = p2.shape[1]
    row = lambda n: pl.BlockSpec((ROW_TILE, n), lambda i: (i, 0))
    vec = lambda n: pl.BlockSpec((1, n), lambda i: (0, 0))
    return pl.pallas_call(
        _final_kernel,
        grid=(nt,),
        in_specs=[pl.BlockSpec((1, 1, TOP_K * ROW_TILE), lambda i: (i, 0, 0), memory_space=pltpu.SMEM),
                  row(D), row(ROUTER_LANES),
                  pl.BlockSpec(memory_space=pl.ANY),
                  row(PD),
                  pl.BlockSpec((D, D), lambda i: (0, 0)),
                  pl.BlockSpec((PD, D), lambda i: (0, 0)),
                  vec(D), vec(D)],
        out_specs=row(D),
        out_shape=jax.ShapeDtypeStruct((T, D), _F32),
        scratch_shapes=[pltpu.VMEM((TOP_K, ROW_TILE) + TOKEN_TILE, _F32), pltpu.SemaphoreType.DMA(())],
        compiler_params=_cparams("arbitrary"),
        name="moe_gather_ple_final",
    )(dest.reshape(nt, 1, TOP_K * ROW_TILE), h1, meta, yb, p2, w_gate.astype(_BF16), w_proj.astype(_BF16),
      g_ple.reshape(1, D), g_final.reshape(1, D))


def _routing_tables(meta, counts_f, n_blocks):
    ids = meta[:, _META_ID:_META_ID + TOP_K].astype(jnp.int32)
    rank = meta[:, _META_RANK:_META_RANK + TOP_K].astype(jnp.int32)
    counts = counts_f[0, N_GROUPS:N_GROUPS + N_EXPERTS].astype(jnp.int32)
    padded = (counts + MOE_BLOCK - 1) // MOE_BLOCK * MOE_BLOCK
    pad_end = jnp.cumsum(padded)
    pad_start = pad_end - padded
    dest = pad_start[ids] + rank
    block_e = jnp.minimum(jnp.searchsorted(pad_end, jnp.arange(n_blocks, dtype=jnp.int32) * MOE_BLOCK,
                                           side='right'), N_EXPERTS - 1).astype(jnp.int32)
    n_blk = (pad_end[-1:] // MOE_BLOCK).astype(jnp.int32)
    return dest.astype(jnp.int32), (pad_start + counts).astype(jnp.int32), pad_end.astype(jnp.int32), block_e, n_blk


def _one_layer(h, p, g_mix, w_in, hy_conv_w, hy_conv_b, hy_f_w1, hy_f_b1, hy_f_freq1, hy_f_w2, hy_f_b2,
               hy_f_freq2, hy_f_w3, hy_skip, na_rpb, g_out_hy, g_out_na, w_out, g_ffn, router_wg, router_bg,
               router_we, router_be, exp_w1, exp_w3, exp_w2, g_ple, w_ple_gate, w_ple_proj):
    B, L, D = h.shape
    T = B * L
    assert T % ROW_TILE == 0 and L % FREQ_CHUNK == 0 and w_in.shape[1] == HY_COLS + 3 * NA_WIDTH
    x2 = h.reshape(T, D)
    u_hy, qkv = _in_proj(x2, g_mix, w_in)

    hf = _filter_mlp(L, hy_f_w1, hy_f_b1, hy_f_freq1, hy_f_w2, hy_f_b2, hy_f_freq2, hy_f_w3)
    fmat = _dft_matrix(L)
    fmat_bf = fmat.astype(_BF16)
    coef_a, coef_b, coef_d = _filter_spec(fmat_bf, hf)
    y_hy = _hyena(u_hy.reshape(B, L, HY_COLS), hy_conv_w.astype(_F32), hy_conv_b.astype(_F32),
                  fmat_bf, fmat.T.astype(_BF16), coef_a, coef_b, coef_d, hy_skip.astype(_F32))
    y_na = _natten(qkv.reshape(B, L, 3 * NA_WIDTH), na_rpb)

    h1, xn, meta, counts = _out_router(y_hy.reshape(T, HY_WIDTH), y_na.reshape(T, NA_WIDTH), g_out_hy, g_out_na,
                                       w_out, x2, g_ffn, router_wg, router_bg, router_we, router_be)
    n_blocks = -(-(T * TOP_K + N_EXPERTS * (MOE_BLOCK - 1)) // MOE_BLOCK)
    dest, pad_lo, pad_hi, block_e, n_blk = _routing_tables(meta, counts, n_blocks)
    xb = _moe_scatter(xn, dest, pad_lo, pad_hi, n_blocks * MOE_BLOCK)
    yb = _moe_experts(xb, block_e, n_blk, exp_w1, exp_w3, exp_w2)
    return h1, dest, meta, yb


def kernel(x, p, g_mix, w_in, hy_conv_w, hy_conv_b, hy_f_w1, hy_f_b1, hy_f_freq1, hy_f_w2, hy_f_b2, hy_f_freq2, hy_f_w3, hy_skip, na_rpb, g_out_hy, g_out_na, w_out, g_ffn, router_wg, router_bg, router_we, router_be, exp_w1, exp_w3, exp_w2, g_ple, w_ple_gate, w_ple_proj, g_final):
    depth = p.shape[0]
    assert depth == 1, "the final RMSNorm is fused into the last layer's kernel; one layer is supported"
    B, L, D = x.shape
    i = 0
    h1, dest, meta, yb = _one_layer(
        x, p[i], g_mix[i], w_in[i], hy_conv_w[i], hy_conv_b[i], hy_f_w1[i], hy_f_b1[i], hy_f_freq1[i],
        hy_f_w2[i], hy_f_b2[i], hy_f_freq2[i], hy_f_w3[i], hy_skip[i], na_rpb[i], g_out_hy[i], g_out_na[i],
        w_out[i], g_ffn[i], router_wg[i], router_bg[i], router_we[i], router_be[i], exp_w1[i], exp_w3[i],
        exp_w2[i], g_ple[i], w_ple_gate[i], w_ple_proj[i])
    out = _final(dest, h1, meta, yb, p[i].reshape(B * L, -1), w_ple_gate[i], w_ple_proj[i], g_ple[i], g_final)
    return out.reshape(B, L, D)
```

```python
import functools
import math

import numpy as np
import jax
import jax.numpy as jnp
from jax import lax
from jax.experimental import pallas as pl
from jax.experimental.pallas import tpu as pltpu

_F32 = jnp.float32
_BF16 = jnp.bfloat16

GRID_W = 64
HY_WIDTH = 512
NA_WIDTH = 512
NA_HEADS = 8
NA_HEAD_DIM = 64
HY_ORDER = 2
SHORT_CONV = 3
FILTER_EMB = 33
FILTER_BANDS = (FILTER_EMB - 1) // 2
DECAY_TARGET = 1e-2
FAST_DECAY_PCT = 0.3
SLOW_DECAY_PCT = 1.5
WIN_ROWS = 8
WIN_COLS = 16
Q_ROWS = 2
N_GROUPS = 4
EXPERTS_PER_GROUP = 8
N_EXPERTS = N_GROUPS * EXPERTS_PER_GROUP
TOP_K = 2
MOE_BLOCK = 256
EPS = 1e-6
NEG_INF = -1e30
HY_COLS = (HY_ORDER + 1) * HY_WIDTH

V7X_LANES = 128
V7X_SUBLANES = 8
V7X_VMEM_LIMIT_BYTES = 56 * 2 ** 20

ROW_TILE = 512
FREQ_CHUNK = 512
HY_CH_TILE = 256
ROUTER_LANES = V7X_LANES
DMA_UNROLL = 8


def _cparams(*sem):
    return pltpu.CompilerParams(dimension_semantics=sem, vmem_limit_bytes=V7X_VMEM_LIMIT_BYTES)


def _resident(shape, index_map):
    return pl.BlockSpec(shape, index_map, pipeline_mode=pl.Buffered(1))


def _rmsnorm(x, g):
    return x * lax.rsqrt(jnp.mean(x * x, axis=-1, keepdims=True) + EPS) * g


def _dot(a, b):
    return jnp.dot(a, b, preferred_element_type=_F32)


def _dot_f32(a, b):
    return jnp.dot(a, b, preferred_element_type=_F32, precision=lax.Precision.HIGHEST)


def _in_proj_kernel(x_ref, g_ref, w_ref, uhy_ref, qkv_ref):
    xn = _rmsnorm(x_ref[...], g_ref[...]).astype(_BF16)
    n_hy = uhy_ref.shape[1]
    for c0 in range(0, n_hy, 512):
        uhy_ref[:, c0:c0 + 512] = _dot(xn, w_ref[:, c0:c0 + 512]).astype(_BF16)
    for c0 in range(0, qkv_ref.shape[1], 512):
        qkv_ref[:, c0:c0 + 512] = _dot(xn, w_ref[:, n_hy + c0:n_hy + c0 + 512]).astype(_BF16)


def _in_proj(x2, g_mix, w_in):
    T, D = x2.shape
    n_in = w_in.shape[1]
    n_qkv = n_in - HY_COLS
    return pl.pallas_call(
        _in_proj_kernel,
        grid=(T // ROW_TILE,),
        in_specs=[pl.BlockSpec((ROW_TILE, D), lambda i: (i, 0)),
                  pl.BlockSpec((1, D), lambda i: (0, 0)),
                  _resident((D, n_in), lambda i: (0, 0))],
        out_specs=[pl.BlockSpec((ROW_TILE, HY_COLS), lambda i: (i, 0)),
                   pl.BlockSpec((ROW_TILE, n_qkv), lambda i: (i, 0))],
        out_shape=[jax.ShapeDtypeStruct((T, HY_COLS), _BF16),
                   jax.ShapeDtypeStruct((T, n_qkv), _BF16)],
        compiler_params=_cparams("arbitrary"),
        name="in_proj",
    )(x2, g_mix.reshape(1, D), w_in.astype(_BF16))


def _filter_mlp_kernel(z_ref, w1_ref, b1_ref, f1_ref, w2_ref, b2_ref, f2_ref, w3_ref, delta_ref, hf_ref):
    z = z_ref[...]
    hid = jnp.sin(f1_ref[...] * (_dot_f32(z, w1_ref[...]) + b1_ref[...]))
    hid = jnp.sin(f2_ref[...] * (_dot_f32(hid, w2_ref[...]) + b2_ref[...]))
    hf = _dot_f32(hid, w3_ref[...])
    decay = jnp.exp(-z[:, 0:1] * delta_ref[...])
    tl = z.shape[0]
    row = lax.broadcasted_iota(jnp.int32, (tl, HY_WIDTH), 0) + pl.program_id(0) * tl
    for k in range(2 * HY_ORDER):
        blk = hf[:, k * HY_WIDTH:(k + 1) * HY_WIDTH] * decay
        if k >= HY_ORDER:
            blk = jnp.where(row == 0, 0.0, blk)
        hf_ref[:, k * HY_WIDTH:(k + 1) * HY_WIDTH] = blk


def _filter_mlp(L, w1, b1, f1, w2, b2, f2, w3):
    t = jnp.linspace(0.0, 1.0, L, dtype=_F32)[:, None]
    w = 2.0 * math.pi * jnp.arange(L, dtype=_F32)[:, None] / L
    bands = jnp.linspace(1e-4, FILTER_BANDS - 1, FILTER_BANDS, dtype=_F32)[None, :]
    z = jnp.concatenate([t, jnp.cos(bands * w), -jnp.sin(bands * w)], axis=-1)
    z = jnp.pad(z, ((0, 0), (0, V7X_LANES - FILTER_EMB)))
    w1p = jnp.pad(w1.astype(_F32), ((0, V7X_LANES - FILTER_EMB), (0, 0)))
    max_decay = math.log(DECAY_TARGET) / FAST_DECAY_PCT
    min_decay = math.log(DECAY_TARGET) / SLOW_DECAY_PCT
    deltas = jnp.abs(jnp.linspace(min_decay, max_decay, HY_WIDTH, dtype=_F32))[None, :]
    hid = w1.shape[1]
    n_out = w3.shape[1]
    tl = min(L, ROW_TILE)
    full = lambda shape: pl.BlockSpec(shape, lambda i: (0, 0))
    return pl.pallas_call(
        _filter_mlp_kernel,
        grid=(L // tl,),
        in_specs=[pl.BlockSpec((tl, V7X_LANES), lambda i: (i, 0)),
                  full((V7X_LANES, hid)), full((1, hid)), full((1, hid)),
                  full((hid, hid)), full((1, hid)), full((1, hid)),
                  full((hid, n_out)), full((1, HY_WIDTH))],
        out_specs=pl.BlockSpec((tl, n_out), lambda i: (i, 0)),
        out_shape=jax.ShapeDtypeStruct((L, n_out), _F32),
        compiler_params=_cparams("arbitrary"),
        name="hyena_filter_mlp",
    )(z, w1p, b1.reshape(1, hid), f1.reshape(1, hid), w2.astype(_F32), b2.reshape(1, hid),
      f2.reshape(1, hid), w3.astype(_F32), deltas)


def _dft_matrix(L):
    n = 2 * L
    s = 64
    assert L % s == 0
    t = jnp.arange(L, dtype=jnp.int32)[None, :]
    ang = lambda f: ((f[:, None] * t) % n).astype(_F32) * (2.0 * math.pi / n)
    a_hi = ang(jnp.arange(L // s, dtype=jnp.int32) * s)
    a_lo = ang(jnp.arange(s, dtype=jnp.int32))
    c_hi, s_hi, c_lo, s_lo = jnp.cos(a_hi), jnp.sin(a_hi), jnp.cos(a_lo), jnp.sin(a_lo)
    alt = jnp.where(t % 2 == 0, 1.0, -1.0).astype(_F32)
    first = lax.broadcasted_iota(jnp.int32, (L, 1), 0) == 0

    top = (c_hi[:, None, :] * c_lo[None] - s_hi[:, None, :] * s_lo[None]).reshape(L, L)
    bot = -(s_hi[:, None, :] * c_lo[None] + c_hi[:, None, :] * s_lo[None]).reshape(L, L)
    fmat = jnp.concatenate([top, jnp.where(first, alt, bot)], axis=0)

    c_hi, s_hi, c_lo, s_lo = c_hi.T, s_hi.T, c_lo.T, s_lo.T
    top_t = (c_hi[:, :, None] * c_lo[:, None, :] - s_hi[:, :, None] * s_lo[:, None, :]).reshape(L, L)
    bot_t = -(s_hi[:, :, None] * c_lo[:, None, :] + c_hi[:, :, None] * s_lo[:, None, :]).reshape(L, L)
    fmat_t = jnp.concatenate([top_t, jnp.where(first.T, alt.T, bot_t)], axis=1)
    return fmat.astype(_BF16), fmat_t.astype(_BF16)


def _filter_spec_kernel(f_ref, xf_ref, xb_ref, a_ref, b_ref, d_ref):
    L = xf_ref.shape[0]
    n = 2 * L
    ct = xf_ref.shape[1]
    x = jnp.concatenate([xf_ref[...], xb_ref[...]], axis=1).astype(_BF16)
    for r0 in range(0, L, FREQ_CHUNK):
        top = _dot(f_ref[r0:r0 + FREQ_CHUNK, :], x)
        bot = _dot(f_ref[L + r0:L + r0 + FREQ_CHUNK, :], x)
        kre = top[:, :ct] + top[:, ct:]
        kim = bot[:, :ct] - bot[:, ct:]
        knyq = bot[:, :ct] + bot[:, ct:]
        if r0 == 0:
            row0 = lax.broadcasted_iota(jnp.int32, kre.shape, 0) == 0
            a = jnp.where(row0, kre * (1.0 / n), kre * (2.0 / n))
            b = jnp.where(row0, 0.0, kim * (2.0 / n))
            d = jnp.where(row0, knyq * (1.0 / n), kre * (2.0 / n))
        else:
            a = kre * (2.0 / n)
            b = kim * (2.0 / n)
            d = a
        a_ref[0, r0:r0 + FREQ_CHUNK, :] = a
        b_ref[0, r0:r0 + FREQ_CHUNK, :] = b
        d_ref[0, r0:r0 + FREQ_CHUNK, :] = d


def _filter_spec(fmat, hf):
    L = hf.shape[0]
    n_ct = HY_WIDTH // HY_CH_TILE
    out = jax.ShapeDtypeStruct((HY_ORDER, L, HY_WIDTH), _F32)
    ospec = pl.BlockSpec((1, L, HY_CH_TILE), lambda j: (j // n_ct, 0, j % n_ct))
    return pl.pallas_call(
        _filter_spec_kernel,
        grid=(HY_ORDER * n_ct,),
        in_specs=[_resident((2 * L, L), lambda j: (0, 0)),
                  pl.BlockSpec((L, HY_CH_TILE), lambda j: (0, j)),
                  pl.BlockSpec((L, HY_CH_TILE), lambda j: (0, HY_ORDER * n_ct + j))],
        out_specs=[ospec, ospec, ospec],
        out_shape=[out, out, out],
        compiler_params=_cparams("arbitrary"),
        name="hyena_filter_spectrum",
    )(fmat, hf, hf)


def _short_conv(z, w_ref, b_ref):
    L = z.shape[0]
    row = lax.broadcasted_iota(jnp.int32, z.shape, 0)
    zm = jnp.where(row == 0, 0.0, pltpu.roll(z, 1, axis=0))
    zp = jnp.where(row == L - 1, 0.0, pltpu.roll(z, L - 1, axis=0))
    return ((b_ref[...] + zm * w_ref[0:1, :]) + z * w_ref[1:2, :]) + zp * w_ref[2:3, :]


def _hy_fwd_kernel(z_ref, cw_ref, cb_ref, f_ref, a_ref, b_ref, d_ref, y_ref, *, conv_input):
    z = z_ref[0].astype(_F32)
    if conv_input:
        z = _short_conv(z, cw_ref, cb_ref)
    zb = z.astype(_BF16)
    L = z.shape[0]
    for r0 in range(0, L, FREQ_CHUNK):
        top = _dot(f_ref[r0:r0 + FREQ_CHUNK, :], zb)
        bot = _dot(f_ref[L + r0:L + r0 + FREQ_CHUNK, :], zb)
        a = a_ref[0, r0:r0 + FREQ_CHUNK, :]
        b = b_ref[0, r0:r0 + FREQ_CHUNK, :]
        d = d_ref[0, r0:r0 + FREQ_CHUNK, :]
        y_ref[0, r0:r0 + FREQ_CHUNK, :] = (top * a - bot * b).astype(_BF16)
        y_ref[0, L + r0:L + r0 + FREQ_CHUNK, :] = (top * b + bot * d).astype(_BF16)


def _hy_inv_kernel(y_ref, ft_ref, gate_ref, gw_ref, gb_ref, zp_ref, zw_ref, zb_ref, skip_ref, o_ref, *, conv_prev):
    gate = _short_conv(gate_ref[0].astype(_F32), gw_ref, gb_ref)
    zprev = zp_ref[0].astype(_F32)
    if conv_prev:
        zprev = _short_conv(zprev, zw_ref, zb_ref)
    res = gate * (zprev * skip_ref[0])
    L = gate.shape[0]
    y = y_ref[0]
    for r0 in range(0, L, FREQ_CHUNK):
        conv = _dot(ft_ref[r0:r0 + FREQ_CHUNK, :], y)
        o_ref[0, r0:r0 + FREQ_CHUNK, :] = gate[r0:r0 + FREQ_CHUNK] * conv + res[r0:r0 + FREQ_CHUNK]


def _hyena(u_hy, conv_w, conv_b, fmat, fmat_t, coef_a, coef_b, coef_d, skip):
    B, L, _ = u_hy.shape
    n_ct = HY_WIDTH // HY_CH_TILE
    grid = (n_ct, B)
    conv_b2 = conv_b.reshape(1, HY_COLS)
    col = lambda blk0: pl.BlockSpec((1, L, HY_CH_TILE), lambda c, b: (b, 0, blk0 + c))
    cw = lambda blk0: pl.BlockSpec((SHORT_CONV, HY_CH_TILE), lambda c, b: (0, blk0 + c))
    cb = lambda blk0: pl.BlockSpec((1, HY_CH_TILE), lambda c, b: (0, blk0 + c))
    z = u_hy
    z_blk0 = 2 * n_ct
    for o in range(HY_ORDER):
        first = o == 0
        coef = lambda: pl.BlockSpec((1, L, HY_CH_TILE), lambda c, b, o=o: (o, 0, c),
                                    pipeline_mode=pl.Buffered(1))
        y = pl.pallas_call(
            functools.partial(_hy_fwd_kernel, conv_input=first),
            grid=grid,
            in_specs=[col(z_blk0), cw(2 * n_ct), cb(2 * n_ct),
                      _resident((2 * L, L), lambda c, b: (0, 0)),
                      coef(), coef(), coef()],
            out_specs=pl.BlockSpec((1, 2 * L, HY_CH_TILE), lambda c, b: (b, 0, c)),
            out_shape=jax.ShapeDtypeStruct((B, 2 * L, HY_WIDTH), _BF16),
            compiler_params=_cparams("arbitrary", "arbitrary"),
            name=f"hyena_fwd_dft_{o}",
        )(z, conv_w, conv_b2, fmat, coef_a, coef_b, coef_d)
        z = pl.pallas_call(
            functools.partial(_hy_inv_kernel, conv_prev=first),
            grid=grid,
            in_specs=[pl.BlockSpec((1, 2 * L, HY_CH_TILE), lambda c, b: (b, 0, c)),
                      _resident((L, 2 * L), lambda c, b: (0, 0)),
                      col(o * n_ct), cw(o * n_ct), cb(o * n_ct),
                      col(z_blk0), cw(2 * n_ct), cb(2 * n_ct),
                      pl.BlockSpec((1, 1, HY_CH_TILE), lambda c, b, o=o: (o, 0, c))],
            out_specs=pl.BlockSpec((1, L, HY_CH_TILE), lambda c, b: (b, 0, c)),
            out_shape=jax.ShapeDtypeStruct((B, L, HY_WIDTH), _F32),
            compiler_params=_cparams("arbitrary", "arbitrary"),
            name=f"hyena_inv_dft_{o}",
        )(y, fmat_t, u_hy, conv_w, conv_b2, z, conv_w, conv_b2, skip.reshape(HY_ORDER, 1, HY_WIDTH))
        z_blk0 = 0
    return z


def _natten_tables(rows):
    kr = min(WIN_ROWS, rows)
    krb = min(Q_ROWS + kr - 1, rows)
    rs = np.clip(np.arange(rows) - kr // 2, 0, rows - kr)
    cs = np.clip(np.arange(GRID_W) - WIN_COLS // 2, 0, GRID_W - WIN_COLS)
    qc = np.tile(np.arange(GRID_W), Q_ROWS)[:, None]
    kc = np.tile(np.arange(GRID_W), krb)[None, :]
    cases, case_of, kstart = [], [], []
    for p in range(rows // Q_ROWS):
        k_r0 = min(rs[p * Q_ROWS], rows - krb)
        qr = (p * Q_ROWS + np.repeat(np.arange(Q_ROWS), GRID_W))[:, None]
        kr_ = (k_r0 + np.repeat(np.arange(krb), GRID_W))[None, :]
        valid = ((kr_ >= rs[qr]) & (kr_ < rs[qr] + kr) & (kc >= cs[qc]) & (kc < cs[qc] + WIN_COLS))
        dr = np.clip(kr_ - qr + WIN_ROWS - 1, 0, 2 * WIN_ROWS - 2)
        dc = np.clip(kc - qc + WIN_COLS - 1, 0, 2 * WIN_COLS - 2)
        idx = np.where(valid, dr * (2 * WIN_COLS - 1) + dc, -1).astype(np.int32)
        for ci, c in enumerate(cases):
            if np.array_equal(c, idx):
                break
        else:
            ci = len(cases)
            cases.append(idx)
        case_of.append(ci)
        kstart.append(k_r0 * GRID_W)
    return np.stack(cases), np.asarray(case_of, np.int32), np.asarray(kstart, np.int32), krb * GRID_W


def _natten_bias(rpb, cases):
    n_case, nq, nk = cases.shape
    krb = nk // GRID_W
    n_dr, n_dc = 2 * WIN_ROWS - 1, 2 * WIN_COLS - 1
    c5 = cases.reshape(n_case, Q_ROWS, GRID_W, krb, GRID_W)
    dr_blk = np.where(c5 >= 0, c5 // n_dc, -1).max(axis=(2, 4))
    dc_idx = np.clip(np.arange(GRID_W)[None, :] - np.arange(GRID_W)[:, None] + WIN_COLS - 1, 0, n_dc - 1)
    assert np.all((c5 < 0) | (c5 // n_dc == dr_blk[:, :, None, :, None]))
    assert np.all((c5 < 0) | (c5 % n_dc == dc_idx[None, None, :, None, :]))
    onehot_c = (dc_idx.reshape(1, -1) == np.arange(n_dc)[:, None]).astype(np.float32)
    sel = (np.maximum(dr_blk, 0).reshape(-1, 1) == np.arange(n_dr)[None, :]).astype(np.float32)
    hi = lax.Precision.HIGHEST
    toe = jnp.einsum('hdj,jx->hdx', rpb.astype(_F32), onehot_c, precision=hi)
    blk = jnp.einsum('sd,hdx->hsx', sel, toe, precision=hi)
    blk = blk.reshape(NA_HEADS, n_case, Q_ROWS, krb, GRID_W, GRID_W).transpose(1, 0, 2, 4, 3, 5)
    return jnp.where((cases >= 0)[:, None], blk.reshape(n_case, NA_HEADS, nq, nk), NEG_INF)


def _natten_kernel(case_ref, kstart_ref, qkv_ref, bias_ref, o_ref, *, n_pairs, n_keys):
    nq = Q_ROWS * GRID_W
    pair_w = 2 * NA_HEAD_DIM
    lane = lax.broadcasted_iota(jnp.int32, (nq, pair_w), 1)
    lo_half = lane < NA_HEAD_DIM
    scale = NA_HEAD_DIM ** -0.5

    def body(p, carry):
        q0 = pl.multiple_of(p * nq, nq)
        k0 = pl.multiple_of(kstart_ref[p], GRID_W)
        case = case_ref[p]
        for hp in range(NA_HEADS // 2):
            c0 = hp * pair_w
            q2 = qkv_ref[0, pl.ds(q0, nq), c0:c0 + pair_w]
            k2 = qkv_ref[0, pl.ds(k0, n_keys), NA_WIDTH + c0:NA_WIDTH + c0 + pair_w]
            v2 = qkv_ref[0, pl.ds(k0, n_keys), 2 * NA_WIDTH + c0:2 * NA_WIDTH + c0 + pair_w]
            zero = jnp.zeros_like(q2)
            qq = jnp.concatenate([jnp.where(lo_half, q2, zero), jnp.where(lo_half, zero, q2)], axis=0)
            s = lax.dot_general(qq, k2, (((1,), (1,)), ((), ())), preferred_element_type=_F32)
            s = s * scale + bias_ref[case, hp]
            m = jnp.max(s, axis=-1, keepdims=True)
            e = jnp.exp(s - m)
            l = jnp.sum(e, axis=-1, keepdims=True)
            o = _dot(e.astype(_BF16), v2) / l
            o_ref[0, pl.ds(q0, nq), c0:c0 + pair_w] = jnp.where(lo_half, o[:nq], o[nq:])
        return carry

    lax.fori_loop(0, n_pairs, body, 0, unroll=2)


def _natten(qkv, rpb):
    B, L, _ = qkv.shape
    rows = L // GRID_W
    assert rows % Q_ROWS == 0 and rows >= Q_ROWS + WIN_ROWS - 1
    cases, case_of, kstart, n_keys = _natten_tables(rows)
    n_case = cases.shape[0]
    nq = Q_ROWS * GRID_W
    bias = _natten_bias(rpb, cases).reshape(n_case, NA_HEADS // 2, 2 * nq, n_keys)
    grid_spec = pltpu.PrefetchScalarGridSpec(
        num_scalar_prefetch=2,
        grid=(B,),
        in_specs=[pl.BlockSpec((1, L, 3 * NA_WIDTH), lambda b, *_: (b, 0, 0)),
                  _resident((n_case, NA_HEADS // 2, 2 * nq, n_keys), lambda b, *_: (0, 0, 0, 0))],
        out_specs=pl.BlockSpec((1, L, NA_WIDTH), lambda b, *_: (b, 0, 0)),
    )
    return pl.pallas_call(
        functools.partial(_natten_kernel, n_pairs=rows // Q_ROWS, n_keys=n_keys),
        grid_spec=grid_spec,
        out_shape=jax.ShapeDtypeStruct((B, L, NA_WIDTH), _F32),
        compiler_params=_cparams("arbitrary"),
        name="natten",
    )(jnp.asarray(case_of), jnp.asarray(kstart), qkv, bias)


_META_ID, _META_GATE, _META_RANK = 0, 2, 4


def _out_router_kernel(yhy_ref, yna_ref, ghy_ref, gna_ref, wtop_ref, wbot_ref, x_ref, gffn_ref, rw_ref, rb_ref,
                       h1_ref, xn_ref, meta_ref, cnt_ref, carry_ref):
    @pl.when(pl.program_id(0) == 0)
    def _():
        carry_ref[...] = jnp.zeros_like(carry_ref)

    nh = _rmsnorm(yhy_ref[...], ghy_ref[...]).astype(_BF16)
    nn = _rmsnorm(yna_ref[...], gna_ref[...]).astype(_BF16)
    h1 = x_ref[...] + (_dot(nh, wtop_ref[...]) + _dot(nn, wbot_ref[...]))
    h1_ref[...] = h1
    xn = _rmsnorm(h1, gffn_ref[...])
    xn_ref[...] = xn

    logits = _dot_f32(xn, rw_ref[...]) + rb_ref[...]
    tm = logits.shape[0]
    lane = lax.broadcasted_iota(jnp.int32, logits.shape, 1).astype(_F32)
    first_max = lambda v, m: jnp.min(jnp.where(v == m, lane, float(ROUTER_LANES)), axis=-1, keepdims=True)

    gl = jnp.where(lane < N_GROUPS, logits, -jnp.inf)
    gmax = jnp.max(gl, axis=-1, keepdims=True)
    g_w = 1.0 / jnp.sum(jnp.exp(gl - gmax), axis=-1, keepdims=True)
    g_idx = first_max(gl, gmax)
    lo = N_GROUPS + EXPERTS_PER_GROUP * g_idx
    el = jnp.where((lane >= lo) & (lane < lo + EXPERTS_PER_GROUP), logits, -jnp.inf)
    m1 = jnp.max(el, axis=-1, keepdims=True)
    i1 = first_max(el, m1)
    el2 = jnp.where(lane == i1, -jnp.inf, el)
    m2 = jnp.max(el2, axis=-1, keepdims=True)
    i2 = first_max(el2, m2)
    r = jnp.exp(m2 - m1)
    gate1 = g_w / (1.0 + r)
    gate2 = g_w * r / (1.0 + r)

    sel1 = lane == i1
    sel2 = lane == i2
    onehot = jnp.where(sel1 | sel2, 1.0, 0.0)
    tri_r = lax.broadcasted_iota(jnp.int32, (tm, tm), 0)
    tri_c = lax.broadcasted_iota(jnp.int32, (tm, tm), 1)
    tri = jnp.where(tri_r > tri_c, 1.0, 0.0).astype(_BF16)
    before = carry_ref[0:1, :] + _dot(tri, onehot.astype(_BF16))
    rank1 = jnp.sum(jnp.where(sel1, before, 0.0), axis=-1, keepdims=True)
    rank2 = jnp.sum(jnp.where(sel2, before, 0.0), axis=-1, keepdims=True)
    new_carry = carry_ref[0:1, :] + jnp.sum(onehot, axis=0, keepdims=True)
    carry_ref[...] = jnp.broadcast_to(new_carry, carry_ref.shape)
    cnt_ref[...] = jnp.broadcast_to(new_carry, cnt_ref.shape)

    fields = (i1 - N_GROUPS, i2 - N_GROUPS, gate1, gate2, rank1, rank2)
    meta = jnp.zeros(logits.shape, _F32)
    for k, v in enumerate(fields):
        meta = jnp.where(lane == k, v, meta)
    meta_ref[...] = meta


def _out_router(y_hy, y_na, g_hy, g_na, w_out, x2, g_ffn, wg, bg, we, be):
    T, D = x2.shape
    n_r = N_GROUPS + N_EXPERTS
    rw = jnp.pad(jnp.concatenate([wg, we], axis=1).astype(_F32), ((0, 0), (0, ROUTER_LANES - n_r)))
    rb = jnp.pad(jnp.concatenate([bg, be]).astype(_F32), (0, ROUTER_LANES - n_r)).reshape(1, ROUTER_LANES)
    w_bf = w_out.astype(_BF16)
    row = lambda n: pl.BlockSpec((ROW_TILE, n), lambda i: (i, 0))
    vec = lambda n: pl.BlockSpec((1, n), lambda i: (0, 0))
    mat = lambda r, c: pl.BlockSpec((r, c), lambda i: (0, 0))
    return pl.pallas_call(
        _out_router_kernel,
        grid=(T // ROW_TILE,),
        in_specs=[row(HY_WIDTH), row(NA_WIDTH), vec(HY_WIDTH), vec(NA_WIDTH),
                  mat(HY_WIDTH, D), mat(NA_WIDTH, D), row(D), vec(D),
                  mat(D, ROUTER_LANES), vec(ROUTER_LANES)],
        out_specs=[row(D),
                   row(D),
                   row(ROUTER_LANES),
                   pl.BlockSpec((V7X_SUBLANES, ROUTER_LANES), lambda i: (0, 0))],
        out_shape=[jax.ShapeDtypeStruct((T, D), _F32),
                   jax.ShapeDtypeStruct((T, D), _F32),
                   jax.ShapeDtypeStruct((T, ROUTER_LANES), _F32),
                   jax.ShapeDtypeStruct((V7X_SUBLANES, ROUTER_LANES), _F32)],
        scratch_shapes=[pltpu.VMEM((V7X_SUBLANES, ROUTER_LANES), _F32)],
        compiler_params=_cparams("arbitrary"),
        name="out_proj_router",
    )(y_hy, y_na, g_hy.reshape(1, -1), g_na.reshape(1, -1), w_bf[:HY_WIDTH], w_bf[HY_WIDTH:],
      x2, g_ffn.reshape(1, D), rw, rb)


def _scatter_kernel(padlo_ref, padhi_ref, dest_ref, xn_ref, xb_ref, zero_ref, sem):
    tm = xn_ref.shape[0]
    row_copy = lambda src, d: pltpu.make_async_copy(src, xb_ref.at[pl.ds(d, 1)], sem)

    @pl.when(pl.program_id(0) == 0)
    def _():
        zero_ref[...] = jnp.zeros_like(zero_ref)

        def per_range(fn):
            def outer(e, c):
                return lax.fori_loop(padlo_ref[e], padhi_ref[e], fn, c)
            lax.fori_loop(0, N_EXPERTS + 1, outer, 0)

        def start(r, c):
            row_copy(zero_ref, r).start()
            return c

        def wait(r, c):
            row_copy(zero_ref, r).wait()
            return c

        per_range(start)
        per_range(wait)

    def start(t, c):
        for k in range(TOP_K):
            row_copy(xn_ref.at[pl.ds(t, 1)], dest_ref[0, 0, TOP_K * t + k]).start()
        return c

    lax.fori_loop(0, tm, start, 0, unroll=DMA_UNROLL)
    all_rows = xb_ref.at[pl.ds(0, TOP_K * tm)]
    pltpu.make_async_copy(all_rows, all_rows, sem).wait()


def _moe_scatter(xn, dest, pad_lo, pad_hi, n_slots):
    T, D = xn.shape
    nt = T // ROW_TILE
    pad_lo = jnp.concatenate([pad_lo, pad_hi[-1:]])
    pad_hi = jnp.concatenate([pad_hi, jnp.full((1,), n_slots, jnp.int32)])
    grid_spec = pltpu.PrefetchScalarGridSpec(
        num_scalar_prefetch=2,
        grid=(nt,),
        in_specs=[pl.BlockSpec((1, 1, TOP_K * ROW_TILE), lambda i, *_: (i, 0, 0), memory_space=pltpu.SMEM),
                  pl.BlockSpec((ROW_TILE, D), lambda i, *_: (i, 0))],
        out_specs=pl.BlockSpec(memory_space=pl.ANY),
        scratch_shapes=[pltpu.VMEM((1, D), _F32), pltpu.SemaphoreType.DMA(())],
    )
    return pl.pallas_call(
        _scatter_kernel,
        grid_spec=grid_spec,
        out_shape=jax.ShapeDtypeStruct((n_slots, D), _F32),
        compiler_params=_cparams("arbitrary"),
        name="moe_scatter",
    )(pad_lo, pad_hi, dest.reshape(nt, 1, TOP_K * ROW_TILE), xn)


def _expert_kernel(be_ref, nblk_ref, xb_ref, w1_ref, w3_ref, w2_ref, yb_ref, wb1_ref, wb3_ref, wb2_ref):
    i = pl.program_id(0)

    @pl.when(i < nblk_ref[0])
    def _():
        e = be_ref[i]
        e_prev = be_ref[jnp.maximum(i - 1, 0)]

        @pl.when((i == 0) | (e != e_prev))
        def _():
            wb1_ref[...] = w1_ref[0].astype(_BF16)
            wb3_ref[...] = w3_ref[0].astype(_BF16)
            wb2_ref[...] = w2_ref[0].astype(_BF16)

        x = xb_ref[...].astype(_BF16)
        a = _dot(x, wb1_ref[...])
        hid = (a * jax.nn.sigmoid(a) * _dot(x, wb3_ref[...])).astype(_BF16)
        yb_ref[...] = _dot(hid, wb2_ref[...])

    @pl.when(i >= nblk_ref[0])
    def _():
        yb_ref[...] = jnp.zeros_like(yb_ref)


def _moe_experts(xb, block_e, n_blk, w1, w3, w2):
    n_blocks = xb.shape[0] // MOE_BLOCK
    _, D, DE = w1.shape
    live = lambda i, nb: jnp.minimum(i, nb[0] - 1)
    grid_spec = pltpu.PrefetchScalarGridSpec(
        num_scalar_prefetch=2,
        grid=(n_blocks,),
        in_specs=[pl.BlockSpec((MOE_BLOCK, D), lambda i, be, nb: (live(i, nb), 0)),
                  pl.BlockSpec((1, D, DE), lambda i, be, nb: (be[live(i, nb)], 0, 0)),
                  pl.BlockSpec((1, D, DE), lambda i, be, nb: (be[live(i, nb)], 0, 0)),
                  pl.BlockSpec((1, DE, D), lambda i, be, nb: (be[live(i, nb)], 0, 0))],
        out_specs=pl.BlockSpec((MOE_BLOCK, D), lambda i, be, nb: (i, 0)),
        scratch_shapes=[pltpu.VMEM((D, DE), _BF16), pltpu.VMEM((D, DE), _BF16), pltpu.VMEM((DE, D), _BF16)],
    )
    return pl.pallas_call(
        _expert_kernel,
        grid_spec=grid_spec,
        out_shape=jax.ShapeDtypeStruct(xb.shape, _F32),
        compiler_params=_cparams("arbitrary"),
        name="moe_experts",
    )(block_e, n_blk, xb, w1, w3, w2)


def _final_kernel(dest_ref, h1_ref, meta_ref, yb_ref, p_ref, wg_ref, wp_ref, gple_ref, gfin_ref, o_ref, ybuf_ref, sem):
    tm = h1_ref.shape[0]

    def start(t, c):
        for k in range(TOP_K):
            pltpu.make_async_copy(yb_ref.at[pl.ds(dest_ref[0, 0, TOP_K * t + k], 1)],
                                  ybuf_ref.at[k, pl.ds(t, 1)], sem).start()
        return c

    lax.fori_loop(0, tm, start, 0, unroll=DMA_UNROLL)
    pltpu.make_async_copy(ybuf_ref, ybuf_ref, sem).wait()

    meta = meta_ref[...]
    moe = (ybuf_ref[0] * meta[:, _META_GATE:_META_GATE + 1]
           + ybuf_ref[1] * meta[:, _META_GATE + 1:_META_GATE + 2])
    h2 = h1_ref[...] + moe
    gate = jax.nn.sigmoid(_dot(_rmsnorm(h2, gple_ref[...]).astype(_BF16), wg_ref[...]))
    h3 = h2 + _dot(p_ref[...].astype(_BF16), wp_ref[...]) * gate
    o_ref[...] = _rmsnorm(h3, gfin_ref[...])


def _final(dest, h1, meta, yb, p2, w_gate, w_proj, g_ple, g_final):
    T, D = h1.shape
    nt = T // ROW_TILE
    PD = p2.shape[1]
    row = lambda n: pl.BlockSpec((ROW_TILE, n), lambda i: (i, 0))
    vec = lambda n: pl.BlockSpec((1, n), lambda i: (0, 0))
    return pl.pallas_call(
        _final_kernel,
        grid=(nt,),
        in_specs=[pl.BlockSpec((1, 1, TOP_K * ROW_TILE), lambda i: (i, 0, 0), memory_space=pltpu.SMEM),
                  row(D), row(ROUTER_LANES),
                  pl.BlockSpec(memory_space=pl.ANY),
                  row(PD),
                  pl.BlockSpec((D, D), lambda i: (0, 0)),
                  pl.BlockSpec((PD, D), lambda i: (0, 0)),
                  vec(D), vec(D)],
        out_specs=row(D),
        out_shape=jax.ShapeDtypeStruct((T, D), _F32),
        scratch_shapes=[pltpu.VMEM((TOP_K, ROW_TILE, D), _F32), pltpu.SemaphoreType.DMA(())],
        compiler_params=_cparams("arbitrary"),
        name="moe_gather_ple_final",
    )(dest.reshape(nt, 1, TOP_K * ROW_TILE), h1, meta, yb, p2, w_gate.astype(_BF16), w_proj.astype(_BF16),
      g_ple.reshape(1, D), g_final.reshape(1, D))


def _routing_tables(meta, counts_f, n_blocks):
    ids = meta[:, _META_ID:_META_ID + TOP_K].astype(jnp.int32)
    rank = meta[:, _META_RANK:_META_RANK + TOP_K].astype(jnp.int32)
    counts = counts_f[0, N_GROUPS:N_GROUPS + N_EXPERTS].astype(jnp.int32)
    padded = (counts + MOE_BLOCK - 1) // MOE_BLOCK * MOE_BLOCK
    e_iota = jnp.arange(N_EXPERTS, dtype=jnp.int32)
    pad_end = jnp.sum(jnp.where(e_iota[None, :] <= e_iota[:, None], padded[None, :], 0), axis=1)
    pad_start = pad_end - padded
    dest = jnp.sum(jnp.where(ids[..., None] == e_iota, pad_start, 0), axis=-1) + rank
    blk_row = jnp.arange(n_blocks, dtype=jnp.int32)[:, None] * MOE_BLOCK
    block_e = jnp.minimum(jnp.sum((pad_end[None, :] <= blk_row).astype(jnp.int32), axis=1), N_EXPERTS - 1)
    n_blk = (pad_end[-1:] // MOE_BLOCK).astype(jnp.int32)
    return dest.astype(jnp.int32), (pad_start + counts).astype(jnp.int32), pad_end.astype(jnp.int32), block_e, n_blk


def _one_layer(h, p, g_mix, w_in, hy_conv_w, hy_conv_b, hy_f_w1, hy_f_b1, hy_f_freq1, hy_f_w2, hy_f_b2,
               hy_f_freq2, hy_f_w3, hy_skip, na_rpb, g_out_hy, g_out_na, w_out, g_ffn, router_wg, router_bg,
               router_we, router_be, exp_w1, exp_w3, exp_w2, g_ple, w_ple_gate, w_ple_proj):
    B, L, D = h.shape
    T = B * L
    assert T % ROW_TILE == 0 and L % FREQ_CHUNK == 0 and w_in.shape[1] == HY_COLS + 3 * NA_WIDTH
    x2 = h.reshape(T, D)
    u_hy, qkv = _in_proj(x2, g_mix, w_in)

    hf = _filter_mlp(L, hy_f_w1, hy_f_b1, hy_f_freq1, hy_f_w2, hy_f_b2, hy_f_freq2, hy_f_w3)
    fmat, fmat_t = _dft_matrix(L)
    coef_a, coef_b, coef_d = _filter_spec(fmat, hf)
    y_hy = _hyena(u_hy.reshape(B, L, HY_COLS), hy_conv_w.astype(_F32), hy_conv_b.astype(_F32),
                  fmat, fmat_t, coef_a, coef_b, coef_d, hy_skip.astype(_F32))
    y_na = _natten(qkv.reshape(B, L, 3 * NA_WIDTH), na_rpb)

    h1, xn, meta, counts = _out_router(y_hy.reshape(T, HY_WIDTH), y_na.reshape(T, NA_WIDTH), g_out_hy, g_out_na,
                                       w_out, x2, g_ffn, router_wg, router_bg, router_we, router_be)
    n_blocks = -(-(T * TOP_K + N_EXPERTS * (MOE_BLOCK - 1)) // MOE_BLOCK)
    dest, pad_lo, pad_hi, block_e, n_blk = _routing_tables(meta, counts, n_blocks)
    xb = _moe_scatter(xn, dest, pad_lo, pad_hi, n_blocks * MOE_BLOCK)
    yb = _moe_experts(xb, block_e, n_blk, exp_w1, exp_w3, exp_w2)
    return h1, dest, meta, yb


def kernel(x, p, g_mix, w_in, hy_conv_w, hy_conv_b, hy_f_w1, hy_f_b1, hy_f_freq1, hy_f_w2, hy_f_b2, hy_f_freq2, hy_f_w3, hy_skip, na_rpb, g_out_hy, g_out_na, w_out, g_ffn, router_wg, router_bg, router_we, router_be, exp_w1, exp_w3, exp_w2, g_ple, w_ple_gate, w_ple_proj, g_final):
    depth = p.shape[0]
    assert depth == 1, "the final RMSNorm is fused into the last layer's kernel; one layer is supported"
    B, L, D = x.shape
    i = 0
    h1, dest, meta, yb = _one_layer(
        x, p[i], g_mix[i], w_in[i], hy_conv_w[i], hy_conv_b[i], hy_f_w1[i], hy_f_b1[i], hy_f_freq1[i],
        hy_f_w2[i], hy_f_b2[i], hy_f_freq2[i], hy_f_w3[i], hy_skip[i], na_rpb[i], g_out_hy[i], g_out_na[i],
        w_out[i], g_ffn[i], router_wg[i], router_bg[i], router_we[i], router_be[i], exp_w1[i], exp_w3[i],
        exp_w2[i], g_ple[i], w_ple_gate[i], w_ple_proj[i])
    out = _final(dest, h1, meta, yb, p[i].reshape(B * L, -1), w_ple_gate[i], w_ple_proj[i], g_ple[i], g_final)
    return out.reshape(B, L, D)
```

```python
import functools
import math

import numpy as np
import jax
import jax.numpy as jnp
from jax import lax
from jax.experimental import pallas as pl
from jax.experimental.pallas import tpu as pltpu

_F32 = jnp.float32
_BF16 = jnp.bfloat16

GRID_W = 64
HY_WIDTH = 512
NA_WIDTH = 512
NA_HEADS = 8
NA_HEAD_DIM = 64
HY_ORDER = 2
SHORT_CONV = 3
FILTER_EMB = 33
FILTER_BANDS = (FILTER_EMB - 1) // 2
DECAY_TARGET = 1e-2
FAST_DECAY_PCT = 0.3
SLOW_DECAY_PCT = 1.5
WIN_ROWS = 8
WIN_COLS = 16
Q_ROWS = 2
N_GROUPS = 4
EXPERTS_PER_GROUP = 8
N_EXPERTS = N_GROUPS * EXPERTS_PER_GROUP
TOP_K = 2
MOE_BLOCK = 512
EPS = 1e-6
NEG_INF = -1e30
HY_COLS = (HY_ORDER + 1) * HY_WIDTH

V7X_LANES = 128
V7X_SUBLANES = 8
V7X_VMEM_LIMIT_BYTES = 56 * 2 ** 20

ROW_TILE = 512
FREQ_CHUNK = 512
HY_CH_TILE = 256
ROUTER_LANES = V7X_LANES
DMA_UNROLL = 8


def _cparams(*sem):
    return pltpu.CompilerParams(dimension_semantics=sem, vmem_limit_bytes=V7X_VMEM_LIMIT_BYTES)


def _resident(shape, index_map):
    return pl.BlockSpec(shape, index_map, pipeline_mode=pl.Buffered(1))


def _rmsnorm(x, g):
    return x * lax.rsqrt(jnp.mean(x * x, axis=-1, keepdims=True) + EPS) * g


def _dot(a, b):
    return jnp.dot(a, b, preferred_element_type=_F32)


def _dot_f32(a, b):
    return jnp.dot(a, b, preferred_element_type=_F32, precision=lax.Precision.HIGHEST)


def _in_proj_kernel(x_ref, g_ref, w_ref, uhy_ref, qkv_ref):
    xn = _rmsnorm(x_ref[...], g_ref[...]).astype(_BF16)
    n_hy = uhy_ref.shape[1]
    for c0 in range(0, n_hy, 512):
        uhy_ref[:, c0:c0 + 512] = _dot(xn, w_ref[:, c0:c0 + 512]).astype(_BF16)
    for c0 in range(0, qkv_ref.shape[1], 512):
        qkv_ref[:, c0:c0 + 512] = _dot(xn, w_ref[:, n_hy + c0:n_hy + c0 + 512]).astype(_BF16)


def _in_proj(x2, g_mix, w_in):
    T, D = x2.shape
    n_in = w_in.shape[1]
    n_qkv = n_in - HY_COLS
    return pl.pallas_call(
        _in_proj_kernel,
        grid=(T // ROW_TILE,),
        in_specs=[pl.BlockSpec((ROW_TILE, D), lambda i: (i, 0)),
                  pl.BlockSpec((1, D), lambda i: (0, 0)),
                  _resident((D, n_in), lambda i: (0, 0))],
        out_specs=[pl.BlockSpec((ROW_TILE, HY_COLS), lambda i: (i, 0)),
                   pl.BlockSpec((ROW_TILE, n_qkv), lambda i: (i, 0))],
        out_shape=[jax.ShapeDtypeStruct((T, HY_COLS), _BF16),
                   jax.ShapeDtypeStruct((T, n_qkv), _BF16)],
        compiler_params=_cparams("arbitrary"),
        name="in_proj",
    )(x2, g_mix.reshape(1, D), w_in.astype(_BF16))


def _filter_mlp_kernel(z_ref, w1_ref, b1_ref, f1_ref, w2_ref, b2_ref, f2_ref, w3_ref, delta_ref, hf_ref):
    z = z_ref[...]
    hid = jnp.sin(f1_ref[...] * (_dot_f32(z, w1_ref[...]) + b1_ref[...]))
    hid = jnp.sin(f2_ref[...] * (_dot_f32(hid, w2_ref[...]) + b2_ref[...]))
    hf = _dot_f32(hid, w3_ref[...])
    decay = jnp.exp(-z[:, 0:1] * delta_ref[...])
    tl = z.shape[0]
    row = lax.broadcasted_iota(jnp.int32, (tl, HY_WIDTH), 0) + pl.program_id(0) * tl
    for k in range(2 * HY_ORDER):
        blk = hf[:, k * HY_WIDTH:(k + 1) * HY_WIDTH] * decay
        if k >= HY_ORDER:
            blk = jnp.where(row == 0, 0.0, blk)
        hf_ref[:, k * HY_WIDTH:(k + 1) * HY_WIDTH] = blk


def _filter_mlp(L, w1, b1, f1, w2, b2, f2, w3):
    t = jnp.linspace(0.0, 1.0, L, dtype=_F32)[:, None]
    w = 2.0 * math.pi * jnp.arange(L, dtype=_F32)[:, None] / L
    bands = jnp.linspace(1e-4, FILTER_BANDS - 1, FILTER_BANDS, dtype=_F32)[None, :]
    z = jnp.concatenate([t, jnp.cos(bands * w), -jnp.sin(bands * w)], axis=-1)
    z = jnp.pad(z, ((0, 0), (0, V7X_LANES - FILTER_EMB)))
    w1p = jnp.pad(w1.astype(_F32), ((0, V7X_LANES - FILTER_EMB), (0, 0)))
    max_decay = math.log(DECAY_TARGET) / FAST_DECAY_PCT
    min_decay = math.log(DECAY_TARGET) / SLOW_DECAY_PCT
    deltas = jnp.abs(jnp.linspace(min_decay, max_decay, HY_WIDTH, dtype=_F32))[None, :]
    hid = w1.shape[1]
    n_out = w3.shape[1]
    tl = min(L, ROW_TILE)
    full = lambda shape: pl.BlockSpec(shape, lambda i: (0, 0))
    return pl.pallas_call(
        _filter_mlp_kernel,
        grid=(L // tl,),
        in_specs=[pl.BlockSpec((tl, V7X_LANES), lambda i: (i, 0)),
                  full((V7X_LANES, hid)), full((1, hid)), full((1, hid)),
                  full((hid, hid)), full((1, hid)), full((1, hid)),
                  full((hid, n_out)), full((1, HY_WIDTH))],
        out_specs=pl.BlockSpec((tl, n_out), lambda i: (i, 0)),
        out_shape=jax.ShapeDtypeStruct((L, n_out), _F32),
        compiler_params=_cparams("arbitrary"),
        name="hyena_filter_mlp",
    )(z, w1p, b1.reshape(1, hid), f1.reshape(1, hid), w2.astype(_F32), b2.reshape(1, hid),
      f2.reshape(1, hid), w3.astype(_F32), deltas)


def _dft_matrix(L):
    n = 2 * L
    s = 64
    assert L % s == 0
    t = jnp.arange(L, dtype=jnp.int32)[None, :]
    ang = lambda f: ((f[:, None] * t) % n).astype(_F32) * (2.0 * math.pi / n)
    a_hi = ang(jnp.arange(L // s, dtype=jnp.int32) * s)
    a_lo = ang(jnp.arange(s, dtype=jnp.int32))
    c_hi, s_hi, c_lo, s_lo = jnp.cos(a_hi), jnp.sin(a_hi), jnp.cos(a_lo), jnp.sin(a_lo)
    alt = jnp.where(t % 2 == 0, 1.0, -1.0).astype(_F32)
    first = lax.broadcasted_iota(jnp.int32, (L, 1), 0) == 0

    top = (c_hi[:, None, :] * c_lo[None] - s_hi[:, None, :] * s_lo[None]).reshape(L, L)
    bot = -(s_hi[:, None, :] * c_lo[None] + c_hi[:, None, :] * s_lo[None]).reshape(L, L)
    fmat = jnp.concatenate([top, jnp.where(first, alt, bot)], axis=0)

    c_hi, s_hi, c_lo, s_lo = c_hi.T, s_hi.T, c_lo.T, s_lo.T
    top_t = (c_hi[:, :, None] * c_lo[:, None, :] - s_hi[:, :, None] * s_lo[:, None, :]).reshape(L, L)
    bot_t = -(s_hi[:, :, None] * c_lo[:, None, :] + c_hi[:, :, None] * s_lo[:, None, :]).reshape(L, L)
    fmat_t = jnp.concatenate([top_t, jnp.where(first.T, alt.T, bot_t)], axis=1)
    return fmat.astype(_BF16), fmat_t.astype(_BF16)


def _filter_spec_kernel(f_ref, xf_ref, xb_ref, a_ref, b_ref, d_ref):
    L = xf_ref.shape[0]
    n = 2 * L
    ct = xf_ref.shape[1]
    x = jnp.concatenate([xf_ref[...], xb_ref[...]], axis=1).astype(_BF16)
    for r0 in range(0, L, FREQ_CHUNK):
        top = _dot(f_ref[r0:r0 + FREQ_CHUNK, :], x)
        bot = _dot(f_ref[L + r0:L + r0 + FREQ_CHUNK, :], x)
        kre = top[:, :ct] + top[:, ct:]
        kim = bot[:, :ct] - bot[:, ct:]
        knyq = bot[:, :ct] + bot[:, ct:]
        if r0 == 0:
            row0 = lax.broadcasted_iota(jnp.int32, kre.shape, 0) == 0
            a = jnp.where(row0, kre * (1.0 / n), kre * (2.0 / n))
            b = jnp.where(row0, 0.0, kim * (2.0 / n))
            d = jnp.where(row0, knyq * (1.0 / n), kre * (2.0 / n))
        else:
            a = kre * (2.0 / n)
            b = kim * (2.0 / n)
            d = a
        a_ref[0, r0:r0 + FREQ_CHUNK, :] = a
        b_ref[0, r0:r0 + FREQ_CHUNK, :] = b
        d_ref[0, r0:r0 + FREQ_CHUNK, :] = d


def _filter_spec(fmat, hf):
    L = hf.shape[0]
    n_ct = HY_WIDTH // HY_CH_TILE
    out = jax.ShapeDtypeStruct((HY_ORDER, L, HY_WIDTH), _F32)
    ospec = pl.BlockSpec((1, L, HY_CH_TILE), lambda j: (j // n_ct, 0, j % n_ct))
    return pl.pallas_call(
        _filter_spec_kernel,
        grid=(HY_ORDER * n_ct,),
        in_specs=[_resident((2 * L, L), lambda j: (0, 0)),
                  pl.BlockSpec((L, HY_CH_TILE), lambda j: (0, j)),
                  pl.BlockSpec((L, HY_CH_TILE), lambda j: (0, HY_ORDER * n_ct + j))],
        out_specs=[ospec, ospec, ospec],
        out_shape=[out, out, out],
        compiler_params=_cparams("arbitrary"),
        name="hyena_filter_spectrum",
    )(fmat, hf, hf)


def _short_conv(z, w_ref, b_ref):
    L = z.shape[0]
    row = lax.broadcasted_iota(jnp.int32, z.shape, 0)
    zm = jnp.where(row == 0, 0.0, pltpu.roll(z, 1, axis=0))
    zp = jnp.where(row == L - 1, 0.0, pltpu.roll(z, L - 1, axis=0))
    return ((b_ref[...] + zm * w_ref[0:1, :]) + z * w_ref[1:2, :]) + zp * w_ref[2:3, :]


def _hy_fwd_kernel(z_ref, cw_ref, cb_ref, f_ref, a_ref, b_ref, d_ref, y_ref, *, conv_input):
    z = z_ref[0].astype(_F32)
    if conv_input:
        z = _short_conv(z, cw_ref, cb_ref)
    zb = z.astype(_BF16)
    L = z.shape[0]
    for r0 in range(0, L, FREQ_CHUNK):
        top = _dot(f_ref[r0:r0 + FREQ_CHUNK, :], zb)
        bot = _dot(f_ref[L + r0:L + r0 + FREQ_CHUNK, :], zb)
        a = a_ref[0, r0:r0 + FREQ_CHUNK, :]
        b = b_ref[0, r0:r0 + FREQ_CHUNK, :]
        d = d_ref[0, r0:r0 + FREQ_CHUNK, :]
        y_ref[0, r0:r0 + FREQ_CHUNK, :] = (top * a - bot * b).astype(_BF16)
        y_ref[0, L + r0:L + r0 + FREQ_CHUNK, :] = (top * b + bot * d).astype(_BF16)


def _hy_inv_kernel(y_ref, ft_ref, gate_ref, gw_ref, gb_ref, zp_ref, zw_ref, zb_ref, skip_ref, o_ref, *, conv_prev):
    gate = _short_conv(gate_ref[0].astype(_F32), gw_ref, gb_ref)
    zprev = zp_ref[0].astype(_F32)
    if conv_prev:
        zprev = _short_conv(zprev, zw_ref, zb_ref)
    res = gate * (zprev * skip_ref[0])
    L = gate.shape[0]
    y = y_ref[0]
    for r0 in range(0, L, FREQ_CHUNK):
        conv = _dot(ft_ref[r0:r0 + FREQ_CHUNK, :], y)
        o_ref[0, r0:r0 + FREQ_CHUNK, :] = gate[r0:r0 + FREQ_CHUNK] * conv + res[r0:r0 + FREQ_CHUNK]


def _hyena(u_hy, conv_w, conv_b, fmat, fmat_t, coef_a, coef_b, coef_d, skip):
    B, L, _ = u_hy.shape
    n_ct = HY_WIDTH // HY_CH_TILE
    grid = (n_ct, B)
    conv_b2 = conv_b.reshape(1, HY_COLS)
    col = lambda blk0: pl.BlockSpec((1, L, HY_CH_TILE), lambda c, b: (b, 0, blk0 + c))
    cw = lambda blk0: pl.BlockSpec((SHORT_CONV, HY_CH_TILE), lambda c, b: (0, blk0 + c))
    cb = lambda blk0: pl.BlockSpec((1, HY_CH_TILE), lambda c, b: (0, blk0 + c))
    z = u_hy
    z_blk0 = 2 * n_ct
    for o in range(HY_ORDER):
        first = o == 0
        coef = lambda: pl.BlockSpec((1, L, HY_CH_TILE), lambda c, b, o=o: (o, 0, c),
                                    pipeline_mode=pl.Buffered(1))
        y = pl.pallas_call(
            functools.partial(_hy_fwd_kernel, conv_input=first),
            grid=grid,
            in_specs=[col(z_blk0), cw(2 * n_ct), cb(2 * n_ct),
                      _resident((2 * L, L), lambda c, b: (0, 0)),
                      coef(), coef(), coef()],
            out_specs=pl.BlockSpec((1, 2 * L, HY_CH_TILE), lambda c, b: (b, 0, c)),
            out_shape=jax.ShapeDtypeStruct((B, 2 * L, HY_WIDTH), _BF16),
            compiler_params=_cparams("arbitrary", "arbitrary"),
            name=f"hyena_fwd_dft_{o}",
        )(z, conv_w, conv_b2, fmat, coef_a, coef_b, coef_d)
        z = pl.pallas_call(
            functools.partial(_hy_inv_kernel, conv_prev=first),
            grid=grid,
            in_specs=[pl.BlockSpec((1, 2 * L, HY_CH_TILE), lambda c, b: (b, 0, c)),
                      _resident((L, 2 * L), lambda c, b: (0, 0)),
                      col(o * n_ct), cw(o * n_ct), cb(o * n_ct),
                      col(z_blk0), cw(2 * n_ct), cb(2 * n_ct),
                      pl.BlockSpec((1, 1, HY_CH_TILE), lambda c, b, o=o: (o, 0, c))],
            out_specs=pl.BlockSpec((1, L, HY_CH_TILE), lambda c, b: (b, 0, c)),
            out_shape=jax.ShapeDtypeStruct((B, L, HY_WIDTH), _F32),
            compiler_params=_cparams("arbitrary", "arbitrary"),
            name=f"hyena_inv_dft_{o}",
        )(y, fmat_t, u_hy, conv_w, conv_b2, z, conv_w, conv_b2, skip.reshape(HY_ORDER, 1, HY_WIDTH))
        z_blk0 = 0
    return z


def _natten_tables(rows):
    kr = min(WIN_ROWS, rows)
    krb = min(Q_ROWS + kr - 1, rows)
    rs = np.clip(np.arange(rows) - kr // 2, 0, rows - kr)
    cs = np.clip(np.arange(GRID_W) - WIN_COLS // 2, 0, GRID_W - WIN_COLS)
    qc = np.tile(np.arange(GRID_W), Q_ROWS)[:, None]
    kc = np.tile(np.arange(GRID_W), krb)[None, :]
    cases, case_of, kstart = [], [], []
    for p in range(rows // Q_ROWS):
        k_r0 = min(rs[p * Q_ROWS], rows - krb)
        qr = (p * Q_ROWS + np.repeat(np.arange(Q_ROWS), GRID_W))[:, None]
        kr_ = (k_r0 + np.repeat(np.arange(krb), GRID_W))[None, :]
        valid = ((kr_ >= rs[qr]) & (kr_ < rs[qr] + kr) & (kc >= cs[qc]) & (kc < cs[qc] + WIN_COLS))
        dr = np.clip(kr_ - qr + WIN_ROWS - 1, 0, 2 * WIN_ROWS - 2)
        dc = np.clip(kc - qc + WIN_COLS - 1, 0, 2 * WIN_COLS - 2)
        idx = np.where(valid, dr * (2 * WIN_COLS - 1) + dc, -1).astype(np.int32)
        for ci, c in enumerate(cases):
            if np.array_equal(c, idx):
                break
        else:
            ci = len(cases)
            cases.append(idx)
        case_of.append(ci)
        kstart.append(k_r0 * GRID_W)
    return np.stack(cases), np.asarray(case_of, np.int32), np.asarray(kstart, np.int32), krb * GRID_W


def _natten_bias(rpb, cases):
    n_case, nq, nk = cases.shape
    krb = nk // GRID_W
    n_dr, n_dc = 2 * WIN_ROWS - 1, 2 * WIN_COLS - 1
    c5 = cases.reshape(n_case, Q_ROWS, GRID_W, krb, GRID_W)
    dr_blk = np.where(c5 >= 0, c5 // n_dc, -1).max(axis=(2, 4))
    dc_idx = np.clip(np.arange(GRID_W)[None, :] - np.arange(GRID_W)[:, None] + WIN_COLS - 1, 0, n_dc - 1)
    assert np.all((c5 < 0) | (c5 // n_dc == dr_blk[:, :, None, :, None]))
    assert np.all((c5 < 0) | (c5 % n_dc == dc_idx[None, None, :, None, :]))
    onehot_c = (dc_idx.reshape(1, -1) == np.arange(n_dc)[:, None]).astype(np.float32)
    sel = (np.maximum(dr_blk, 0).reshape(-1, 1) == np.arange(n_dr)[None, :]).astype(np.float32)
    hi = lax.Precision.HIGHEST
    toe = jnp.einsum('hdj,jx->hdx', rpb.astype(_F32), onehot_c, precision=hi)
    blk = jnp.einsum('sd,hdx->hsx', sel, toe, precision=hi)
    blk = blk.reshape(NA_HEADS, n_case, Q_ROWS, krb, GRID_W, GRID_W).transpose(1, 0, 2, 4, 3, 5)
    return jnp.where((cases >= 0)[:, None], blk.reshape(n_case, NA_HEADS, nq, nk), NEG_INF)


def _natten_kernel(case_ref, kstart_ref, qkv_ref, bias_ref, o_ref, *, n_pairs, n_keys):
    nq = Q_ROWS * GRID_W
    pair_w = 2 * NA_HEAD_DIM
    lane = lax.broadcasted_iota(jnp.int32, (nq, pair_w), 1)
    lo_half = lane < NA_HEAD_DIM
    scale = NA_HEAD_DIM ** -0.5

    def body(p, carry):
        q0 = pl.multiple_of(p * nq, nq)
        k0 = pl.multiple_of(kstart_ref[p], GRID_W)
        case = case_ref[p]
        for hp in range(NA_HEADS // 2):
            c0 = hp * pair_w
            q2 = qkv_ref[0, pl.ds(q0, nq), c0:c0 + pair_w]
            k2 = qkv_ref[0, pl.ds(k0, n_keys), NA_WIDTH + c0:NA_WIDTH + c0 + pair_w]
            v2 = qkv_ref[0, pl.ds(k0, n_keys), 2 * NA_WIDTH + c0:2 * NA_WIDTH + c0 + pair_w]
            zero = jnp.zeros_like(q2)
            qq = jnp.concatenate([jnp.where(lo_half, q2, zero), jnp.where(lo_half, zero, q2)], axis=0)
            s = lax.dot_general(qq, k2, (((1,), (1,)), ((), ())), preferred_element_type=_F32)
            s = s * scale + bias_ref[case, hp]
            m = jnp.max(s, axis=-1, keepdims=True)
            e = jnp.exp(s - m)
            l = jnp.sum(e, axis=-1, keepdims=True)
            o = _dot(e.astype(_BF16), v2) / l
            o_ref[0, pl.ds(q0, nq), c0:c0 + pair_w] = jnp.where(lo_half, o[:nq], o[nq:])
        return carry

    lax.fori_loop(0, n_pairs, body, 0, unroll=2)


def _natten(qkv, rpb):
    B, L, _ = qkv.shape
    rows = L // GRID_W
    assert rows % Q_ROWS == 0 and rows >= Q_ROWS + WIN_ROWS - 1
    cases, case_of, kstart, n_keys = _natten_tables(rows)
    n_case = cases.shape[0]
    nq = Q_ROWS * GRID_W
    bias = _natten_bias(rpb, cases).reshape(n_case, NA_HEADS // 2, 2 * nq, n_keys)
    grid_spec = pltpu.PrefetchScalarGridSpec(
        num_scalar_prefetch=2,
        grid=(B,),
        in_specs=[pl.BlockSpec((1, L, 3 * NA_WIDTH), lambda b, *_: (b, 0, 0)),
                  _resident((n_case, NA_HEADS // 2, 2 * nq, n_keys), lambda b, *_: (0, 0, 0, 0))],
        out_specs=pl.BlockSpec((1, L, NA_WIDTH), lambda b, *_: (b, 0, 0)),
    )
    return pl.pallas_call(
        functools.partial(_natten_kernel, n_pairs=rows // Q_ROWS, n_keys=n_keys),
        grid_spec=grid_spec,
        out_shape=jax.ShapeDtypeStruct((B, L, NA_WIDTH), _F32),
        compiler_params=_cparams("arbitrary"),
        name="natten",
    )(jnp.asarray(case_of), jnp.asarray(kstart), qkv, bias)


_META_ID, _META_GATE, _META_RANK = 0, 2, 4
ROUTER_ROWS = 8 + N_EXPERTS


def _split_bf16(v):
    hi = v.astype(_BF16)
    return hi, (v - hi.astype(_F32)).astype(_BF16)


def _out_router_kernel(yhy_ref, yna_ref, ghy_ref, gna_ref, wtop_ref, wbot_ref, x_ref, gffn_ref, rwh_ref, rwl_ref,
                       rb_ref, h1_ref, xn_ref, meta_ref, cnt_ref, carry_ref):
    @pl.when(pl.program_id(0) == 0)
    def _():
        carry_ref[...] = jnp.zeros_like(carry_ref)

    nh = _rmsnorm(yhy_ref[...], ghy_ref[...]).astype(_BF16)
    nn = _rmsnorm(yna_ref[...], gna_ref[...]).astype(_BF16)
    h1 = x_ref[...] + (_dot(nh, wtop_ref[...]) + _dot(nn, wbot_ref[...]))
    h1_ref[...] = h1
    xn = _rmsnorm(h1, gffn_ref[...])
    xn_ref[...] = xn
    tm = xn.shape[0]

    xh, xl = _split_bf16(xn)
    logits = _dot(xh, rwh_ref[...]) + (_dot(xh, rwl_ref[...]) + _dot(xl, rwh_ref[...]))
    lt = logits.T[:ROUTER_ROWS] + rb_ref[:, 0:1]

    neg = -jnp.inf
    row8 = lax.broadcasted_iota(jnp.int32, (EXPERTS_PER_GROUP, tm), 0).astype(_F32)
    col_max = lambda v: jnp.max(v, axis=0, keepdims=True)
    first_max = lambda v, m: jnp.min(jnp.where(v == m, row8, float(EXPERTS_PER_GROUP)), axis=0, keepdims=True)

    gl = jnp.where(row8 < N_GROUPS, lt[0:8], neg)
    gmax = col_max(gl)
    g_w = 1.0 / jnp.sum(jnp.exp(gl - gmax), axis=0, keepdims=True)
    g_idx = first_max(gl, gmax)
    el = lt[8:8 + EXPERTS_PER_GROUP]
    for g in range(1, N_GROUPS):
        el = jnp.where(g_idx == g, lt[8 + g * EXPERTS_PER_GROUP:8 + (g + 1) * EXPERTS_PER_GROUP], el)
    m1 = col_max(el)
    i1 = first_max(el, m1)
    el2 = jnp.where(row8 == i1, neg, el)
    m2 = col_max(el2)
    i2 = first_max(el2, m2)
    r = jnp.exp(m2 - m1)
    gate1 = g_w / (1.0 + r)
    gate2 = g_w * r / (1.0 + r)
    id1 = g_idx * EXPERTS_PER_GROUP + i1
    id2 = g_idx * EXPERTS_PER_GROUP + i2

    row_e = lax.broadcasted_iota(jnp.int32, (N_EXPERTS, tm), 0).astype(_F32)
    sel1 = row_e == id1
    sel2 = row_e == id2
    onehot = jnp.where(sel1 | sel2, 1.0, 0.0)
    tri_r = lax.broadcasted_iota(jnp.int32, (tm, tm), 0)
    tri_c = lax.broadcasted_iota(jnp.int32, (tm, tm), 1)
    tri = jnp.where(tri_r < tri_c, 1.0, 0.0).astype(_BF16)
    carry = carry_ref[:, 0:1]
    before = carry + _dot(onehot.astype(_BF16), tri)
    rank1 = jnp.sum(jnp.where(sel1, before, 0.0), axis=0, keepdims=True)
    rank2 = jnp.sum(jnp.where(sel2, before, 0.0), axis=0, keepdims=True)
    new_carry = carry + jnp.sum(onehot, axis=1, keepdims=True)
    carry_ref[...] = jnp.broadcast_to(new_carry, carry_ref.shape)
    cnt_ref[...] = jnp.broadcast_to(new_carry, cnt_ref.shape)

    meta = jnp.zeros((8, tm), _F32)
    for k, v in enumerate((id1, id2, gate1, gate2, rank1, rank2)):
        meta = jnp.where(row8 == k, v, meta)
    meta_ref[...] = meta


def _out_router(y_hy, y_na, g_hy, g_na, w_out, x2, g_ffn, wg, bg, we, be):
    T, D = x2.shape
    gpad = 8 - N_GROUPS
    rw = jnp.concatenate([wg.astype(_F32), jnp.zeros((D, gpad), _F32), we.astype(_F32)], axis=1)
    rw_hi, rw_lo = _split_bf16(jnp.pad(rw, ((0, 0), (0, ROUTER_LANES - ROUTER_ROWS))))
    rb = jnp.concatenate([bg.astype(_F32), jnp.zeros((gpad,), _F32), be.astype(_F32)])
    rb = jnp.broadcast_to(rb[:, None], (ROUTER_ROWS, ROUTER_LANES))
    w_bf = w_out.astype(_BF16)
    row = lambda n: pl.BlockSpec((ROW_TILE, n), lambda i: (i, 0))
    vec = lambda n: pl.BlockSpec((1, n), lambda i: (0, 0))
    mat = lambda r, c: pl.BlockSpec((r, c), lambda i: (0, 0))
    return pl.pallas_call(
        _out_router_kernel,
        grid=(T // ROW_TILE,),
        in_specs=[row(HY_WIDTH), row(NA_WIDTH), vec(HY_WIDTH), vec(NA_WIDTH),
                  mat(HY_WIDTH, D), mat(NA_WIDTH, D), row(D), vec(D),
                  mat(D, ROUTER_LANES), mat(D, ROUTER_LANES), mat(ROUTER_ROWS, ROUTER_LANES)],
        out_specs=[row(D),
                   row(D),
                   pl.BlockSpec((8, ROW_TILE), lambda i: (0, i)),
                   mat(N_EXPERTS, ROUTER_LANES)],
        out_shape=[jax.ShapeDtypeStruct((T, D), _F32),
                   jax.ShapeDtypeStruct((T, D), _F32),
                   jax.ShapeDtypeStruct((8, T), _F32),
                   jax.ShapeDtypeStruct((N_EXPERTS, ROUTER_LANES), _F32)],
        scratch_shapes=[pltpu.VMEM((N_EXPERTS, ROUTER_LANES), _F32)],
        compiler_params=_cparams("arbitrary"),
        name="out_proj_router",
    )(y_hy, y_na, g_hy.reshape(1, -1), g_na.reshape(1, -1), w_bf[:HY_WIDTH], w_bf[HY_WIDTH:],
      x2, g_ffn.reshape(1, D), rw_hi, rw_lo, rb)


def _scatter_kernel(padlo_ref, padhi_ref, dest_ref, xn_ref, xb_ref, zero_ref, sem):
    tm = xn_ref.shape[0]
    row_copy = lambda src, d: pltpu.make_async_copy(src, xb_ref.at[pl.ds(d, 1)], sem)

    @pl.when(pl.program_id(0) == 0)
    def _():
        zero_ref[...] = jnp.zeros_like(zero_ref)

        def start(r, c):
            row_copy(zero_ref, r).start()
            return c

        def wait(r, c):
            row_copy(zero_ref, r).wait()
            return c

        def one_range(e, c):
            lax.fori_loop(padlo_ref[e], padhi_ref[e], start, c)
            return lax.fori_loop(padlo_ref[e], padhi_ref[e], wait, c)

        lax.fori_loop(0, padlo_ref.shape[0], one_range, 0)

    def start(t, c):
        for k in range(TOP_K):
            row_copy(xn_ref.at[pl.ds(t, 1)], dest_ref[0, 0, TOP_K * t + k]).start()
        return c

    lax.fori_loop(0, tm, start, 0, unroll=DMA_UNROLL)
    all_rows = xb_ref.at[pl.ds(0, TOP_K * tm)]
    pltpu.make_async_copy(all_rows, all_rows, sem).wait()


def _moe_scatter(xn, dest, pad_lo, pad_hi, n_slots):
    T, D = xn.shape
    nt = T // ROW_TILE
    tail_lo = jnp.minimum(pad_hi[-1] + jnp.arange(N_EXPERTS, dtype=jnp.int32) * MOE_BLOCK, n_slots)
    pad_lo = jnp.concatenate([pad_lo, tail_lo])
    pad_hi = jnp.concatenate([pad_hi, jnp.minimum(tail_lo + MOE_BLOCK, n_slots)])
    grid_spec = pltpu.PrefetchScalarGridSpec(
        num_scalar_prefetch=2,
        grid=(nt,),
        in_specs=[pl.BlockSpec((1, 1, TOP_K * ROW_TILE), lambda i, *_: (i, 0, 0), memory_space=pltpu.SMEM),
                  pl.BlockSpec((ROW_TILE, D), lambda i, *_: (i, 0))],
        out_specs=pl.BlockSpec(memory_space=pl.ANY),
        scratch_shapes=[pltpu.VMEM((1, D), _F32), pltpu.SemaphoreType.DMA(())],
    )
    return pl.pallas_call(
        _scatter_kernel,
        grid_spec=grid_spec,
        out_shape=jax.ShapeDtypeStruct((n_slots, D), _F32),
        compiler_params=_cparams("arbitrary"),
        name="moe_scatter",
    )(pad_lo, pad_hi, dest.reshape(nt, 1, TOP_K * ROW_TILE), xn)


def _expert_kernel(be_ref, nblk_ref, xb_ref, w1_ref, w3_ref, w2_ref, yb_ref, wb1_ref, wb3_ref, wb2_ref):
    i = pl.program_id(0)

    @pl.when(i < nblk_ref[0])
    def _():
        e = be_ref[i]
        e_prev = be_ref[jnp.maximum(i - 1, 0)]

        @pl.when((i == 0) | (e != e_prev))
        def _():
            wb1_ref[...] = w1_ref[0].astype(_BF16)
            wb3_ref[...] = w3_ref[0].astype(_BF16)
            wb2_ref[...] = w2_ref[0].astype(_BF16)

        x = xb_ref[...].astype(_BF16)
        a = _dot(x, wb1_ref[...])
        hid = (a * jax.nn.sigmoid(a) * _dot(x, wb3_ref[...])).astype(_BF16)
        yb_ref[...] = _dot(hid, wb2_ref[...])

    @pl.when(i >= nblk_ref[0])
    def _():
        yb_ref[...] = jnp.zeros_like(yb_ref)


def _moe_experts(xb, block_e, n_blk, w1, w3, w2):
    n_blocks = xb.shape[0] // MOE_BLOCK
    _, D, DE = w1.shape
    live = lambda i, nb: jnp.minimum(i, nb[0] - 1)
    grid_spec = pltpu.PrefetchScalarGridSpec(
        num_scalar_prefetch=2,
        grid=(n_blocks,),
        in_specs=[pl.BlockSpec((MOE_BLOCK, D), lambda i, be, nb: (live(i, nb), 0)),
                  pl.BlockSpec((1, D, DE), lambda i, be, nb: (be[live(i, nb)], 0, 0)),
                  pl.BlockSpec((1, D, DE), lambda i, be, nb: (be[live(i, nb)], 0, 0)),
                  pl.BlockSpec((1, DE, D), lambda i, be, nb: (be[live(i, nb)], 0, 0))],
        out_specs=pl.BlockSpec((MOE_BLOCK, D), lambda i, be, nb: (i, 0)),
        scratch_shapes=[pltpu.VMEM((D, DE), _BF16), pltpu.VMEM((D, DE), _BF16), pltpu.VMEM((DE, D), _BF16)],
    )
    return pl.pallas_call(
        _expert_kernel,
        grid_spec=grid_spec,
        out_shape=jax.ShapeDtypeStruct(xb.shape, _F32),
        compiler_params=_cparams("arbitrary"),
        name="moe_experts",
    )(block_e, n_blk, xb, w1, w3, w2)


def _final_kernel(dest_ref, dest_next_ref, h1_ref, gates_ref, yb_ref, p_ref, wg_ref, wp_ref, gple_ref, gfin_ref,
                  o_ref, ybuf_ref, sem):
    tm = h1_ref.shape[0]
    i = pl.program_id(0)
    slot = lax.rem(i, 2)

    def gather(d_ref, s):
        def start(t, c):
            for k in range(TOP_K):
                pltpu.make_async_copy(yb_ref.at[pl.ds(d_ref[0, 0, TOP_K * t + k], 1)],
                                      ybuf_ref.at[s, k, pl.ds(t, 1)], sem.at[s]).start()
            return c
        lax.fori_loop(0, tm, start, 0, unroll=DMA_UNROLL)

    @pl.when(i == 0)
    def _():
        gather(dest_ref, 0)

    @pl.when(i + 1 < pl.num_programs(0))
    def _():
        gather(dest_next_ref, 1 - slot)

    pltpu.make_async_copy(ybuf_ref.at[slot], ybuf_ref.at[slot], sem.at[slot]).wait()

    gates = gates_ref[...]
    moe = ybuf_ref[slot, 0] * gates[:, 0:1] + ybuf_ref[slot, 1] * gates[:, 1:2]
    h2 = h1_ref[...] + moe
    gate = jax.nn.sigmoid(_dot(_rmsnorm(h2, gple_ref[...]).astype(_BF16), wg_ref[...]))
    h3 = h2 + _dot(p_ref[...].astype(_BF16), wp_ref[...]) * gate
    o_ref[...] = _rmsnorm(h3, gfin_ref[...])


def _final(dest, h1, gates, yb, p2, w_gate, w_proj, g_ple, g_final):
    T, D = h1.shape
    nt = T // ROW_TILE
    PD = p2.shape[1]
    row = lambda n: pl.BlockSpec((ROW_TILE, n), lambda i: (i, 0))
    vec = lambda n: pl.BlockSpec((1, n), lambda i: (0, 0))
    dest3 = dest.reshape(nt, 1, TOP_K * ROW_TILE)
    return pl.pallas_call(
        _final_kernel,
        grid=(nt,),
        in_specs=[pl.BlockSpec((1, 1, TOP_K * ROW_TILE), lambda i: (i, 0, 0), memory_space=pltpu.SMEM),
                  pl.BlockSpec((1, 1, TOP_K * ROW_TILE), lambda i: (jnp.minimum(i + 1, nt - 1), 0, 0),
                               memory_space=pltpu.SMEM),
                  row(D), row(ROUTER_LANES),
                  pl.BlockSpec(memory_space=pl.ANY),
                  row(PD),
                  pl.BlockSpec((D, D), lambda i: (0, 0)),
                  pl.BlockSpec((PD, D), lambda i: (0, 0)),
                  vec(D), vec(D)],
        out_specs=row(D),
        out_shape=jax.ShapeDtypeStruct((T, D), _F32),
        scratch_shapes=[pltpu.VMEM((2, TOP_K, ROW_TILE, D), _F32), pltpu.SemaphoreType.DMA((2,))],
        compiler_params=_cparams("arbitrary"),
        name="moe_gather_ple_final",
    )(dest3, dest3, h1, gates, yb, p2, w_gate.astype(_BF16), w_proj.astype(_BF16),
      g_ple.reshape(1, D), g_final.reshape(1, D))


def _routing_tables(meta_t, counts_f, n_blocks):
    ids = meta_t[_META_ID:_META_ID + TOP_K].T.astype(jnp.int32)
    rank = meta_t[_META_RANK:_META_RANK + TOP_K].T.astype(jnp.int32)
    counts = counts_f[:, 0].astype(jnp.int32)
    padded = (counts + MOE_BLOCK - 1) // MOE_BLOCK * MOE_BLOCK
    e_iota = jnp.arange(N_EXPERTS, dtype=jnp.int32)
    pad_end = jnp.sum(jnp.where(e_iota[None, :] <= e_iota[:, None], padded[None, :], 0), axis=1)
    pad_start = pad_end - padded
    dest = jnp.sum(jnp.where(ids[..., None] == e_iota, pad_start, 0), axis=-1) + rank
    blk_row = jnp.arange(n_blocks, dtype=jnp.int32)[:, None] * MOE_BLOCK
    block_e = jnp.minimum(jnp.sum((pad_end[None, :] <= blk_row).astype(jnp.int32), axis=1), N_EXPERTS - 1)
    n_blk = (pad_end[-1:] // MOE_BLOCK).astype(jnp.int32)
    return dest.astype(jnp.int32), (pad_start + counts).astype(jnp.int32), pad_end.astype(jnp.int32), block_e, n_blk


def _one_layer(h, p, g_mix, w_in, hy_conv_w, hy_conv_b, hy_f_w1, hy_f_b1, hy_f_freq1, hy_f_w2, hy_f_b2,
               hy_f_freq2, hy_f_w3, hy_skip, na_rpb, g_out_hy, g_out_na, w_out, g_ffn, router_wg, router_bg,
               router_we, router_be, exp_w1, exp_w3, exp_w2, g_ple, w_ple_gate, w_ple_proj):
    B, L, D = h.shape
    T = B * L
    assert T % ROW_TILE == 0 and L % FREQ_CHUNK == 0 and w_in.shape[1] == HY_COLS + 3 * NA_WIDTH
    x2 = h.reshape(T, D)
    u_hy, qkv = _in_proj(x2, g_mix, w_in)

    hf = _filter_mlp(L, hy_f_w1, hy_f_b1, hy_f_freq1, hy_f_w2, hy_f_b2, hy_f_freq2, hy_f_w3)
    fmat, fmat_t = _dft_matrix(L)
    coef_a, coef_b, coef_d = _filter_spec(fmat, hf)
    y_hy = _hyena(u_hy.reshape(B, L, HY_COLS), hy_conv_w.astype(_F32), hy_conv_b.astype(_F32),
                  fmat, fmat_t, coef_a, coef_b, coef_d, hy_skip.astype(_F32))
    y_na = _natten(qkv.reshape(B, L, 3 * NA_WIDTH), na_rpb)

    h1, xn, meta, counts = _out_router(y_hy.reshape(T, HY_WIDTH), y_na.reshape(T, NA_WIDTH), g_out_hy, g_out_na,
                                       w_out, x2, g_ffn, router_wg, router_bg, router_we, router_be)
    n_blocks = -(-(T * TOP_K + N_EXPERTS * (MOE_BLOCK - 1)) // MOE_BLOCK)
    dest, pad_lo, pad_hi, block_e, n_blk = _routing_tables(meta, counts, n_blocks)
    xb = _moe_scatter(xn, dest, pad_lo, pad_hi, n_blocks * MOE_BLOCK)
    yb = _moe_experts(xb, block_e, n_blk, exp_w1, exp_w3, exp_w2)
    gates = jnp.pad(meta[_META_GATE:_META_GATE + TOP_K].T, ((0, 0), (0, ROUTER_LANES - TOP_K)))
    return h1, dest, gates, yb


def kernel(x, p, g_mix, w_in, hy_conv_w, hy_conv_b, hy_f_w1, hy_f_b1, hy_f_freq1, hy_f_w2, hy_f_b2, hy_f_freq2, hy_f_w3, hy_skip, na_rpb, g_out_hy, g_out_na, w_out, g_ffn, router_wg, router_bg, router_we, router_be, exp_w1, exp_w3, exp_w2, g_ple, w_ple_gate, w_ple_proj, g_final):
    depth = p.shape[0]
    assert depth == 1, "the final RMSNorm is fused into the last layer's kernel; one layer is supported"
    B, L, D = x.shape
    i = 0
    h1, dest, gates, yb = _one_layer(
        x, p[i], g_mix[i], w_in[i], hy_conv_w[i], hy_conv_b[i], hy_f_w1[i], hy_f_b1[i], hy_f_freq1[i],
        hy_f_w2[i], hy_f_b2[i], hy_f_freq2[i], hy_f_w3[i], hy_skip[i], na_rpb[i], g_out_hy[i], g_out_na[i],
        w_out[i], g_ffn[i], router_wg[i], router_bg[i], router_we[i], router_be[i], exp_w1[i], exp_w3[i],
        exp_w2[i], g_ple[i], w_ple_gate[i], w_ple_proj[i])
    out = _final(dest, h1, gates, yb, p[i].reshape(B * L, -1), w_ple_gate[i], w_ple_proj[i], g_ple[i], g_final)
    return out.reshape(B, L, D)
```

```python
import functools
import math

import numpy as np
import jax
import jax.numpy as jnp
from jax import lax
from jax.experimental import pallas as pl
from jax.experimental.pallas import tpu as pltpu

_F32 = jnp.float32
_BF16 = jnp.bfloat16

GRID_W = 64
HY_WIDTH = 512
NA_WIDTH = 512
NA_HEADS = 8
NA_HEAD_DIM = 64
HY_ORDER = 2
SHORT_CONV = 3
FILTER_EMB = 33
FILTER_BANDS = (FILTER_EMB - 1) // 2
DECAY_TARGET = 1e-2
FAST_DECAY_PCT = 0.3
SLOW_DECAY_PCT = 1.5
WIN_ROWS = 8
WIN_COLS = 16
Q_ROWS = 2
N_GROUPS = 4
EXPERTS_PER_GROUP = 8
N_EXPERTS = N_GROUPS * EXPERTS_PER_GROUP
TOP_K = 2
MOE_BLOCK = 512
EPS = 1e-6
NEG_INF = -1e30
HY_COLS = (HY_ORDER + 1) * HY_WIDTH

V7X_LANES = 128
V7X_SUBLANES = 8
V7X_VMEM_LIMIT_BYTES = 56 * 2 ** 20

ROW_TILE = 512
FREQ_CHUNK = 512
HY_CH_TILE = 256
ROUTER_LANES = V7X_LANES
DMA_UNROLL = 8


def _cparams(*sem):
    return pltpu.CompilerParams(dimension_semantics=sem, vmem_limit_bytes=V7X_VMEM_LIMIT_BYTES)


def _resident(shape, index_map):
    return pl.BlockSpec(shape, index_map, pipeline_mode=pl.Buffered(1))


def _rmsnorm(x, g):
    return x * lax.rsqrt(jnp.mean(x * x, axis=-1, keepdims=True) + EPS) * g


def _dot(a, b):
    return jnp.dot(a, b, preferred_element_type=_F32)


def _dot_f32(a, b):
    return jnp.dot(a, b, preferred_element_type=_F32, precision=lax.Precision.HIGHEST)


def _in_proj_kernel(x_ref, g_ref, w_ref, uhy_ref, qkv_ref):
    xn = _rmsnorm(x_ref[...], g_ref[...]).astype(_BF16)
    n_hy = uhy_ref.shape[1]
    for c0 in range(0, n_hy, 512):
        uhy_ref[:, c0:c0 + 512] = _dot(xn, w_ref[:, c0:c0 + 512]).astype(_BF16)
    for c0 in range(0, qkv_ref.shape[1], 512):
        qkv_ref[:, c0:c0 + 512] = _dot(xn, w_ref[:, n_hy + c0:n_hy + c0 + 512]).astype(_BF16)


def _in_proj(x2, g_mix, w_in):
    T, D = x2.shape
    n_in = w_in.shape[1]
    n_qkv = n_in - HY_COLS
    return pl.pallas_call(
        _in_proj_kernel,
        grid=(T // ROW_TILE,),
        in_specs=[pl.BlockSpec((ROW_TILE, D), lambda i: (i, 0)),
                  pl.BlockSpec((1, D), lambda i: (0, 0)),
                  _resident((D, n_in), lambda i: (0, 0))],
        out_specs=[pl.BlockSpec((ROW_TILE, HY_COLS), lambda i: (i, 0)),
                   pl.BlockSpec((ROW_TILE, n_qkv), lambda i: (i, 0))],
        out_shape=[jax.ShapeDtypeStruct((T, HY_COLS), _BF16),
                   jax.ShapeDtypeStruct((T, n_qkv), _BF16)],
        compiler_params=_cparams("arbitrary"),
        name="in_proj",
    )(x2, g_mix.reshape(1, D), w_in.astype(_BF16))


def _filter_mlp_kernel(z_ref, w1_ref, b1_ref, f1_ref, w2_ref, b2_ref, f2_ref, w3_ref, delta_ref, hf_ref):
    z = z_ref[...]
    hid = jnp.sin(f1_ref[...] * (_dot_f32(z, w1_ref[...]) + b1_ref[...]))
    hid = jnp.sin(f2_ref[...] * (_dot_f32(hid, w2_ref[...]) + b2_ref[...]))
    hf = _dot_f32(hid, w3_ref[...])
    decay = jnp.exp(-z[:, 0:1] * delta_ref[...])
    tl = z.shape[0]
    row = lax.broadcasted_iota(jnp.int32, (tl, HY_WIDTH), 0) + pl.program_id(0) * tl
    for k in range(2 * HY_ORDER):
        blk = hf[:, k * HY_WIDTH:(k + 1) * HY_WIDTH] * decay
        if k >= HY_ORDER:
            blk = jnp.where(row == 0, 0.0, blk)
        hf_ref[:, k * HY_WIDTH:(k + 1) * HY_WIDTH] = blk


def _filter_mlp(L, w1, b1, f1, w2, b2, f2, w3):
    t = jnp.linspace(0.0, 1.0, L, dtype=_F32)[:, None]
    w = 2.0 * math.pi * jnp.arange(L, dtype=_F32)[:, None] / L
    bands = jnp.linspace(1e-4, FILTER_BANDS - 1, FILTER_BANDS, dtype=_F32)[None, :]
    z = jnp.concatenate([t, jnp.cos(bands * w), -jnp.sin(bands * w)], axis=-1)
    z = jnp.pad(z, ((0, 0), (0, V7X_LANES - FILTER_EMB)))
    w1p = jnp.pad(w1.astype(_F32), ((0, V7X_LANES - FILTER_EMB), (0, 0)))
    max_decay = math.log(DECAY_TARGET) / FAST_DECAY_PCT
    min_decay = math.log(DECAY_TARGET) / SLOW_DECAY_PCT
    deltas = jnp.abs(jnp.linspace(min_decay, max_decay, HY_WIDTH, dtype=_F32))[None, :]
    hid = w1.shape[1]
    n_out = w3.shape[1]
    tl = min(L, ROW_TILE)
    full = lambda shape: pl.BlockSpec(shape, lambda i: (0, 0))
    return pl.pallas_call(
        _filter_mlp_kernel,
        grid=(L // tl,),
        in_specs=[pl.BlockSpec((tl, V7X_LANES), lambda i: (i, 0)),
                  full((V7X_LANES, hid)), full((1, hid)), full((1, hid)),
                  full((hid, hid)), full((1, hid)), full((1, hid)),
                  full((hid, n_out)), full((1, HY_WIDTH))],
        out_specs=pl.BlockSpec((tl, n_out), lambda i: (i, 0)),
        out_shape=jax.ShapeDtypeStruct((L, n_out), _F32),
        compiler_params=_cparams("arbitrary"),
        name="hyena_filter_mlp",
    )(z, w1p, b1.reshape(1, hid), f1.reshape(1, hid), w2.astype(_F32), b2.reshape(1, hid),
      f2.reshape(1, hid), w3.astype(_F32), deltas)


def _dft_matrix(L):
    n = 2 * L
    s = 64
    assert L % s == 0
    t = jnp.arange(L, dtype=jnp.int32)[None, :]
    ang = lambda f: ((f[:, None] * t) % n).astype(_F32) * (2.0 * math.pi / n)
    a_hi = ang(jnp.arange(L // s, dtype=jnp.int32) * s)
    a_lo = ang(jnp.arange(s, dtype=jnp.int32))
    c_hi, s_hi, c_lo, s_lo = jnp.cos(a_hi), jnp.sin(a_hi), jnp.cos(a_lo), jnp.sin(a_lo)
    alt = jnp.where(t % 2 == 0, 1.0, -1.0).astype(_F32)
    first = lax.broadcasted_iota(jnp.int32, (L, 1), 0) == 0

    top = (c_hi[:, None, :] * c_lo[None] - s_hi[:, None, :] * s_lo[None]).reshape(L, L)
    bot = -(s_hi[:, None, :] * c_lo[None] + c_hi[:, None, :] * s_lo[None]).reshape(L, L)
    fmat = jnp.concatenate([top, jnp.where(first, alt, bot)], axis=0)

    c_hi, s_hi, c_lo, s_lo = c_hi.T, s_hi.T, c_lo.T, s_lo.T
    top_t = (c_hi[:, :, None] * c_lo[:, None, :] - s_hi[:, :, None] * s_lo[:, None, :]).reshape(L, L)
    bot_t = -(s_hi[:, :, None] * c_lo[:, None, :] + c_hi[:, :, None] * s_lo[:, None, :]).reshape(L, L)
    fmat_t = jnp.concatenate([top_t, jnp.where(first.T, alt.T, bot_t)], axis=1)
    return fmat.astype(_BF16), fmat_t.astype(_BF16)


def _filter_spec_kernel(f_ref, xf_ref, xb_ref, a_ref, b_ref, d_ref):
    L = xf_ref.shape[0]
    n = 2 * L
    ct = xf_ref.shape[1]
    x = jnp.concatenate([xf_ref[...], xb_ref[...]], axis=1).astype(_BF16)
    for r0 in range(0, L, FREQ_CHUNK):
        top = _dot(f_ref[r0:r0 + FREQ_CHUNK, :], x)
        bot = _dot(f_ref[L + r0:L + r0 + FREQ_CHUNK, :], x)
        kre = top[:, :ct] + top[:, ct:]
        kim = bot[:, :ct] - bot[:, ct:]
        knyq = bot[:, :ct] + bot[:, ct:]
        if r0 == 0:
            row0 = lax.broadcasted_iota(jnp.int32, kre.shape, 0) == 0
            a = jnp.where(row0, kre * (1.0 / n), kre * (2.0 / n))
            b = jnp.where(row0, 0.0, kim * (2.0 / n))
            d = jnp.where(row0, knyq * (1.0 / n), kre * (2.0 / n))
        else:
            a = kre * (2.0 / n)
            b = kim * (2.0 / n)
            d = a
        a_ref[0, r0:r0 + FREQ_CHUNK, :] = a
        b_ref[0, r0:r0 + FREQ_CHUNK, :] = b
        d_ref[0, r0:r0 + FREQ_CHUNK, :] = d


def _filter_spec(fmat, hf):
    L = hf.shape[0]
    n_ct = HY_WIDTH // HY_CH_TILE
    out = jax.ShapeDtypeStruct((HY_ORDER, L, HY_WIDTH), _F32)
    ospec = pl.BlockSpec((1, L, HY_CH_TILE), lambda j: (j // n_ct, 0, j % n_ct))
    return pl.pallas_call(
        _filter_spec_kernel,
        grid=(HY_ORDER * n_ct,),
        in_specs=[_resident((2 * L, L), lambda j: (0, 0)),
                  pl.BlockSpec((L, HY_CH_TILE), lambda j: (0, j)),
                  pl.BlockSpec((L, HY_CH_TILE), lambda j: (0, HY_ORDER * n_ct + j))],
        out_specs=[ospec, ospec, ospec],
        out_shape=[out, out, out],
        compiler_params=_cparams("arbitrary"),
        name="hyena_filter_spectrum",
    )(fmat, hf, hf)


def _short_conv(z, w_ref, b_ref):
    L = z.shape[0]
    row = lax.broadcasted_iota(jnp.int32, z.shape, 0)
    zm = jnp.where(row == 0, 0.0, pltpu.roll(z, 1, axis=0))
    zp = jnp.where(row == L - 1, 0.0, pltpu.roll(z, L - 1, axis=0))
    return ((b_ref[...] + zm * w_ref[0:1, :]) + z * w_ref[1:2, :]) + zp * w_ref[2:3, :]


def _hy_fwd_kernel(z_ref, cw_ref, cb_ref, f_ref, a_ref, b_ref, d_ref, y_ref, *, conv_input):
    z = z_ref[0].astype(_F32)
    if conv_input:
        z = _short_conv(z, cw_ref, cb_ref)
    zb = z.astype(_BF16)
    L = z.shape[0]
    for r0 in range(0, L, FREQ_CHUNK):
        top = _dot(f_ref[r0:r0 + FREQ_CHUNK, :], zb)
        bot = _dot(f_ref[L + r0:L + r0 + FREQ_CHUNK, :], zb)
        a = a_ref[0, r0:r0 + FREQ_CHUNK, :]
        b = b_ref[0, r0:r0 + FREQ_CHUNK, :]
        d = d_ref[0, r0:r0 + FREQ_CHUNK, :]
        y_ref[0, r0:r0 + FREQ_CHUNK, :] = (top * a - bot * b).astype(_BF16)
        y_ref[0, L + r0:L + r0 + FREQ_CHUNK, :] = (top * b + bot * d).astype(_BF16)


def _hy_inv_kernel(y_ref, ft_ref, gate_ref, gw_ref, gb_ref, zp_ref, zw_ref, zb_ref, skip_ref, o_ref, *, conv_prev):
    gate = _short_conv(gate_ref[0].astype(_F32), gw_ref, gb_ref)
    zprev = zp_ref[0].astype(_F32)
    if conv_prev:
        zprev = _short_conv(zprev, zw_ref, zb_ref)
    res = gate * (zprev * skip_ref[0])
    L = gate.shape[0]
    y = y_ref[0]
    for r0 in range(0, L, FREQ_CHUNK):
        conv = _dot(ft_ref[r0:r0 + FREQ_CHUNK, :], y)
        o_ref[0, r0:r0 + FREQ_CHUNK, :] = gate[r0:r0 + FREQ_CHUNK] * conv + res[r0:r0 + FREQ_CHUNK]


def _hyena(u_hy, conv_w, conv_b, fmat, fmat_t, coef_a, coef_b, coef_d, skip):
    B, L, _ = u_hy.shape
    n_ct = HY_WIDTH // HY_CH_TILE
    grid = (n_ct, B)
    conv_b2 = conv_b.reshape(1, HY_COLS)
    col = lambda blk0: pl.BlockSpec((1, L, HY_CH_TILE), lambda c, b: (b, 0, blk0 + c))
    cw = lambda blk0: pl.BlockSpec((SHORT_CONV, HY_CH_TILE), lambda c, b: (0, blk0 + c))
    cb = lambda blk0: pl.BlockSpec((1, HY_CH_TILE), lambda c, b: (0, blk0 + c))
    z = u_hy
    z_blk0 = 2 * n_ct
    for o in range(HY_ORDER):
        first = o == 0
        coef = lambda: pl.BlockSpec((1, L, HY_CH_TILE), lambda c, b, o=o: (o, 0, c),
                                    pipeline_mode=pl.Buffered(1))
        y = pl.pallas_call(
            functools.partial(_hy_fwd_kernel, conv_input=first),
            grid=grid,
            in_specs=[col(z_blk0), cw(2 * n_ct), cb(2 * n_ct),
                      _resident((2 * L, L), lambda c, b: (0, 0)),
                      coef(), coef(), coef()],
            out_specs=pl.BlockSpec((1, 2 * L, HY_CH_TILE), lambda c, b: (b, 0, c)),
            out_shape=jax.ShapeDtypeStruct((B, 2 * L, HY_WIDTH), _BF16),
            compiler_params=_cparams("arbitrary", "arbitrary"),
            name=f"hyena_fwd_dft_{o}",
        )(z, conv_w, conv_b2, fmat, coef_a, coef_b, coef_d)
        z = pl.pallas_call(
            functools.partial(_hy_inv_kernel, conv_prev=first),
            grid=grid,
            in_specs=[pl.BlockSpec((1, 2 * L, HY_CH_TILE), lambda c, b: (b, 0, c)),
                      _resident((L, 2 * L), lambda c, b: (0, 0)),
                      col(o * n_ct), cw(o * n_ct), cb(o * n_ct),
                      col(z_blk0), cw(2 * n_ct), cb(2 * n_ct),
                      pl.BlockSpec((1, 1, HY_CH_TILE), lambda c, b, o=o: (o, 0, c))],
            out_specs=pl.BlockSpec((1, L, HY_CH_TILE), lambda c, b: (b, 0, c)),
            out_shape=jax.ShapeDtypeStruct((B, L, HY_WIDTH), _F32),
            compiler_params=_cparams("arbitrary", "arbitrary"),
            name=f"hyena_inv_dft_{o}",
        )(y, fmat_t, u_hy, conv_w, conv_b2, z, conv_w, conv_b2, skip.reshape(HY_ORDER, 1, HY_WIDTH))
        z_blk0 = 0
    return z


def _natten_tables(rows):
    kr = min(WIN_ROWS, rows)
    krb = min(Q_ROWS + kr - 1, rows)
    rs = np.clip(np.arange(rows) - kr // 2, 0, rows - kr)
    cs = np.clip(np.arange(GRID_W) - WIN_COLS // 2, 0, GRID_W - WIN_COLS)
    qc = np.tile(np.arange(GRID_W), Q_ROWS)[:, None]
    kc = np.tile(np.arange(GRID_W), krb)[None, :]
    cases, case_of, kstart = [], [], []
    for p in range(rows // Q_ROWS):
        k_r0 = min(rs[p * Q_ROWS], rows - krb)
        qr = (p * Q_ROWS + np.repeat(np.arange(Q_ROWS), GRID_W))[:, None]
        kr_ = (k_r0 + np.repeat(np.arange(krb), GRID_W))[None, :]
        valid = ((kr_ >= rs[qr]) & (kr_ < rs[qr] + kr) & (kc >= cs[qc]) & (kc < cs[qc] + WIN_COLS))
        dr = np.clip(kr_ - qr + WIN_ROWS - 1, 0, 2 * WIN_ROWS - 2)
        dc = np.clip(kc - qc + WIN_COLS - 1, 0, 2 * WIN_COLS - 2)
        idx = np.where(valid, dr * (2 * WIN_COLS - 1) + dc, -1).astype(np.int32)
        for ci, c in enumerate(cases):
            if np.array_equal(c, idx):
                break
        else:
            ci = len(cases)
            cases.append(idx)
        case_of.append(ci)
        kstart.append(k_r0 * GRID_W)
    return np.stack(cases), np.asarray(case_of, np.int32), np.asarray(kstart, np.int32), krb * GRID_W


def _natten_bias(rpb, cases):
    n_case, nq, nk = cases.shape
    krb = nk // GRID_W
    n_dr, n_dc = 2 * WIN_ROWS - 1, 2 * WIN_COLS - 1
    c5 = cases.reshape(n_case, Q_ROWS, GRID_W, krb, GRID_W)
    dr_blk = np.where(c5 >= 0, c5 // n_dc, -1).max(axis=(2, 4))
    dc_idx = np.clip(np.arange(GRID_W)[None, :] - np.arange(GRID_W)[:, None] + WIN_COLS - 1, 0, n_dc - 1)
    assert np.all((c5 < 0) | (c5 // n_dc == dr_blk[:, :, None, :, None]))
    assert np.all((c5 < 0) | (c5 % n_dc == dc_idx[None, None, :, None, :]))
    onehot_c = (dc_idx.reshape(1, -1) == np.arange(n_dc)[:, None]).astype(np.float32)
    sel = (np.maximum(dr_blk, 0).reshape(-1, 1) == np.arange(n_dr)[None, :]).astype(np.float32)
    hi = lax.Precision.HIGHEST
    toe = jnp.einsum('hdj,jx->hdx', rpb.astype(_F32), onehot_c, precision=hi)
    blk = jnp.einsum('sd,hdx->hsx', sel, toe, precision=hi)
    blk = blk.reshape(NA_HEADS, n_case, Q_ROWS, krb, GRID_W, GRID_W).transpose(1, 0, 2, 4, 3, 5)
    return jnp.where((cases >= 0)[:, None], blk.reshape(n_case, NA_HEADS, nq, nk), NEG_INF)


def _natten_kernel(case_ref, kstart_ref, qkv_ref, bias_ref, o_ref, *, n_pairs, n_keys):
    nq = Q_ROWS * GRID_W
    pair_w = 2 * NA_HEAD_DIM
    lane = lax.broadcasted_iota(jnp.int32, (nq, pair_w), 1)
    lo_half = lane < NA_HEAD_DIM
    scale = NA_HEAD_DIM ** -0.5

    def body(p, carry):
        q0 = pl.multiple_of(p * nq, nq)
        k0 = pl.multiple_of(kstart_ref[p], GRID_W)
        case = case_ref[p]
        for hp in range(NA_HEADS // 2):
            c0 = hp * pair_w
            q2 = qkv_ref[0, pl.ds(q0, nq), c0:c0 + pair_w]
            k2 = qkv_ref[0, pl.ds(k0, n_keys), NA_WIDTH + c0:NA_WIDTH + c0 + pair_w]
            v2 = qkv_ref[0, pl.ds(k0, n_keys), 2 * NA_WIDTH + c0:2 * NA_WIDTH + c0 + pair_w]
            zero = jnp.zeros_like(q2)
            q2 = q2 * scale
            qq = jnp.concatenate([jnp.where(lo_half, q2, zero), jnp.where(lo_half, zero, q2)], axis=0)
            s = lax.dot_general(qq, k2, (((1,), (1,)), ((), ())), preferred_element_type=_F32)
            s = s + bias_ref[case, hp]
            e = jnp.exp(s - jnp.max(s, axis=-1, keepdims=True))
            l = jnp.sum(e, axis=-1, keepdims=True)
            o = _dot(e.astype(_BF16), v2) / l
            o_ref[0, pl.ds(q0, nq), c0:c0 + pair_w] = jnp.where(lo_half, o[:nq], o[nq:])
        return carry

    lax.fori_loop(0, n_pairs, body, 0, unroll=2)


def _natten(qkv, rpb):
    B, L, _ = qkv.shape
    rows = L // GRID_W
    assert rows % Q_ROWS == 0 and rows >= Q_ROWS + WIN_ROWS - 1
    cases, case_of, kstart, n_keys = _natten_tables(rows)
    n_case = cases.shape[0]
    nq = Q_ROWS * GRID_W
    bias = _natten_bias(rpb, cases).reshape(n_case, NA_HEADS // 2, 2 * nq, n_keys)
    grid_spec = pltpu.PrefetchScalarGridSpec(
        num_scalar_prefetch=2,
        grid=(B,),
        in_specs=[pl.BlockSpec((1, L, 3 * NA_WIDTH), lambda b, *_: (b, 0, 0)),
                  _resident((n_case, NA_HEADS // 2, 2 * nq, n_keys), lambda b, *_: (0, 0, 0, 0))],
        out_specs=pl.BlockSpec((1, L, NA_WIDTH), lambda b, *_: (b, 0, 0)),
    )
    return pl.pallas_call(
        functools.partial(_natten_kernel, n_pairs=rows // Q_ROWS, n_keys=n_keys),
        grid_spec=grid_spec,
        out_shape=jax.ShapeDtypeStruct((B, L, NA_WIDTH), _F32),
        compiler_params=_cparams("arbitrary"),
        name="natten",
    )(jnp.asarray(case_of), jnp.asarray(kstart), qkv, bias)


_META_ID, _META_GATE, _META_RANK = 0, 2, 4
ROUTER_ROWS = 8 + N_EXPERTS


def _split_bf16(v):
    hi = v.astype(_BF16)
    return hi, (v - hi.astype(_F32)).astype(_BF16)


def _out_router_kernel(yhy_ref, yna_ref, ghy_ref, gna_ref, wtop_ref, wbot_ref, x_ref, gffn_ref, rwh_ref, rwl_ref,
                       rb_ref, h1_ref, xn_ref, meta_ref, cnt_ref, carry_ref):
    @pl.when(pl.program_id(0) == 0)
    def _():
        carry_ref[...] = jnp.zeros_like(carry_ref)

    nh = _rmsnorm(yhy_ref[...], ghy_ref[...]).astype(_BF16)
    nn = _rmsnorm(yna_ref[...], gna_ref[...]).astype(_BF16)
    h1 = x_ref[...] + (_dot(nh, wtop_ref[...]) + _dot(nn, wbot_ref[...]))
    h1_ref[...] = h1
    xn = _rmsnorm(h1, gffn_ref[...])
    xn_ref[...] = xn
    tm = xn.shape[0]

    xh, xl = _split_bf16(xn)
    logits = _dot(xh, rwh_ref[...]) + (_dot(xh, rwl_ref[...]) + _dot(xl, rwh_ref[...]))
    lt = logits.T[:ROUTER_ROWS] + rb_ref[:, 0:1]

    neg = -jnp.inf
    row8 = lax.broadcasted_iota(jnp.int32, (EXPERTS_PER_GROUP, tm), 0).astype(_F32)
    col_max = lambda v: jnp.max(v, axis=0, keepdims=True)
    first_max = lambda v, m: jnp.min(jnp.where(v == m, row8, float(EXPERTS_PER_GROUP)), axis=0, keepdims=True)

    gl = jnp.where(row8 < N_GROUPS, lt[0:8], neg)
    gmax = col_max(gl)
    g_w = 1.0 / jnp.sum(jnp.exp(gl - gmax), axis=0, keepdims=True)
    g_idx = first_max(gl, gmax)
    el = lt[8:8 + EXPERTS_PER_GROUP]
    for g in range(1, N_GROUPS):
        el = jnp.where(g_idx == g, lt[8 + g * EXPERTS_PER_GROUP:8 + (g + 1) * EXPERTS_PER_GROUP], el)
    m1 = col_max(el)
    i1 = first_max(el, m1)
    el2 = jnp.where(row8 == i1, neg, el)
    m2 = col_max(el2)
    i2 = first_max(el2, m2)
    r = jnp.exp(m2 - m1)
    gate1 = g_w / (1.0 + r)
    gate2 = g_w * r / (1.0 + r)
    id1 = g_idx * EXPERTS_PER_GROUP + i1
    id2 = g_idx * EXPERTS_PER_GROUP + i2

    row_e = lax.broadcasted_iota(jnp.int32, (N_EXPERTS, tm), 0).astype(_F32)
    sel1 = row_e == id1
    sel2 = row_e == id2
    onehot = jnp.where(sel1 | sel2, 1.0, 0.0)
    tri_r = lax.broadcasted_iota(jnp.int32, (tm, tm), 0)
    tri_c = lax.broadcasted_iota(jnp.int32, (tm, tm), 1)
    tri = jnp.where(tri_r < tri_c, 1.0, 0.0).astype(_BF16)
    carry = carry_ref[:, 0:1]
    before = carry + _dot(onehot.astype(_BF16), tri)
    rank1 = jnp.sum(jnp.where(sel1, before, 0.0), axis=0, keepdims=True)
    rank2 = jnp.sum(jnp.where(sel2, before, 0.0), axis=0, keepdims=True)
    new_carry = carry + jnp.sum(onehot, axis=1, keepdims=True)
    carry_ref[...] = jnp.broadcast_to(new_carry, carry_ref.shape)
    cnt_ref[...] = jnp.broadcast_to(new_carry, cnt_ref.shape)

    meta = jnp.zeros((8, tm), _F32)
    for k, v in enumerate((id1, id2, gate1, gate2, rank1, rank2)):
        meta = jnp.where(row8 == k, v, meta)
    meta_ref[...] = meta


def _out_router(y_hy, y_na, g_hy, g_na, w_out, x2, g_ffn, wg, bg, we, be):
    T, D = x2.shape
    gpad = 8 - N_GROUPS
    rw = jnp.concatenate([wg.astype(_F32), jnp.zeros((D, gpad), _F32), we.astype(_F32)], axis=1)
    rw_hi, rw_lo = _split_bf16(jnp.pad(rw, ((0, 0), (0, ROUTER_LANES - ROUTER_ROWS))))
    rb = jnp.concatenate([bg.astype(_F32), jnp.zeros((gpad,), _F32), be.astype(_F32)])
    rb = jnp.broadcast_to(rb[:, None], (ROUTER_ROWS, ROUTER_LANES))
    w_bf = w_out.astype(_BF16)
    row = lambda n: pl.BlockSpec((ROW_TILE, n), lambda i: (i, 0))
    vec = lambda n: pl.BlockSpec((1, n), lambda i: (0, 0))
    mat = lambda r, c: pl.BlockSpec((r, c), lambda i: (0, 0))
    return pl.pallas_call(
        _out_router_kernel,
        grid=(T // ROW_TILE,),
        in_specs=[row(HY_WIDTH), row(NA_WIDTH), vec(HY_WIDTH), vec(NA_WIDTH),
                  mat(HY_WIDTH, D), mat(NA_WIDTH, D), row(D), vec(D),
                  mat(D, ROUTER_LANES), mat(D, ROUTER_LANES), mat(ROUTER_ROWS, ROUTER_LANES)],
        out_specs=[row(D),
                   row(D),
                   pl.BlockSpec((8, ROW_TILE), lambda i: (0, i)),
                   mat(N_EXPERTS, ROUTER_LANES)],
        out_shape=[jax.ShapeDtypeStruct((T, D), _F32),
                   jax.ShapeDtypeStruct((T, D), _F32),
                   jax.ShapeDtypeStruct((8, T), _F32),
                   jax.ShapeDtypeStruct((N_EXPERTS, ROUTER_LANES), _F32)],
        scratch_shapes=[pltpu.VMEM((N_EXPERTS, ROUTER_LANES), _F32)],
        compiler_params=_cparams("arbitrary"),
        name="out_proj_router",
    )(y_hy, y_na, g_hy.reshape(1, -1), g_na.reshape(1, -1), w_bf[:HY_WIDTH], w_bf[HY_WIDTH:],
      x2, g_ffn.reshape(1, D), rw_hi, rw_lo, rb)


def _scatter_kernel(zblk_ref, dest_ref, xn_ref, xb_ref, zero_ref, sem, zero_sem):
    tm = xn_ref.shape[0]
    row_copy = lambda src, d: pltpu.make_async_copy(src, xb_ref.at[pl.ds(d, 1)], sem)

    @pl.when(pl.program_id(0) == 0)
    def _():
        zero_ref[...] = jnp.zeros_like(zero_ref)

        def blk_copy(j):
            start = pl.multiple_of(zblk_ref[j] * MOE_BLOCK, MOE_BLOCK)
            return pltpu.make_async_copy(zero_ref, xb_ref.at[pl.ds(start, MOE_BLOCK)], zero_sem)

        def start(j, c):
            @pl.when(zblk_ref[j] >= 0)
            def _():
                blk_copy(j).start()
            return c

        def wait(j, c):
            @pl.when(zblk_ref[j] >= 0)
            def _():
                blk_copy(j).wait()
            return c

        lax.fori_loop(0, zblk_ref.shape[0], start, 0)
        lax.fori_loop(0, zblk_ref.shape[0], wait, 0)

    def start(t, c):
        for k in range(TOP_K):
            row_copy(xn_ref.at[pl.ds(t, 1)], dest_ref[0, 0, TOP_K * t + k]).start()
        return c

    lax.fori_loop(0, tm, start, 0, unroll=DMA_UNROLL)
    all_rows = xb_ref.at[pl.ds(0, TOP_K * tm)]
    pltpu.make_async_copy(all_rows, all_rows, sem).wait()


def _moe_scatter(xn, dest, zero_blocks, n_slots):
    T, D = xn.shape
    nt = T // ROW_TILE
    grid_spec = pltpu.PrefetchScalarGridSpec(
        num_scalar_prefetch=1,
        grid=(nt,),
        in_specs=[pl.BlockSpec((1, 1, TOP_K * ROW_TILE), lambda i, *_: (i, 0, 0), memory_space=pltpu.SMEM),
                  pl.BlockSpec((ROW_TILE, D), lambda i, *_: (i, 0))],
        out_specs=pl.BlockSpec(memory_space=pl.ANY),
        scratch_shapes=[pltpu.VMEM((MOE_BLOCK, D), _F32), pltpu.SemaphoreType.DMA(()),
                        pltpu.SemaphoreType.DMA(())],
    )
    return pl.pallas_call(
        _scatter_kernel,
        grid_spec=grid_spec,
        out_shape=jax.ShapeDtypeStruct((n_slots, D), _F32),
        compiler_params=_cparams("arbitrary"),
        name="moe_scatter",
    )(zero_blocks, dest.reshape(nt, 1, TOP_K * ROW_TILE), xn)


def _expert_kernel(be_ref, nblk_ref, xb_ref, w1_ref, w3_ref, w2_ref, yb_ref, wb1_ref, wb3_ref, wb2_ref):
    i = pl.program_id(0)

    @pl.when(i < nblk_ref[0])
    def _():
        e = be_ref[i]
        e_prev = be_ref[jnp.maximum(i - 1, 0)]

        @pl.when((i == 0) | (e != e_prev))
        def _():
            wb1_ref[...] = w1_ref[0].astype(_BF16)
            wb3_ref[...] = w3_ref[0].astype(_BF16)
            wb2_ref[...] = w2_ref[0].astype(_BF16)

        x = xb_ref[...].astype(_BF16)
        a = _dot(x, wb1_ref[...])
        hid = (a * jax.nn.sigmoid(a) * _dot(x, wb3_ref[...])).astype(_BF16)
        yb_ref[...] = _dot(hid, wb2_ref[...])

    @pl.when(i >= nblk_ref[0])
    def _():
        yb_ref[...] = jnp.zeros_like(yb_ref)


def _moe_experts(xb, block_e, n_blk, w1, w3, w2):
    n_blocks = xb.shape[0] // MOE_BLOCK
    _, D, DE = w1.shape
    live = lambda i, nb: jnp.minimum(i, nb[0] - 1)
    grid_spec = pltpu.PrefetchScalarGridSpec(
        num_scalar_prefetch=2,
        grid=(n_blocks,),
        in_specs=[pl.BlockSpec((MOE_BLOCK, D), lambda i, be, nb: (live(i, nb), 0)),
                  pl.BlockSpec((1, D, DE), lambda i, be, nb: (be[live(i, nb)], 0, 0)),
                  pl.BlockSpec((1, D, DE), lambda i, be, nb: (be[live(i, nb)], 0, 0)),
                  pl.BlockSpec((1, DE, D), lambda i, be, nb: (be[live(i, nb)], 0, 0))],
        out_specs=pl.BlockSpec((MOE_BLOCK, D), lambda i, be, nb: (i, 0)),
        scratch_shapes=[pltpu.VMEM((D, DE), _BF16), pltpu.VMEM((D, DE), _BF16), pltpu.VMEM((DE, D), _BF16)],
    )
    return pl.pallas_call(
        _expert_kernel,
        grid_spec=grid_spec,
        out_shape=jax.ShapeDtypeStruct(xb.shape, _F32),
        compiler_params=_cparams("arbitrary"),
        name="moe_experts",
    )(block_e, n_blk, xb, w1, w3, w2)


def _final_kernel(dest_ref, dest_next_ref, h1_ref, gates_ref, yb_ref, p_ref, wg_ref, wp_ref, gple_ref, gfin_ref,
                  o_ref, ybuf_ref, sem):
    tm = h1_ref.shape[0]
    i = pl.program_id(0)
    slot = lax.rem(i, 2)

    def gather(d_ref, s):
        def start(t, c):
            for k in range(TOP_K):
                pltpu.make_async_copy(yb_ref.at[pl.ds(d_ref[0, 0, TOP_K * t + k], 1)],
                                      ybuf_ref.at[s, k, pl.ds(t, 1)], sem.at[s]).start()
            return c
        lax.fori_loop(0, tm, start, 0, unroll=DMA_UNROLL)

    @pl.when(i == 0)
    def _():
        gather(dest_ref, 0)

    @pl.when(i + 1 < pl.num_programs(0))
    def _():
        gather(dest_next_ref, 1 - slot)

    pltpu.make_async_copy(ybuf_ref.at[slot], ybuf_ref.at[slot], sem.at[slot]).wait()

    gates = gates_ref[...]
    moe = ybuf_ref[slot, 0] * gates[:, 0:1] + ybuf_ref[slot, 1] * gates[:, 1:2]
    h2 = h1_ref[...] + moe
    gate = jax.nn.sigmoid(_dot(_rmsnorm(h2, gple_ref[...]).astype(_BF16), wg_ref[...]))
    h3 = h2 + _dot(p_ref[...].astype(_BF16), wp_ref[...]) * gate
    o_ref[...] = _rmsnorm(h3, gfin_ref[...])


def _final(dest, h1, gates, yb, p2, w_gate, w_proj, g_ple, g_final):
    T, D = h1.shape
    nt = T // ROW_TILE
    PD = p2.shape[1]
    row = lambda n: pl.BlockSpec((ROW_TILE, n), lambda i: (i, 0))
    vec = lambda n: pl.BlockSpec((1, n), lambda i: (0, 0))
    dest3 = dest.reshape(nt, 1, TOP_K * ROW_TILE)
    return pl.pallas_call(
        _final_kernel,
        grid=(nt,),
        in_specs=[pl.BlockSpec((1, 1, TOP_K * ROW_TILE), lambda i: (i, 0, 0), memory_space=pltpu.SMEM),
                  pl.BlockSpec((1, 1, TOP_K * ROW_TILE), lambda i: (jnp.minimum(i + 1, nt - 1), 0, 0),
                               memory_space=pltpu.SMEM),
                  row(D), row(ROUTER_LANES),
                  pl.BlockSpec(memory_space=pl.ANY),
                  row(PD),
                  pl.BlockSpec((D, D), lambda i: (0, 0)),
                  pl.BlockSpec((PD, D), lambda i: (0, 0)),
                  vec(D), vec(D)],
        out_specs=row(D),
        out_shape=jax.ShapeDtypeStruct((T, D), _F32),
        scratch_shapes=[pltpu.VMEM((2, TOP_K, ROW_TILE, D), _F32), pltpu.SemaphoreType.DMA((2,))],
        compiler_params=_cparams("arbitrary"),
        name="moe_gather_ple_final",
    )(dest3, dest3, h1, gates, yb, p2, w_gate.astype(_BF16), w_proj.astype(_BF16),
      g_ple.reshape(1, D), g_final.reshape(1, D))


def _routing_tables(meta_t, counts_f, n_blocks):
    ids = meta_t[_META_ID:_META_ID + TOP_K].T.astype(jnp.int32)
    rank = meta_t[_META_RANK:_META_RANK + TOP_K].T.astype(jnp.int32)
    counts = counts_f[:, 0].astype(jnp.int32)
    padded = (counts + MOE_BLOCK - 1) // MOE_BLOCK * MOE_BLOCK
    e_iota = jnp.arange(N_EXPERTS, dtype=jnp.int32)
    pad_end = jnp.sum(jnp.where(e_iota[None, :] <= e_iota[:, None], padded[None, :], 0), axis=1)
    pad_start = pad_end - padded
    dest = jnp.sum(jnp.where(ids[..., None] == e_iota, pad_start, 0), axis=-1) + rank
    blk_row = jnp.arange(n_blocks, dtype=jnp.int32)[:, None] * MOE_BLOCK
    block_e = jnp.minimum(jnp.sum((pad_end[None, :] <= blk_row).astype(jnp.int32), axis=1), N_EXPERTS - 1)
    n_blk = (pad_end[-1:] // MOE_BLOCK).astype(jnp.int32)
    seg_last = jnp.where(padded > counts, pad_end // MOE_BLOCK - 1, -1)
    tail = n_blk[0] + e_iota
    zero_blocks = jnp.concatenate([seg_last, jnp.where(tail < n_blocks, tail, -1)]).astype(jnp.int32)
    return dest.astype(jnp.int32), zero_blocks, block_e, n_blk


def _one_layer(h, p, g_mix, w_in, hy_conv_w, hy_conv_b, hy_f_w1, hy_f_b1, hy_f_freq1, hy_f_w2, hy_f_b2,
               hy_f_freq2, hy_f_w3, hy_skip, na_rpb, g_out_hy, g_out_na, w_out, g_ffn, router_wg, router_bg,
               router_we, router_be, exp_w1, exp_w3, exp_w2, g_ple, w_ple_gate, w_ple_proj):
    B, L, D = h.shape
    T = B * L
    assert T % ROW_TILE == 0 and L % FREQ_CHUNK == 0 and w_in.shape[1] == HY_COLS + 3 * NA_WIDTH
    x2 = h.reshape(T, D)
    u_hy, qkv = _in_proj(x2, g_mix, w_in)

    hf = _filter_mlp(L, hy_f_w1, hy_f_b1, hy_f_freq1, hy_f_w2, hy_f_b2, hy_f_freq2, hy_f_w3)
    fmat, fmat_t = _dft_matrix(L)
    coef_a, coef_b, coef_d = _filter_spec(fmat, hf)
    y_hy = _hyena(u_hy.reshape(B, L, HY_COLS), hy_conv_w.astype(_F32), hy_conv_b.astype(_F32),
                  fmat, fmat_t, coef_a, coef_b, coef_d, hy_skip.astype(_F32))
    y_na = _natten(qkv.reshape(B, L, 3 * NA_WIDTH), na_rpb)

    h1, xn, meta, counts = _out_router(y_hy.reshape(T, HY_WIDTH), y_na.reshape(T, NA_WIDTH), g_out_hy, g_out_na,
                                       w_out, x2, g_ffn, router_wg, router_bg, router_we, router_be)
    n_blocks = -(-(T * TOP_K + N_EXPERTS * (MOE_BLOCK - 1)) // MOE_BLOCK)
    dest, zero_blocks, block_e, n_blk = _routing_tables(meta, counts, n_blocks)
    xb = _moe_scatter(xn, dest, zero_blocks, n_blocks * MOE_BLOCK)
    yb = _moe_experts(xb, block_e, n_blk, exp_w1, exp_w3, exp_w2)
    gates = jnp.pad(meta[_META_GATE:_META_GATE + TOP_K].T, ((0, 0), (0, ROUTER_LANES - TOP_K)))
    return h1, dest, gates, yb


def kernel(x, p, g_mix, w_in, hy_conv_w, hy_conv_b, hy_f_w1, hy_f_b1, hy_f_freq1, hy_f_w2, hy_f_b2, hy_f_freq2, hy_f_w3, hy_skip, na_rpb, g_out_hy, g_out_na, w_out, g_ffn, router_wg, router_bg, router_we, router_be, exp_w1, exp_w3, exp_w2, g_ple, w_ple_gate, w_ple_proj, g_final):
    depth = p.shape[0]
    assert depth == 1, "the final RMSNorm is fused into the last layer's kernel; one layer is supported"
    B, L, D = x.shape
    i = 0
    h1, dest, gates, yb = _one_layer(
        x, p[i], g_mix[i], w_in[i], hy_conv_w[i], hy_conv_b[i], hy_f_w1[i], hy_f_b1[i], hy_f_freq1[i],
        hy_f_w2[i], hy_f_b2[i], hy_f_freq2[i], hy_f_w3[i], hy_skip[i], na_rpb[i], g_out_hy[i], g_out_na[i],
        w_out[i], g_ffn[i], router_wg[i], router_bg[i], router_we[i], router_be[i], exp_w1[i], exp_w3[i],
        exp_w2[i], g_ple[i], w_ple_gate[i], w_ple_proj[i])
    out = _final(dest, h1, gates, yb, p[i].reshape(B * L, -1), w_ple_gate[i], w_ple_proj[i], g_ple[i], g_final)
    return out.reshape(B, L, D)
```

```python
import functools
import math

import numpy as np
import jax
import jax.numpy as jnp
from jax import lax
from jax.experimental import pallas as pl
from jax.experimental.pallas import tpu as pltpu

_F32 = jnp.float32
_BF16 = jnp.bfloat16

GRID_W = 64
HY_WIDTH = 512
NA_WIDTH = 512
NA_HEADS = 8
NA_HEAD_DIM = 64
HY_ORDER = 2
SHORT_CONV = 3
FILTER_EMB = 33
FILTER_BANDS = (FILTER_EMB - 1) // 2
DECAY_TARGET = 1e-2
FAST_DECAY_PCT = 0.3
SLOW_DECAY_PCT = 1.5
WIN_ROWS = 8
WIN_COLS = 16
Q_ROWS = 2
N_GROUPS = 4
EXPERTS_PER_GROUP = 8
N_EXPERTS = N_GROUPS * EXPERTS_PER_GROUP
TOP_K = 2
MOE_BLOCK = 512
EPS = 1e-6
NEG_INF = -1e30
HY_COLS = (HY_ORDER + 1) * HY_WIDTH

V7X_LANES = 128
V7X_SUBLANES = 8
V7X_VMEM_LIMIT_BYTES = 56 * 2 ** 20

ROW_TILE = 512
FREQ_CHUNK = 512
HY_CH_TILE = 256
ROUTER_LANES = V7X_LANES
DMA_UNROLL = 8


def _cparams(*sem):
    return pltpu.CompilerParams(dimension_semantics=sem, vmem_limit_bytes=V7X_VMEM_LIMIT_BYTES)


def _resident(shape, index_map):
    return pl.BlockSpec(shape, index_map, pipeline_mode=pl.Buffered(1))


def _rmsnorm(x, g):
    return x * lax.rsqrt(jnp.mean(x * x, axis=-1, keepdims=True) + EPS) * g


def _dot(a, b):
    return jnp.dot(a, b, preferred_element_type=_F32)


def _dot_f32(a, b):
    return jnp.dot(a, b, preferred_element_type=_F32, precision=lax.Precision.HIGHEST)


def _in_proj_kernel(x_ref, g_ref, w_ref, uhy_ref, qkv_ref):
    xn = _rmsnorm(x_ref[...], g_ref[...]).astype(_BF16)
    n_hy = uhy_ref.shape[1]
    for c0 in range(0, n_hy, 512):
        uhy_ref[:, c0:c0 + 512] = _dot(xn, w_ref[:, c0:c0 + 512]).astype(_BF16)
    for c0 in range(0, qkv_ref.shape[1], 512):
        qkv_ref[:, c0:c0 + 512] = _dot(xn, w_ref[:, n_hy + c0:n_hy + c0 + 512]).astype(_BF16)


def _in_proj(x2, g_mix, w_in):
    T, D = x2.shape
    n_in = w_in.shape[1]
    n_qkv = n_in - HY_COLS
    return pl.pallas_call(
        _in_proj_kernel,
        grid=(T // ROW_TILE,),
        in_specs=[pl.BlockSpec((ROW_TILE, D), lambda i: (i, 0)),
                  pl.BlockSpec((1, D), lambda i: (0, 0)),
                  _resident((D, n_in), lambda i: (0, 0))],
        out_specs=[pl.BlockSpec((ROW_TILE, HY_COLS), lambda i: (i, 0)),
                   pl.BlockSpec((ROW_TILE, n_qkv), lambda i: (i, 0))],
        out_shape=[jax.ShapeDtypeStruct((T, HY_COLS), _BF16),
                   jax.ShapeDtypeStruct((T, n_qkv), _BF16)],
        compiler_params=_cparams("arbitrary"),
        name="in_proj",
    )(x2, g_mix.reshape(1, D), w_in.astype(_BF16))


def _filter_mlp_kernel(z_ref, w1_ref, b1_ref, f1_ref, w2_ref, b2_ref, f2_ref, w3_ref, delta_ref, hf_ref):
    z = z_ref[...]
    hid = jnp.sin(f1_ref[...] * (_dot_f32(z, w1_ref[...]) + b1_ref[...]))
    hid = jnp.sin(f2_ref[...] * (_dot_f32(hid, w2_ref[...]) + b2_ref[...]))
    hf = _dot_f32(hid, w3_ref[...])
    decay = jnp.exp(-z[:, 0:1] * delta_ref[...])
    tl = z.shape[0]
    row = lax.broadcasted_iota(jnp.int32, (tl, HY_WIDTH), 0) + pl.program_id(0) * tl
    for k in range(2 * HY_ORDER):
        blk = hf[:, k * HY_WIDTH:(k + 1) * HY_WIDTH] * decay
        if k >= HY_ORDER:
            blk = jnp.where(row == 0, 0.0, blk)
        hf_ref[:, k * HY_WIDTH:(k + 1) * HY_WIDTH] = blk


def _filter_mlp(L, w1, b1, f1, w2, b2, f2, w3):
    t = jnp.linspace(0.0, 1.0, L, dtype=_F32)[:, None]
    w = 2.0 * math.pi * jnp.arange(L, dtype=_F32)[:, None] / L
    bands = jnp.linspace(1e-4, FILTER_BANDS - 1, FILTER_BANDS, dtype=_F32)[None, :]
    z = jnp.concatenate([t, jnp.cos(bands * w), -jnp.sin(bands * w)], axis=-1)
    z = jnp.pad(z, ((0, 0), (0, V7X_LANES - FILTER_EMB)))
    w1p = jnp.pad(w1.astype(_F32), ((0, V7X_LANES - FILTER_EMB), (0, 0)))
    max_decay = math.log(DECAY_TARGET) / FAST_DECAY_PCT
    min_decay = math.log(DECAY_TARGET) / SLOW_DECAY_PCT
    deltas = jnp.abs(jnp.linspace(min_decay, max_decay, HY_WIDTH, dtype=_F32))[None, :]
    hid = w1.shape[1]
    n_out = w3.shape[1]
    tl = min(L, ROW_TILE)
    full = lambda shape: pl.BlockSpec(shape, lambda i: (0, 0))
    return pl.pallas_call(
        _filter_mlp_kernel,
        grid=(L // tl,),
        in_specs=[pl.BlockSpec((tl, V7X_LANES), lambda i: (i, 0)),
                  full((V7X_LANES, hid)), full((1, hid)), full((1, hid)),
                  full((hid, hid)), full((1, hid)), full((1, hid)),
                  full((hid, n_out)), full((1, HY_WIDTH))],
        out_specs=pl.BlockSpec((tl, n_out), lambda i: (i, 0)),
        out_shape=jax.ShapeDtypeStruct((L, n_out), _F32),
        compiler_params=_cparams("arbitrary"),
        name="hyena_filter_mlp",
    )(z, w1p, b1.reshape(1, hid), f1.reshape(1, hid), w2.astype(_F32), b2.reshape(1, hid),
      f2.reshape(1, hid), w3.astype(_F32), deltas)


def _freq_chunks(gp):
    return [(r0, min(FREQ_CHUNK, gp - r0)) for r0 in range(0, gp, FREQ_CHUNK)]


def _half_dft_tables(L):
    lh, n, sb = L // 2, 2 * L, 32
    gp = -(-(lh + 1) // V7X_SUBLANES) * V7X_SUBLANES
    assert lh % sb == 0
    g = jnp.arange(gp, dtype=jnp.int32)
    live = (g <= lh)[:, None]
    ang = lambda s: ((g[:, None] * s[None, :]) % L).astype(_F32) * (2.0 * math.pi / L)
    a_hi = ang(jnp.arange(lh // sb, dtype=jnp.int32) * sb)
    a_lo = ang(jnp.arange(sb, dtype=jnp.int32))
    c_hi, s_hi, c_lo, s_lo = jnp.cos(a_hi), jnp.sin(a_hi), jnp.cos(a_lo), jnp.sin(a_lo)
    cosm = (c_hi[:, :, None] * c_lo[:, None, :] - s_hi[:, :, None] * s_lo[:, None, :]).reshape(gp, lh)
    sinm = (s_hi[:, :, None] * c_lo[:, None, :] + c_hi[:, :, None] * s_lo[:, None, :]).reshape(gp, lh)
    cosm = jnp.where(live, cosm, 0.0)
    sinm = jnp.where(live, sinm, 0.0)
    ff = jnp.concatenate([cosm, -sinm], axis=0).astype(_BF16)
    wgt = jnp.where((g == 0) | (g == lh), 1.0, 2.0)[:, None] / n
    gf = jnp.concatenate([(cosm * wgt).T, (-sinm * wgt).T], axis=1).astype(_BF16)
    tw = g.astype(_F32)[:, None] * (2.0 * math.pi / n)
    wr = jnp.broadcast_to(jnp.cos(tw), (gp, HY_CH_TILE))
    wi = jnp.broadcast_to(-jnp.sin(tw), (gp, HY_CH_TILE))
    return ff, gf, wr, wi


def _half_spectra(ff_ref, wr, wi, xe, xo, r0, n):
    gp = ff_ref.shape[0] // 2
    fc = ff_ref[r0:r0 + n, :]
    fs = ff_ref[gp + r0:gp + r0 + n, :]
    er, ei = _dot(fc, xe), _dot(fs, xe)
    orr, oi = _dot(fc, xo), _dot(fs, xo)
    pr = wr * orr - wi * oi
    pi = wr * oi + wi * orr
    return (er + pr, ei + pi), (er - pr, pi - ei)


def _filter_spec_kernel(ff_ref, wr_ref, wi_ref, fe_ref, fo_ref, be_ref, bo_ref, k1r_ref, k1i_ref, k2r_ref, k2i_ref):
    fe, fo, be, bo = (r[...].astype(_BF16) for r in (fe_ref, fo_ref, be_ref, bo_ref))
    for r0, n in _freq_chunks(wr_ref.shape[0]):
        wr, wi = wr_ref[r0:r0 + n, :], wi_ref[r0:r0 + n, :]
        (f1r, f1i), (f2r, f2i) = _half_spectra(ff_ref, wr, wi, fe, fo, r0, n)
        (b1r, b1i), (b2r, b2i) = _half_spectra(ff_ref, wr, wi, be, bo, r0, n)
        k1r_ref[0, r0:r0 + n, :] = f1r + b1r
        k1i_ref[0, r0:r0 + n, :] = f1i - b1i
        k2r_ref[0, r0:r0 + n, :] = f2r + b2r
        k2i_ref[0, r0:r0 + n, :] = f2i - b2i


def _filter_spec(tables, hf):
    ff, _, wr, wi = tables
    L, n_out = hf.shape
    lh, gp = L // 2, wr.shape[0]
    n_ct = HY_WIDTH // HY_CH_TILE
    n_blk = n_out // HY_CH_TILE
    hf2 = hf.reshape(lh, 2 * n_out)
    col = lambda blk0: pl.BlockSpec((lh, HY_CH_TILE), lambda j: (0, blk0 + j))
    out = jax.ShapeDtypeStruct((HY_ORDER, gp, HY_WIDTH), _F32)
    ospec = pl.BlockSpec((1, gp, HY_CH_TILE), lambda j: (j // n_ct, 0, j % n_ct))
    full = lambda a: pl.BlockSpec(a.shape, lambda j: (0, 0))
    return pl.pallas_call(
        _filter_spec_kernel,
        grid=(HY_ORDER * n_ct,),
        in_specs=[full(ff), full(wr), full(wi),
                  col(0), col(n_blk),
                  col(HY_ORDER * n_ct), col(n_blk + HY_ORDER * n_ct)],
        out_specs=[ospec] * 4,
        out_shape=[out] * 4,
        compiler_params=_cparams("arbitrary"),
        name="hyena_filter_spectrum",
    )(ff, wr, wi, hf2, hf2, hf2, hf2)


def _parity_conv(ze, zo, w_ref, b_ref):
    lh = ze.shape[0]
    row = lax.broadcasted_iota(jnp.int32, ze.shape, 0)
    zo_prev = jnp.where(row == 0, 0.0, pltpu.roll(zo, 1, axis=0))
    ze_next = jnp.where(row == lh - 1, 0.0, pltpu.roll(ze, lh - 1, axis=0))
    w0, w1, w2, b = w_ref[0:1, :], w_ref[1:2, :], w_ref[2:3, :], b_ref[...]
    return ((b + zo_prev * w0) + ze * w1) + zo * w2, ((b + ze * w0) + zo * w1) + ze_next * w2


def _hyena_kernel(ze_ref, zo_ref, zw_ref, zb_ref, ge_ref, go_ref, gw_ref, gb_ref, skip_ref, ff_ref, gf_ref,
                  wr_ref, wi_ref, k1r_ref, k1i_ref, k2r_ref, k2i_ref, oe_ref, oo_ref, a_ref, b_ref, *, conv_input):
    ze, zo = ze_ref[0].astype(_F32), zo_ref[0].astype(_F32)
    if conv_input:
        ze, zo = _parity_conv(ze, zo, zw_ref, zb_ref)
    xe, xo = ze.astype(_BF16), zo.astype(_BF16)
    gp = wr_ref.shape[0]
    for r0, n in _freq_chunks(gp):
        rows = slice(r0, r0 + n)
        wr, wi = wr_ref[rows, :], wi_ref[rows, :]
        (x1r, x1i), (x2r, x2i) = _half_spectra(ff_ref, wr, wi, xe, xo, r0, n)
        k1r, k1i, k2r, k2i = k1r_ref[0, rows, :], k1i_ref[0, rows, :], k2r_ref[0, rows, :], k2i_ref[0, rows, :]
        y1r, y1i = x1r * k1r - x1i * k1i, x1r * k1i + x1i * k1r
        y2r, y2i = x2r * k2r - x2i * k2i, x2r * k2i + x2i * k2r
        tr, ti = y1r - y2r, y1i + y2i
        a_ref[rows, :] = (y1r + y2r).astype(_BF16)
        a_ref[gp + r0:gp + r0 + n, :] = (y1i - y2i).astype(_BF16)
        b_ref[rows, :] = (tr * wr + ti * wi).astype(_BF16)
        b_ref[gp + r0:gp + r0 + n, :] = (ti * wr - tr * wi).astype(_BF16)
    ge, go = _parity_conv(ge_ref[0].astype(_F32), go_ref[0].astype(_F32), gw_ref, gb_ref)
    skip = skip_ref[0]
    for r0 in range(0, ze.shape[0], FREQ_CHUNK):
        rows = slice(r0, r0 + FREQ_CHUNK)
        g = gf_ref[rows, :]
        oe_ref[0, rows, :] = ge[rows] * (_dot(g, a_ref[...]) + ze[rows] * skip)
        oo_ref[0, rows, :] = go[rows] * (_dot(g, b_ref[...]) + zo[rows] * skip)


def _hyena(u_hy, conv_w, conv_b, tables, k1r, k1i, k2r, k2i, skip):
    ff, gf, wr, wi = tables
    B, L, _ = u_hy.shape
    lh, gp = L // 2, wr.shape[0]
    assert lh % FREQ_CHUNK == 0
    n_ct = HY_WIDTH // HY_CH_TILE
    u2 = u_hy.reshape(B, lh, 2 * HY_COLS)
    odd = HY_COLS // HY_CH_TILE
    conv_b2 = conv_b.reshape(1, HY_COLS)
    skip3 = skip.reshape(HY_ORDER, 1, HY_WIDTH)
    col = lambda blk0: pl.BlockSpec((1, lh, HY_CH_TILE), lambda c, b: (b, 0, blk0 + c))
    cw = lambda blk0: pl.BlockSpec((SHORT_CONV, HY_CH_TILE), lambda c, b: (0, blk0 + c))
    cb = lambda blk0: pl.BlockSpec((1, HY_CH_TILE), lambda c, b: (0, blk0 + c))
    const = lambda a: _resident(a.shape, lambda c, b: (0, 0))
    out = jax.ShapeDtypeStruct((B, lh, HY_WIDTH), _F32)
    ze_arr, zo_arr, ze_spec, zo_spec = u2, u2, col(2 * n_ct), col(odd + 2 * n_ct)
    for o in range(HY_ORDER):
        coef = lambda: pl.BlockSpec((1, gp, HY_CH_TILE), lambda c, b, o=o: (o, 0, c), pipeline_mode=pl.Buffered(1))
        ze_arr, zo_arr = pl.pallas_call(
            functools.partial(_hyena_kernel, conv_input=(o == 0)),
            grid=(n_ct, B),
            in_specs=[ze_spec, zo_spec, cw(2 * n_ct), cb(2 * n_ct),
                      col(o * n_ct), col(odd + o * n_ct), cw(o * n_ct), cb(o * n_ct),
                      pl.BlockSpec((1, 1, HY_CH_TILE), lambda c, b, o=o: (o, 0, c)),
                      const(ff), const(gf), const(wr), const(wi), coef(), coef(), coef(), coef()],
            out_specs=[col(0), col(0)],
            out_shape=[out, out],
            scratch_shapes=[pltpu.VMEM((2 * gp, HY_CH_TILE), _BF16), pltpu.VMEM((2 * gp, HY_CH_TILE), _BF16)],
            compiler_params=_cparams("arbitrary", "arbitrary"),
            name=f"hyena_order_{o}",
        )(ze_arr, zo_arr, conv_w, conv_b2, u2, u2, conv_w, conv_b2, skip3, ff, gf, wr, wi, k1r, k1i, k2r, k2i)
        ze_spec = zo_spec = col(0)
    return jnp.stack([ze_arr, zo_arr], axis=2).reshape(B, L, HY_WIDTH)


def _natten_tables(rows):
    kr = min(WIN_ROWS, rows)
    krb = min(Q_ROWS + kr - 1, rows)
    rs = np.clip(np.arange(rows) - kr // 2, 0, rows - kr)
    cs = np.clip(np.arange(GRID_W) - WIN_COLS // 2, 0, GRID_W - WIN_COLS)
    qc = np.tile(np.arange(GRID_W), Q_ROWS)[:, None]
    kc = np.tile(np.arange(GRID_W), krb)[None, :]
    cases, case_of, kstart = [], [], []
    for p in range(rows // Q_ROWS):
        k_r0 = min(rs[p * Q_ROWS], rows - krb)
        qr = (p * Q_ROWS + np.repeat(np.arange(Q_ROWS), GRID_W))[:, None]
        kr_ = (k_r0 + np.repeat(np.arange(krb), GRID_W))[None, :]
        valid = ((kr_ >= rs[qr]) & (kr_ < rs[qr] + kr) & (kc >= cs[qc]) & (kc < cs[qc] + WIN_COLS))
        dr = np.clip(kr_ - qr + WIN_ROWS - 1, 0, 2 * WIN_ROWS - 2)
        dc = np.clip(kc - qc + WIN_COLS - 1, 0, 2 * WIN_COLS - 2)
        idx = np.where(valid, dr * (2 * WIN_COLS - 1) + dc, -1).astype(np.int32)
        for ci, c in enumerate(cases):
            if np.array_equal(c, idx):
                break
        else:
            ci = len(cases)
            cases.append(idx)
        case_of.append(ci)
        kstart.append(k_r0 * GRID_W)
    return np.stack(cases), np.asarray(case_of, np.int32), np.asarray(kstart, np.int32), krb * GRID_W


def _natten_bias(rpb, cases):
    n_case, nq, nk = cases.shape
    krb = nk // GRID_W
    n_dr, n_dc = 2 * WIN_ROWS - 1, 2 * WIN_COLS - 1
    c5 = cases.reshape(n_case, Q_ROWS, GRID_W, krb, GRID_W)
    dr_blk = np.where(c5 >= 0, c5 // n_dc, -1).max(axis=(2, 4))
    dc_idx = np.clip(np.arange(GRID_W)[None, :] - np.arange(GRID_W)[:, None] + WIN_COLS - 1, 0, n_dc - 1)
    assert np.all((c5 < 0) | (c5 // n_dc == dr_blk[:, :, None, :, None]))
    assert np.all((c5 < 0) | (c5 % n_dc == dc_idx[None, None, :, None, :]))
    onehot_c = (dc_idx.reshape(1, -1) == np.arange(n_dc)[:, None]).astype(np.float32)
    sel = (np.maximum(dr_blk, 0).reshape(-1, 1) == np.arange(n_dr)[None, :]).astype(np.float32)
    hi = lax.Precision.HIGHEST
    toe = jnp.einsum('hdj,jx->hdx', rpb.astype(_F32), onehot_c, precision=hi)
    blk = jnp.einsum('sd,hdx->hsx', sel, toe, precision=hi)
    blk = blk.reshape(NA_HEADS, n_case, Q_ROWS, krb, GRID_W, GRID_W).transpose(1, 0, 2, 4, 3, 5)
    return jnp.where((cases >= 0)[:, None], blk.reshape(n_case, NA_HEADS, nq, nk), NEG_INF)


def _natten_kernel(case_ref, kstart_ref, qkv_ref, bias_ref, o_ref, *, n_pairs, n_keys):
    nq = Q_ROWS * GRID_W
    pair_w = 2 * NA_HEAD_DIM
    lane = lax.broadcasted_iota(jnp.int32, (nq, pair_w), 1)
    lo_half = lane < NA_HEAD_DIM
    scale = NA_HEAD_DIM ** -0.5

    def body(p, carry):
        q0 = pl.multiple_of(p * nq, nq)
        k0 = pl.multiple_of(kstart_ref[p], GRID_W)
        case = case_ref[p]
        for hp in range(NA_HEADS // 2):
            c0 = hp * pair_w
            q2 = qkv_ref[0, pl.ds(q0, nq), c0:c0 + pair_w]
            k2 = qkv_ref[0, pl.ds(k0, n_keys), NA_WIDTH + c0:NA_WIDTH + c0 + pair_w]
            v2 = qkv_ref[0, pl.ds(k0, n_keys), 2 * NA_WIDTH + c0:2 * NA_WIDTH + c0 + pair_w]
            zero = jnp.zeros_like(q2)
            q2 = q2 * scale
            qq = jnp.concatenate([jnp.where(lo_half, q2, zero), jnp.where(lo_half, zero, q2)], axis=0)
            s = lax.dot_general(qq, k2, (((1,), (1,)), ((), ())), preferred_element_type=_F32)
            s = s + bias_ref[case, hp]
            e = jnp.exp(s - jnp.max(s, axis=-1, keepdims=True))
            l = jnp.sum(e, axis=-1, keepdims=True)
            o = _dot(e.astype(_BF16), v2) / l
            o_ref[0, pl.ds(q0, nq), c0:c0 + pair_w] = jnp.where(lo_half, o[:nq], o[nq:])
        return carry

    lax.fori_loop(0, n_pairs, body, 0, unroll=2)


def _natten(qkv, rpb):
    B, L, _ = qkv.shape
    rows = L // GRID_W
    assert rows % Q_ROWS == 0 and rows >= Q_ROWS + WIN_ROWS - 1
    cases, case_of, kstart, n_keys = _natten_tables(rows)
    n_case = cases.shape[0]
    nq = Q_ROWS * GRID_W
    bias = _natten_bias(rpb, cases).reshape(n_case, NA_HEADS // 2, 2 * nq, n_keys)
    grid_spec = pltpu.PrefetchScalarGridSpec(
        num_scalar_prefetch=2,
        grid=(B,),
        in_specs=[pl.BlockSpec((1, L, 3 * NA_WIDTH), lambda b, *_: (b, 0, 0)),
                  _resident((n_case, NA_HEADS // 2, 2 * nq, n_keys), lambda b, *_: (0, 0, 0, 0))],
        out_specs=pl.BlockSpec((1, L, NA_WIDTH), lambda b, *_: (b, 0, 0)),
    )
    return pl.pallas_call(
        functools.partial(_natten_kernel, n_pairs=rows // Q_ROWS, n_keys=n_keys),
        grid_spec=grid_spec,
        out_shape=jax.ShapeDtypeStruct((B, L, NA_WIDTH), _F32),
        compiler_params=_cparams("arbitrary"),
        name="natten",
    )(jnp.asarray(case_of), jnp.asarray(kstart), qkv, bias)


_META_ID, _META_GATE, _META_RANK = 0, 2, 4
ROUTER_ROWS = 8 + N_EXPERTS


def _split_bf16(v):
    hi = v.astype(_BF16)
    return hi, (v - hi.astype(_F32)).astype(_BF16)


def _out_router_kernel(yhy_ref, yna_ref, ghy_ref, gna_ref, wtop_ref, wbot_ref, x_ref, gffn_ref, rwh_ref, rwl_ref,
                       rb_ref, h1_ref, xn_ref, meta_ref, cnt_ref, carry_ref):
    @pl.when(pl.program_id(0) == 0)
    def _():
        carry_ref[...] = jnp.zeros_like(carry_ref)

    nh = _rmsnorm(yhy_ref[...], ghy_ref[...]).astype(_BF16)
    nn = _rmsnorm(yna_ref[...], gna_ref[...]).astype(_BF16)
    h1 = x_ref[...] + (_dot(nh, wtop_ref[...]) + _dot(nn, wbot_ref[...]))
    h1_ref[...] = h1
    xn = _rmsnorm(h1, gffn_ref[...])
    xn_ref[...] = xn
    tm = xn.shape[0]

    xh, xl = _split_bf16(xn)
    logits = _dot(xh, rwh_ref[...]) + (_dot(xh, rwl_ref[...]) + _dot(xl, rwh_ref[...]))
    lt = logits.T[:ROUTER_ROWS] + rb_ref[:, 0:1]

    neg = -jnp.inf
    row8 = lax.broadcasted_iota(jnp.int32, (EXPERTS_PER_GROUP, tm), 0).astype(_F32)
    col_max = lambda v: jnp.max(v, axis=0, keepdims=True)
    first_max = lambda v, m: jnp.min(jnp.where(v == m, row8, float(EXPERTS_PER_GROUP)), axis=0, keepdims=True)

    gl = jnp.where(row8 < N_GROUPS, lt[0:8], neg)
    gmax = col_max(gl)
    g_w = 1.0 / jnp.sum(jnp.exp(gl - gmax), axis=0, keepdims=True)
    g_idx = first_max(gl, gmax)
    el = lt[8:8 + EXPERTS_PER_GROUP]
    for g in range(1, N_GROUPS):
        el = jnp.where(g_idx == g, lt[8 + g * EXPERTS_PER_GROUP:8 + (g + 1) * EXPERTS_PER_GROUP], el)
    m1 = col_max(el)
    i1 = first_max(el, m1)
    el2 = jnp.where(row8 == i1, neg, el)
    m2 = col_max(el2)
    i2 = first_max(el2, m2)
    r = jnp.exp(m2 - m1)
    gate1 = g_w / (1.0 + r)
    gate2 = g_w * r / (1.0 + r)
    id1 = g_idx * EXPERTS_PER_GROUP + i1
    id2 = g_idx * EXPERTS_PER_GROUP + i2

    row_e = lax.broadcasted_iota(jnp.int32, (N_EXPERTS, tm), 0).astype(_F32)
    sel1 = row_e == id1
    sel2 = row_e == id2
    onehot = jnp.where(sel1 | sel2, 1.0, 0.0)
    tri_r = lax.broadcasted_iota(jnp.int32, (tm, tm), 0)
    tri_c = lax.broadcasted_iota(jnp.int32, (tm, tm), 1)
    tri = jnp.where(tri_r < tri_c, 1.0, 0.0).astype(_BF16)
    carry = carry_ref[:, 0:1]
    before = carry + _dot(onehot.astype(_BF16), tri)
    rank1 = jnp.sum(jnp.where(sel1, before, 0.0), axis=0, keepdims=True)
    rank2 = jnp.sum(jnp.where(sel2, before, 0.0), axis=0, keepdims=True)
    new_carry = carry + jnp.sum(onehot, axis=1, keepdims=True)
    carry_ref[...] = jnp.broadcast_to(new_carry, carry_ref.shape)
    cnt_ref[...] = jnp.broadcast_to(new_carry, cnt_ref.shape)

    meta = jnp.zeros((8, tm), _F32)
    for k, v in enumerate((id1, id2, gate1, gate2, rank1, rank2)):
        meta = jnp.where(row8 == k, v, meta)
    meta_ref[...] = meta


def _out_router(y_hy, y_na, g_hy, g_na, w_out, x2, g_ffn, wg, bg, we, be):
    T, D = x2.shape
    gpad = 8 - N_GROUPS
    rw = jnp.concatenate([wg.astype(_F32), jnp.zeros((D, gpad), _F32), we.astype(_F32)], axis=1)
    rw_hi, rw_lo = _split_bf16(jnp.pad(rw, ((0, 0), (0, ROUTER_LANES - ROUTER_ROWS))))
    rb = jnp.concatenate([bg.astype(_F32), jnp.zeros((gpad,), _F32), be.astype(_F32)])
    rb = jnp.broadcast_to(rb[:, None], (ROUTER_ROWS, ROUTER_LANES))
    w_bf = w_out.astype(_BF16)
    row = lambda n: pl.BlockSpec((ROW_TILE, n), lambda i: (i, 0))
    vec = lambda n: pl.BlockSpec((1, n), lambda i: (0, 0))
    mat = lambda r, c: pl.BlockSpec((r, c), lambda i: (0, 0))
    return pl.pallas_call(
        _out_router_kernel,
        grid=(T // ROW_TILE,),
        in_specs=[row(HY_WIDTH), row(NA_WIDTH), vec(HY_WIDTH), vec(NA_WIDTH),
                  mat(HY_WIDTH, D), mat(NA_WIDTH, D), row(D), vec(D),
                  mat(D, ROUTER_LANES), mat(D, ROUTER_LANES), mat(ROUTER_ROWS, ROUTER_LANES)],
        out_specs=[row(D),
                   row(D),
                   pl.BlockSpec((8, ROW_TILE), lambda i: (0, i)),
                   mat(N_EXPERTS, ROUTER_LANES)],
        out_shape=[jax.ShapeDtypeStruct((T, D), _F32),
                   jax.ShapeDtypeStruct((T, D), _F32),
                   jax.ShapeDtypeStruct((8, T), _F32),
                   jax.ShapeDtypeStruct((N_EXPERTS, ROUTER_LANES), _F32)],
        scratch_shapes=[pltpu.VMEM((N_EXPERTS, ROUTER_LANES), _F32)],
        compiler_params=_cparams("arbitrary"),
        name="out_proj_router",
    )(y_hy, y_na, g_hy.reshape(1, -1), g_na.reshape(1, -1), w_bf[:HY_WIDTH], w_bf[HY_WIDTH:],
      x2, g_ffn.reshape(1, D), rw_hi, rw_lo, rb)


def _scatter_kernel(zblk_ref, dest_ref, xn_ref, xb_ref, zero_ref, sem, zero_sem):
    tm = xn_ref.shape[0]
    row_copy = lambda src, d: pltpu.make_async_copy(src, xb_ref.at[pl.ds(d, 1)], sem)

    @pl.when(pl.program_id(0) == 0)
    def _():
        zero_ref[...] = jnp.zeros_like(zero_ref)

        def blk_copy(j):
            start = pl.multiple_of(zblk_ref[j] * MOE_BLOCK, MOE_BLOCK)
            return pltpu.make_async_copy(zero_ref, xb_ref.at[pl.ds(start, MOE_BLOCK)], zero_sem)

        def start(j, c):
            @pl.when(zblk_ref[j] >= 0)
            def _():
                blk_copy(j).start()
            return c

        def wait(j, c):
            @pl.when(zblk_ref[j] >= 0)
            def _():
                blk_copy(j).wait()
            return c

        lax.fori_loop(0, zblk_ref.shape[0], start, 0)
        lax.fori_loop(0, zblk_ref.shape[0], wait, 0)

    def start(t, c):
        for k in range(TOP_K):
            row_copy(xn_ref.at[pl.ds(t, 1)], dest_ref[0, 0, TOP_K * t + k]).start()
        return c

    lax.fori_loop(0, tm, start, 0, unroll=DMA_UNROLL)
    all_rows = xb_ref.at[pl.ds(0, TOP_K * tm)]
    pltpu.make_async_copy(all_rows, all_rows, sem).wait()


def _moe_scatter(xn, dest, zero_blocks, n_slots):
    T, D = xn.shape
    nt = T // ROW_TILE
    grid_spec = pltpu.PrefetchScalarGridSpec(
        num_scalar_prefetch=1,
        grid=(nt,),
        in_specs=[pl.BlockSpec((1, 1, TOP_K * ROW_TILE), lambda i, *_: (i, 0, 0), memory_space=pltpu.SMEM),
                  pl.BlockSpec((ROW_TILE, D), lambda i, *_: (i, 0))],
        out_specs=pl.BlockSpec(memory_space=pl.ANY),
        scratch_shapes=[pltpu.VMEM((MOE_BLOCK, D), _F32), pltpu.SemaphoreType.DMA(()),
                        pltpu.SemaphoreType.DMA(())],
    )
    return pl.pallas_call(
        _scatter_kernel,
        grid_spec=grid_spec,
        out_shape=jax.ShapeDtypeStruct((n_slots, D), _F32),
        compiler_params=_cparams("arbitrary"),
        name="moe_scatter",
    )(zero_blocks, dest.reshape(nt, 1, TOP_K * ROW_TILE), xn)


def _expert_kernel(be_ref, nblk_ref, xb_ref, w1_ref, w3_ref, w2_ref, yb_ref, wb1_ref, wb3_ref, wb2_ref):
    i = pl.program_id(0)

    @pl.when(i < nblk_ref[0])
    def _():
        e = be_ref[i]
        e_prev = be_ref[jnp.maximum(i - 1, 0)]

        @pl.when((i == 0) | (e != e_prev))
        def _():
            wb1_ref[...] = w1_ref[0].astype(_BF16)
            wb3_ref[...] = w3_ref[0].astype(_BF16)
            wb2_ref[...] = w2_ref[0].astype(_BF16)

        x = xb_ref[...].astype(_BF16)
        a = _dot(x, wb1_ref[...])
        hid = (a * jax.nn.sigmoid(a) * _dot(x, wb3_ref[...])).astype(_BF16)
        yb_ref[...] = _dot(hid, wb2_ref[...])

    @pl.when(i >= nblk_ref[0])
    def _():
        yb_ref[...] = jnp.zeros_like(yb_ref)


def _moe_experts(xb, block_e, n_blk, w1, w3, w2):
    n_blocks = xb.shape[0] // MOE_BLOCK
    _, D, DE = w1.shape
    live = lambda i, nb: jnp.minimum(i, nb[0] - 1)
    grid_spec = pltpu.PrefetchScalarGridSpec(
        num_scalar_prefetch=2,
        grid=(n_blocks,),
        in_specs=[pl.BlockSpec((MOE_BLOCK, D), lambda i, be, nb: (live(i, nb), 0)),
                  pl.BlockSpec((1, D, DE), lambda i, be, nb: (be[live(i, nb)], 0, 0)),
                  pl.BlockSpec((1, D, DE), lambda i, be, nb: (be[live(i, nb)], 0, 0)),
                  pl.BlockSpec((1, DE, D), lambda i, be, nb: (be[live(i, nb)], 0, 0))],
        out_specs=pl.BlockSpec((MOE_BLOCK, D), lambda i, be, nb: (i, 0)),
        scratch_shapes=[pltpu.VMEM((D, DE), _BF16), pltpu.VMEM((D, DE), _BF16), pltpu.VMEM((DE, D), _BF16)],
    )
    return pl.pallas_call(
        _expert_kernel,
        grid_spec=grid_spec,
        out_shape=jax.ShapeDtypeStruct(xb.shape, _F32),
        compiler_params=_cparams("arbitrary"),
        name="moe_experts",
    )(block_e, n_blk, xb, w1, w3, w2)


def _final_kernel(dest_ref, dest_next_ref, h1_ref, gates_ref, yb_ref, p_ref, wg_ref, wp_ref, gple_ref, gfin_ref,
                  o_ref, ybuf_ref, sem):
    tm = h1_ref.shape[0]
    i = pl.program_id(0)
    slot = lax.rem(i, 2)

    def gather(d_ref, s):
        def start(t, c):
            for k in range(TOP_K):
                pltpu.make_async_copy(yb_ref.at[pl.ds(d_ref[0, 0, TOP_K * t + k], 1)],
                                      ybuf_ref.at[s, k, pl.ds(t, 1)], sem.at[s]).start()
            return c
        lax.fori_loop(0, tm, start, 0, unroll=DMA_UNROLL)

    @pl.when(i == 0)
    def _():
        gather(dest_ref, 0)

    @pl.when(i + 1 < pl.num_programs(0))
    def _():
        gather(dest_next_ref, 1 - slot)

    pltpu.make_async_copy(ybuf_ref.at[slot], ybuf_ref.at[slot], sem.at[slot]).wait()

    gates = gates_ref[...]
    moe = ybuf_ref[slot, 0] * gates[:, 0:1] + ybuf_ref[slot, 1] * gates[:, 1:2]
    h2 = h1_ref[...] + moe
    gate = jax.nn.sigmoid(_dot(_rmsnorm(h2, gple_ref[...]).astype(_BF16), wg_ref[...]))
    h3 = h2 + _dot(p_ref[...].astype(_BF16), wp_ref[...]) * gate
    o_ref[...] = _rmsnorm(h3, gfin_ref[...])


def _final(dest, h1, gates, yb, p2, w_gate, w_proj, g_ple, g_final):
    T, D = h1.shape
    nt = T // ROW_TILE
    PD = p2.shape[1]
    row = lambda n: pl.BlockSpec((ROW_TILE, n), lambda i: (i, 0))
    vec = lambda n: pl.BlockSpec((1, n), lambda i: (0, 0))
    dest3 = dest.reshape(nt, 1, TOP_K * ROW_TILE)
    return pl.pallas_call(
        _final_kernel,
        grid=(nt,),
        in_specs=[pl.BlockSpec((1, 1, TOP_K * ROW_TILE), lambda i: (i, 0, 0), memory_space=pltpu.SMEM),
                  pl.BlockSpec((1, 1, TOP_K * ROW_TILE), lambda i: (jnp.minimum(i + 1, nt - 1), 0, 0),
                               memory_space=pltpu.SMEM),
                  row(D), row(ROUTER_LANES),
                  pl.BlockSpec(memory_space=pl.ANY),
                  row(PD),
                  pl.BlockSpec((D, D), lambda i: (0, 0)),
                  pl.BlockSpec((PD, D), lambda i: (0, 0)),
                  vec(D), vec(D)],
        out_specs=row(D),
        out_shape=jax.ShapeDtypeStruct((T, D), _F32),
        scratch_shapes=[pltpu.VMEM((2, TOP_K, ROW_TILE, D), _F32), pltpu.SemaphoreType.DMA((2,))],
        compiler_params=_cparams("arbitrary"),
        name="moe_gather_ple_final",
    )(dest3, dest3, h1, gates, yb, p2, w_gate.astype(_BF16), w_proj.astype(_BF16),
      g_ple.reshape(1, D), g_final.reshape(1, D))


def _routing_tables(meta_t, counts_f, n_blocks):
    ids = meta_t[_META_ID:_META_ID + TOP_K].T.astype(jnp.int32)
    rank = meta_t[_META_RANK:_META_RANK + TOP_K].T.astype(jnp.int32)
    counts = counts_f[:, 0].astype(jnp.int32)
    padded = (counts + MOE_BLOCK - 1) // MOE_BLOCK * MOE_BLOCK
    e_iota = jnp.arange(N_EXPERTS, dtype=jnp.int32)
    pad_end = jnp.sum(jnp.where(e_iota[None, :] <= e_iota[:, None], padded[None, :], 0), axis=1)
    pad_start = pad_end - padded
    dest = jnp.sum(jnp.where(ids[..., None] == e_iota, pad_start, 0), axis=-1) + rank
    blk_row = jnp.arange(n_blocks, dtype=jnp.int32)[:, None] * MOE_BLOCK
    block_e = jnp.minimum(jnp.sum((pad_end[None, :] <= blk_row).astype(jnp.int32), axis=1), N_EXPERTS - 1)
    n_blk = (pad_end[-1:] // MOE_BLOCK).astype(jnp.int32)
    seg_last = jnp.where(padded > counts, pad_end // MOE_BLOCK - 1, -1)
    tail = n_blk[0] + e_iota
    zero_blocks = jnp.concatenate([seg_last, jnp.where(tail < n_blocks, tail, -1)]).astype(jnp.int32)
    return dest.astype(jnp.int32), zero_blocks, block_e, n_blk


def _one_layer(h, p, g_mix, w_in, hy_conv_w, hy_conv_b, hy_f_w1, hy_f_b1, hy_f_freq1, hy_f_w2, hy_f_b2,
               hy_f_freq2, hy_f_w3, hy_skip, na_rpb, g_out_hy, g_out_na, w_out, g_ffn, router_wg, router_bg,
               router_we, router_be, exp_w1, exp_w3, exp_w2, g_ple, w_ple_gate, w_ple_proj):
    B, L, D = h.shape
    T = B * L
    assert T % ROW_TILE == 0 and L % FREQ_CHUNK == 0 and w_in.shape[1] == HY_COLS + 3 * NA_WIDTH
    x2 = h.reshape(T, D)
    u_hy, qkv = _in_proj(x2, g_mix, w_in)

    hf = _filter_mlp(L, hy_f_w1, hy_f_b1, hy_f_freq1, hy_f_w2, hy_f_b2, hy_f_freq2, hy_f_w3)
    tables = _half_dft_tables(L)
    k1r, k1i, k2r, k2i = _filter_spec(tables, hf)
    y_hy = _hyena(u_hy.reshape(B, L, HY_COLS), hy_conv_w.astype(_F32), hy_conv_b.astype(_F32),
                  tables, k1r, k1i, k2r, k2i, hy_skip.astype(_F32))
    y_na = _natten(qkv.reshape(B, L, 3 * NA_WIDTH), na_rpb)

    h1, xn, meta, counts = _out_router(y_hy.reshape(T, HY_WIDTH), y_na.reshape(T, NA_WIDTH), g_out_hy, g_out_na,
                                       w_out, x2, g_ffn, router_wg, router_bg, router_we, router_be)
    n_blocks = -(-(T * TOP_K + N_EXPERTS * (MOE_BLOCK - 1)) // MOE_BLOCK)
    dest, zero_blocks, block_e, n_blk = _routing_tables(meta, counts, n_blocks)
    xb = _moe_scatter(xn, dest, zero_blocks, n_blocks * MOE_BLOCK)
    yb = _moe_experts(xb, block_e, n_blk, exp_w1, exp_w3, exp_w2)
    gates = jnp.pad(meta[_META_GATE:_META_GATE + TOP_K].T, ((0, 0), (0, ROUTER_LANES - TOP_K)))
    return h1, dest, gates, yb


def kernel(x, p, g_mix, w_in, hy_conv_w, hy_conv_b, hy_f_w1, hy_f_b1, hy_f_freq1, hy_f_w2, hy_f_b2, hy_f_freq2, hy_f_w3, hy_skip, na_rpb, g_out_hy, g_out_na, w_out, g_ffn, router_wg, router_bg, router_we, router_be, exp_w1, exp_w3, exp_w2, g_ple, w_ple_gate, w_ple_proj, g_final):
    depth = p.shape[0]
    assert depth == 1, "the final RMSNorm is fused into the last layer's kernel; one layer is supported"
    B, L, D = x.shape
    i = 0
    h1, dest, gates, yb = _one_layer(
        x, p[i], g_mix[i], w_in[i], hy_conv_w[i], hy_conv_b[i], hy_f_w1[i], hy_f_b1[i], hy_f_freq1[i],
        hy_f_w2[i], hy_f_b2[i], hy_f_freq2[i], hy_f_w3[i], hy_skip[i], na_rpb[i], g_out_hy[i], g_out_na[i],
        w_out[i], g_ffn[i], router_wg[i], router_bg[i], router_we[i], router_be[i], exp_w1[i], exp_w3[i],
        exp_w2[i], g_ple[i], w_ple_gate[i], w_ple_proj[i])
    out = _final(dest, h1, gates, yb, p[i].reshape(B * L, -1), w_ple_gate[i], w_ple_proj[i], g_ple[i], g_final)
    return out.reshape(B, L, D)
```

```python
import functools
import math

import numpy as np
import jax
import jax.numpy as jnp
from jax import lax
from jax.experimental import pallas as pl
from jax.experimental.pallas import tpu as pltpu

_F32 = jnp.float32
_BF16 = jnp.bfloat16

GRID_W = 64
HY_WIDTH = 512
NA_WIDTH = 512
NA_HEADS = 8
NA_HEAD_DIM = 64
HY_ORDER = 2
SHORT_CONV = 3
FILTER_EMB = 33
FILTER_BANDS = (FILTER_EMB - 1) // 2
DECAY_TARGET = 1e-2
FAST_DECAY_PCT = 0.3
SLOW_DECAY_PCT = 1.5
WIN_ROWS = 8
WIN_COLS = 16
Q_ROWS = 2
N_GROUPS = 4
EXPERTS_PER_GROUP = 8
N_EXPERTS = N_GROUPS * EXPERTS_PER_GROUP
TOP_K = 2
MOE_BLOCK = 512
EPS = 1e-6
NEG_INF = -1e30
HY_COLS = (HY_ORDER + 1) * HY_WIDTH

V7X_LANES = 128
V7X_SUBLANES = 8
V7X_VMEM_LIMIT_BYTES = 56 * 2 ** 20

ROW_TILE = 512
FREQ_CHUNK = 512
HY_CH_TILE = 256
ROUTER_LANES = V7X_LANES
DMA_UNROLL = 8


def _cparams(*sem):
    return pltpu.CompilerParams(dimension_semantics=sem, vmem_limit_bytes=V7X_VMEM_LIMIT_BYTES)


def _resident(shape, index_map):
    return pl.BlockSpec(shape, index_map, pipeline_mode=pl.Buffered(1))


def _rmsnorm(x, g):
    return x * lax.rsqrt(jnp.mean(x * x, axis=-1, keepdims=True) + EPS) * g


def _dot(a, b):
    return jnp.dot(a, b, preferred_element_type=_F32)


def _dot_f32(a, b):
    return jnp.dot(a, b, preferred_element_type=_F32, precision=lax.Precision.HIGHEST)


def _parity_select(n):
    r = jnp.arange(n // 2, dtype=jnp.int32)[None, :, None]
    t = jnp.arange(n, dtype=jnp.int32)[None, None, :]
    p = jnp.arange(2, dtype=jnp.int32)[:, None, None]
    return (t == 2 * r + p).astype(_BF16)


def _in_proj_kernel(x_ref, g_ref, w_ref, sel_ref, u2_ref, qkv_ref):
    xn = _rmsnorm(x_ref[...], g_ref[...]).astype(_BF16)
    n_hy = u2_ref.shape[1] // 2
    xe = _dot(sel_ref[0], xn).astype(_BF16)
    xo = _dot(sel_ref[1], xn).astype(_BF16)
    for c0 in range(0, n_hy, 512):
        w = w_ref[:, c0:c0 + 512]
        u2_ref[:, c0:c0 + 512] = _dot(xe, w).astype(_BF16)
        u2_ref[:, n_hy + c0:n_hy + c0 + 512] = _dot(xo, w).astype(_BF16)
    for c0 in range(0, qkv_ref.shape[1], 512):
        qkv_ref[:, c0:c0 + 512] = _dot(xn, w_ref[:, n_hy + c0:n_hy + c0 + 512]).astype(_BF16)


def _in_proj(x2, g_mix, w_in):
    T, D = x2.shape
    n_in = w_in.shape[1]
    n_qkv = n_in - HY_COLS
    half = ROW_TILE // 2
    return pl.pallas_call(
        _in_proj_kernel,
        grid=(T // ROW_TILE,),
        in_specs=[pl.BlockSpec((ROW_TILE, D), lambda i: (i, 0)),
                  pl.BlockSpec((1, D), lambda i: (0, 0)),
                  _resident((D, n_in), lambda i: (0, 0)),
                  pl.BlockSpec((2, half, ROW_TILE), lambda i: (0, 0, 0))],
        out_specs=[pl.BlockSpec((half, 2 * HY_COLS), lambda i: (i, 0)),
                   pl.BlockSpec((ROW_TILE, n_qkv), lambda i: (i, 0))],
        out_shape=[jax.ShapeDtypeStruct((T // 2, 2 * HY_COLS), _BF16),
                   jax.ShapeDtypeStruct((T, n_qkv), _BF16)],
        compiler_params=_cparams("arbitrary"),
        name="in_proj",
    )(x2, g_mix.reshape(1, D), w_in.astype(_BF16), _parity_select(ROW_TILE))


def _filter_mlp_kernel(z_ref, w1_ref, b1_ref, f1_ref, w2_ref, b2_ref, f2_ref, w3_ref, delta_ref, hf_ref):
    z = z_ref[0]
    hid = jnp.sin(f1_ref[...] * (_dot_f32(z, w1_ref[...]) + b1_ref[...]))
    hid = jnp.sin(f2_ref[...] * (_dot_f32(hid, w2_ref[...]) + b2_ref[...]))
    hf = _dot_f32(hid, w3_ref[...])
    decay = jnp.exp(-z[:, 0:1] * delta_ref[...])
    tl = z.shape[0]
    row = lax.broadcasted_iota(jnp.int32, (tl, HY_WIDTH), 0)
    offset0 = (row == 0) & (pl.program_id(0) == 0) & (pl.program_id(1) == 0)
    for k in range(2 * HY_ORDER):
        blk = hf[:, k * HY_WIDTH:(k + 1) * HY_WIDTH] * decay
        if k >= HY_ORDER:
            blk = jnp.where(offset0, 0.0, blk)
        hf_ref[:, k * HY_WIDTH:(k + 1) * HY_WIDTH] = blk


def _filter_mlp(L, w1, b1, f1, w2, b2, f2, w3):
    t = jnp.linspace(0.0, 1.0, L, dtype=_F32)[:, None]
    w = 2.0 * math.pi * jnp.arange(L, dtype=_F32)[:, None] / L
    bands = jnp.linspace(1e-4, FILTER_BANDS - 1, FILTER_BANDS, dtype=_F32)[None, :]
    z = jnp.concatenate([t, jnp.cos(bands * w), -jnp.sin(bands * w)], axis=-1)
    z = jnp.pad(z, ((0, 0), (0, V7X_LANES - FILTER_EMB)))
    z = jnp.stack([z[0::2], z[1::2]])
    w1p = jnp.pad(w1.astype(_F32), ((0, V7X_LANES - FILTER_EMB), (0, 0)))
    max_decay = math.log(DECAY_TARGET) / FAST_DECAY_PCT
    min_decay = math.log(DECAY_TARGET) / SLOW_DECAY_PCT
    deltas = jnp.abs(jnp.linspace(min_decay, max_decay, HY_WIDTH, dtype=_F32))[None, :]
    hid = w1.shape[1]
    n_out = w3.shape[1]
    lh = L // 2
    tl = min(lh, ROW_TILE)
    full = lambda shape: pl.BlockSpec(shape, lambda p, i: (0, 0))
    return pl.pallas_call(
        _filter_mlp_kernel,
        grid=(2, lh // tl),
        in_specs=[pl.BlockSpec((1, tl, V7X_LANES), lambda p, i: (p, i, 0)),
                  full((V7X_LANES, hid)), full((1, hid)), full((1, hid)),
                  full((hid, hid)), full((1, hid)), full((1, hid)),
                  full((hid, n_out)), full((1, HY_WIDTH))],
        out_specs=pl.BlockSpec((tl, n_out), lambda p, i: (i, p)),
        out_shape=jax.ShapeDtypeStruct((lh, 2 * n_out), _F32),
        compiler_params=_cparams("arbitrary", "arbitrary"),
        name="hyena_filter_mlp",
    )(z, w1p, b1.reshape(1, hid), f1.reshape(1, hid), w2.astype(_F32), b2.reshape(1, hid),
      f2.reshape(1, hid), w3.astype(_F32), deltas)


def _freq_chunks(gp):
    return [(r0, min(FREQ_CHUNK, gp - r0)) for r0 in range(0, gp, FREQ_CHUNK)]


def _half_dft_tables(L):
    lh, n, sb = L // 2, 2 * L, 32
    gp = -(-(lh + 1) // V7X_SUBLANES) * V7X_SUBLANES
    assert lh % sb == 0
    g = jnp.arange(gp, dtype=jnp.int32)
    live = (g <= lh)[:, None]
    ang = lambda s: ((g[:, None] * s[None, :]) % L).astype(_F32) * (2.0 * math.pi / L)
    a_hi = ang(jnp.arange(lh // sb, dtype=jnp.int32) * sb)
    a_lo = ang(jnp.arange(sb, dtype=jnp.int32))
    c_hi, s_hi, c_lo, s_lo = jnp.cos(a_hi), jnp.sin(a_hi), jnp.cos(a_lo), jnp.sin(a_lo)
    cosm = (c_hi[:, :, None] * c_lo[:, None, :] - s_hi[:, :, None] * s_lo[:, None, :]).reshape(gp, lh)
    sinm = (s_hi[:, :, None] * c_lo[:, None, :] + c_hi[:, :, None] * s_lo[:, None, :]).reshape(gp, lh)
    cosm = jnp.where(live, cosm, 0.0)
    sinm = jnp.where(live, sinm, 0.0)
    ff = jnp.concatenate([cosm, -sinm], axis=0).astype(_BF16)
    wgt = jnp.where((g == 0) | (g == lh), 1.0, 2.0)[:, None] / n
    gf = jnp.concatenate([(cosm * wgt).T, (-sinm * wgt).T], axis=1).astype(_BF16)
    tw = g.astype(_F32)[:, None] * (2.0 * math.pi / n)
    wr = jnp.broadcast_to(jnp.cos(tw), (gp, HY_CH_TILE))
    wi = jnp.broadcast_to(-jnp.sin(tw), (gp, HY_CH_TILE))
    return ff, gf, wr, wi


def _half_spectra(ff_ref, wr, wi, xe, xo, r0, n):
    gp = ff_ref.shape[0] // 2
    fc = ff_ref[r0:r0 + n, :]
    fs = ff_ref[gp + r0:gp + r0 + n, :]
    er, ei = _dot(fc, xe), _dot(fs, xe)
    orr, oi = _dot(fc, xo), _dot(fs, xo)
    pr = wr * orr - wi * oi
    pi = wr * oi + wi * orr
    return (er + pr, ei + pi), (er - pr, pi - ei)


def _filter_spec_kernel(ff_ref, wr_ref, wi_ref, fe_ref, fo_ref, be_ref, bo_ref, k1r_ref, k1i_ref, k2r_ref, k2i_ref):
    fe, fo, be, bo = (r[...].astype(_BF16) for r in (fe_ref, fo_ref, be_ref, bo_ref))
    for r0, n in _freq_chunks(wr_ref.shape[0]):
        wr, wi = wr_ref[r0:r0 + n, :], wi_ref[r0:r0 + n, :]
        (f1r, f1i), (f2r, f2i) = _half_spectra(ff_ref, wr, wi, fe, fo, r0, n)
        (b1r, b1i), (b2r, b2i) = _half_spectra(ff_ref, wr, wi, be, bo, r0, n)
        k1r_ref[0, r0:r0 + n, :] = f1r + b1r
        k1i_ref[0, r0:r0 + n, :] = f1i - b1i
        k2r_ref[0, r0:r0 + n, :] = f2r + b2r
        k2i_ref[0, r0:r0 + n, :] = f2i - b2i


def _filter_spec(tables, hf2):
    ff, _, wr, wi = tables
    lh, gp = hf2.shape[0], wr.shape[0]
    n_ct = HY_WIDTH // HY_CH_TILE
    n_blk = hf2.shape[1] // 2 // HY_CH_TILE
    col = lambda blk0: pl.BlockSpec((lh, HY_CH_TILE), lambda j: (0, blk0 + j))
    out = jax.ShapeDtypeStruct((HY_ORDER, gp, HY_WIDTH), _F32)
    ospec = pl.BlockSpec((1, gp, HY_CH_TILE), lambda j: (j // n_ct, 0, j % n_ct))
    full = lambda a: pl.BlockSpec(a.shape, lambda j: (0, 0))
    return pl.pallas_call(
        _filter_spec_kernel,
        grid=(HY_ORDER * n_ct,),
        in_specs=[full(ff), full(wr), full(wi),
                  col(0), col(n_blk),
                  col(HY_ORDER * n_ct), col(n_blk + HY_ORDER * n_ct)],
        out_specs=[ospec] * 4,
        out_shape=[out] * 4,
        compiler_params=_cparams("arbitrary"),
        name="hyena_filter_spectrum",
    )(ff, wr, wi, hf2, hf2, hf2, hf2)


def _parity_conv(ze, zo, w_ref, b_ref):
    lh = ze.shape[0]
    row = lax.broadcasted_iota(jnp.int32, ze.shape, 0)
    zo_prev = jnp.where(row == 0, 0.0, pltpu.roll(zo, 1, axis=0))
    ze_next = jnp.where(row == lh - 1, 0.0, pltpu.roll(ze, lh - 1, axis=0))
    w0, w1, w2, b = w_ref[0:1, :], w_ref[1:2, :], w_ref[2:3, :], b_ref[...]
    return ((b + zo_prev * w0) + ze * w1) + zo * w2, ((b + ze * w0) + zo * w1) + ze_next * w2


def _hyena_kernel(ze_ref, zo_ref, zw_ref, zb_ref, ge_ref, go_ref, gw_ref, gb_ref, skip_ref, ff_ref, gf_ref,
                  wr_ref, wi_ref, k1r_ref, k1i_ref, k2r_ref, k2i_ref, oe_ref, oo_ref, a_ref, b_ref, *, conv_input):
    ze, zo = ze_ref[0].astype(_F32), zo_ref[0].astype(_F32)
    if conv_input:
        ze, zo = _parity_conv(ze, zo, zw_ref, zb_ref)
    xe, xo = ze.astype(_BF16), zo.astype(_BF16)
    gp = wr_ref.shape[0]
    for r0, n in _freq_chunks(gp):
        rows = slice(r0, r0 + n)
        wr, wi = wr_ref[rows, :], wi_ref[rows, :]
        (x1r, x1i), (x2r, x2i) = _half_spectra(ff_ref, wr, wi, xe, xo, r0, n)
        k1r, k1i, k2r, k2i = k1r_ref[0, rows, :], k1i_ref[0, rows, :], k2r_ref[0, rows, :], k2i_ref[0, rows, :]
        y1r, y1i = x1r * k1r - x1i * k1i, x1r * k1i + x1i * k1r
        y2r, y2i = x2r * k2r - x2i * k2i, x2r * k2i + x2i * k2r
        tr, ti = y1r - y2r, y1i + y2i
        a_ref[rows, :] = (y1r + y2r).astype(_BF16)
        a_ref[gp + r0:gp + r0 + n, :] = (y1i - y2i).astype(_BF16)
        b_ref[rows, :] = (tr * wr + ti * wi).astype(_BF16)
        b_ref[gp + r0:gp + r0 + n, :] = (ti * wr - tr * wi).astype(_BF16)
    ge, go = _parity_conv(ge_ref[0].astype(_F32), go_ref[0].astype(_F32), gw_ref, gb_ref)
    skip = skip_ref[0]
    for r0 in range(0, ze.shape[0], FREQ_CHUNK):
        rows = slice(r0, r0 + FREQ_CHUNK)
        g = gf_ref[rows, :]
        oe_ref[0, rows, :] = ge[rows] * (_dot(g, a_ref[...]) + ze[rows] * skip)
        oo_ref[0, rows, :] = go[rows] * (_dot(g, b_ref[...]) + zo[rows] * skip)


def _hyena(u2, conv_w, conv_b, tables, k1r, k1i, k2r, k2i, skip):
    ff, gf, wr, wi = tables
    B, lh, _ = u2.shape
    gp = wr.shape[0]
    assert lh % FREQ_CHUNK == 0
    n_ct = HY_WIDTH // HY_CH_TILE
    odd = HY_COLS // HY_CH_TILE
    conv_b2 = conv_b.reshape(1, HY_COLS)
    skip3 = skip.reshape(HY_ORDER, 1, HY_WIDTH)
    col = lambda blk0: pl.BlockSpec((1, lh, HY_CH_TILE), lambda c, b: (b, 0, blk0 + c))
    cw = lambda blk0: pl.BlockSpec((SHORT_CONV, HY_CH_TILE), lambda c, b: (0, blk0 + c))
    cb = lambda blk0: pl.BlockSpec((1, HY_CH_TILE), lambda c, b: (0, blk0 + c))
    const = lambda a: _resident(a.shape, lambda c, b: (0, 0))
    out = jax.ShapeDtypeStruct((B, lh, HY_WIDTH), _F32)
    ze_arr, zo_arr, ze_spec, zo_spec = u2, u2, col(2 * n_ct), col(odd + 2 * n_ct)
    for o in range(HY_ORDER):
        coef = lambda: pl.BlockSpec((1, gp, HY_CH_TILE), lambda c, b, o=o: (o, 0, c), pipeline_mode=pl.Buffered(1))
        ze_arr, zo_arr = pl.pallas_call(
            functools.partial(_hyena_kernel, conv_input=(o == 0)),
            grid=(n_ct, B),
            in_specs=[ze_spec, zo_spec, cw(2 * n_ct), cb(2 * n_ct),
                      col(o * n_ct), col(odd + o * n_ct), cw(o * n_ct), cb(o * n_ct),
                      pl.BlockSpec((1, 1, HY_CH_TILE), lambda c, b, o=o: (o, 0, c)),
                      const(ff), const(gf), const(wr), const(wi), coef(), coef(), coef(), coef()],
            out_specs=[col(0), col(0)],
            out_shape=[out, out],
            scratch_shapes=[pltpu.VMEM((2 * gp, HY_CH_TILE), _BF16), pltpu.VMEM((2 * gp, HY_CH_TILE), _BF16)],
            compiler_params=_cparams("arbitrary", "arbitrary"),
            name=f"hyena_order_{o}",
        )(ze_arr, zo_arr, conv_w, conv_b2, u2, u2, conv_w, conv_b2, skip3, ff, gf, wr, wi, k1r, k1i, k2r, k2i)
        ze_spec = zo_spec = col(0)
    return ze_arr, zo_arr


def _natten_tables(rows):
    kr = min(WIN_ROWS, rows)
    krb = min(Q_ROWS + kr - 1, rows)
    rs = np.clip(np.arange(rows) - kr // 2, 0, rows - kr)
    cs = np.clip(np.arange(GRID_W) - WIN_COLS // 2, 0, GRID_W - WIN_COLS)
    qc = np.tile(np.arange(GRID_W), Q_ROWS)[:, None]
    kc = np.tile(np.arange(GRID_W), krb)[None, :]
    cases, case_of, kstart = [], [], []
    for p in range(rows // Q_ROWS):
        k_r0 = min(rs[p * Q_ROWS], rows - krb)
        qr = (p * Q_ROWS + np.repeat(np.arange(Q_ROWS), GRID_W))[:, None]
        kr_ = (k_r0 + np.repeat(np.arange(krb), GRID_W))[None, :]
        valid = ((kr_ >= rs[qr]) & (kr_ < rs[qr] + kr) & (kc >= cs[qc]) & (kc < cs[qc] + WIN_COLS))
        dr = np.clip(kr_ - qr + WIN_ROWS - 1, 0, 2 * WIN_ROWS - 2)
        dc = np.clip(kc - qc + WIN_COLS - 1, 0, 2 * WIN_COLS - 2)
        idx = np.where(valid, dr * (2 * WIN_COLS - 1) + dc, -1).astype(np.int32)
        for ci, c in enumerate(cases):
            if np.array_equal(c, idx):
                break
        else:
            ci = len(cases)
            cases.append(idx)
        case_of.append(ci)
        kstart.append(k_r0 * GRID_W)
    return np.stack(cases), np.asarray(case_of, np.int32), np.asarray(kstart, np.int32), krb * GRID_W


def _natten_bias(rpb, cases):
    n_case, nq, nk = cases.shape
    krb = nk // GRID_W
    n_dr, n_dc = 2 * WIN_ROWS - 1, 2 * WIN_COLS - 1
    c5 = cases.reshape(n_case, Q_ROWS, GRID_W, krb, GRID_W)
    dr_blk = np.where(c5 >= 0, c5 // n_dc, -1).max(axis=(2, 4))
    dc_idx = np.clip(np.arange(GRID_W)[None, :] - np.arange(GRID_W)[:, None] + WIN_COLS - 1, 0, n_dc - 1)
    assert np.all((c5 < 0) | (c5 // n_dc == dr_blk[:, :, None, :, None]))
    assert np.all((c5 < 0) | (c5 % n_dc == dc_idx[None, None, :, None, :]))
    onehot_c = (dc_idx.reshape(1, -1) == np.arange(n_dc)[:, None]).astype(np.float32)
    sel = (np.maximum(dr_blk, 0).reshape(-1, 1) == np.arange(n_dr)[None, :]).astype(np.float32)
    hi = lax.Precision.HIGHEST
    toe = jnp.einsum('hdj,jx->hdx', rpb.astype(_F32), onehot_c, precision=hi)
    blk = jnp.einsum('sd,hdx->hsx', sel, toe, precision=hi)
    blk = blk.reshape(NA_HEADS, n_case, Q_ROWS, krb, GRID_W, GRID_W).transpose(1, 0, 2, 4, 3, 5)
    return jnp.where((cases >= 0)[:, None], blk.reshape(n_case, NA_HEADS, nq, nk), NEG_INF)


def _natten_kernel(case_ref, kstart_ref, qkv_ref, bias_ref, o_ref, *, n_pairs, n_keys):
    nq = Q_ROWS * GRID_W
    pair_w = 2 * NA_HEAD_DIM
    lane = lax.broadcasted_iota(jnp.int32, (nq, pair_w), 1)
    lo_half = lane < NA_HEAD_DIM
    scale = NA_HEAD_DIM ** -0.5

    def body(p, carry):
        q0 = pl.multiple_of(p * nq, nq)
        k0 = pl.multiple_of(kstart_ref[p], GRID_W)
        case = case_ref[p]
        for hp in range(NA_HEADS // 2):
            c0 = hp * pair_w
            q2 = qkv_ref[0, pl.ds(q0, nq), c0:c0 + pair_w]
            k2 = qkv_ref[0, pl.ds(k0, n_keys), NA_WIDTH + c0:NA_WIDTH + c0 + pair_w]
            v2 = qkv_ref[0, pl.ds(k0, n_keys), 2 * NA_WIDTH + c0:2 * NA_WIDTH + c0 + pair_w]
            zero = jnp.zeros_like(q2)
            q2 = q2 * scale
            qq = jnp.concatenate([jnp.where(lo_half, q2, zero), jnp.where(lo_half, zero, q2)], axis=0)
            s = lax.dot_general(qq, k2, (((1,), (1,)), ((), ())), preferred_element_type=_F32)
            s = s + bias_ref[case, hp]
            e = jnp.exp(s - jnp.max(s, axis=-1, keepdims=True))
            l = jnp.sum(e, axis=-1, keepdims=True)
            o = _dot(e.astype(_BF16), v2) / l
            o_ref[0, pl.ds(q0, nq), c0:c0 + pair_w] = jnp.where(lo_half, o[:nq], o[nq:])
        return carry

    lax.fori_loop(0, n_pairs, body, 0, unroll=2)


def _natten(qkv, rpb):
    B, L, _ = qkv.shape
    rows = L // GRID_W
    assert rows % Q_ROWS == 0 and rows >= Q_ROWS + WIN_ROWS - 1
    cases, case_of, kstart, n_keys = _natten_tables(rows)
    n_case = cases.shape[0]
    nq = Q_ROWS * GRID_W
    bias = _natten_bias(rpb, cases).reshape(n_case, NA_HEADS // 2, 2 * nq, n_keys)
    grid_spec = pltpu.PrefetchScalarGridSpec(
        num_scalar_prefetch=2,
        grid=(B,),
        in_specs=[pl.BlockSpec((1, L, 3 * NA_WIDTH), lambda b, *_: (b, 0, 0)),
                  _resident((n_case, NA_HEADS // 2, 2 * nq, n_keys), lambda b, *_: (0, 0, 0, 0))],
        out_specs=pl.BlockSpec((1, L, NA_WIDTH), lambda b, *_: (b, 0, 0)),
    )
    return pl.pallas_call(
        functools.partial(_natten_kernel, n_pairs=rows // Q_ROWS, n_keys=n_keys),
        grid_spec=grid_spec,
        out_shape=jax.ShapeDtypeStruct((B, L, NA_WIDTH), _F32),
        compiler_params=_cparams("arbitrary"),
        name="natten",
    )(jnp.asarray(case_of), jnp.asarray(kstart), qkv, bias)


_META_ID, _META_GATE, _META_RANK = 0, 2, 4
ROUTER_ROWS = 8 + N_EXPERTS


def _split_bf16(v):
    hi = v.astype(_BF16)
    return hi, (v - hi.astype(_F32)).astype(_BF16)


def _out_router_kernel(yhe_ref, yho_ref, sel_ref, yna_ref, ghy_ref, gna_ref, wtop_ref, wbot_ref, x_ref, gffn_ref,
                       rwh_ref, rwl_ref, rb_ref, h1_ref, xn_ref, meta_ref, cnt_ref, carry_ref):
    @pl.when(pl.program_id(0) == 0)
    def _():
        carry_ref[...] = jnp.zeros_like(carry_ref)

    nhe = _rmsnorm(yhe_ref[...], ghy_ref[...]).astype(_BF16)
    nho = _rmsnorm(yho_ref[...], ghy_ref[...]).astype(_BF16)
    nh = (_dot(sel_ref[0], nhe) + _dot(sel_ref[1], nho)).astype(_BF16)
    nn = _rmsnorm(yna_ref[...], gna_ref[...]).astype(_BF16)
    h1 = x_ref[...] + (_dot(nh, wtop_ref[...]) + _dot(nn, wbot_ref[...]))
    h1_ref[...] = h1
    xn = _rmsnorm(h1, gffn_ref[...])
    xn_ref[...] = xn
    tm = xn.shape[0]

    xh, xl = _split_bf16(xn)
    logits = _dot(xh, rwh_ref[...]) + (_dot(xh, rwl_ref[...]) + _dot(xl, rwh_ref[...]))
    lt = logits.T[:ROUTER_ROWS] + rb_ref[:, 0:1]

    neg = -jnp.inf
    row8 = lax.broadcasted_iota(jnp.int32, (EXPERTS_PER_GROUP, tm), 0).astype(_F32)
    col_max = lambda v: jnp.max(v, axis=0, keepdims=True)
    first_max = lambda v, m: jnp.min(jnp.where(v == m, row8, float(EXPERTS_PER_GROUP)), axis=0, keepdims=True)

    gl = jnp.where(row8 < N_GROUPS, lt[0:8], neg)
    gmax = col_max(gl)
    g_w = 1.0 / jnp.sum(jnp.exp(gl - gmax), axis=0, keepdims=True)
    g_idx = first_max(gl, gmax)
    el = lt[8:8 + EXPERTS_PER_GROUP]
    for g in range(1, N_GROUPS):
        el = jnp.where(g_idx == g, lt[8 + g * EXPERTS_PER_GROUP:8 + (g + 1) * EXPERTS_PER_GROUP], el)
    m1 = col_max(el)
    i1 = first_max(el, m1)
    el2 = jnp.where(row8 == i1, neg, el)
    m2 = col_max(el2)
    i2 = first_max(el2, m2)
    r = jnp.exp(m2 - m1)
    gate1 = g_w / (1.0 + r)
    gate2 = g_w * r / (1.0 + r)
    id1 = g_idx * EXPERTS_PER_GROUP + i1
    id2 = g_idx * EXPERTS_PER_GROUP + i2

    row_e = lax.broadcasted_iota(jnp.int32, (N_EXPERTS, tm), 0).astype(_F32)
    sel1 = row_e == id1
    sel2 = row_e == id2
    onehot = jnp.where(sel1 | sel2, 1.0, 0.0)
    tri_r = lax.broadcasted_iota(jnp.int32, (tm, tm), 0)
    tri_c = lax.broadcasted_iota(jnp.int32, (tm, tm), 1)
    tri = jnp.where(tri_r < tri_c, 1.0, 0.0).astype(_BF16)
    carry = carry_ref[:, 0:1]
    before = carry + _dot(onehot.astype(_BF16), tri)
    rank1 = jnp.sum(jnp.where(sel1, before, 0.0), axis=0, keepdims=True)
    rank2 = jnp.sum(jnp.where(sel2, before, 0.0), axis=0, keepdims=True)
    new_carry = carry + jnp.sum(onehot, axis=1, keepdims=True)
    carry_ref[...] = jnp.broadcast_to(new_carry, carry_ref.shape)
    cnt_ref[...] = jnp.broadcast_to(new_carry, cnt_ref.shape)

    meta = jnp.zeros((8, tm), _F32)
    for k, v in enumerate((id1, id2, gate1, gate2, rank1, rank2)):
        meta = jnp.where(row8 == k, v, meta)
    meta_ref[...] = meta


def _out_router(y_hy_even, y_hy_odd, y_na, g_hy, g_na, w_out, x2, g_ffn, wg, bg, we, be):
    T, D = x2.shape
    half = ROW_TILE // 2
    sel_t = _parity_select(ROW_TILE).transpose(0, 2, 1)
    gpad = 8 - N_GROUPS
    rw = jnp.concatenate([wg.astype(_F32), jnp.zeros((D, gpad), _F32), we.astype(_F32)], axis=1)
    rw_hi, rw_lo = _split_bf16(jnp.pad(rw, ((0, 0), (0, ROUTER_LANES - ROUTER_ROWS))))
    rb = jnp.concatenate([bg.astype(_F32), jnp.zeros((gpad,), _F32), be.astype(_F32)])
    rb = jnp.broadcast_to(rb[:, None], (ROUTER_ROWS, ROUTER_LANES))
    w_bf = w_out.astype(_BF16)
    row = lambda n: pl.BlockSpec((ROW_TILE, n), lambda i: (i, 0))
    vec = lambda n: pl.BlockSpec((1, n), lambda i: (0, 0))
    mat = lambda r, c: pl.BlockSpec((r, c), lambda i: (0, 0))
    return pl.pallas_call(
        _out_router_kernel,
        grid=(T // ROW_TILE,),
        in_specs=[pl.BlockSpec((half, HY_WIDTH), lambda i: (i, 0)), pl.BlockSpec((half, HY_WIDTH), lambda i: (i, 0)),
                  pl.BlockSpec((2, ROW_TILE, half), lambda i: (0, 0, 0)),
                  row(NA_WIDTH), vec(HY_WIDTH), vec(NA_WIDTH),
                  mat(HY_WIDTH, D), mat(NA_WIDTH, D), row(D), vec(D),
                  mat(D, ROUTER_LANES), mat(D, ROUTER_LANES), mat(ROUTER_ROWS, ROUTER_LANES)],
        out_specs=[row(D),
                   row(D),
                   pl.BlockSpec((8, ROW_TILE), lambda i: (0, i)),
                   mat(N_EXPERTS, ROUTER_LANES)],
        out_shape=[jax.ShapeDtypeStruct((T, D), _F32),
                   jax.ShapeDtypeStruct((T, D), _F32),
                   jax.ShapeDtypeStruct((8, T), _F32),
                   jax.ShapeDtypeStruct((N_EXPERTS, ROUTER_LANES), _F32)],
        scratch_shapes=[pltpu.VMEM((N_EXPERTS, ROUTER_LANES), _F32)],
        compiler_params=_cparams("arbitrary"),
        name="out_proj_router",
    )(y_hy_even, y_hy_odd, sel_t, y_na, g_hy.reshape(1, -1), g_na.reshape(1, -1), w_bf[:HY_WIDTH], w_bf[HY_WIDTH:],
      x2, g_ffn.reshape(1, D), rw_hi, rw_lo, rb)


def _scatter_kernel(zblk_ref, dest_ref, xn_ref, xb_ref, zero_ref, sem, zero_sem):
    tm = xn_ref.shape[0]
    row_copy = lambda src, d: pltpu.make_async_copy(src, xb_ref.at[pl.ds(d, 1)], sem)

    @pl.when(pl.program_id(0) == 0)
    def _():
        zero_ref[...] = jnp.zeros_like(zero_ref)

        def blk_copy(j):
            start = pl.multiple_of(zblk_ref[j] * MOE_BLOCK, MOE_BLOCK)
            return pltpu.make_async_copy(zero_ref, xb_ref.at[pl.ds(start, MOE_BLOCK)], zero_sem)

        def start(j, c):
            @pl.when(zblk_ref[j] >= 0)
            def _():
                blk_copy(j).start()
            return c

        def wait(j, c):
            @pl.when(zblk_ref[j] >= 0)
            def _():
                blk_copy(j).wait()
            return c

        lax.fori_loop(0, zblk_ref.shape[0], start, 0)
        lax.fori_loop(0, zblk_ref.shape[0], wait, 0)

    def start(t, c):
        for k in range(TOP_K):
            row_copy(xn_ref.at[pl.ds(t, 1)], dest_ref[0, 0, TOP_K * t + k]).start()
        return c

    lax.fori_loop(0, tm, start, 0, unroll=DMA_UNROLL)
    all_rows = xb_ref.at[pl.ds(0, TOP_K * tm)]
    pltpu.make_async_copy(all_rows, all_rows, sem).wait()


def _moe_scatter(xn, dest, zero_blocks, n_slots):
    T, D = xn.shape
    nt = T // ROW_TILE
    grid_spec = pltpu.PrefetchScalarGridSpec(
        num_scalar_prefetch=1,
        grid=(nt,),
        in_specs=[pl.BlockSpec((1, 1, TOP_K * ROW_TILE), lambda i, *_: (i, 0, 0), memory_space=pltpu.SMEM),
                  pl.BlockSpec((ROW_TILE, D), lambda i, *_: (i, 0))],
        out_specs=pl.BlockSpec(memory_space=pl.ANY),
        scratch_shapes=[pltpu.VMEM((MOE_BLOCK, D), _F32), pltpu.SemaphoreType.DMA(()),
                        pltpu.SemaphoreType.DMA(())],
    )
    return pl.pallas_call(
        _scatter_kernel,
        grid_spec=grid_spec,
        out_shape=jax.ShapeDtypeStruct((n_slots, D), _F32),
        compiler_params=_cparams("arbitrary"),
        name="moe_scatter",
    )(zero_blocks, dest.reshape(nt, 1, TOP_K * ROW_TILE), xn)


def _expert_kernel(be_ref, nblk_ref, xb_ref, w1_ref, w3_ref, w2_ref, yb_ref, wb1_ref, wb3_ref, wb2_ref):
    i = pl.program_id(0)

    @pl.when(i < nblk_ref[0])
    def _():
        e = be_ref[i]
        e_prev = be_ref[jnp.maximum(i - 1, 0)]

        @pl.when((i == 0) | (e != e_prev))
        def _():
            wb1_ref[...] = w1_ref[0].astype(_BF16)
            wb3_ref[...] = w3_ref[0].astype(_BF16)
            wb2_ref[...] = w2_ref[0].astype(_BF16)

        x = xb_ref[...].astype(_BF16)
        a = _dot(x, wb1_ref[...])
        hid = (a * jax.nn.sigmoid(a) * _dot(x, wb3_ref[...])).astype(_BF16)
        yb_ref[...] = _dot(hid, wb2_ref[...])

    @pl.when(i >= nblk_ref[0])
    def _():
        yb_ref[...] = jnp.zeros_like(yb_ref)


def _moe_experts(xb, block_e, n_blk, w1, w3, w2):
    n_blocks = xb.shape[0] // MOE_BLOCK
    _, D, DE = w1.shape
    live = lambda i, nb: jnp.minimum(i, nb[0] - 1)
    grid_spec = pltpu.PrefetchScalarGridSpec(
        num_scalar_prefetch=2,
        grid=(n_blocks,),
        in_specs=[pl.BlockSpec((MOE_BLOCK, D), lambda i, be, nb: (live(i, nb), 0)),
                  pl.BlockSpec((1, D, DE), lambda i, be, nb: (be[live(i, nb)], 0, 0)),
                  pl.BlockSpec((1, D, DE), lambda i, be, nb: (be[live(i, nb)], 0, 0)),
                  pl.BlockSpec((1, DE, D), lambda i, be, nb: (be[live(i, nb)], 0, 0))],
        out_specs=pl.BlockSpec((MOE_BLOCK, D), lambda i, be, nb: (i, 0)),
        scratch_shapes=[pltpu.VMEM((D, DE), _BF16), pltpu.VMEM((D, DE), _BF16), pltpu.VMEM((DE, D), _BF16)],
    )
    return pl.pallas_call(
        _expert_kernel,
        grid_spec=grid_spec,
        out_shape=jax.ShapeDtypeStruct(xb.shape, _F32),
        compiler_params=_cparams("arbitrary"),
        name="moe_experts",
    )(block_e, n_blk, xb, w1, w3, w2)


def _final_kernel(dest_ref, dest_next_ref, h1_ref, gates_ref, yb_ref, p_ref, wg_ref, wp_ref, gple_ref, gfin_ref,
                  o_ref, ybuf_ref, sem):
    tm = h1_ref.shape[0]
    i = pl.program_id(0)
    slot = lax.rem(i, 2)

    def gather(d_ref, s):
        def start(t, c):
            for k in range(TOP_K):
                pltpu.make_async_copy(yb_ref.at[pl.ds(d_ref[0, 0, TOP_K * t + k], 1)],
                                      ybuf_ref.at[s, k, pl.ds(t, 1)], sem.at[s]).start()
            return c
        lax.fori_loop(0, tm, start, 0, unroll=DMA_UNROLL)

    @pl.when(i == 0)
    def _():
        gather(dest_ref, 0)

    @pl.when(i + 1 < pl.num_programs(0))
    def _():
        gather(dest_next_ref, 1 - slot)

    pltpu.make_async_copy(ybuf_ref.at[slot], ybuf_ref.at[slot], sem.at[slot]).wait()

    gates = gates_ref[...]
    moe = ybuf_ref[slot, 0] * gates[:, 0:1] + ybuf_ref[slot, 1] * gates[:, 1:2]
    h2 = h1_ref[...] + moe
    gate = jax.nn.sigmoid(_dot(_rmsnorm(h2, gple_ref[...]).astype(_BF16), wg_ref[...]))
    h3 = h2 + _dot(p_ref[...].astype(_BF16), wp_ref[...]) * gate
    o_ref[...] = _rmsnorm(h3, gfin_ref[...])


def _final(dest, h1, gates, yb, p2, w_gate, w_proj, g_ple, g_final):
    T, D = h1.shape
    nt = T // ROW_TILE
    PD = p2.shape[1]
    row = lambda n: pl.BlockSpec((ROW_TILE, n), lambda i: (i, 0))
    vec = lambda n: pl.BlockSpec((1, n), lambda i: (0, 0))
    dest3 = dest.reshape(nt, 1, TOP_K * ROW_TILE)
    return pl.pallas_call(
        _final_kernel,
        grid=(nt,),
        in_specs=[pl.BlockSpec((1, 1, TOP_K * ROW_TILE), lambda i: (i, 0, 0), memory_space=pltpu.SMEM),
                  pl.BlockSpec((1, 1, TOP_K * ROW_TILE), lambda i: (jnp.minimum(i + 1, nt - 1), 0, 0),
                               memory_space=pltpu.SMEM),
                  row(D), row(ROUTER_LANES),
                  pl.BlockSpec(memory_space=pl.ANY),
                  row(PD),
                  pl.BlockSpec((D, D), lambda i: (0, 0)),
                  pl.BlockSpec((PD, D), lambda i: (0, 0)),
                  vec(D), vec(D)],
        out_specs=row(D),
        out_shape=jax.ShapeDtypeStruct((T, D), _F32),
        scratch_shapes=[pltpu.VMEM((2, TOP_K, ROW_TILE, D), _F32), pltpu.SemaphoreType.DMA((2,))],
        compiler_params=_cparams("arbitrary"),
        name="moe_gather_ple_final",
    )(dest3, dest3, h1, gates, yb, p2, w_gate.astype(_BF16), w_proj.astype(_BF16),
      g_ple.reshape(1, D), g_final.reshape(1, D))


def _routing_tables(meta_t, counts_f, n_blocks):
    ids = meta_t[_META_ID:_META_ID + TOP_K].T.astype(jnp.int32)
    rank = meta_t[_META_RANK:_META_RANK + TOP_K].T.astype(jnp.int32)
    counts = counts_f[:, 0].astype(jnp.int32)
    padded = (counts + MOE_BLOCK - 1) // MOE_BLOCK * MOE_BLOCK
    e_iota = jnp.arange(N_EXPERTS, dtype=jnp.int32)
    pad_end = jnp.sum(jnp.where(e_iota[None, :] <= e_iota[:, None], padded[None, :], 0), axis=1)
    pad_start = pad_end - padded
    dest = jnp.sum(jnp.where(ids[..., None] == e_iota, pad_start, 0), axis=-1) + rank
    blk_row = jnp.arange(n_blocks, dtype=jnp.int32)[:, None] * MOE_BLOCK
    block_e = jnp.minimum(jnp.sum((pad_end[None, :] <= blk_row).astype(jnp.int32), axis=1), N_EXPERTS - 1)
    n_blk = (pad_end[-1:] // MOE_BLOCK).astype(jnp.int32)
    seg_last = jnp.where(padded > counts, pad_end // MOE_BLOCK - 1, -1)
    tail = n_blk[0] + e_iota
    zero_blocks = jnp.concatenate([seg_last, jnp.where(tail < n_blocks, tail, -1)]).astype(jnp.int32)
    return dest.astype(jnp.int32), zero_blocks, block_e, n_blk


def _one_layer(h, p, g_mix, w_in, hy_conv_w, hy_conv_b, hy_f_w1, hy_f_b1, hy_f_freq1, hy_f_w2, hy_f_b2,
               hy_f_freq2, hy_f_w3, hy_skip, na_rpb, g_out_hy, g_out_na, w_out, g_ffn, router_wg, router_bg,
               router_we, router_be, exp_w1, exp_w3, exp_w2, g_ple, w_ple_gate, w_ple_proj):
    B, L, D = h.shape
    T = B * L
    assert T % ROW_TILE == 0 and L % FREQ_CHUNK == 0 and w_in.shape[1] == HY_COLS + 3 * NA_WIDTH
    x2 = h.reshape(T, D)
    u2, qkv = _in_proj(x2, g_mix, w_in)

    hf2 = _filter_mlp(L, hy_f_w1, hy_f_b1, hy_f_freq1, hy_f_w2, hy_f_b2, hy_f_freq2, hy_f_w3)
    tables = _half_dft_tables(L)
    k1r, k1i, k2r, k2i = _filter_spec(tables, hf2)
    y_hy_even, y_hy_odd = _hyena(u2.reshape(B, L // 2, 2 * HY_COLS), hy_conv_w.astype(_F32), hy_conv_b.astype(_F32),
                                 tables, k1r, k1i, k2r, k2i, hy_skip.astype(_F32))
    y_na = _natten(qkv.reshape(B, L, 3 * NA_WIDTH), na_rpb)

    h1, xn, meta, counts = _out_router(y_hy_even.reshape(T // 2, HY_WIDTH), y_hy_odd.reshape(T // 2, HY_WIDTH),
                                       y_na.reshape(T, NA_WIDTH), g_out_hy, g_out_na,
                                       w_out, x2, g_ffn, router_wg, router_bg, router_we, router_be)
    n_blocks = -(-(T * TOP_K + N_EXPERTS * (MOE_BLOCK - 1)) // MOE_BLOCK)
    dest, zero_blocks, block_e, n_blk = _routing_tables(meta, counts, n_blocks)
    xb = _moe_scatter(xn, dest, zero_blocks, n_blocks * MOE_BLOCK)
    yb = _moe_experts(xb, block_e, n_blk, exp_w1, exp_w3, exp_w2)
    gates = jnp.pad(meta[_META_GATE:_META_GATE + TOP_K].T, ((0, 0), (0, ROUTER_LANES - TOP_K)))
    return h1, dest, gates, yb


def kernel(x, p, g_mix, w_in, hy_conv_w, hy_conv_b, hy_f_w1, hy_f_b1, hy_f_freq1, hy_f_w2, hy_f_b2, hy_f_freq2, hy_f_w3, hy_skip, na_rpb, g_out_hy, g_out_na, w_out, g_ffn, router_wg, router_bg, router_we, router_be, exp_w1, exp_w3, exp_w2, g_ple, w_ple_gate, w_ple_proj, g_final):
    depth = p.shape[0]
    assert depth == 1, "the final RMSNorm is fused into the last layer's kernel; one layer is supported"
    B, L, D = x.shape
    i = 0
    h1, dest, gates, yb = _one_layer(
        x, p[i], g_mix[i], w_in[i], hy_conv_w[i], hy_conv_b[i], hy_f_w1[i], hy_f_b1[i], hy_f_freq1[i],
        hy_f_w2[i], hy_f_b2[i], hy_f_freq2[i], hy_f_w3[i], hy_skip[i], na_rpb[i], g_out_hy[i], g_out_na[i],
        w_out[i], g_ffn[i], router_wg[i], router_bg[i], router_we[i], router_be[i], exp_w1[i], exp_w3[i],
        exp_w2[i], g_ple[i], w_ple_gate[i], w_ple_proj[i])
    out = _final(dest, h1, gates, yb, p[i].reshape(B * L, -1), w_ple_gate[i], w_ple_proj[i], g_ple[i], g_final)
    return out.reshape(B, L, D)
```

```python
import functools
import math

import numpy as np
import jax
import jax.numpy as jnp
from jax import lax
from jax.experimental import pallas as pl
from jax.experimental.pallas import tpu as pltpu

_F32 = jnp.float32
_BF16 = jnp.bfloat16

GRID_W = 64
HY_WIDTH = 512
NA_WIDTH = 512
NA_HEADS = 8
NA_HEAD_DIM = 64
HY_ORDER = 2
SHORT_CONV = 3
FILTER_EMB = 33
FILTER_BANDS = (FILTER_EMB - 1) // 2
DECAY_TARGET = 1e-2
FAST_DECAY_PCT = 0.3
SLOW_DECAY_PCT = 1.5
WIN_ROWS = 8
WIN_COLS = 16
Q_ROWS = 2
N_GROUPS = 4
EXPERTS_PER_GROUP = 8
N_EXPERTS = N_GROUPS * EXPERTS_PER_GROUP
TOP_K = 2
MOE_BLOCK = 512
EPS = 1e-6
NEG_INF = -1e30
HY_COLS = (HY_ORDER + 1) * HY_WIDTH

V7X_LANES = 128
V7X_SUBLANES = 8
V7X_VMEM_LIMIT_BYTES = 56 * 2 ** 20

ROW_TILE = 512
FREQ_CHUNK = 512
HY_CH_TILE = 256
ROUTER_LANES = V7X_LANES
MOE_CHUNK = V7X_SUBLANES
SORT_ROWS = -(-(TOP_K * ROW_TILE + N_EXPERTS * (MOE_CHUNK - 1)) // V7X_LANES) * V7X_LANES
SORT_CHUNKS = SORT_ROWS // MOE_CHUNK


def _cparams(*sem):
    return pltpu.CompilerParams(dimension_semantics=sem, vmem_limit_bytes=V7X_VMEM_LIMIT_BYTES)


def _resident(shape, index_map):
    return pl.BlockSpec(shape, index_map, pipeline_mode=pl.Buffered(1))


def _rmsnorm(x, g):
    return x * lax.rsqrt(jnp.mean(x * x, axis=-1, keepdims=True) + EPS) * g


def _dot(a, b):
    return jnp.dot(a, b, preferred_element_type=_F32)


def _dot_f32(a, b):
    return jnp.dot(a, b, preferred_element_type=_F32, precision=lax.Precision.HIGHEST)


def _parity_select(n):
    r = jnp.arange(n // 2, dtype=jnp.int32)[None, :, None]
    t = jnp.arange(n, dtype=jnp.int32)[None, None, :]
    p = jnp.arange(2, dtype=jnp.int32)[:, None, None]
    return (t == 2 * r + p).astype(_BF16)


def _in_proj_kernel(x_ref, g_ref, w_ref, sel_ref, u2_ref, qkv_ref):
    xn = _rmsnorm(x_ref[...], g_ref[...]).astype(_BF16)
    n_hy = u2_ref.shape[1] // 2
    xe = _dot(sel_ref[0], xn).astype(_BF16)
    xo = _dot(sel_ref[1], xn).astype(_BF16)
    for c0 in range(0, n_hy, 512):
        w = w_ref[:, c0:c0 + 512]
        u2_ref[:, c0:c0 + 512] = _dot(xe, w).astype(_BF16)
        u2_ref[:, n_hy + c0:n_hy + c0 + 512] = _dot(xo, w).astype(_BF16)
    for c0 in range(0, qkv_ref.shape[1], 512):
        qkv_ref[:, c0:c0 + 512] = _dot(xn, w_ref[:, n_hy + c0:n_hy + c0 + 512]).astype(_BF16)


def _in_proj(x2, g_mix, w_in):
    T, D = x2.shape
    n_in = w_in.shape[1]
    n_qkv = n_in - HY_COLS
    half = ROW_TILE // 2
    return pl.pallas_call(
        _in_proj_kernel,
        grid=(T // ROW_TILE,),
        in_specs=[pl.BlockSpec((ROW_TILE, D), lambda i: (i, 0)),
                  pl.BlockSpec((1, D), lambda i: (0, 0)),
                  _resident((D, n_in), lambda i: (0, 0)),
                  pl.BlockSpec((2, half, ROW_TILE), lambda i: (0, 0, 0))],
        out_specs=[pl.BlockSpec((half, 2 * HY_COLS), lambda i: (i, 0)),
                   pl.BlockSpec((ROW_TILE, n_qkv), lambda i: (i, 0))],
        out_shape=[jax.ShapeDtypeStruct((T // 2, 2 * HY_COLS), _BF16),
                   jax.ShapeDtypeStruct((T, n_qkv), _BF16)],
        compiler_params=_cparams("arbitrary"),
        name="in_proj",
    )(x2, g_mix.reshape(1, D), w_in.astype(_BF16), _parity_select(ROW_TILE))


def _filter_mlp_kernel(z_ref, w1_ref, b1_ref, f1_ref, w2_ref, b2_ref, f2_ref, w3_ref, delta_ref, hf_ref):
    z = z_ref[0]
    hid = jnp.sin(f1_ref[...] * (_dot_f32(z, w1_ref[...]) + b1_ref[...]))
    hid = jnp.sin(f2_ref[...] * (_dot_f32(hid, w2_ref[...]) + b2_ref[...]))
    hf = _dot_f32(hid, w3_ref[...])
    decay = jnp.exp(-z[:, 0:1] * delta_ref[...])
    tl = z.shape[0]
    row = lax.broadcasted_iota(jnp.int32, (tl, HY_WIDTH), 0)
    offset0 = (row == 0) & (pl.program_id(0) == 0) & (pl.program_id(1) == 0)
    for k in range(2 * HY_ORDER):
        blk = hf[:, k * HY_WIDTH:(k + 1) * HY_WIDTH] * decay
        if k >= HY_ORDER:
            blk = jnp.where(offset0, 0.0, blk)
        hf_ref[:, k * HY_WIDTH:(k + 1) * HY_WIDTH] = blk


def _filter_mlp(L, w1, b1, f1, w2, b2, f2, w3):
    t = jnp.linspace(0.0, 1.0, L, dtype=_F32)[:, None]
    w = 2.0 * math.pi * jnp.arange(L, dtype=_F32)[:, None] / L
    bands = jnp.linspace(1e-4, FILTER_BANDS - 1, FILTER_BANDS, dtype=_F32)[None, :]
    z = jnp.concatenate([t, jnp.cos(bands * w), -jnp.sin(bands * w)], axis=-1)
    z = jnp.pad(z, ((0, 0), (0, V7X_LANES - FILTER_EMB)))
    z = jnp.stack([z[0::2], z[1::2]])
    w1p = jnp.pad(w1.astype(_F32), ((0, V7X_LANES - FILTER_EMB), (0, 0)))
    max_decay = math.log(DECAY_TARGET) / FAST_DECAY_PCT
    min_decay = math.log(DECAY_TARGET) / SLOW_DECAY_PCT
    deltas = jnp.abs(jnp.linspace(min_decay, max_decay, HY_WIDTH, dtype=_F32))[None, :]
    hid = w1.shape[1]
    n_out = w3.shape[1]
    lh = L // 2
    tl = min(lh, ROW_TILE)
    full = lambda shape: pl.BlockSpec(shape, lambda p, i: (0, 0))
    return pl.pallas_call(
        _filter_mlp_kernel,
        grid=(2, lh // tl),
        in_specs=[pl.BlockSpec((1, tl, V7X_LANES), lambda p, i: (p, i, 0)),
                  full((V7X_LANES, hid)), full((1, hid)), full((1, hid)),
                  full((hid, hid)), full((1, hid)), full((1, hid)),
                  full((hid, n_out)), full((1, HY_WIDTH))],
        out_specs=pl.BlockSpec((tl, n_out), lambda p, i: (i, p)),
        out_shape=jax.ShapeDtypeStruct((lh, 2 * n_out), _F32),
        compiler_params=_cparams("arbitrary", "arbitrary"),
        name="hyena_filter_mlp",
    )(z, w1p, b1.reshape(1, hid), f1.reshape(1, hid), w2.astype(_F32), b2.reshape(1, hid),
      f2.reshape(1, hid), w3.astype(_F32), deltas)


def _freq_chunks(gp):
    return [(r0, min(FREQ_CHUNK, gp - r0)) for r0 in range(0, gp, FREQ_CHUNK)]


def _half_dft_tables(L):
    lh, n, sb = L // 2, 2 * L, 32
    gp = -(-(lh + 1) // V7X_SUBLANES) * V7X_SUBLANES
    assert lh % sb == 0
    g = jnp.arange(gp, dtype=jnp.int32)
    live = (g <= lh)[:, None]
    ang = lambda s: ((g[:, None] * s[None, :]) % L).astype(_F32) * (2.0 * math.pi / L)
    a_hi = ang(jnp.arange(lh // sb, dtype=jnp.int32) * sb)
    a_lo = ang(jnp.arange(sb, dtype=jnp.int32))
    c_hi, s_hi, c_lo, s_lo = jnp.cos(a_hi), jnp.sin(a_hi), jnp.cos(a_lo), jnp.sin(a_lo)
    cosm = (c_hi[:, :, None] * c_lo[:, None, :] - s_hi[:, :, None] * s_lo[:, None, :]).reshape(gp, lh)
    sinm = (s_hi[:, :, None] * c_lo[:, None, :] + c_hi[:, :, None] * s_lo[:, None, :]).reshape(gp, lh)
    cosm = jnp.where(live, cosm, 0.0)
    sinm = jnp.where(live, sinm, 0.0)
    ff = jnp.concatenate([cosm, -sinm], axis=0).astype(_BF16)
    wgt = jnp.where((g == 0) | (g == lh), 1.0, 2.0)[:, None] / n
    gf = jnp.concatenate([(cosm * wgt).T, (-sinm * wgt).T], axis=1).astype(_BF16)
    tw = g.astype(_F32)[:, None] * (2.0 * math.pi / n)
    wr = jnp.broadcast_to(jnp.cos(tw), (gp, HY_CH_TILE))
    wi = jnp.broadcast_to(-jnp.sin(tw), (gp, HY_CH_TILE))
    return ff, gf, wr, wi


def _half_spectra(ff_ref, wr, wi, xe, xo, r0, n):
    gp = ff_ref.shape[0] // 2
    fc = ff_ref[r0:r0 + n, :]
    fs = ff_ref[gp + r0:gp + r0 + n, :]
    er, ei = _dot(fc, xe), _dot(fs, xe)
    orr, oi = _dot(fc, xo), _dot(fs, xo)
    pr = wr * orr - wi * oi
    pi = wr * oi + wi * orr
    return (er + pr, ei + pi), (er - pr, pi - ei)


def _filter_spec_kernel(ff_ref, wr_ref, wi_ref, fe_ref, fo_ref, be_ref, bo_ref, k1r_ref, k1i_ref, k2r_ref, k2i_ref):
    fe, fo, be, bo = (r[...].astype(_BF16) for r in (fe_ref, fo_ref, be_ref, bo_ref))
    for r0, n in _freq_chunks(wr_ref.shape[0]):
        wr, wi = wr_ref[r0:r0 + n, :], wi_ref[r0:r0 + n, :]
        (f1r, f1i), (f2r, f2i) = _half_spectra(ff_ref, wr, wi, fe, fo, r0, n)
        (b1r, b1i), (b2r, b2i) = _half_spectra(ff_ref, wr, wi, be, bo, r0, n)
        k1r_ref[0, r0:r0 + n, :] = f1r + b1r
        k1i_ref[0, r0:r0 + n, :] = f1i - b1i
        k2r_ref[0, r0:r0 + n, :] = f2r + b2r
        k2i_ref[0, r0:r0 + n, :] = f2i - b2i


def _filter_spec(tables, hf2):
    ff, _, wr, wi = tables
    lh, gp = hf2.shape[0], wr.shape[0]
    n_ct = HY_WIDTH // HY_CH_TILE
    n_blk = hf2.shape[1] // 2 // HY_CH_TILE
    col = lambda blk0: pl.BlockSpec((lh, HY_CH_TILE), lambda j: (0, blk0 + j))
    out = jax.ShapeDtypeStruct((HY_ORDER, gp, HY_WIDTH), _F32)
    ospec = pl.BlockSpec((1, gp, HY_CH_TILE), lambda j: (j // n_ct, 0, j % n_ct))
    full = lambda a: pl.BlockSpec(a.shape, lambda j: (0, 0))
    return pl.pallas_call(
        _filter_spec_kernel,
        grid=(HY_ORDER * n_ct,),
        in_specs=[full(ff), full(wr), full(wi),
                  col(0), col(n_blk),
                  col(HY_ORDER * n_ct), col(n_blk + HY_ORDER * n_ct)],
        out_specs=[ospec] * 4,
        out_shape=[out] * 4,
        compiler_params=_cparams("arbitrary"),
        name="hyena_filter_spectrum",
    )(ff, wr, wi, hf2, hf2, hf2, hf2)


def _parity_conv(ze, zo, w_ref, b_ref):
    lh = ze.shape[0]
    row = lax.broadcasted_iota(jnp.int32, ze.shape, 0)
    zo_prev = jnp.where(row == 0, 0.0, pltpu.roll(zo, 1, axis=0))
    ze_next = jnp.where(row == lh - 1, 0.0, pltpu.roll(ze, lh - 1, axis=0))
    w0, w1, w2, b = w_ref[0:1, :], w_ref[1:2, :], w_ref[2:3, :], b_ref[...]
    return ((b + zo_prev * w0) + ze * w1) + zo * w2, ((b + ze * w0) + zo * w1) + ze_next * w2


def _hyena_kernel(ze_ref, zo_ref, zw_ref, zb_ref, ge_ref, go_ref, gw_ref, gb_ref, skip_ref, ff_ref, gf_ref,
                  wr_ref, wi_ref, k1r_ref, k1i_ref, k2r_ref, k2i_ref, oe_ref, oo_ref, a_ref, b_ref, *, conv_input):
    ze, zo = ze_ref[0].astype(_F32), zo_ref[0].astype(_F32)
    if conv_input:
        ze, zo = _parity_conv(ze, zo, zw_ref, zb_ref)
    xe, xo = ze.astype(_BF16), zo.astype(_BF16)
    gp = wr_ref.shape[0]
    for r0, n in _freq_chunks(gp):
        rows = slice(r0, r0 + n)
        wr, wi = wr_ref[rows, :], wi_ref[rows, :]
        (x1r, x1i), (x2r, x2i) = _half_spectra(ff_ref, wr, wi, xe, xo, r0, n)
        k1r, k1i, k2r, k2i = k1r_ref[0, rows, :], k1i_ref[0, rows, :], k2r_ref[0, rows, :], k2i_ref[0, rows, :]
        y1r, y1i = x1r * k1r - x1i * k1i, x1r * k1i + x1i * k1r
        y2r, y2i = x2r * k2r - x2i * k2i, x2r * k2i + x2i * k2r
        tr, ti = y1r - y2r, y1i + y2i
        a_ref[rows, :] = (y1r + y2r).astype(_BF16)
        a_ref[gp + r0:gp + r0 + n, :] = (y1i - y2i).astype(_BF16)
        b_ref[rows, :] = (tr * wr + ti * wi).astype(_BF16)
        b_ref[gp + r0:gp + r0 + n, :] = (ti * wr - tr * wi).astype(_BF16)
    ge, go = _parity_conv(ge_ref[0].astype(_F32), go_ref[0].astype(_F32), gw_ref, gb_ref)
    skip = skip_ref[0]
    for r0 in range(0, ze.shape[0], FREQ_CHUNK):
        rows = slice(r0, r0 + FREQ_CHUNK)
        g = gf_ref[rows, :]
        oe_ref[0, rows, :] = ge[rows] * (_dot(g, a_ref[...]) + ze[rows] * skip)
        oo_ref[0, rows, :] = go[rows] * (_dot(g, b_ref[...]) + zo[rows] * skip)


def _hyena(u2, conv_w, conv_b, tables, k1r, k1i, k2r, k2i, skip):
    ff, gf, wr, wi = tables
    B, lh, _ = u2.shape
    gp = wr.shape[0]
    assert lh % FREQ_CHUNK == 0
    n_ct = HY_WIDTH // HY_CH_TILE
    odd = HY_COLS // HY_CH_TILE
    conv_b2 = conv_b.reshape(1, HY_COLS)
    skip3 = skip.reshape(HY_ORDER, 1, HY_WIDTH)
    col = lambda blk0: pl.BlockSpec((1, lh, HY_CH_TILE), lambda c, b: (b, 0, blk0 + c))
    cw = lambda blk0: pl.BlockSpec((SHORT_CONV, HY_CH_TILE), lambda c, b: (0, blk0 + c))
    cb = lambda blk0: pl.BlockSpec((1, HY_CH_TILE), lambda c, b: (0, blk0 + c))
    const = lambda a: _resident(a.shape, lambda c, b: (0, 0))
    out = jax.ShapeDtypeStruct((B, lh, HY_WIDTH), _F32)
    ze_arr, zo_arr, ze_spec, zo_spec = u2, u2, col(2 * n_ct), col(odd + 2 * n_ct)
    for o in range(HY_ORDER):
        coef = lambda: pl.BlockSpec((1, gp, HY_CH_TILE), lambda c, b, o=o: (o, 0, c), pipeline_mode=pl.Buffered(1))
        ze_arr, zo_arr = pl.pallas_call(
            functools.partial(_hyena_kernel, conv_input=(o == 0)),
            grid=(n_ct, B),
            in_specs=[ze_spec, zo_spec, cw(2 * n_ct), cb(2 * n_ct),
                      col(o * n_ct), col(odd + o * n_ct), cw(o * n_ct), cb(o * n_ct),
                      pl.BlockSpec((1, 1, HY_CH_TILE), lambda c, b, o=o: (o, 0, c)),
                      const(ff), const(gf), const(wr), const(wi), coef(), coef(), coef(), coef()],
            out_specs=[col(0), col(0)],
            out_shape=[out, out],
            scratch_shapes=[pltpu.VMEM((2 * gp, HY_CH_TILE), _BF16), pltpu.VMEM((2 * gp, HY_CH_TILE), _BF16)],
            compiler_params=_cparams("arbitrary", "arbitrary"),
            name=f"hyena_order_{o}",
        )(ze_arr, zo_arr, conv_w, conv_b2, u2, u2, conv_w, conv_b2, skip3, ff, gf, wr, wi, k1r, k1i, k2r, k2i)
        ze_spec = zo_spec = col(0)
    return ze_arr, zo_arr


def _natten_tables(rows):
    kr = min(WIN_ROWS, rows)
    krb = min(Q_ROWS + kr - 1, rows)
    rs = np.clip(np.arange(rows) - kr // 2, 0, rows - kr)
    cs = np.clip(np.arange(GRID_W) - WIN_COLS // 2, 0, GRID_W - WIN_COLS)
    qc = np.tile(np.arange(GRID_W), Q_ROWS)[:, None]
    kc = np.tile(np.arange(GRID_W), krb)[None, :]
    cases, case_of, kstart = [], [], []
    for p in range(rows // Q_ROWS):
        k_r0 = min(rs[p * Q_ROWS], rows - krb)
        qr = (p * Q_ROWS + np.repeat(np.arange(Q_ROWS), GRID_W))[:, None]
        kr_ = (k_r0 + np.repeat(np.arange(krb), GRID_W))[None, :]
        valid = ((kr_ >= rs[qr]) & (kr_ < rs[qr] + kr) & (kc >= cs[qc]) & (kc < cs[qc] + WIN_COLS))
        dr = np.clip(kr_ - qr + WIN_ROWS - 1, 0, 2 * WIN_ROWS - 2)
        dc = np.clip(kc - qc + WIN_COLS - 1, 0, 2 * WIN_COLS - 2)
        idx = np.where(valid, dr * (2 * WIN_COLS - 1) + dc, -1).astype(np.int32)
        for ci, c in enumerate(cases):
            if np.array_equal(c, idx):
                break
        else:
            ci = len(cases)
            cases.append(idx)
        case_of.append(ci)
        kstart.append(k_r0 * GRID_W)
    return np.stack(cases), np.asarray(case_of, np.int32), np.asarray(kstart, np.int32), krb * GRID_W


def _natten_bias(rpb, cases):
    n_case, nq, nk = cases.shape
    krb = nk // GRID_W
    n_dr, n_dc = 2 * WIN_ROWS - 1, 2 * WIN_COLS - 1
    c5 = cases.reshape(n_case, Q_ROWS, GRID_W, krb, GRID_W)
    dr_blk = np.where(c5 >= 0, c5 // n_dc, -1).max(axis=(2, 4))
    dc_idx = np.clip(np.arange(GRID_W)[None, :] - np.arange(GRID_W)[:, None] + WIN_COLS - 1, 0, n_dc - 1)
    assert np.all((c5 < 0) | (c5 // n_dc == dr_blk[:, :, None, :, None]))
    assert np.all((c5 < 0) | (c5 % n_dc == dc_idx[None, None, :, None, :]))
    onehot_c = (dc_idx.reshape(1, -1) == np.arange(n_dc)[:, None]).astype(np.float32)
    sel = (np.maximum(dr_blk, 0).reshape(-1, 1) == np.arange(n_dr)[None, :]).astype(np.float32)
    hi = lax.Precision.HIGHEST
    toe = jnp.einsum('hdj,jx->hdx', rpb.astype(_F32), onehot_c, precision=hi)
    blk = jnp.einsum('sd,hdx->hsx', sel, toe, precision=hi)
    blk = blk.reshape(NA_HEADS, n_case, Q_ROWS, krb, GRID_W, GRID_W).transpose(1, 0, 2, 4, 3, 5)
    return jnp.where((cases >= 0)[:, None], blk.reshape(n_case, NA_HEADS, nq, nk), NEG_INF)


def _natten_kernel(case_ref, kstart_ref, qkv_ref, bias_ref, o_ref, *, n_pairs, n_keys):
    nq = Q_ROWS * GRID_W
    pair_w = 2 * NA_HEAD_DIM
    lane = lax.broadcasted_iota(jnp.int32, (nq, pair_w), 1)
    lo_half = lane < NA_HEAD_DIM
    scale = NA_HEAD_DIM ** -0.5

    def body(p, carry):
        q0 = pl.multiple_of(p * nq, nq)
        k0 = pl.multiple_of(kstart_ref[p], GRID_W)
        case = case_ref[p]
        for hp in range(NA_HEADS // 2):
            c0 = hp * pair_w
            q2 = qkv_ref[0, pl.ds(q0, nq), c0:c0 + pair_w]
            k2 = qkv_ref[0, pl.ds(k0, n_keys), NA_WIDTH + c0:NA_WIDTH + c0 + pair_w]
            v2 = qkv_ref[0, pl.ds(k0, n_keys), 2 * NA_WIDTH + c0:2 * NA_WIDTH + c0 + pair_w]
            zero = jnp.zeros_like(q2)
            q2 = q2 * scale
            qq = jnp.concatenate([jnp.where(lo_half, q2, zero), jnp.where(lo_half, zero, q2)], axis=0)
            s = lax.dot_general(qq, k2, (((1,), (1,)), ((), ())), preferred_element_type=_F32)
            s = s + bias_ref[case, hp]
            e = jnp.exp(s - jnp.max(s, axis=-1, keepdims=True))
            l = jnp.sum(e, axis=-1, keepdims=True)
            o = _dot(e.astype(_BF16), v2) / l
            o_ref[0, pl.ds(q0, nq), c0:c0 + pair_w] = jnp.where(lo_half, o[:nq], o[nq:])
        return carry

    lax.fori_loop(0, n_pairs, body, 0, unroll=2)


def _natten(qkv, rpb):
    B, L, _ = qkv.shape
    rows = L // GRID_W
    assert rows % Q_ROWS == 0 and rows >= Q_ROWS + WIN_ROWS - 1
    cases, case_of, kstart, n_keys = _natten_tables(rows)
    n_case = cases.shape[0]
    nq = Q_ROWS * GRID_W
    bias = _natten_bias(rpb, cases).reshape(n_case, NA_HEADS // 2, 2 * nq, n_keys)
    grid_spec = pltpu.PrefetchScalarGridSpec(
        num_scalar_prefetch=2,
        grid=(B,),
        in_specs=[pl.BlockSpec((1, L, 3 * NA_WIDTH), lambda b, *_: (b, 0, 0)),
                  _resident((n_case, NA_HEADS // 2, 2 * nq, n_keys), lambda b, *_: (0, 0, 0, 0))],
        out_specs=pl.BlockSpec((1, L, NA_WIDTH), lambda b, *_: (b, 0, 0)),
    )
    return pl.pallas_call(
        functools.partial(_natten_kernel, n_pairs=rows // Q_ROWS, n_keys=n_keys),
        grid_spec=grid_spec,
        out_shape=jax.ShapeDtypeStruct((B, L, NA_WIDTH), _F32),
        compiler_params=_cparams("arbitrary"),
        name="natten",
    )(jnp.asarray(case_of), jnp.asarray(kstart), qkv, bias)


_META_ID, _META_GATE, _META_POS = 0, 2, 4
ROUTER_ROWS = 8 + N_EXPERTS


def _split_bf16(v):
    hi = v.astype(_BF16)
    return hi, (v - hi.astype(_F32)).astype(_BF16)


def _out_router_kernel(yhe_ref, yho_ref, sel_ref, yna_ref, ghy_ref, gna_ref, wtop_ref, wbot_ref, x_ref, gffn_ref,
                       rwh_ref, rwl_ref, rb_ref, h1_ref, xs_ref, meta_ref, tile_ref, cnt_ref, carry_ref):
    @pl.when(pl.program_id(0) == 0)
    def _():
        carry_ref[...] = jnp.zeros_like(carry_ref)

    nhe = _rmsnorm(yhe_ref[...], ghy_ref[...]).astype(_BF16)
    nho = _rmsnorm(yho_ref[...], ghy_ref[...]).astype(_BF16)
    nh = (_dot(sel_ref[0], nhe) + _dot(sel_ref[1], nho)).astype(_BF16)
    nn = _rmsnorm(yna_ref[...], gna_ref[...]).astype(_BF16)
    h1 = x_ref[...] + (_dot(nh, wtop_ref[...]) + _dot(nn, wbot_ref[...]))
    h1_ref[...] = h1
    xn = _rmsnorm(h1, gffn_ref[...])
    tm = xn.shape[0]

    xh, xl = _split_bf16(xn)
    logits = _dot(xh, rwh_ref[...]) + (_dot(xh, rwl_ref[...]) + _dot(xl, rwh_ref[...]))
    lt = logits.T[:ROUTER_ROWS] + rb_ref[:, 0:1]

    neg = -jnp.inf
    row8 = lax.broadcasted_iota(jnp.int32, (EXPERTS_PER_GROUP, tm), 0).astype(_F32)
    col_max = lambda v: jnp.max(v, axis=0, keepdims=True)
    first_max = lambda v, m: jnp.min(jnp.where(v == m, row8, float(EXPERTS_PER_GROUP)), axis=0, keepdims=True)

    gl = jnp.where(row8 < N_GROUPS, lt[0:8], neg)
    gmax = col_max(gl)
    g_w = 1.0 / jnp.sum(jnp.exp(gl - gmax), axis=0, keepdims=True)
    g_idx = first_max(gl, gmax)
    el = lt[8:8 + EXPERTS_PER_GROUP]
    for g in range(1, N_GROUPS):
        el = jnp.where(g_idx == g, lt[8 + g * EXPERTS_PER_GROUP:8 + (g + 1) * EXPERTS_PER_GROUP], el)
    m1 = col_max(el)
    i1 = first_max(el, m1)
    el2 = jnp.where(row8 == i1, neg, el)
    m2 = col_max(el2)
    i2 = first_max(el2, m2)
    r = jnp.exp(m2 - m1)
    gate1 = g_w / (1.0 + r)
    gate2 = g_w * r / (1.0 + r)
    id1 = g_idx * EXPERTS_PER_GROUP + i1
    id2 = g_idx * EXPERTS_PER_GROUP + i2

    row_e = lax.broadcasted_iota(jnp.int32, (N_EXPERTS, tm), 0).astype(_F32)
    sel1 = row_e == id1
    sel2 = row_e == id2
    onehot = jnp.where(sel1 | sel2, 1.0, 0.0)
    tri_r = lax.broadcasted_iota(jnp.int32, (tm, tm), 0)
    tri_c = lax.broadcasted_iota(jnp.int32, (tm, tm), 1)
    tri = jnp.where(tri_r < tri_c, 1.0, 0.0).astype(_BF16)
    before = _dot(onehot.astype(_BF16), tri)
    cnt = jnp.sum(onehot, axis=1, keepdims=True)
    run = jnp.floor((cnt + (MOE_CHUNK - 1)) * (1.0 / MOE_CHUNK)) * MOE_CHUNK
    run_b = jnp.broadcast_to(run, (N_EXPERTS, ROUTER_LANES))
    e_r = lax.broadcasted_iota(jnp.int32, (N_EXPERTS, N_EXPERTS), 0)
    e_c = lax.broadcasted_iota(jnp.int32, (N_EXPERTS, N_EXPERTS), 1)
    start_b = _dot_f32(jnp.where(e_c < e_r, 1.0, 0.0), run_b)
    start = start_b[:, 0:1]
    pos1 = jnp.sum(jnp.where(sel1, start + before, 0.0), axis=0, keepdims=True)
    pos2 = jnp.sum(jnp.where(sel2, start + before, 0.0), axis=0, keepdims=True)
    srow = lax.broadcasted_iota(jnp.int32, (SORT_ROWS, tm), 0).astype(_F32)
    perm = jnp.where((srow == pos1) | (srow == pos2), 1.0, 0.0).astype(_BF16)
    xs_ref[...] = _dot(perm, xn.astype(_BF16))

    base_b = carry_ref[...]
    lane = lax.broadcasted_iota(jnp.int32, (N_EXPERTS, ROUTER_LANES), 1)
    tile_ref[0] = jnp.where(lane == 0, start_b, jnp.where(lane == 1, run_b, jnp.where(lane == 2, base_b, 0.0)))
    carry_ref[...] = base_b + run_b
    cnt_ref[...] = base_b + run_b

    meta = jnp.zeros((8, tm), _F32)
    for k, v in enumerate((id1, id2, gate1, gate2, pos1, pos2)):
        meta = jnp.where(row8 == k, v, meta)
    meta_ref[...] = meta


def _out_router(y_hy_even, y_hy_odd, y_na, g_hy, g_na, w_out, x2, g_ffn, wg, bg, we, be):
    T, D = x2.shape
    half = ROW_TILE // 2
    sel_t = _parity_select(ROW_TILE).transpose(0, 2, 1)
    gpad = 8 - N_GROUPS
    rw = jnp.concatenate([wg.astype(_F32), jnp.zeros((D, gpad), _F32), we.astype(_F32)], axis=1)
    rw_hi, rw_lo = _split_bf16(jnp.pad(rw, ((0, 0), (0, ROUTER_LANES - ROUTER_ROWS))))
    rb = jnp.concatenate([bg.astype(_F32), jnp.zeros((gpad,), _F32), be.astype(_F32)])
    rb = jnp.broadcast_to(rb[:, None], (ROUTER_ROWS, ROUTER_LANES))
    w_bf = w_out.astype(_BF16)
    row = lambda n: pl.BlockSpec((ROW_TILE, n), lambda i: (i, 0))
    vec = lambda n: pl.BlockSpec((1, n), lambda i: (0, 0))
    mat = lambda r, c: pl.BlockSpec((r, c), lambda i: (0, 0))
    return pl.pallas_call(
        _out_router_kernel,
        grid=(T // ROW_TILE,),
        in_specs=[pl.BlockSpec((half, HY_WIDTH), lambda i: (i, 0)), pl.BlockSpec((half, HY_WIDTH), lambda i: (i, 0)),
                  pl.BlockSpec((2, ROW_TILE, half), lambda i: (0, 0, 0)),
                  row(NA_WIDTH), vec(HY_WIDTH), vec(NA_WIDTH),
                  mat(HY_WIDTH, D), mat(NA_WIDTH, D), row(D), vec(D),
                  mat(D, ROUTER_LANES), mat(D, ROUTER_LANES), mat(ROUTER_ROWS, ROUTER_LANES)],
        out_specs=[row(D),
                   pl.BlockSpec((SORT_ROWS, D), lambda i: (i, 0)),
                   pl.BlockSpec((8, ROW_TILE), lambda i: (0, i)),
                   pl.BlockSpec((1, N_EXPERTS, ROUTER_LANES), lambda i: (i, 0, 0)),
                   mat(N_EXPERTS, ROUTER_LANES)],
        out_shape=[jax.ShapeDtypeStruct((T, D), _F32),
                   jax.ShapeDtypeStruct((T // ROW_TILE * SORT_ROWS, D), _F32),
                   jax.ShapeDtypeStruct((8, T), _F32),
                   jax.ShapeDtypeStruct((T // ROW_TILE, N_EXPERTS, ROUTER_LANES), _F32),
                   jax.ShapeDtypeStruct((N_EXPERTS, ROUTER_LANES), _F32)],
        scratch_shapes=[pltpu.VMEM((N_EXPERTS, ROUTER_LANES), _F32)],
        compiler_params=_cparams("arbitrary"),
        name="out_proj_router",
    )(y_hy_even, y_hy_odd, sel_t, y_na, g_hy.reshape(1, -1), g_na.reshape(1, -1), w_bf[:HY_WIDTH], w_bf[HY_WIDTH:],
      x2, g_ffn.reshape(1, D), rw_hi, rw_lo, rb)


def _chunk_rows(c):
    return pl.ds(pl.multiple_of(c * MOE_CHUNK, MOE_CHUNK), MOE_CHUNK)


def _scatter_kernel(zblk_ref, dst_ref, xs_ref, xb_ref, zero_ref, sem, zero_sem):
    @pl.when(pl.program_id(0) == 0)
    def _():
        zero_ref[...] = jnp.zeros_like(zero_ref)

        def blk_copy(j):
            start = pl.multiple_of(zblk_ref[j] * MOE_BLOCK, MOE_BLOCK)
            return pltpu.make_async_copy(zero_ref, xb_ref.at[pl.ds(start, MOE_BLOCK)], zero_sem)

        def start(j, c):
            @pl.when(zblk_ref[j] >= 0)
            def _():
                blk_copy(j).start()
            return c

        def wait(j, c):
            @pl.when(zblk_ref[j] >= 0)
            def _():
                blk_copy(j).wait()
            return c

        lax.fori_loop(0, zblk_ref.shape[0], start, 0)
        lax.fori_loop(0, zblk_ref.shape[0], wait, 0)

    chunk_copy = lambda c: pltpu.make_async_copy(xs_ref.at[_chunk_rows(c)], xb_ref.at[_chunk_rows(dst_ref[0, 0, c])], sem)

    def start(c, carry):
        @pl.when(dst_ref[0, 0, c] >= 0)
        def _():
            chunk_copy(c).start()
        return carry

    def wait(c, carry):
        @pl.when(dst_ref[0, 0, c] >= 0)
        def _():
            chunk_copy(c).wait()
        return carry

    lax.fori_loop(0, SORT_CHUNKS, start, 0)
    lax.fori_loop(0, SORT_CHUNKS, wait, 0)


def _moe_scatter(xs, dst_chunk, zero_blocks, n_slots):
    D = xs.shape[1]
    nt = xs.shape[0] // SORT_ROWS
    grid_spec = pltpu.PrefetchScalarGridSpec(
        num_scalar_prefetch=1,
        grid=(nt,),
        in_specs=[pl.BlockSpec((1, 1, SORT_CHUNKS), lambda i, *_: (i, 0, 0), memory_space=pltpu.SMEM),
                  pl.BlockSpec((SORT_ROWS, D), lambda i, *_: (i, 0))],
        out_specs=pl.BlockSpec(memory_space=pl.ANY),
        scratch_shapes=[pltpu.VMEM((MOE_BLOCK, D), _F32), pltpu.SemaphoreType.DMA(()),
                        pltpu.SemaphoreType.DMA(())],
    )
    return pl.pallas_call(
        _scatter_kernel,
        grid_spec=grid_spec,
        out_shape=jax.ShapeDtypeStruct((n_slots, D), _F32),
        compiler_params=_cparams("arbitrary"),
        name="moe_scatter",
    )(zero_blocks, dst_chunk.reshape(nt, 1, SORT_CHUNKS), xs)


def _expert_kernel(be_ref, nblk_ref, xb_ref, w1_ref, w3_ref, w2_ref, yb_ref, wb1_ref, wb3_ref, wb2_ref):
    i = pl.program_id(0)

    @pl.when(i < nblk_ref[0])
    def _():
        e = be_ref[i]
        e_prev = be_ref[jnp.maximum(i - 1, 0)]

        @pl.when((i == 0) | (e != e_prev))
        def _():
            wb1_ref[...] = w1_ref[0].astype(_BF16)
            wb3_ref[...] = w3_ref[0].astype(_BF16)
            wb2_ref[...] = w2_ref[0].astype(_BF16)

        x = xb_ref[...].astype(_BF16)
        a = _dot(x, wb1_ref[...])
        hid = (a * jax.nn.sigmoid(a) * _dot(x, wb3_ref[...])).astype(_BF16)
        yb_ref[...] = _dot(hid, wb2_ref[...])

    @pl.when(i >= nblk_ref[0])
    def _():
        yb_ref[...] = jnp.zeros_like(yb_ref)


def _moe_experts(xb, block_e, n_blk, w1, w3, w2):
    n_blocks = xb.shape[0] // MOE_BLOCK
    _, D, DE = w1.shape
    live = lambda i, nb: jnp.minimum(i, nb[0] - 1)
    grid_spec = pltpu.PrefetchScalarGridSpec(
        num_scalar_prefetch=2,
        grid=(n_blocks,),
        in_specs=[pl.BlockSpec((MOE_BLOCK, D), lambda i, be, nb: (live(i, nb), 0)),
                  pl.BlockSpec((1, D, DE), lambda i, be, nb: (be[live(i, nb)], 0, 0)),
                  pl.BlockSpec((1, D, DE), lambda i, be, nb: (be[live(i, nb)], 0, 0)),
                  pl.BlockSpec((1, DE, D), lambda i, be, nb: (be[live(i, nb)], 0, 0))],
        out_specs=pl.BlockSpec((MOE_BLOCK, D), lambda i, be, nb: (i, 0)),
        scratch_shapes=[pltpu.VMEM((D, DE), _BF16), pltpu.VMEM((D, DE), _BF16), pltpu.VMEM((DE, D), _BF16)],
    )
    return pl.pallas_call(
        _expert_kernel,
        grid_spec=grid_spec,
        out_shape=jax.ShapeDtypeStruct(xb.shape, _F32),
        compiler_params=_cparams("arbitrary"),
        name="moe_experts",
    )(block_e, n_blk, xb, w1, w3, w2)


def _final_kernel(dst_ref, dst_next_ref, h1_ref, comb_ref, yb_ref, p_ref, wg_ref, wp_ref, gple_ref, gfin_ref,
                  o_ref, ybuf_ref, sem):
    tm = h1_ref.shape[0]
    i = pl.program_id(0)
    slot = lax.rem(i, 2)

    def chunk_copy(d_ref, s, c):
        return pltpu.make_async_copy(yb_ref.at[_chunk_rows(d_ref[0, 0, c])], ybuf_ref.at[s, _chunk_rows(c)], sem.at[s])

    def gather(d_ref, s):
        def start(c, carry):
            @pl.when(d_ref[0, 0, c] >= 0)
            def _():
                chunk_copy(d_ref, s, c).start()
            return carry
        lax.fori_loop(0, SORT_CHUNKS, start, 0)

    @pl.when(i == 0)
    def _():
        ybuf_ref[...] = jnp.zeros_like(ybuf_ref)
        gather(dst_ref, 0)

    @pl.when(i + 1 < pl.num_programs(0))
    def _():
        gather(dst_next_ref, 1 - slot)

    def wait(c, carry):
        @pl.when(dst_ref[0, 0, c] >= 0)
        def _():
            chunk_copy(dst_ref, slot, c).wait()
        return carry
    lax.fori_loop(0, SORT_CHUNKS, wait, 0)

    comb = comb_ref[...]
    g1, g2 = comb[:, _META_GATE:_META_GATE + 1], comb[:, _META_GATE + 1:_META_GATE + 2]
    p1, p2 = comb[:, _META_POS:_META_POS + 1], comb[:, _META_POS + 1:_META_POS + 2]
    lane = lax.broadcasted_iota(jnp.int32, (tm, SORT_ROWS), 1).astype(_F32)
    wmat = (jnp.where(lane == p1, g1, 0.0) + jnp.where(lane == p2, g2, 0.0)).astype(_BF16)
    moe = _dot(wmat, ybuf_ref[slot].astype(_BF16))
    h2 = h1_ref[...] + moe
    gate = jax.nn.sigmoid(_dot(_rmsnorm(h2, gple_ref[...]).astype(_BF16), wg_ref[...]))
    h3 = h2 + _dot(p_ref[...].astype(_BF16), wp_ref[...]) * gate
    o_ref[...] = _rmsnorm(h3, gfin_ref[...])


def _final(dst_chunk, h1, comb, yb, p2, w_gate, w_proj, g_ple, g_final):
    T, D = h1.shape
    nt = T // ROW_TILE
    PD = p2.shape[1]
    row = lambda n: pl.BlockSpec((ROW_TILE, n), lambda i: (i, 0))
    vec = lambda n: pl.BlockSpec((1, n), lambda i: (0, 0))
    dest3 = dst_chunk.reshape(nt, 1, SORT_CHUNKS)
    return pl.pallas_call(
        _final_kernel,
        grid=(nt,),
        in_specs=[pl.BlockSpec((1, 1, SORT_CHUNKS), lambda i: (i, 0, 0), memory_space=pltpu.SMEM),
                  pl.BlockSpec((1, 1, SORT_CHUNKS), lambda i: (jnp.minimum(i + 1, nt - 1), 0, 0),
                               memory_space=pltpu.SMEM),
                  row(D), row(ROUTER_LANES),
                  pl.BlockSpec(memory_space=pl.ANY),
                  row(PD),
                  pl.BlockSpec((D, D), lambda i: (0, 0)),
                  pl.BlockSpec((PD, D), lambda i: (0, 0)),
                  vec(D), vec(D)],
        out_specs=row(D),
        out_shape=jax.ShapeDtypeStruct((T, D), _F32),
        scratch_shapes=[pltpu.VMEM((2, SORT_ROWS, D), _F32), pltpu.SemaphoreType.DMA((2,))],
        compiler_params=_cparams("arbitrary"),
        name="moe_gather_ple_final",
    )(dest3, dest3, h1, comb, yb, p2, w_gate.astype(_BF16), w_proj.astype(_BF16),
      g_ple.reshape(1, D), g_final.reshape(1, D))


def _routing_tables(tile_tbl, counts_f, n_blocks, n_tail):
    start = tile_tbl[:, :, 0].astype(jnp.int32)
    run = tile_tbl[:, :, 1].astype(jnp.int32)
    base = tile_tbl[:, :, 2].astype(jnp.int32)
    counts = counts_f[:, 0].astype(jnp.int32)
    padded = (counts + MOE_BLOCK - 1) // MOE_BLOCK * MOE_BLOCK
    e_iota = jnp.arange(N_EXPERTS, dtype=jnp.int32)
    pad_end = jnp.sum(jnp.where(e_iota[None, :] <= e_iota[:, None], padded[None, :], 0), axis=1)
    pad_start = pad_end - padded
    row0 = (jnp.arange(SORT_CHUNKS, dtype=jnp.int32) * MOE_CHUNK)[None, :, None]
    mine = (start[:, None, :] <= row0) & (row0 < (start + run)[:, None, :])
    slot0 = (pad_start[None, :] + base - start)[:, None, :] + row0
    dst_chunk = jnp.where(jnp.any(mine, axis=-1), jnp.sum(jnp.where(mine, slot0, 0), axis=-1) // MOE_CHUNK, -1)
    blk_row = jnp.arange(n_blocks, dtype=jnp.int32)[:, None] * MOE_BLOCK
    block_e = jnp.minimum(jnp.sum((pad_end[None, :] <= blk_row).astype(jnp.int32), axis=1), N_EXPERTS - 1)
    n_blk = (pad_end[-1:] // MOE_BLOCK).astype(jnp.int32)
    seg_last = jnp.where(padded > counts, pad_end // MOE_BLOCK - 1, -1)
    tail = n_blk[0] + jnp.arange(n_tail, dtype=jnp.int32)
    zero_blocks = jnp.concatenate([seg_last, jnp.where(tail < n_blocks, tail, -1)]).astype(jnp.int32)
    return dst_chunk.astype(jnp.int32), zero_blocks, block_e, n_blk


def _one_layer(h, p, g_mix, w_in, hy_conv_w, hy_conv_b, hy_f_w1, hy_f_b1, hy_f_freq1, hy_f_w2, hy_f_b2,
               hy_f_freq2, hy_f_w3, hy_skip, na_rpb, g_out_hy, g_out_na, w_out, g_ffn, router_wg, router_bg,
               router_we, router_be, exp_w1, exp_w3, exp_w2, g_ple, w_ple_gate, w_ple_proj):
    B, L, D = h.shape
    T = B * L
    assert T % ROW_TILE == 0 and L % FREQ_CHUNK == 0 and w_in.shape[1] == HY_COLS + 3 * NA_WIDTH
    x2 = h.reshape(T, D)
    u2, qkv = _in_proj(x2, g_mix, w_in)

    hf2 = _filter_mlp(L, hy_f_w1, hy_f_b1, hy_f_freq1, hy_f_w2, hy_f_b2, hy_f_freq2, hy_f_w3)
    tables = _half_dft_tables(L)
    k1r, k1i, k2r, k2i = _filter_spec(tables, hf2)
    y_hy_even, y_hy_odd = _hyena(u2.reshape(B, L // 2, 2 * HY_COLS), hy_conv_w.astype(_F32), hy_conv_b.astype(_F32),
                                 tables, k1r, k1i, k2r, k2i, hy_skip.astype(_F32))
    y_na = _natten(qkv.reshape(B, L, 3 * NA_WIDTH), na_rpb)

    h1, xs, meta, tile_tbl, counts = _out_router(
        y_hy_even.reshape(T // 2, HY_WIDTH), y_hy_odd.reshape(T // 2, HY_WIDTH), y_na.reshape(T, NA_WIDTH),
        g_out_hy, g_out_na, w_out, x2, g_ffn, router_wg, router_bg, router_we, router_be)
    max_rows = T * TOP_K + (T // ROW_TILE) * N_EXPERTS * (MOE_CHUNK - 1) + N_EXPERTS * (MOE_BLOCK - 1)
    n_blocks = -(-max_rows // MOE_BLOCK)
    n_tail = n_blocks - T * TOP_K // MOE_BLOCK
    dst_chunk, zero_blocks, block_e, n_blk = _routing_tables(tile_tbl, counts, n_blocks, n_tail)
    xb = _moe_scatter(xs, dst_chunk, zero_blocks, n_blocks * MOE_BLOCK)
    yb = _moe_experts(xb, block_e, n_blk, exp_w1, exp_w3, exp_w2)
    comb = jnp.pad(meta[:_META_POS + TOP_K].T, ((0, 0), (0, ROUTER_LANES - _META_POS - TOP_K)))
    return h1, dst_chunk, comb, yb


def kernel(x, p, g_mix, w_in, hy_conv_w, hy_conv_b, hy_f_w1, hy_f_b1, hy_f_freq1, hy_f_w2, hy_f_b2, hy_f_freq2, hy_f_w3, hy_skip, na_rpb, g_out_hy, g_out_na, w_out, g_ffn, router_wg, router_bg, router_we, router_be, exp_w1, exp_w3, exp_w2, g_ple, w_ple_gate, w_ple_proj, g_final):
    depth = p.shape[0]
    assert depth == 1, "the final RMSNorm is fused into the last layer's kernel; one layer is supported"
    B, L, D = x.shape
    i = 0
    h1, dest, gates, yb = _one_layer(
        x, p[i], g_mix[i], w_in[i], hy_conv_w[i], hy_conv_b[i], hy_f_w1[i], hy_f_b1[i], hy_f_freq1[i],
        hy_f_w2[i], hy_f_b2[i], hy_f_freq2[i], hy_f_w3[i], hy_skip[i], na_rpb[i], g_out_hy[i], g_out_na[i],
        w_out[i], g_ffn[i], router_wg[i], router_bg[i], router_we[i], router_be[i], exp_w1[i], exp_w3[i],
        exp_w2[i], g_ple[i], w_ple_gate[i], w_ple_proj[i])
    out = _final(dest, h1, gates, yb, p[i].reshape(B * L, -1), w_ple_gate[i], w_ple_proj[i], g_ple[i], g_final)
    return out.reshape(B, L, D)
```

```python
import functools
import math

import numpy as np
import jax
import jax.numpy as jnp
from jax import lax
from jax.experimental import pallas as pl
from jax.experimental.pallas import tpu as pltpu

_F32 = jnp.float32
_BF16 = jnp.bfloat16

GRID_W = 64
HY_WIDTH = 512
NA_WIDTH = 512
NA_HEADS = 8
NA_HEAD_DIM = 64
HY_ORDER = 2
SHORT_CONV = 3
FILTER_EMB = 33
FILTER_BANDS = (FILTER_EMB - 1) // 2
DECAY_TARGET = 1e-2
FAST_DECAY_PCT = 0.3
SLOW_DECAY_PCT = 1.5
WIN_ROWS = 8
WIN_COLS = 16
Q_ROWS = 2
N_GROUPS = 4
EXPERTS_PER_GROUP = 8
N_EXPERTS = N_GROUPS * EXPERTS_PER_GROUP
TOP_K = 2
MOE_BLOCK = 512
EPS = 1e-6
NEG_INF = -1e30
HY_COLS = (HY_ORDER + 1) * HY_WIDTH

V7X_LANES = 128
V7X_SUBLANES = 8
V7X_VMEM_LIMIT_BYTES = 56 * 2 ** 20

ROW_TILE = 512
FREQ_CHUNK = 512
HY_CH_TILE = 256
ROUTER_LANES = V7X_LANES
MOE_CHUNK = V7X_SUBLANES
SORT_ROWS = -(-(TOP_K * ROW_TILE + N_EXPERTS * (MOE_CHUNK - 1)) // V7X_LANES) * V7X_LANES
SORT_CHUNKS = SORT_ROWS // MOE_CHUNK


def _cparams(*sem):
    return pltpu.CompilerParams(dimension_semantics=sem, vmem_limit_bytes=V7X_VMEM_LIMIT_BYTES)


def _resident(shape, index_map):
    return pl.BlockSpec(shape, index_map, pipeline_mode=pl.Buffered(1))


def _rmsnorm(x, g):
    return x * lax.rsqrt(jnp.mean(x * x, axis=-1, keepdims=True) + EPS) * g


def _dot(a, b):
    return jnp.dot(a, b, preferred_element_type=_F32)


def _dot_f32(a, b):
    return jnp.dot(a, b, preferred_element_type=_F32, precision=lax.Precision.HIGHEST)


def _parity_select(n):
    r = jnp.arange(n // 2, dtype=jnp.int32)[None, :, None]
    t = jnp.arange(n, dtype=jnp.int32)[None, None, :]
    p = jnp.arange(2, dtype=jnp.int32)[:, None, None]
    return (t == 2 * r + p).astype(_BF16)


def _in_proj_kernel(x_ref, g_ref, w_ref, sel_ref, u2_ref, qkv_ref):
    xn = _rmsnorm(x_ref[...], g_ref[...]).astype(_BF16)
    n_hy = u2_ref.shape[1] // 2
    xe = _dot(sel_ref[0], xn).astype(_BF16)
    xo = _dot(sel_ref[1], xn).astype(_BF16)
    for c0 in range(0, n_hy, 512):
        w = w_ref[:, c0:c0 + 512]
        u2_ref[:, c0:c0 + 512] = _dot(xe, w).astype(_BF16)
        u2_ref[:, n_hy + c0:n_hy + c0 + 512] = _dot(xo, w).astype(_BF16)
    for c0 in range(0, qkv_ref.shape[1], 512):
        qkv_ref[:, c0:c0 + 512] = _dot(xn, w_ref[:, n_hy + c0:n_hy + c0 + 512]).astype(_BF16)


def _in_proj(x2, g_mix, w_in):
    T, D = x2.shape
    n_in = w_in.shape[1]
    n_qkv = n_in - HY_COLS
    half = ROW_TILE // 2
    return pl.pallas_call(
        _in_proj_kernel,
        grid=(T // ROW_TILE,),
        in_specs=[pl.BlockSpec((ROW_TILE, D), lambda i: (i, 0)),
                  pl.BlockSpec((1, D), lambda i: (0, 0)),
                  _resident((D, n_in), lambda i: (0, 0)),
                  pl.BlockSpec((2, half, ROW_TILE), lambda i: (0, 0, 0))],
        out_specs=[pl.BlockSpec((half, 2 * HY_COLS), lambda i: (i, 0)),
                   pl.BlockSpec((ROW_TILE, n_qkv), lambda i: (i, 0))],
        out_shape=[jax.ShapeDtypeStruct((T // 2, 2 * HY_COLS), _BF16),
                   jax.ShapeDtypeStruct((T, n_qkv), _BF16)],
        compiler_params=_cparams("arbitrary"),
        name="in_proj",
    )(x2, g_mix.reshape(1, D), w_in.astype(_BF16), _parity_select(ROW_TILE))


def _filter_mlp_kernel(z_ref, w1_ref, b1_ref, f1_ref, w2_ref, b2_ref, f2_ref, w3_ref, delta_ref, hf_ref):
    z = z_ref[0]
    hid = jnp.sin(f1_ref[...] * (_dot_f32(z, w1_ref[...]) + b1_ref[...]))
    hid = jnp.sin(f2_ref[...] * (_dot_f32(hid, w2_ref[...]) + b2_ref[...]))
    hf = _dot_f32(hid, w3_ref[...])
    decay = jnp.exp(-z[:, 0:1] * delta_ref[...])
    tl = z.shape[0]
    row = lax.broadcasted_iota(jnp.int32, (tl, HY_WIDTH), 0)
    offset0 = (row == 0) & (pl.program_id(0) == 0) & (pl.program_id(1) == 0)
    for k in range(2 * HY_ORDER):
        blk = hf[:, k * HY_WIDTH:(k + 1) * HY_WIDTH] * decay
        if k >= HY_ORDER:
            blk = jnp.where(offset0, 0.0, blk)
        hf_ref[:, k * HY_WIDTH:(k + 1) * HY_WIDTH] = blk


def _filter_mlp(L, w1, b1, f1, w2, b2, f2, w3):
    t = jnp.linspace(0.0, 1.0, L, dtype=_F32)[:, None]
    w = 2.0 * math.pi * jnp.arange(L, dtype=_F32)[:, None] / L
    bands = jnp.linspace(1e-4, FILTER_BANDS - 1, FILTER_BANDS, dtype=_F32)[None, :]
    z = jnp.concatenate([t, jnp.cos(bands * w), -jnp.sin(bands * w)], axis=-1)
    z = jnp.pad(z, ((0, 0), (0, V7X_LANES - FILTER_EMB)))
    z = jnp.stack([z[0::2], z[1::2]])
    w1p = jnp.pad(w1.astype(_F32), ((0, V7X_LANES - FILTER_EMB), (0, 0)))
    max_decay = math.log(DECAY_TARGET) / FAST_DECAY_PCT
    min_decay = math.log(DECAY_TARGET) / SLOW_DECAY_PCT
    deltas = jnp.abs(jnp.linspace(min_decay, max_decay, HY_WIDTH, dtype=_F32))[None, :]
    hid = w1.shape[1]
    n_out = w3.shape[1]
    lh = L // 2
    tl = min(lh, ROW_TILE)
    full = lambda shape: pl.BlockSpec(shape, lambda p, i: (0, 0))
    return pl.pallas_call(
        _filter_mlp_kernel,
        grid=(2, lh // tl),
        in_specs=[pl.BlockSpec((1, tl, V7X_LANES), lambda p, i: (p, i, 0)),
                  full((V7X_LANES, hid)), full((1, hid)), full((1, hid)),
                  full((hid, hid)), full((1, hid)), full((1, hid)),
                  full((hid, n_out)), full((1, HY_WIDTH))],
        out_specs=pl.BlockSpec((tl, n_out), lambda p, i: (i, p)),
        out_shape=jax.ShapeDtypeStruct((lh, 2 * n_out), _F32),
        compiler_params=_cparams("arbitrary", "arbitrary"),
        name="hyena_filter_mlp",
    )(z, w1p, b1.reshape(1, hid), f1.reshape(1, hid), w2.astype(_F32), b2.reshape(1, hid),
      f2.reshape(1, hid), w3.astype(_F32), deltas)


def _freq_chunks(gp):
    return [(r0, min(FREQ_CHUNK, gp - r0)) for r0 in range(0, gp, FREQ_CHUNK)]


def _half_dft_tables(L):
    lh, n, sb = L // 2, 2 * L, 32
    gp = -(-(lh + 1) // V7X_SUBLANES) * V7X_SUBLANES
    assert lh % sb == 0
    g = jnp.arange(gp, dtype=jnp.int32)
    live = (g <= lh)[:, None]
    ang = lambda s: ((g[:, None] * s[None, :]) % L).astype(_F32) * (2.0 * math.pi / L)
    a_hi = ang(jnp.arange(lh // sb, dtype=jnp.int32) * sb)
    a_lo = ang(jnp.arange(sb, dtype=jnp.int32))
    c_hi, s_hi, c_lo, s_lo = jnp.cos(a_hi), jnp.sin(a_hi), jnp.cos(a_lo), jnp.sin(a_lo)
    cosm = (c_hi[:, :, None] * c_lo[:, None, :] - s_hi[:, :, None] * s_lo[:, None, :]).reshape(gp, lh)
    sinm = (s_hi[:, :, None] * c_lo[:, None, :] + c_hi[:, :, None] * s_lo[:, None, :]).reshape(gp, lh)
    cosm = jnp.where(live, cosm, 0.0)
    sinm = jnp.where(live, sinm, 0.0)
    ff = jnp.concatenate([cosm, -sinm], axis=0).astype(_BF16)
    wgt = jnp.where((g == 0) | (g == lh), 1.0, 2.0)[:, None] / n
    gf = jnp.concatenate([(cosm * wgt).T, (-sinm * wgt).T], axis=1).astype(_BF16)
    tw = g.astype(_F32)[:, None] * (2.0 * math.pi / n)
    wr = jnp.broadcast_to(jnp.cos(tw), (gp, HY_CH_TILE))
    wi = jnp.broadcast_to(-jnp.sin(tw), (gp, HY_CH_TILE))
    return ff, gf, wr, wi


def _half_spectra(ff_ref, wr, wi, xe, xo, r0, n):
    gp = ff_ref.shape[0] // 2
    fc = ff_ref[r0:r0 + n, :]
    fs = ff_ref[gp + r0:gp + r0 + n, :]
    er, ei = _dot(fc, xe), _dot(fs, xe)
    orr, oi = _dot(fc, xo), _dot(fs, xo)
    pr = wr * orr - wi * oi
    pi = wr * oi + wi * orr
    return (er + pr, ei + pi), (er - pr, pi - ei)


def _filter_spec_kernel(ff_ref, wr_ref, wi_ref, fe_ref, fo_ref, be_ref, bo_ref, k1r_ref, k1i_ref, k2r_ref, k2i_ref):
    fe, fo, be, bo = (r[...].astype(_BF16) for r in (fe_ref, fo_ref, be_ref, bo_ref))
    for r0, n in _freq_chunks(wr_ref.shape[0]):
        wr, wi = wr_ref[r0:r0 + n, :], wi_ref[r0:r0 + n, :]
        (f1r, f1i), (f2r, f2i) = _half_spectra(ff_ref, wr, wi, fe, fo, r0, n)
        (b1r, b1i), (b2r, b2i) = _half_spectra(ff_ref, wr, wi, be, bo, r0, n)
        k1r_ref[0, r0:r0 + n, :] = f1r + b1r
        k1i_ref[0, r0:r0 + n, :] = f1i - b1i
        k2r_ref[0, r0:r0 + n, :] = f2r + b2r
        k2i_ref[0, r0:r0 + n, :] = f2i - b2i


def _filter_spec(tables, hf2):
    ff, _, wr, wi = tables
    lh, gp = hf2.shape[0], wr.shape[0]
    n_ct = HY_WIDTH // HY_CH_TILE
    n_blk = hf2.shape[1] // 2 // HY_CH_TILE
    col = lambda blk0: pl.BlockSpec((lh, HY_CH_TILE), lambda j: (0, blk0 + j))
    out = jax.ShapeDtypeStruct((HY_ORDER, gp, HY_WIDTH), _F32)
    ospec = pl.BlockSpec((1, gp, HY_CH_TILE), lambda j: (j // n_ct, 0, j % n_ct))
    full = lambda a: pl.BlockSpec(a.shape, lambda j: (0, 0))
    return pl.pallas_call(
        _filter_spec_kernel,
        grid=(HY_ORDER * n_ct,),
        in_specs=[full(ff), full(wr), full(wi),
                  col(0), col(n_blk),
                  col(HY_ORDER * n_ct), col(n_blk + HY_ORDER * n_ct)],
        out_specs=[ospec] * 4,
        out_shape=[out] * 4,
        compiler_params=_cparams("arbitrary"),
        name="hyena_filter_spectrum",
    )(ff, wr, wi, hf2, hf2, hf2, hf2)


def _parity_conv(ze, zo, w_ref, b_ref):
    lh = ze.shape[0]
    row = lax.broadcasted_iota(jnp.int32, ze.shape, 0)
    zo_prev = jnp.where(row == 0, 0.0, pltpu.roll(zo, 1, axis=0))
    ze_next = jnp.where(row == lh - 1, 0.0, pltpu.roll(ze, lh - 1, axis=0))
    w0, w1, w2, b = w_ref[0:1, :], w_ref[1:2, :], w_ref[2:3, :], b_ref[...]
    return ((b + zo_prev * w0) + ze * w1) + zo * w2, ((b + ze * w0) + zo * w1) + ze_next * w2


def _hyena_kernel(ze_ref, zo_ref, zw_ref, zb_ref, ge_ref, go_ref, gw_ref, gb_ref, skip_ref, ff_ref, gf_ref,
                  wr_ref, wi_ref, k1r_ref, k1i_ref, k2r_ref, k2i_ref, oe_ref, oo_ref, a_ref, b_ref, *, conv_input):
    ze, zo = ze_ref[0].astype(_F32), zo_ref[0].astype(_F32)
    if conv_input:
        ze, zo = _parity_conv(ze, zo, zw_ref, zb_ref)
    xe, xo = ze.astype(_BF16), zo.astype(_BF16)
    gp = wr_ref.shape[0]
    for r0, n in _freq_chunks(gp):
        rows = slice(r0, r0 + n)
        wr, wi = wr_ref[rows, :], wi_ref[rows, :]
        (x1r, x1i), (x2r, x2i) = _half_spectra(ff_ref, wr, wi, xe, xo, r0, n)
        k1r, k1i, k2r, k2i = k1r_ref[0, rows, :], k1i_ref[0, rows, :], k2r_ref[0, rows, :], k2i_ref[0, rows, :]
        y1r, y1i = x1r * k1r - x1i * k1i, x1r * k1i + x1i * k1r
        y2r, y2i = x2r * k2r - x2i * k2i, x2r * k2i + x2i * k2r
        tr, ti = y1r - y2r, y1i + y2i
        a_ref[rows, :] = (y1r + y2r).astype(_BF16)
        a_ref[gp + r0:gp + r0 + n, :] = (y1i - y2i).astype(_BF16)
        b_ref[rows, :] = (tr * wr + ti * wi).astype(_BF16)
        b_ref[gp + r0:gp + r0 + n, :] = (ti * wr - tr * wi).astype(_BF16)
    ge, go = _parity_conv(ge_ref[0].astype(_F32), go_ref[0].astype(_F32), gw_ref, gb_ref)
    skip = skip_ref[0]
    for r0 in range(0, ze.shape[0], FREQ_CHUNK):
        rows = slice(r0, r0 + FREQ_CHUNK)
        g = gf_ref[rows, :]
        oe_ref[0, rows, :] = ge[rows] * (_dot(g, a_ref[...]) + ze[rows] * skip)
        oo_ref[0, rows, :] = go[rows] * (_dot(g, b_ref[...]) + zo[rows] * skip)


def _hyena(u2, conv_w, conv_b, tables, k1r, k1i, k2r, k2i, skip):
    ff, gf, wr, wi = tables
    B, lh, _ = u2.shape
    gp = wr.shape[0]
    assert lh % FREQ_CHUNK == 0
    n_ct = HY_WIDTH // HY_CH_TILE
    odd = HY_COLS // HY_CH_TILE
    conv_b2 = conv_b.reshape(1, HY_COLS)
    skip3 = skip.reshape(HY_ORDER, 1, HY_WIDTH)
    col = lambda blk0: pl.BlockSpec((1, lh, HY_CH_TILE), lambda c, b: (b, 0, blk0 + c))
    cw = lambda blk0: pl.BlockSpec((SHORT_CONV, HY_CH_TILE), lambda c, b: (0, blk0 + c))
    cb = lambda blk0: pl.BlockSpec((1, HY_CH_TILE), lambda c, b: (0, blk0 + c))
    const = lambda a: _resident(a.shape, lambda c, b: (0, 0))
    out = jax.ShapeDtypeStruct((B, lh, HY_WIDTH), _F32)
    ze_arr, zo_arr, ze_spec, zo_spec = u2, u2, col(2 * n_ct), col(odd + 2 * n_ct)
    for o in range(HY_ORDER):
        coef = lambda: pl.BlockSpec((1, gp, HY_CH_TILE), lambda c, b, o=o: (o, 0, c), pipeline_mode=pl.Buffered(1))
        ze_arr, zo_arr = pl.pallas_call(
            functools.partial(_hyena_kernel, conv_input=(o == 0)),
            grid=(n_ct, B),
            in_specs=[ze_spec, zo_spec, cw(2 * n_ct), cb(2 * n_ct),
                      col(o * n_ct), col(odd + o * n_ct), cw(o * n_ct), cb(o * n_ct),
                      pl.BlockSpec((1, 1, HY_CH_TILE), lambda c, b, o=o: (o, 0, c)),
                      const(ff), const(gf), const(wr), const(wi), coef(), coef(), coef(), coef()],
            out_specs=[col(0), col(0)],
            out_shape=[out, out],
            scratch_shapes=[pltpu.VMEM((2 * gp, HY_CH_TILE), _BF16), pltpu.VMEM((2 * gp, HY_CH_TILE), _BF16)],
            compiler_params=_cparams("arbitrary", "arbitrary"),
            name=f"hyena_order_{o}",
        )(ze_arr, zo_arr, conv_w, conv_b2, u2, u2, conv_w, conv_b2, skip3, ff, gf, wr, wi, k1r, k1i, k2r, k2i)
        ze_spec = zo_spec = col(0)
    return ze_arr, zo_arr


def _natten_tables(rows):
    kr = min(WIN_ROWS, rows)
    krb = min(Q_ROWS + kr - 1, rows)
    rs = np.clip(np.arange(rows) - kr // 2, 0, rows - kr)
    cs = np.clip(np.arange(GRID_W) - WIN_COLS // 2, 0, GRID_W - WIN_COLS)
    qc = np.tile(np.arange(GRID_W), Q_ROWS)[:, None]
    kc = np.tile(np.arange(GRID_W), krb)[None, :]
    cases, case_of, kstart = [], [], []
    for p in range(rows // Q_ROWS):
        k_r0 = min(rs[p * Q_ROWS], rows - krb)
        qr = (p * Q_ROWS + np.repeat(np.arange(Q_ROWS), GRID_W))[:, None]
        kr_ = (k_r0 + np.repeat(np.arange(krb), GRID_W))[None, :]
        valid = ((kr_ >= rs[qr]) & (kr_ < rs[qr] + kr) & (kc >= cs[qc]) & (kc < cs[qc] + WIN_COLS))
        dr = np.clip(kr_ - qr + WIN_ROWS - 1, 0, 2 * WIN_ROWS - 2)
        dc = np.clip(kc - qc + WIN_COLS - 1, 0, 2 * WIN_COLS - 2)
        idx = np.where(valid, dr * (2 * WIN_COLS - 1) + dc, -1).astype(np.int32)
        for ci, c in enumerate(cases):
            if np.array_equal(c, idx):
                break
        else:
            ci = len(cases)
            cases.append(idx)
        case_of.append(ci)
        kstart.append(k_r0 * GRID_W)
    return np.stack(cases), np.asarray(case_of, np.int32), np.asarray(kstart, np.int32), krb * GRID_W


def _natten_bias(rpb, cases):
    n_case, nq, nk = cases.shape
    krb = nk // GRID_W
    n_dr, n_dc = 2 * WIN_ROWS - 1, 2 * WIN_COLS - 1
    c5 = cases.reshape(n_case, Q_ROWS, GRID_W, krb, GRID_W)
    dr_blk = np.where(c5 >= 0, c5 // n_dc, -1).max(axis=(2, 4))
    dc_idx = np.clip(np.arange(GRID_W)[None, :] - np.arange(GRID_W)[:, None] + WIN_COLS - 1, 0, n_dc - 1)
    assert np.all((c5 < 0) | (c5 // n_dc == dr_blk[:, :, None, :, None]))
    assert np.all((c5 < 0) | (c5 % n_dc == dc_idx[None, None, :, None, :]))
    onehot_c = (dc_idx.reshape(1, -1) == np.arange(n_dc)[:, None]).astype(np.float32)
    sel = (np.maximum(dr_blk, 0).reshape(-1, 1) == np.arange(n_dr)[None, :]).astype(np.float32)
    hi = lax.Precision.HIGHEST
    toe = jnp.einsum('hdj,jx->hdx', rpb.astype(_F32), onehot_c, precision=hi)
    blk = jnp.einsum('sd,hdx->hsx', sel, toe, precision=hi)
    blk = blk.reshape(NA_HEADS, n_case, Q_ROWS, krb, GRID_W, GRID_W).transpose(1, 0, 2, 4, 3, 5)
    return jnp.where((cases >= 0)[:, None], blk.reshape(n_case, NA_HEADS, nq, nk), NEG_INF)


def _natten_kernel(case_ref, kstart_ref, qkv_ref, bias_ref, o_ref, *, n_pairs, n_keys):
    nq = Q_ROWS * GRID_W
    pair_w = 2 * NA_HEAD_DIM
    lane = lax.broadcasted_iota(jnp.int32, (nq, pair_w), 1)
    lo_half = lane < NA_HEAD_DIM
    scale = NA_HEAD_DIM ** -0.5

    def body(p, carry):
        q0 = pl.multiple_of(p * nq, nq)
        k0 = pl.multiple_of(kstart_ref[p], GRID_W)
        case = case_ref[p]
        for hp in range(NA_HEADS // 2):
            c0 = hp * pair_w
            q2 = qkv_ref[0, pl.ds(q0, nq), c0:c0 + pair_w]
            k2 = qkv_ref[0, pl.ds(k0, n_keys), NA_WIDTH + c0:NA_WIDTH + c0 + pair_w]
            v2 = qkv_ref[0, pl.ds(k0, n_keys), 2 * NA_WIDTH + c0:2 * NA_WIDTH + c0 + pair_w]
            zero = jnp.zeros_like(q2)
            q2 = q2 * scale
            qq = jnp.concatenate([jnp.where(lo_half, q2, zero), jnp.where(lo_half, zero, q2)], axis=0)
            s = lax.dot_general(qq, k2, (((1,), (1,)), ((), ())), preferred_element_type=_F32)
            s = s + bias_ref[case, hp]
            e = jnp.exp(s - jnp.max(s, axis=-1, keepdims=True))
            l = jnp.sum(e, axis=-1, keepdims=True)
            o = _dot(e.astype(_BF16), v2) / l
            o_ref[0, pl.ds(q0, nq), c0:c0 + pair_w] = jnp.where(lo_half, o[:nq], o[nq:])
        return carry

    lax.fori_loop(0, n_pairs, body, 0, unroll=2)


def _natten(qkv, rpb):
    B, L, _ = qkv.shape
    rows = L // GRID_W
    assert rows % Q_ROWS == 0 and rows >= Q_ROWS + WIN_ROWS - 1
    cases, case_of, kstart, n_keys = _natten_tables(rows)
    n_case = cases.shape[0]
    nq = Q_ROWS * GRID_W
    bias = _natten_bias(rpb, cases).reshape(n_case, NA_HEADS // 2, 2 * nq, n_keys)
    grid_spec = pltpu.PrefetchScalarGridSpec(
        num_scalar_prefetch=2,
        grid=(B,),
        in_specs=[pl.BlockSpec((1, L, 3 * NA_WIDTH), lambda b, *_: (b, 0, 0)),
                  _resident((n_case, NA_HEADS // 2, 2 * nq, n_keys), lambda b, *_: (0, 0, 0, 0))],
        out_specs=pl.BlockSpec((1, L, NA_WIDTH), lambda b, *_: (b, 0, 0)),
    )
    return pl.pallas_call(
        functools.partial(_natten_kernel, n_pairs=rows // Q_ROWS, n_keys=n_keys),
        grid_spec=grid_spec,
        out_shape=jax.ShapeDtypeStruct((B, L, NA_WIDTH), _F32),
        compiler_params=_cparams("arbitrary"),
        name="natten",
    )(jnp.asarray(case_of), jnp.asarray(kstart), qkv, bias)


_META_ID, _META_GATE, _META_POS = 0, 2, 4
ROUTER_ROWS = 8 + N_EXPERTS


def _split_bf16(v):
    hi = v.astype(_BF16)
    return hi, (v - hi.astype(_F32)).astype(_BF16)


def _out_router_kernel(yhe_ref, yho_ref, sel_ref, yna_ref, ghy_ref, gna_ref, wtop_ref, wbot_ref, x_ref, gffn_ref,
                       rwh_ref, rwl_ref, rb_ref, h1_ref, xs_ref, meta_ref, tile_ref, cnt_ref, carry_ref):
    @pl.when(pl.program_id(0) == 0)
    def _():
        carry_ref[...] = jnp.zeros_like(carry_ref)

    nhe = _rmsnorm(yhe_ref[...], ghy_ref[...]).astype(_BF16)
    nho = _rmsnorm(yho_ref[...], ghy_ref[...]).astype(_BF16)
    nh = (_dot(sel_ref[0], nhe) + _dot(sel_ref[1], nho)).astype(_BF16)
    nn = _rmsnorm(yna_ref[...], gna_ref[...]).astype(_BF16)
    h1 = x_ref[...] + (_dot(nh, wtop_ref[...]) + _dot(nn, wbot_ref[...]))
    h1_ref[...] = h1
    xn = _rmsnorm(h1, gffn_ref[...])
    tm = xn.shape[0]

    xh, xl = _split_bf16(xn)
    logits = _dot(xh, rwh_ref[...]) + (_dot(xh, rwl_ref[...]) + _dot(xl, rwh_ref[...]))
    lt = logits.T[:ROUTER_ROWS] + rb_ref[:, 0:1]

    neg = -jnp.inf
    row8 = lax.broadcasted_iota(jnp.int32, (EXPERTS_PER_GROUP, tm), 0).astype(_F32)
    col_max = lambda v: jnp.max(v, axis=0, keepdims=True)
    first_max = lambda v, m: jnp.min(jnp.where(v == m, row8, float(EXPERTS_PER_GROUP)), axis=0, keepdims=True)

    gl = jnp.where(row8 < N_GROUPS, lt[0:8], neg)
    gmax = col_max(gl)
    g_w = 1.0 / jnp.sum(jnp.exp(gl - gmax), axis=0, keepdims=True)
    g_idx = first_max(gl, gmax)
    el = lt[8:8 + EXPERTS_PER_GROUP]
    for g in range(1, N_GROUPS):
        el = jnp.where(g_idx == g, lt[8 + g * EXPERTS_PER_GROUP:8 + (g + 1) * EXPERTS_PER_GROUP], el)
    m1 = col_max(el)
    i1 = first_max(el, m1)
    el2 = jnp.where(row8 == i1, neg, el)
    m2 = col_max(el2)
    i2 = first_max(el2, m2)
    r = jnp.exp(m2 - m1)
    gate1 = g_w / (1.0 + r)
    gate2 = g_w * r / (1.0 + r)
    id1 = g_idx * EXPERTS_PER_GROUP + i1
    id2 = g_idx * EXPERTS_PER_GROUP + i2

    row_e = lax.broadcasted_iota(jnp.int32, (N_EXPERTS, tm), 0).astype(_F32)
    sel1 = row_e == id1
    sel2 = row_e == id2
    onehot = jnp.where(sel1 | sel2, 1.0, 0.0)
    tri_r = lax.broadcasted_iota(jnp.int32, (tm, tm), 0)
    tri_c = lax.broadcasted_iota(jnp.int32, (tm, tm), 1)
    tri = jnp.where(tri_r < tri_c, 1.0, 0.0).astype(_BF16)
    before = _dot(onehot.astype(_BF16), tri)
    cnt = jnp.sum(onehot, axis=1, keepdims=True)
    run = jnp.floor((cnt + (MOE_CHUNK - 1)) * (1.0 / MOE_CHUNK)) * MOE_CHUNK
    run_b = jnp.broadcast_to(run, (N_EXPERTS, ROUTER_LANES))
    e_r = lax.broadcasted_iota(jnp.int32, (N_EXPERTS, N_EXPERTS), 0)
    e_c = lax.broadcasted_iota(jnp.int32, (N_EXPERTS, N_EXPERTS), 1)
    start_b = _dot_f32(jnp.where(e_c < e_r, 1.0, 0.0), run_b)
    start = start_b[:, 0:1]
    pos1 = jnp.sum(jnp.where(sel1, start + before, 0.0), axis=0, keepdims=True)
    pos2 = jnp.sum(jnp.where(sel2, start + before, 0.0), axis=0, keepdims=True)
    xs_ref[...] = xn.astype(_BF16)

    base_b = carry_ref[...]
    lane = lax.broadcasted_iota(jnp.int32, (N_EXPERTS, ROUTER_LANES), 1)
    tile_ref[0] = jnp.where(lane == 0, start_b, jnp.where(lane == 1, run_b, jnp.where(lane == 2, base_b, 0.0)))
    carry_ref[...] = base_b + run_b
    cnt_ref[...] = base_b + run_b

    meta = jnp.zeros((8, tm), _F32)
    for k, v in enumerate((id1, id2, gate1, gate2, pos1, pos2)):
        meta = jnp.where(row8 == k, v, meta)
    meta_ref[...] = meta


def _out_router(y_hy_even, y_hy_odd, y_na, g_hy, g_na, w_out, x2, g_ffn, wg, bg, we, be):
    T, D = x2.shape
    half = ROW_TILE // 2
    sel_t = _parity_select(ROW_TILE).transpose(0, 2, 1)
    gpad = 8 - N_GROUPS
    rw = jnp.concatenate([wg.astype(_F32), jnp.zeros((D, gpad), _F32), we.astype(_F32)], axis=1)
    rw_hi, rw_lo = _split_bf16(jnp.pad(rw, ((0, 0), (0, ROUTER_LANES - ROUTER_ROWS))))
    rb = jnp.concatenate([bg.astype(_F32), jnp.zeros((gpad,), _F32), be.astype(_F32)])
    rb = jnp.broadcast_to(rb[:, None], (ROUTER_ROWS, ROUTER_LANES))
    w_bf = w_out.astype(_BF16)
    row = lambda n: pl.BlockSpec((ROW_TILE, n), lambda i: (i, 0))
    vec = lambda n: pl.BlockSpec((1, n), lambda i: (0, 0))
    mat = lambda r, c: pl.BlockSpec((r, c), lambda i: (0, 0))
    return pl.pallas_call(
        _out_router_kernel,
        grid=(T // ROW_TILE,),
        in_specs=[pl.BlockSpec((half, HY_WIDTH), lambda i: (i, 0)), pl.BlockSpec((half, HY_WIDTH), lambda i: (i, 0)),
                  pl.BlockSpec((2, ROW_TILE, half), lambda i: (0, 0, 0)),
                  row(NA_WIDTH), vec(HY_WIDTH), vec(NA_WIDTH),
                  mat(HY_WIDTH, D), mat(NA_WIDTH, D), row(D), vec(D),
                  mat(D, ROUTER_LANES), mat(D, ROUTER_LANES), mat(ROUTER_ROWS, ROUTER_LANES)],
        out_specs=[row(D),
                   row(D),
                   pl.BlockSpec((8, ROW_TILE), lambda i: (0, i)),
                   pl.BlockSpec((1, N_EXPERTS, ROUTER_LANES), lambda i: (i, 0, 0)),
                   mat(N_EXPERTS, ROUTER_LANES)],
        out_shape=[jax.ShapeDtypeStruct((T, D), _F32),
                   jax.ShapeDtypeStruct((T, D), _BF16),
                   jax.ShapeDtypeStruct((8, T), _F32),
                   jax.ShapeDtypeStruct((T // ROW_TILE, N_EXPERTS, ROUTER_LANES), _F32),
                   jax.ShapeDtypeStruct((N_EXPERTS, ROUTER_LANES), _F32)],
        scratch_shapes=[pltpu.VMEM((N_EXPERTS, ROUTER_LANES), _F32)],
        compiler_params=_cparams("arbitrary"),
        name="out_proj_router",
    )(y_hy_even, y_hy_odd, sel_t, y_na, g_hy.reshape(1, -1), g_na.reshape(1, -1), w_bf[:HY_WIDTH], w_bf[HY_WIDTH:],
      x2, g_ffn.reshape(1, D), rw_hi, rw_lo, rb)


def _chunk_rows(c):
    return pl.ds(pl.multiple_of(c * MOE_CHUNK, MOE_CHUNK), MOE_CHUNK)


_HI_HALF = 0xFFFF0000


def _pack_halves(v):
    c = v.shape[1] // 2
    bits = lax.bitcast_convert_type(v.astype(_BF16).astype(_F32), jnp.uint32)
    return (bits[:, c:] & jnp.uint32(_HI_HALF)) | lax.shift_right_logical(bits[:, :c], jnp.uint32(16))


def _unpack_halves(w):
    lo = lax.bitcast_convert_type(lax.shift_left(w, jnp.uint32(16)), _F32)
    hi = lax.bitcast_convert_type(w & jnp.uint32(_HI_HALF), _F32)
    return lo.astype(_BF16), hi.astype(_BF16)


def _scatter_kernel(zblk_ref, dst_ref, dst_prev_ref, meta_ref, xn_ref, xb_ref, zero_ref, xs_ref, sem, zero_sem):
    i = pl.program_id(0)
    slot = lax.rem(i, 2)

    @pl.when(i == 0)
    def _():
        zero_ref[...] = jnp.zeros_like(zero_ref)

        def blk_copy(j):
            start = pl.multiple_of(zblk_ref[j] * MOE_BLOCK, MOE_BLOCK)
            return pltpu.make_async_copy(zero_ref, xb_ref.at[pl.ds(start, MOE_BLOCK)], zero_sem)

        def start(j, c):
            @pl.when(zblk_ref[j] >= 0)
            def _():
                blk_copy(j).start()
            return c

        def wait(j, c):
            @pl.when(zblk_ref[j] >= 0)
            def _():
                blk_copy(j).wait()
            return c

        lax.fori_loop(0, zblk_ref.shape[0], start, 0)
        lax.fori_loop(0, zblk_ref.shape[0], wait, 0)

    tm = xn_ref.shape[0]
    meta = meta_ref[...]
    pos1, pos2 = meta[_META_POS:_META_POS + 1, :], meta[_META_POS + 1:_META_POS + 2, :]
    srow = lax.broadcasted_iota(jnp.int32, (SORT_ROWS, tm), 0).astype(_F32)
    perm = jnp.where((srow == pos1) | (srow == pos2), 1.0, 0.0).astype(_BF16)
    xs_ref[slot] = _pack_halves(_dot(perm, xn_ref[...]))

    def chunk_copy(d_ref, s, c):
        return pltpu.make_async_copy(xs_ref.at[s, _chunk_rows(c)], xb_ref.at[_chunk_rows(d_ref[0, 0, c])], sem.at[s])

    def start(c, carry):
        @pl.when(dst_ref[0, 0, c] >= 0)
        def _():
            chunk_copy(dst_ref, slot, c).start()
        return carry

    def wait_prev(c, carry):
        @pl.when(dst_prev_ref[0, 0, c] >= 0)
        def _():
            chunk_copy(dst_prev_ref, 1 - slot, c).wait()
        return carry

    def wait_own(c, carry):
        @pl.when(dst_ref[0, 0, c] >= 0)
        def _():
            chunk_copy(dst_ref, slot, c).wait()
        return carry

    lax.fori_loop(0, SORT_CHUNKS, start, 0)

    @pl.when(i > 0)
    def _():
        lax.fori_loop(0, SORT_CHUNKS, wait_prev, 0)

    @pl.when(i == pl.num_programs(0) - 1)
    def _():
        lax.fori_loop(0, SORT_CHUNKS, wait_own, 0)


def _moe_scatter(xn, meta, dst_chunk, zero_blocks, n_slots):
    T, D = xn.shape
    nt = T // ROW_TILE
    dst3 = dst_chunk.reshape(nt, 1, SORT_CHUNKS)
    grid_spec = pltpu.PrefetchScalarGridSpec(
        num_scalar_prefetch=1,
        grid=(nt,),
        in_specs=[pl.BlockSpec((1, 1, SORT_CHUNKS), lambda i, *_: (i, 0, 0), memory_space=pltpu.SMEM),
                  pl.BlockSpec((1, 1, SORT_CHUNKS), lambda i, *_: (jnp.maximum(i - 1, 0), 0, 0),
                               memory_space=pltpu.SMEM),
                  pl.BlockSpec((8, ROW_TILE), lambda i, *_: (0, i)),
                  pl.BlockSpec((ROW_TILE, D), lambda i, *_: (i, 0))],
        out_specs=pl.BlockSpec(memory_space=pl.ANY),
        scratch_shapes=[pltpu.VMEM((MOE_BLOCK, D // 2), jnp.uint32), pltpu.VMEM((2, SORT_ROWS, D // 2), jnp.uint32),
                        pltpu.SemaphoreType.DMA((2,)), pltpu.SemaphoreType.DMA(())],
    )
    return pl.pallas_call(
        _scatter_kernel,
        grid_spec=grid_spec,
        out_shape=jax.ShapeDtypeStruct((n_slots, D // 2), jnp.uint32),
        compiler_params=_cparams("arbitrary"),
        name="moe_scatter",
    )(zero_blocks, dst3, dst3, meta, xn)


def _expert_kernel(be_ref, nblk_ref, xb_ref, w1_ref, w3_ref, w2_ref, yb_ref, wb1_ref, wb3_ref, wb2_ref):
    i = pl.program_id(0)

    @pl.when(i < nblk_ref[0])
    def _():
        e = be_ref[i]
        e_prev = be_ref[jnp.maximum(i - 1, 0)]

        @pl.when((i == 0) | (e != e_prev))
        def _():
            wb1_ref[...] = w1_ref[0].astype(_BF16)
            wb3_ref[...] = w3_ref[0].astype(_BF16)
            wb2_ref[...] = w2_ref[0].astype(_BF16)

        x_lo, x_hi = _unpack_halves(xb_ref[...])
        half = x_lo.shape[1]
        up = lambda w_ref: _dot(x_lo, w_ref[:half, :]) + _dot(x_hi, w_ref[half:, :])
        a = up(wb1_ref)
        hid = (a * jax.nn.sigmoid(a) * up(wb3_ref)).astype(_BF16)
        yb_ref[...] = _pack_halves(_dot(hid, wb2_ref[...]))

    @pl.when(i >= nblk_ref[0])
    def _():
        yb_ref[...] = jnp.zeros_like(yb_ref)


def _moe_experts(xb, block_e, n_blk, w1, w3, w2):
    n_blocks = xb.shape[0] // MOE_BLOCK
    _, D, DE = w1.shape
    live = lambda i, nb: jnp.minimum(i, nb[0] - 1)
    grid_spec = pltpu.PrefetchScalarGridSpec(
        num_scalar_prefetch=2,
        grid=(n_blocks,),
        in_specs=[pl.BlockSpec((MOE_BLOCK, D // 2), lambda i, be, nb: (live(i, nb), 0)),
                  pl.BlockSpec((1, D, DE), lambda i, be, nb: (be[live(i, nb)], 0, 0)),
                  pl.BlockSpec((1, D, DE), lambda i, be, nb: (be[live(i, nb)], 0, 0)),
                  pl.BlockSpec((1, DE, D), lambda i, be, nb: (be[live(i, nb)], 0, 0))],
        out_specs=pl.BlockSpec((MOE_BLOCK, D // 2), lambda i, be, nb: (i, 0)),
        scratch_shapes=[pltpu.VMEM((D, DE), _BF16), pltpu.VMEM((D, DE), _BF16), pltpu.VMEM((DE, D), _BF16)],
    )
    return pl.pallas_call(
        _expert_kernel,
        grid_spec=grid_spec,
        out_shape=jax.ShapeDtypeStruct(xb.shape, jnp.uint32),
        compiler_params=_cparams("arbitrary"),
        name="moe_experts",
    )(block_e, n_blk, xb, w1, w3, w2)


def _final_kernel(dst_ref, dst_next_ref, h1_ref, comb_ref, yb_ref, p_ref, wg_ref, wp_ref, gple_ref, gfin_ref,
                  o_ref, ybuf_ref, sem):
    tm = h1_ref.shape[0]
    i = pl.program_id(0)
    slot = lax.rem(i, 2)

    def chunk_copy(d_ref, s, c):
        return pltpu.make_async_copy(yb_ref.at[_chunk_rows(d_ref[0, 0, c])], ybuf_ref.at[s, _chunk_rows(c)], sem.at[s])

    def gather(d_ref, s):
        def start(c, carry):
            @pl.when(d_ref[0, 0, c] >= 0)
            def _():
                chunk_copy(d_ref, s, c).start()
            return carry
        lax.fori_loop(0, SORT_CHUNKS, start, 0)

    @pl.when(i == 0)
    def _():
        ybuf_ref[...] = jnp.zeros_like(ybuf_ref)
        gather(dst_ref, 0)

    @pl.when(i + 1 < pl.num_programs(0))
    def _():
        gather(dst_next_ref, 1 - slot)

    def wait(c, carry):
        @pl.when(dst_ref[0, 0, c] >= 0)
        def _():
            chunk_copy(dst_ref, slot, c).wait()
        return carry
    lax.fori_loop(0, SORT_CHUNKS, wait, 0)

    comb = comb_ref[...]
    g1, g2 = comb[:, _META_GATE:_META_GATE + 1], comb[:, _META_GATE + 1:_META_GATE + 2]
    p1, p2 = comb[:, _META_POS:_META_POS + 1], comb[:, _META_POS + 1:_META_POS + 2]
    lane = lax.broadcasted_iota(jnp.int32, (tm, SORT_ROWS), 1).astype(_F32)
    wmat = (jnp.where(lane == p1, g1, 0.0) + jnp.where(lane == p2, g2, 0.0)).astype(_BF16)
    y_lo, y_hi = _unpack_halves(ybuf_ref[slot])
    moe = jnp.concatenate([_dot(wmat, y_lo), _dot(wmat, y_hi)], axis=1)
    h2 = h1_ref[...] + moe
    gate = jax.nn.sigmoid(_dot(_rmsnorm(h2, gple_ref[...]).astype(_BF16), wg_ref[...]))
    h3 = h2 + _dot(p_ref[...].astype(_BF16), wp_ref[...]) * gate
    o_ref[...] = _rmsnorm(h3, gfin_ref[...])


def _final(dst_chunk, h1, comb, yb, p2, w_gate, w_proj, g_ple, g_final):
    T, D = h1.shape
    nt = T // ROW_TILE
    PD = p2.shape[1]
    row = lambda n: pl.BlockSpec((ROW_TILE, n), lambda i: (i, 0))
    vec = lambda n: pl.BlockSpec((1, n), lambda i: (0, 0))
    dest3 = dst_chunk.reshape(nt, 1, SORT_CHUNKS)
    return pl.pallas_call(
        _final_kernel,
        grid=(nt,),
        in_specs=[pl.BlockSpec((1, 1, SORT_CHUNKS), lambda i: (i, 0, 0), memory_space=pltpu.SMEM),
                  pl.BlockSpec((1, 1, SORT_CHUNKS), lambda i: (jnp.minimum(i + 1, nt - 1), 0, 0),
                               memory_space=pltpu.SMEM),
                  row(D), row(ROUTER_LANES),
                  pl.BlockSpec(memory_space=pl.ANY),
                  row(PD),
                  pl.BlockSpec((D, D), lambda i: (0, 0)),
                  pl.BlockSpec((PD, D), lambda i: (0, 0)),
                  vec(D), vec(D)],
        out_specs=row(D),
        out_shape=jax.ShapeDtypeStruct((T, D), _F32),
        scratch_shapes=[pltpu.VMEM((2, SORT_ROWS, D // 2), jnp.uint32), pltpu.SemaphoreType.DMA((2,))],
        compiler_params=_cparams("arbitrary"),
        name="moe_gather_ple_final",
    )(dest3, dest3, h1, comb, yb, p2, w_gate.astype(_BF16), w_proj.astype(_BF16),
      g_ple.reshape(1, D), g_final.reshape(1, D))


def _routing_tables(tile_tbl, counts_f, n_blocks, n_tail):
    start = tile_tbl[:, :, 0].astype(jnp.int32)
    run = tile_tbl[:, :, 1].astype(jnp.int32)
    base = tile_tbl[:, :, 2].astype(jnp.int32)
    counts = counts_f[:, 0].astype(jnp.int32)
    padded = (counts + MOE_BLOCK - 1) // MOE_BLOCK * MOE_BLOCK
    e_iota = jnp.arange(N_EXPERTS, dtype=jnp.int32)
    pad_end = jnp.sum(jnp.where(e_iota[None, :] <= e_iota[:, None], padded[None, :], 0), axis=1)
    pad_start = pad_end - padded
    row0 = (jnp.arange(SORT_CHUNKS, dtype=jnp.int32) * MOE_CHUNK)[None, :, None]
    mine = (start[:, None, :] <= row0) & (row0 < (start + run)[:, None, :])
    slot0 = (pad_start[None, :] + base - start)[:, None, :] + row0
    dst_chunk = jnp.where(jnp.any(mine, axis=-1), jnp.sum(jnp.where(mine, slot0, 0), axis=-1) // MOE_CHUNK, -1)
    blk_row = jnp.arange(n_blocks, dtype=jnp.int32)[:, None] * MOE_BLOCK
    block_e = jnp.minimum(jnp.sum((pad_end[None, :] <= blk_row).astype(jnp.int32), axis=1), N_EXPERTS - 1)
    n_blk = (pad_end[-1:] // MOE_BLOCK).astype(jnp.int32)
    seg_last = jnp.where(padded > counts, pad_end // MOE_BLOCK - 1, -1)
    tail = n_blk[0] + jnp.arange(n_tail, dtype=jnp.int32)
    zero_blocks = jnp.concatenate([seg_last, jnp.where(tail < n_blocks, tail, -1)]).astype(jnp.int32)
    return dst_chunk.astype(jnp.int32), zero_blocks, block_e, n_blk


def _one_layer(h, p, g_mix, w_in, hy_conv_w, hy_conv_b, hy_f_w1, hy_f_b1, hy_f_freq1, hy_f_w2, hy_f_b2,
               hy_f_freq2, hy_f_w3, hy_skip, na_rpb, g_out_hy, g_out_na, w_out, g_ffn, router_wg, router_bg,
               router_we, router_be, exp_w1, exp_w3, exp_w2, g_ple, w_ple_gate, w_ple_proj):
    B, L, D = h.shape
    T = B * L
    assert T % ROW_TILE == 0 and L % FREQ_CHUNK == 0 and w_in.shape[1] == HY_COLS + 3 * NA_WIDTH
    x2 = h.reshape(T, D)
    u2, qkv = _in_proj(x2, g_mix, w_in)

    hf2 = _filter_mlp(L, hy_f_w1, hy_f_b1, hy_f_freq1, hy_f_w2, hy_f_b2, hy_f_freq2, hy_f_w3)
    tables = _half_dft_tables(L)
    k1r, k1i, k2r, k2i = _filter_spec(tables, hf2)
    y_hy_even, y_hy_odd = _hyena(u2.reshape(B, L // 2, 2 * HY_COLS), hy_conv_w.astype(_F32), hy_conv_b.astype(_F32),
                                 tables, k1r, k1i, k2r, k2i, hy_skip.astype(_F32))
    y_na = _natten(qkv.reshape(B, L, 3 * NA_WIDTH), na_rpb)

    h1, xn, meta, tile_tbl, counts = _out_router(
        y_hy_even.reshape(T // 2, HY_WIDTH), y_hy_odd.reshape(T // 2, HY_WIDTH), y_na.reshape(T, NA_WIDTH),
        g_out_hy, g_out_na, w_out, x2, g_ffn, router_wg, router_bg, router_we, router_be)
    max_rows = T * TOP_K + (T // ROW_TILE) * N_EXPERTS * (MOE_CHUNK - 1) + N_EXPERTS * (MOE_BLOCK - 1)
    n_blocks = -(-max_rows // MOE_BLOCK)
    n_tail = n_blocks - T * TOP_K // MOE_BLOCK
    dst_chunk, zero_blocks, block_e, n_blk = _routing_tables(tile_tbl, counts, n_blocks, n_tail)
    xb = _moe_scatter(xn, meta, dst_chunk, zero_blocks, n_blocks * MOE_BLOCK)
    yb = _moe_experts(xb, block_e, n_blk, exp_w1, exp_w3, exp_w2)
    comb = jnp.pad(meta[:_META_POS + TOP_K].T, ((0, 0), (0, ROUTER_LANES - _META_POS - TOP_K)))
    return h1, dst_chunk, comb, yb


def kernel(x, p, g_mix, w_in, hy_conv_w, hy_conv_b, hy_f_w1, hy_f_b1, hy_f_freq1, hy_f_w2, hy_f_b2, hy_f_freq2, hy_f_w3, hy_skip, na_rpb, g_out_hy, g_out_na, w_out, g_ffn, router_wg, router_bg, router_we, router_be, exp_w1, exp_w3, exp_w2, g_ple, w_ple_gate, w_ple_proj, g_final):
    depth = p.shape[0]
    assert depth == 1, "the final RMSNorm is fused into the last layer's kernel; one layer is supported"
    B, L, D = x.shape
    i = 0
    h1, dest, gates, yb = _one_layer(
        x, p[i], g_mix[i], w_in[i], hy_conv_w[i], hy_conv_b[i], hy_f_w1[i], hy_f_b1[i], hy_f_freq1[i],
        hy_f_w2[i], hy_f_b2[i], hy_f_freq2[i], hy_f_w3[i], hy_skip[i], na_rpb[i], g_out_hy[i], g_out_na[i],
        w_out[i], g_ffn[i], router_wg[i], router_bg[i], router_we[i], router_be[i], exp_w1[i], exp_w3[i],
        exp_w2[i], g_ple[i], w_ple_gate[i], w_ple_proj[i])
    out = _final(dest, h1, gates, yb, p[i].reshape(B * L, -1), w_ple_gate[i], w_ple_proj[i], g_ple[i], g_final)
    return out.reshape(B, L, D)
```

```python
import functools
import math

import numpy as np
import jax
import jax.numpy as jnp
from jax import lax
from jax.experimental import pallas as pl
from jax.experimental.pallas import tpu as pltpu

_F32 = jnp.float32
_BF16 = jnp.bfloat16

GRID_W = 64
HY_WIDTH = 512
NA_WIDTH = 512
NA_HEADS = 8
NA_HEAD_DIM = 64
HY_ORDER = 2
SHORT_CONV = 3
FILTER_EMB = 33
FILTER_BANDS = (FILTER_EMB - 1) // 2
DECAY_TARGET = 1e-2
FAST_DECAY_PCT = 0.3
SLOW_DECAY_PCT = 1.5
WIN_ROWS = 8
WIN_COLS = 16
Q_ROWS = 2
N_GROUPS = 4
EXPERTS_PER_GROUP = 8
N_EXPERTS = N_GROUPS * EXPERTS_PER_GROUP
TOP_K = 2
MOE_BLOCK = 512
EPS = 1e-6
NEG_INF = -1e30
HY_COLS = (HY_ORDER + 1) * HY_WIDTH

V7X_LANES = 128
V7X_SUBLANES = 8
V7X_VMEM_LIMIT_BYTES = 56 * 2 ** 20

ROW_TILE = 512
FREQ_CHUNK = 512
HY_CH_TILE = 256
ROUTER_LANES = V7X_LANES
MOE_CHUNK = V7X_SUBLANES
SORT_ROWS = -(-(TOP_K * ROW_TILE + N_EXPERTS * (MOE_CHUNK - 1)) // V7X_LANES) * V7X_LANES
SORT_CHUNKS = SORT_ROWS // MOE_CHUNK
DMA_UNROLL = 8


def _cparams(*sem):
    return pltpu.CompilerParams(dimension_semantics=sem, vmem_limit_bytes=V7X_VMEM_LIMIT_BYTES)


def _resident(shape, index_map):
    return pl.BlockSpec(shape, index_map, pipeline_mode=pl.Buffered(1))


def _rmsnorm(x, g):
    return x * lax.rsqrt(jnp.mean(x * x, axis=-1, keepdims=True) + EPS) * g


def _dot(a, b):
    return jnp.dot(a, b, preferred_element_type=_F32)


def _dot_f32(a, b):
    return jnp.dot(a, b, preferred_element_type=_F32, precision=lax.Precision.HIGHEST)


def _parity_select(n):
    r = jnp.arange(n // 2, dtype=jnp.int32)[None, :, None]
    t = jnp.arange(n, dtype=jnp.int32)[None, None, :]
    p = jnp.arange(2, dtype=jnp.int32)[:, None, None]
    return (t == 2 * r + p).astype(_BF16)


def _in_proj_kernel(x_ref, g_ref, w_ref, sel_ref, u2_ref, qkv_ref):
    xn = _rmsnorm(x_ref[...], g_ref[...]).astype(_BF16)
    n_hy = u2_ref.shape[1] // 2
    xe = _dot(sel_ref[0], xn).astype(_BF16)
    xo = _dot(sel_ref[1], xn).astype(_BF16)
    for c0 in range(0, n_hy, 512):
        w = w_ref[:, c0:c0 + 512]
        u2_ref[:, c0:c0 + 512] = _dot(xe, w).astype(_BF16)
        u2_ref[:, n_hy + c0:n_hy + c0 + 512] = _dot(xo, w).astype(_BF16)
    for c0 in range(0, qkv_ref.shape[1], 512):
        qkv_ref[:, c0:c0 + 512] = _dot(xn, w_ref[:, n_hy + c0:n_hy + c0 + 512]).astype(_BF16)


def _in_proj(x2, g_mix, w_in):
    T, D = x2.shape
    n_in = w_in.shape[1]
    n_qkv = n_in - HY_COLS
    half = ROW_TILE // 2
    return pl.pallas_call(
        _in_proj_kernel,
        grid=(T // ROW_TILE,),
        in_specs=[pl.BlockSpec((ROW_TILE, D), lambda i: (i, 0)),
                  pl.BlockSpec((1, D), lambda i: (0, 0)),
                  _resident((D, n_in), lambda i: (0, 0)),
                  pl.BlockSpec((2, half, ROW_TILE), lambda i: (0, 0, 0))],
        out_specs=[pl.BlockSpec((half, 2 * HY_COLS), lambda i: (i, 0)),
                   pl.BlockSpec((ROW_TILE, n_qkv), lambda i: (i, 0))],
        out_shape=[jax.ShapeDtypeStruct((T // 2, 2 * HY_COLS), _BF16),
                   jax.ShapeDtypeStruct((T, n_qkv), _BF16)],
        compiler_params=_cparams("arbitrary"),
        name="in_proj",
    )(x2, g_mix.reshape(1, D), w_in.astype(_BF16), _parity_select(ROW_TILE))


def _filter_mlp_kernel(z_ref, w1_ref, b1_ref, f1_ref, w2_ref, b2_ref, f2_ref, w3_ref, delta_ref, hf_ref):
    z = z_ref[0]
    hid = jnp.sin(f1_ref[...] * (_dot_f32(z, w1_ref[...]) + b1_ref[...]))
    hid = jnp.sin(f2_ref[...] * (_dot_f32(hid, w2_ref[...]) + b2_ref[...]))
    hf = _dot_f32(hid, w3_ref[...])
    decay = jnp.exp(-z[:, 0:1] * delta_ref[...])
    tl = z.shape[0]
    row = lax.broadcasted_iota(jnp.int32, (tl, HY_WIDTH), 0)
    offset0 = (row == 0) & (pl.program_id(0) == 0) & (pl.program_id(1) == 0)
    for k in range(2 * HY_ORDER):
        blk = hf[:, k * HY_WIDTH:(k + 1) * HY_WIDTH] * decay
        if k >= HY_ORDER:
            blk = jnp.where(offset0, 0.0, blk)
        hf_ref[:, k * HY_WIDTH:(k + 1) * HY_WIDTH] = blk


def _filter_mlp(L, w1, b1, f1, w2, b2, f2, w3):
    t = jnp.linspace(0.0, 1.0, L, dtype=_F32)[:, None]
    w = 2.0 * math.pi * jnp.arange(L, dtype=_F32)[:, None] / L
    bands = jnp.linspace(1e-4, FILTER_BANDS - 1, FILTER_BANDS, dtype=_F32)[None, :]
    z = jnp.concatenate([t, jnp.cos(bands * w), -jnp.sin(bands * w)], axis=-1)
    z = jnp.pad(z, ((0, 0), (0, V7X_LANES - FILTER_EMB)))
    z = jnp.stack([z[0::2], z[1::2]])
    w1p = jnp.pad(w1.astype(_F32), ((0, V7X_LANES - FILTER_EMB), (0, 0)))
    max_decay = math.log(DECAY_TARGET) / FAST_DECAY_PCT
    min_decay = math.log(DECAY_TARGET) / SLOW_DECAY_PCT
    deltas = jnp.abs(jnp.linspace(min_decay, max_decay, HY_WIDTH, dtype=_F32))[None, :]
    hid = w1.shape[1]
    n_out = w3.shape[1]
    lh = L // 2
    tl = min(lh, ROW_TILE)
    full = lambda shape: pl.BlockSpec(shape, lambda p, i: (0, 0))
    return pl.pallas_call(
        _filter_mlp_kernel,
        grid=(2, lh // tl),
        in_specs=[pl.BlockSpec((1, tl, V7X_LANES), lambda p, i: (p, i, 0)),
                  full((V7X_LANES, hid)), full((1, hid)), full((1, hid)),
                  full((hid, hid)), full((1, hid)), full((1, hid)),
                  full((hid, n_out)), full((1, HY_WIDTH))],
        out_specs=pl.BlockSpec((tl, n_out), lambda p, i: (i, p)),
        out_shape=jax.ShapeDtypeStruct((lh, 2 * n_out), _F32),
        compiler_params=_cparams("arbitrary", "arbitrary"),
        name="hyena_filter_mlp",
    )(z, w1p, b1.reshape(1, hid), f1.reshape(1, hid), w2.astype(_F32), b2.reshape(1, hid),
      f2.reshape(1, hid), w3.astype(_F32), deltas)


def _freq_chunks(gp):
    return [(r0, min(FREQ_CHUNK, gp - r0)) for r0 in range(0, gp, FREQ_CHUNK)]


def _half_dft_tables(L):
    lh, n, sb = L // 2, 2 * L, 32
    gp = -(-(lh + 1) // V7X_SUBLANES) * V7X_SUBLANES
    assert lh % sb == 0
    g = jnp.arange(gp, dtype=jnp.int32)
    live = (g <= lh)[:, None]
    ang = lambda s: ((g[:, None] * s[None, :]) % L).astype(_F32) * (2.0 * math.pi / L)
    a_hi = ang(jnp.arange(lh // sb, dtype=jnp.int32) * sb)
    a_lo = ang(jnp.arange(sb, dtype=jnp.int32))
    c_hi, s_hi, c_lo, s_lo = jnp.cos(a_hi), jnp.sin(a_hi), jnp.cos(a_lo), jnp.sin(a_lo)
    cosm = (c_hi[:, :, None] * c_lo[:, None, :] - s_hi[:, :, None] * s_lo[:, None, :]).reshape(gp, lh)
    sinm = (s_hi[:, :, None] * c_lo[:, None, :] + c_hi[:, :, None] * s_lo[:, None, :]).reshape(gp, lh)
    cosm = jnp.where(live, cosm, 0.0)
    sinm = jnp.where(live, sinm, 0.0)
    ff = jnp.concatenate([cosm, -sinm], axis=0).astype(_BF16)
    wgt = jnp.where((g == 0) | (g == lh), 1.0, 2.0)[:, None] / n
    gf = jnp.concatenate([(cosm * wgt).T, (-sinm * wgt).T], axis=1).astype(_BF16)
    tw = g.astype(_F32)[:, None] * (2.0 * math.pi / n)
    wr = jnp.broadcast_to(jnp.cos(tw), (gp, HY_CH_TILE))
    wi = jnp.broadcast_to(-jnp.sin(tw), (gp, HY_CH_TILE))
    return ff, gf, wr, wi


def _half_spectra(ff_ref, wr, wi, xe, xo, r0, n):
    gp = ff_ref.shape[0] // 2
    fc = ff_ref[r0:r0 + n, :]
    fs = ff_ref[gp + r0:gp + r0 + n, :]
    er, ei = _dot(fc, xe), _dot(fs, xe)
    orr, oi = _dot(fc, xo), _dot(fs, xo)
    pr = wr * orr - wi * oi
    pi = wr * oi + wi * orr
    return (er + pr, ei + pi), (er - pr, pi - ei)


def _filter_spec_kernel(ff_ref, wr_ref, wi_ref, fe_ref, fo_ref, be_ref, bo_ref, k1r_ref, k1i_ref, k2r_ref, k2i_ref):
    fe, fo, be, bo = (r[...].astype(_BF16) for r in (fe_ref, fo_ref, be_ref, bo_ref))
    for r0, n in _freq_chunks(wr_ref.shape[0]):
        wr, wi = wr_ref[r0:r0 + n, :], wi_ref[r0:r0 + n, :]
        (f1r, f1i), (f2r, f2i) = _half_spectra(ff_ref, wr, wi, fe, fo, r0, n)
        (b1r, b1i), (b2r, b2i) = _half_spectra(ff_ref, wr, wi, be, bo, r0, n)
        k1r_ref[0, r0:r0 + n, :] = f1r + b1r
        k1i_ref[0, r0:r0 + n, :] = f1i - b1i
        k2r_ref[0, r0:r0 + n, :] = f2r + b2r
        k2i_ref[0, r0:r0 + n, :] = f2i - b2i


def _filter_spec(tables, hf2):
    ff, _, wr, wi = tables
    lh, gp = hf2.shape[0], wr.shape[0]
    n_ct = HY_WIDTH // HY_CH_TILE
    n_blk = hf2.shape[1] // 2 // HY_CH_TILE
    col = lambda blk0: pl.BlockSpec((lh, HY_CH_TILE), lambda j: (0, blk0 + j))
    out = jax.ShapeDtypeStruct((HY_ORDER, gp, HY_WIDTH), _F32)
    ospec = pl.BlockSpec((1, gp, HY_CH_TILE), lambda j: (j // n_ct, 0, j % n_ct))
    full = lambda a: pl.BlockSpec(a.shape, lambda j: (0, 0))
    return pl.pallas_call(
        _filter_spec_kernel,
        grid=(HY_ORDER * n_ct,),
        in_specs=[full(ff), full(wr), full(wi),
                  col(0), col(n_blk),
                  col(HY_ORDER * n_ct), col(n_blk + HY_ORDER * n_ct)],
        out_specs=[ospec] * 4,
        out_shape=[out] * 4,
        compiler_params=_cparams("arbitrary"),
        name="hyena_filter_spectrum",
    )(ff, wr, wi, hf2, hf2, hf2, hf2)


def _parity_conv(ze, zo, w_ref, b_ref):
    lh = ze.shape[0]
    row = lax.broadcasted_iota(jnp.int32, ze.shape, 0)
    zo_prev = jnp.where(row == 0, 0.0, pltpu.roll(zo, 1, axis=0))
    ze_next = jnp.where(row == lh - 1, 0.0, pltpu.roll(ze, lh - 1, axis=0))
    w0, w1, w2, b = w_ref[0:1, :], w_ref[1:2, :], w_ref[2:3, :], b_ref[...]
    return ((b + zo_prev * w0) + ze * w1) + zo * w2, ((b + ze * w0) + zo * w1) + ze_next * w2


def _hyena_kernel(ze_ref, zo_ref, zw_ref, zb_ref, ge_ref, go_ref, gw_ref, gb_ref, skip_ref, ff_ref, gf_ref,
                  wr_ref, wi_ref, k1r_ref, k1i_ref, k2r_ref, k2i_ref, oe_ref, oo_ref, a_ref, b_ref, *, conv_input):
    ze, zo = ze_ref[0].astype(_F32), zo_ref[0].astype(_F32)
    if conv_input:
        ze, zo = _parity_conv(ze, zo, zw_ref, zb_ref)
    xe, xo = ze.astype(_BF16), zo.astype(_BF16)
    gp = wr_ref.shape[0]
    for r0, n in _freq_chunks(gp):
        rows = slice(r0, r0 + n)
        wr, wi = wr_ref[rows, :], wi_ref[rows, :]
        (x1r, x1i), (x2r, x2i) = _half_spectra(ff_ref, wr, wi, xe, xo, r0, n)
        k1r, k1i, k2r, k2i = k1r_ref[0, rows, :], k1i_ref[0, rows, :], k2r_ref[0, rows, :], k2i_ref[0, rows, :]
        y1r, y1i = x1r * k1r - x1i * k1i, x1r * k1i + x1i * k1r
        y2r, y2i = x2r * k2r - x2i * k2i, x2r * k2i + x2i * k2r
        tr, ti = y1r - y2r, y1i + y2i
        a_ref[rows, :] = (y1r + y2r).astype(_BF16)
        a_ref[gp + r0:gp + r0 + n, :] = (y1i - y2i).astype(_BF16)
        b_ref[rows, :] = (tr * wr + ti * wi).astype(_BF16)
        b_ref[gp + r0:gp + r0 + n, :] = (ti * wr - tr * wi).astype(_BF16)
    ge, go = _parity_conv(ge_ref[0].astype(_F32), go_ref[0].astype(_F32), gw_ref, gb_ref)
    skip = skip_ref[0]
    for r0 in range(0, ze.shape[0], FREQ_CHUNK):
        rows = slice(r0, r0 + FREQ_CHUNK)
        g = gf_ref[rows, :]
        oe_ref[0, rows, :] = ge[rows] * (_dot(g, a_ref[...]) + ze[rows] * skip)
        oo_ref[0, rows, :] = go[rows] * (_dot(g, b_ref[...]) + zo[rows] * skip)


def _hyena(u2, conv_w, conv_b, tables, k1r, k1i, k2r, k2i, skip):
    ff, gf, wr, wi = tables
    B, lh, _ = u2.shape
    gp = wr.shape[0]
    assert lh % FREQ_CHUNK == 0
    n_ct = HY_WIDTH // HY_CH_TILE
    odd = HY_COLS // HY_CH_TILE
    conv_b2 = conv_b.reshape(1, HY_COLS)
    skip3 = skip.reshape(HY_ORDER, 1, HY_WIDTH)
    col = lambda blk0: pl.BlockSpec((1, lh, HY_CH_TILE), lambda c, b: (b, 0, blk0 + c))
    cw = lambda blk0: pl.BlockSpec((SHORT_CONV, HY_CH_TILE), lambda c, b: (0, blk0 + c))
    cb = lambda blk0: pl.BlockSpec((1, HY_CH_TILE), lambda c, b: (0, blk0 + c))
    const = lambda a: _resident(a.shape, lambda c, b: (0, 0))
    out = jax.ShapeDtypeStruct((B, lh, HY_WIDTH), _F32)
    ze_arr, zo_arr, ze_spec, zo_spec = u2, u2, col(2 * n_ct), col(odd + 2 * n_ct)
    for o in range(HY_ORDER):
        coef = lambda: pl.BlockSpec((1, gp, HY_CH_TILE), lambda c, b, o=o: (o, 0, c), pipeline_mode=pl.Buffered(1))
        ze_arr, zo_arr = pl.pallas_call(
            functools.partial(_hyena_kernel, conv_input=(o == 0)),
            grid=(n_ct, B),
            in_specs=[ze_spec, zo_spec, cw(2 * n_ct), cb(2 * n_ct),
                      col(o * n_ct), col(odd + o * n_ct), cw(o * n_ct), cb(o * n_ct),
                      pl.BlockSpec((1, 1, HY_CH_TILE), lambda c, b, o=o: (o, 0, c)),
                      const(ff), const(gf), const(wr), const(wi), coef(), coef(), coef(), coef()],
            out_specs=[col(0), col(0)],
            out_shape=[out, out],
            scratch_shapes=[pltpu.VMEM((2 * gp, HY_CH_TILE), _BF16), pltpu.VMEM((2 * gp, HY_CH_TILE), _BF16)],
            compiler_params=_cparams("arbitrary", "arbitrary"),
            name=f"hyena_order_{o}",
        )(ze_arr, zo_arr, conv_w, conv_b2, u2, u2, conv_w, conv_b2, skip3, ff, gf, wr, wi, k1r, k1i, k2r, k2i)
        ze_spec = zo_spec = col(0)
    return ze_arr, zo_arr


def _natten_tables(rows):
    kr = min(WIN_ROWS, rows)
    krb = min(Q_ROWS + kr - 1, rows)
    rs = np.clip(np.arange(rows) - kr // 2, 0, rows - kr)
    cs = np.clip(np.arange(GRID_W) - WIN_COLS // 2, 0, GRID_W - WIN_COLS)
    qc = np.tile(np.arange(GRID_W), Q_ROWS)[:, None]
    kc = np.tile(np.arange(GRID_W), krb)[None, :]
    cases, case_of, kstart = [], [], []
    for p in range(rows // Q_ROWS):
        k_r0 = min(rs[p * Q_ROWS], rows - krb)
        qr = (p * Q_ROWS + np.repeat(np.arange(Q_ROWS), GRID_W))[:, None]
        kr_ = (k_r0 + np.repeat(np.arange(krb), GRID_W))[None, :]
        valid = ((kr_ >= rs[qr]) & (kr_ < rs[qr] + kr) & (kc >= cs[qc]) & (kc < cs[qc] + WIN_COLS))
        dr = np.clip(kr_ - qr + WIN_ROWS - 1, 0, 2 * WIN_ROWS - 2)
        dc = np.clip(kc - qc + WIN_COLS - 1, 0, 2 * WIN_COLS - 2)
        idx = np.where(valid, dr * (2 * WIN_COLS - 1) + dc, -1).astype(np.int32)
        for ci, c in enumerate(cases):
            if np.array_equal(c, idx):
                break
        else:
            ci = len(cases)
            cases.append(idx)
        case_of.append(ci)
        kstart.append(k_r0 * GRID_W)
    return np.stack(cases), np.asarray(case_of, np.int32), np.asarray(kstart, np.int32), krb * GRID_W


def _natten_bias(rpb, cases):
    n_case, nq, nk = cases.shape
    krb = nk // GRID_W
    n_dr, n_dc = 2 * WIN_ROWS - 1, 2 * WIN_COLS - 1
    c5 = cases.reshape(n_case, Q_ROWS, GRID_W, krb, GRID_W)
    dr_blk = np.where(c5 >= 0, c5 // n_dc, -1).max(axis=(2, 4))
    dc_idx = np.clip(np.arange(GRID_W)[None, :] - np.arange(GRID_W)[:, None] + WIN_COLS - 1, 0, n_dc - 1)
    assert np.all((c5 < 0) | (c5 // n_dc == dr_blk[:, :, None, :, None]))
    assert np.all((c5 < 0) | (c5 % n_dc == dc_idx[None, None, :, None, :]))
    onehot_c = (dc_idx.reshape(1, -1) == np.arange(n_dc)[:, None]).astype(np.float32)
    sel = (np.maximum(dr_blk, 0).reshape(-1, 1) == np.arange(n_dr)[None, :]).astype(np.float32)
    hi = lax.Precision.HIGHEST
    toe = jnp.einsum('hdj,jx->hdx', rpb.astype(_F32), onehot_c, precision=hi)
    blk = jnp.einsum('sd,hdx->hsx', sel, toe, precision=hi)
    blk = blk.reshape(NA_HEADS, n_case, Q_ROWS, krb, GRID_W, GRID_W).transpose(1, 0, 2, 4, 3, 5)
    return jnp.where((cases >= 0)[:, None], blk.reshape(n_case, NA_HEADS, nq, nk), NEG_INF)


def _natten_kernel(case_ref, kstart_ref, qkv_ref, bias_ref, o_ref, *, n_pairs, n_keys):
    nq = Q_ROWS * GRID_W
    pair_w = 2 * NA_HEAD_DIM
    lane = lax.broadcasted_iota(jnp.int32, (nq, pair_w), 1)
    lo_half = lane < NA_HEAD_DIM
    scale = NA_HEAD_DIM ** -0.5

    def body(p, carry):
        q0 = pl.multiple_of(p * nq, nq)
        k0 = pl.multiple_of(kstart_ref[p], GRID_W)
        case = case_ref[p]
        for hp in range(NA_HEADS // 2):
            c0 = hp * pair_w
            q2 = qkv_ref[0, pl.ds(q0, nq), c0:c0 + pair_w]
            k2 = qkv_ref[0, pl.ds(k0, n_keys), NA_WIDTH + c0:NA_WIDTH + c0 + pair_w]
            v2 = qkv_ref[0, pl.ds(k0, n_keys), 2 * NA_WIDTH + c0:2 * NA_WIDTH + c0 + pair_w]
            zero = jnp.zeros_like(q2)
            q2 = q2 * scale
            qq = jnp.concatenate([jnp.where(lo_half, q2, zero), jnp.where(lo_half, zero, q2)], axis=0)
            s = lax.dot_general(qq, k2, (((1,), (1,)), ((), ())), preferred_element_type=_F32)
            s = s + bias_ref[case, hp]
            e = jnp.exp(s - jnp.max(s, axis=-1, keepdims=True))
            l = jnp.sum(e, axis=-1, keepdims=True)
            o = _dot(e.astype(_BF16), v2) / l
            o_ref[0, pl.ds(q0, nq), c0:c0 + pair_w] = jnp.where(lo_half, o[:nq], o[nq:])
        return carry

    lax.fori_loop(0, n_pairs, body, 0, unroll=2)


def _natten(qkv, rpb):
    B, L, _ = qkv.shape
    rows = L // GRID_W
    assert rows % Q_ROWS == 0 and rows >= Q_ROWS + WIN_ROWS - 1
    cases, case_of, kstart, n_keys = _natten_tables(rows)
    n_case = cases.shape[0]
    nq = Q_ROWS * GRID_W
    bias = _natten_bias(rpb, cases).reshape(n_case, NA_HEADS // 2, 2 * nq, n_keys)
    grid_spec = pltpu.PrefetchScalarGridSpec(
        num_scalar_prefetch=2,
        grid=(B,),
        in_specs=[pl.BlockSpec((1, L, 3 * NA_WIDTH), lambda b, *_: (b, 0, 0)),
                  _resident((n_case, NA_HEADS // 2, 2 * nq, n_keys), lambda b, *_: (0, 0, 0, 0))],
        out_specs=pl.BlockSpec((1, L, NA_WIDTH), lambda b, *_: (b, 0, 0)),
    )
    return pl.pallas_call(
        functools.partial(_natten_kernel, n_pairs=rows // Q_ROWS, n_keys=n_keys),
        grid_spec=grid_spec,
        out_shape=jax.ShapeDtypeStruct((B, L, NA_WIDTH), _F32),
        compiler_params=_cparams("arbitrary"),
        name="natten",
    )(jnp.asarray(case_of), jnp.asarray(kstart), qkv, bias)


_META_ID, _META_GATE, _META_POS = 0, 2, 4
ROUTER_ROWS = 8 + N_EXPERTS


def _split_bf16(v):
    hi = v.astype(_BF16)
    return hi, (v - hi.astype(_F32)).astype(_BF16)


def _out_router_kernel(yhe_ref, yho_ref, sel_ref, yna_ref, ghy_ref, gna_ref, wtop_ref, wbot_ref, x_ref, gffn_ref,
                       rwh_ref, rwl_ref, rb_ref, h1_ref, xs_ref, meta_ref, tile_ref, cnt_ref, carry_ref):
    @pl.when(pl.program_id(0) == 0)
    def _():
        carry_ref[...] = jnp.zeros_like(carry_ref)

    nhe = _rmsnorm(yhe_ref[...], ghy_ref[...]).astype(_BF16)
    nho = _rmsnorm(yho_ref[...], ghy_ref[...]).astype(_BF16)
    nh = (_dot(sel_ref[0], nhe) + _dot(sel_ref[1], nho)).astype(_BF16)
    nn = _rmsnorm(yna_ref[...], gna_ref[...]).astype(_BF16)
    h1 = x_ref[...] + (_dot(nh, wtop_ref[...]) + _dot(nn, wbot_ref[...]))
    h1_ref[...] = h1
    xn = _rmsnorm(h1, gffn_ref[...])
    tm = xn.shape[0]

    xh, xl = _split_bf16(xn)
    logits = _dot(xh, rwh_ref[...]) + (_dot(xh, rwl_ref[...]) + _dot(xl, rwh_ref[...]))
    lt = logits.T[:ROUTER_ROWS] + rb_ref[:, 0:1]

    neg = -jnp.inf
    row8 = lax.broadcasted_iota(jnp.int32, (EXPERTS_PER_GROUP, tm), 0).astype(_F32)
    col_max = lambda v: jnp.max(v, axis=0, keepdims=True)
    first_max = lambda v, m: jnp.min(jnp.where(v == m, row8, float(EXPERTS_PER_GROUP)), axis=0, keepdims=True)

    gl = jnp.where(row8 < N_GROUPS, lt[0:8], neg)
    gmax = col_max(gl)
    g_w = 1.0 / jnp.sum(jnp.exp(gl - gmax), axis=0, keepdims=True)
    g_idx = first_max(gl, gmax)
    el = lt[8:8 + EXPERTS_PER_GROUP]
    for g in range(1, N_GROUPS):
        el = jnp.where(g_idx == g, lt[8 + g * EXPERTS_PER_GROUP:8 + (g + 1) * EXPERTS_PER_GROUP], el)
    m1 = col_max(el)
    i1 = first_max(el, m1)
    el2 = jnp.where(row8 == i1, neg, el)
    m2 = col_max(el2)
    i2 = first_max(el2, m2)
    r = jnp.exp(m2 - m1)
    gate1 = g_w / (1.0 + r)
    gate2 = g_w * r / (1.0 + r)
    id1 = g_idx * EXPERTS_PER_GROUP + i1
    id2 = g_idx * EXPERTS_PER_GROUP + i2

    row_e = lax.broadcasted_iota(jnp.int32, (N_EXPERTS, tm), 0).astype(_F32)
    sel1 = row_e == id1
    sel2 = row_e == id2
    onehot = jnp.where(sel1 | sel2, 1.0, 0.0)
    tri_r = lax.broadcasted_iota(jnp.int32, (tm, tm), 0)
    tri_c = lax.broadcasted_iota(jnp.int32, (tm, tm), 1)
    tri = jnp.where(tri_r < tri_c, 1.0, 0.0).astype(_BF16)
    before = _dot(onehot.astype(_BF16), tri)
    cnt = jnp.sum(onehot, axis=1, keepdims=True)
    run = jnp.floor((cnt + (MOE_CHUNK - 1)) * (1.0 / MOE_CHUNK)) * MOE_CHUNK
    run_b = jnp.broadcast_to(run, (N_EXPERTS, ROUTER_LANES))
    e_r = lax.broadcasted_iota(jnp.int32, (N_EXPERTS, N_EXPERTS), 0)
    e_c = lax.broadcasted_iota(jnp.int32, (N_EXPERTS, N_EXPERTS), 1)
    start_b = _dot_f32(jnp.where(e_c < e_r, 1.0, 0.0), run_b)
    start = start_b[:, 0:1]
    pos1 = jnp.sum(jnp.where(sel1, start + before, 0.0), axis=0, keepdims=True)
    pos2 = jnp.sum(jnp.where(sel2, start + before, 0.0), axis=0, keepdims=True)
    xs_ref[...] = xn.astype(_BF16)

    base_b = carry_ref[...]
    lane = lax.broadcasted_iota(jnp.int32, (N_EXPERTS, ROUTER_LANES), 1)
    tile_ref[0] = jnp.where(lane == 0, start_b, jnp.where(lane == 1, run_b, jnp.where(lane == 2, base_b, 0.0)))
    carry_ref[...] = base_b + run_b
    cnt_ref[...] = base_b + run_b

    meta = jnp.zeros((8, tm), _F32)
    for k, v in enumerate((id1, id2, gate1, gate2, pos1, pos2)):
        meta = jnp.where(row8 == k, v, meta)
    meta_ref[...] = meta


def _out_router(y_hy_even, y_hy_odd, y_na, g_hy, g_na, w_out, x2, g_ffn, wg, bg, we, be):
    T, D = x2.shape
    half = ROW_TILE // 2
    sel_t = _parity_select(ROW_TILE).transpose(0, 2, 1)
    gpad = 8 - N_GROUPS
    rw = jnp.concatenate([wg.astype(_F32), jnp.zeros((D, gpad), _F32), we.astype(_F32)], axis=1)
    rw_hi, rw_lo = _split_bf16(jnp.pad(rw, ((0, 0), (0, ROUTER_LANES - ROUTER_ROWS))))
    rb = jnp.concatenate([bg.astype(_F32), jnp.zeros((gpad,), _F32), be.astype(_F32)])
    rb = jnp.broadcast_to(rb[:, None], (ROUTER_ROWS, ROUTER_LANES))
    w_bf = w_out.astype(_BF16)
    row = lambda n: pl.BlockSpec((ROW_TILE, n), lambda i: (i, 0))
    vec = lambda n: pl.BlockSpec((1, n), lambda i: (0, 0))
    mat = lambda r, c: pl.BlockSpec((r, c), lambda i: (0, 0))
    return pl.pallas_call(
        _out_router_kernel,
        grid=(T // ROW_TILE,),
        in_specs=[pl.BlockSpec((half, HY_WIDTH), lambda i: (i, 0)), pl.BlockSpec((half, HY_WIDTH), lambda i: (i, 0)),
                  pl.BlockSpec((2, ROW_TILE, half), lambda i: (0, 0, 0)),
                  row(NA_WIDTH), vec(HY_WIDTH), vec(NA_WIDTH),
                  mat(HY_WIDTH, D), mat(NA_WIDTH, D), row(D), vec(D),
                  mat(D, ROUTER_LANES), mat(D, ROUTER_LANES), mat(ROUTER_ROWS, ROUTER_LANES)],
        out_specs=[row(D),
                   row(D),
                   pl.BlockSpec((8, ROW_TILE), lambda i: (0, i)),
                   pl.BlockSpec((1, N_EXPERTS, ROUTER_LANES), lambda i: (i, 0, 0)),
                   mat(N_EXPERTS, ROUTER_LANES)],
        out_shape=[jax.ShapeDtypeStruct((T, D), _F32),
                   jax.ShapeDtypeStruct((T, D), _BF16),
                   jax.ShapeDtypeStruct((8, T), _F32),
                   jax.ShapeDtypeStruct((T // ROW_TILE, N_EXPERTS, ROUTER_LANES), _F32),
                   jax.ShapeDtypeStruct((N_EXPERTS, ROUTER_LANES), _F32)],
        scratch_shapes=[pltpu.VMEM((N_EXPERTS, ROUTER_LANES), _F32)],
        compiler_params=_cparams("arbitrary"),
        name="out_proj_router",
    )(y_hy_even, y_hy_odd, sel_t, y_na, g_hy.reshape(1, -1), g_na.reshape(1, -1), w_bf[:HY_WIDTH], w_bf[HY_WIDTH:],
      x2, g_ffn.reshape(1, D), rw_hi, rw_lo, rb)


def _chunk_rows(c):
    return pl.ds(pl.multiple_of(c * MOE_CHUNK, MOE_CHUNK), MOE_CHUNK)


_HI_HALF = 0xFFFF0000


def _pack_halves(v):
    c = v.shape[1] // 2
    bits = lax.bitcast_convert_type(v.astype(_BF16).astype(_F32), jnp.uint32)
    return (bits[:, c:] & jnp.uint32(_HI_HALF)) | lax.shift_right_logical(bits[:, :c], jnp.uint32(16))


def _unpack_halves(w):
    lo = lax.bitcast_convert_type(lax.shift_left(w, jnp.uint32(16)), _F32)
    hi = lax.bitcast_convert_type(w & jnp.uint32(_HI_HALF), _F32)
    return lo.astype(_BF16), hi.astype(_BF16)


def _scatter_kernel(zblk_ref, dst_ref, meta_ref, xn_ref, xb_ref, zero_ref, xs_ref, sem, zero_sem):
    i = pl.program_id(0)
    slot = lax.rem(i, 2)

    @pl.when(i == 0)
    def _():
        zero_ref[...] = jnp.zeros_like(zero_ref)

        def blk_copy(j):
            start = pl.multiple_of(zblk_ref[j] * MOE_BLOCK, MOE_BLOCK)
            return pltpu.make_async_copy(zero_ref, xb_ref.at[pl.ds(start, MOE_BLOCK)], zero_sem)

        def start(j, c):
            @pl.when(zblk_ref[j] >= 0)
            def _():
                blk_copy(j).start()
            return c

        def wait(j, c):
            @pl.when(zblk_ref[j] >= 0)
            def _():
                blk_copy(j).wait()
            return c

        lax.fori_loop(0, zblk_ref.shape[0], start, 0)
        lax.fori_loop(0, zblk_ref.shape[0], wait, 0)

    tm = xn_ref.shape[0]
    meta = meta_ref[...]
    pos1, pos2 = meta[_META_POS:_META_POS + 1, :], meta[_META_POS + 1:_META_POS + 2, :]
    srow = lax.broadcasted_iota(jnp.int32, (SORT_ROWS, tm), 0).astype(_F32)
    perm = jnp.where((srow == pos1) | (srow == pos2), 1.0, 0.0).astype(_BF16)
    xs_ref[slot] = _pack_halves(_dot(perm, xn_ref[...]))

    def start(c, carry):
        pltpu.make_async_copy(xs_ref.at[slot, _chunk_rows(c)], xb_ref.at[_chunk_rows(dst_ref[0, 0, c])],
                              sem.at[slot]).start()
        return carry

    lax.fori_loop(0, SORT_CHUNKS, start, 0, unroll=DMA_UNROLL)

    wait_all = lambda s: pltpu.make_async_copy(xs_ref.at[s], xs_ref.at[s], sem.at[s]).wait()

    @pl.when(i > 0)
    def _():
        wait_all(1 - slot)

    @pl.when(i == pl.num_programs(0) - 1)
    def _():
        wait_all(slot)


def _moe_scatter(xn, meta, dst_chunk, zero_blocks, n_slots):
    T, D = xn.shape
    nt = T // ROW_TILE
    dst3 = dst_chunk.reshape(nt, 1, SORT_CHUNKS)
    grid_spec = pltpu.PrefetchScalarGridSpec(
        num_scalar_prefetch=1,
        grid=(nt,),
        in_specs=[pl.BlockSpec((1, 1, SORT_CHUNKS), lambda i, *_: (i, 0, 0), memory_space=pltpu.SMEM),
                  pl.BlockSpec((8, ROW_TILE), lambda i, *_: (0, i)),
                  pl.BlockSpec((ROW_TILE, D), lambda i, *_: (i, 0))],
        out_specs=pl.BlockSpec(memory_space=pl.ANY),
        scratch_shapes=[pltpu.VMEM((MOE_BLOCK, D // 2), jnp.uint32), pltpu.VMEM((2, SORT_ROWS, D // 2), jnp.uint32),
                        pltpu.SemaphoreType.DMA((2,)), pltpu.SemaphoreType.DMA(())],
    )
    return pl.pallas_call(
        _scatter_kernel,
        grid_spec=grid_spec,
        out_shape=jax.ShapeDtypeStruct((n_slots, D // 2), jnp.uint32),
        compiler_params=_cparams("arbitrary"),
        name="moe_scatter",
    )(zero_blocks, dst3, meta, xn)


def _expert_kernel(be_ref, nblk_ref, xb_ref, w1_ref, w3_ref, w2_ref, yb_ref, wb1_ref, wb3_ref, wb2_ref):
    i = pl.program_id(0)

    @pl.when(i < nblk_ref[0])
    def _():
        e = be_ref[i]
        e_prev = be_ref[jnp.maximum(i - 1, 0)]

        @pl.when((i == 0) | (e != e_prev))
        def _():
            wb1_ref[...] = w1_ref[0].astype(_BF16)
            wb3_ref[...] = w3_ref[0].astype(_BF16)
            wb2_ref[...] = w2_ref[0].astype(_BF16)

        x_lo, x_hi = _unpack_halves(xb_ref[...])
        half = x_lo.shape[1]
        up = lambda w_ref: _dot(x_lo, w_ref[:half, :]) + _dot(x_hi, w_ref[half:, :])
        a = up(wb1_ref)
        hid = (a * jax.nn.sigmoid(a) * up(wb3_ref)).astype(_BF16)
        yb_ref[...] = _pack_halves(_dot(hid, wb2_ref[...]))

    @pl.when(i >= nblk_ref[0])
    def _():
        yb_ref[...] = jnp.zeros_like(yb_ref)


def _moe_experts(xb, n_blocks, block_e, n_blk, w1, w3, w2):
    _, D, DE = w1.shape
    live = lambda i, nb: jnp.minimum(i, nb[0] - 1)
    grid_spec = pltpu.PrefetchScalarGridSpec(
        num_scalar_prefetch=2,
        grid=(n_blocks,),
        in_specs=[pl.BlockSpec((MOE_BLOCK, D // 2), lambda i, be, nb: (live(i, nb), 0)),
                  pl.BlockSpec((1, D, DE), lambda i, be, nb: (be[live(i, nb)], 0, 0)),
                  pl.BlockSpec((1, D, DE), lambda i, be, nb: (be[live(i, nb)], 0, 0)),
                  pl.BlockSpec((1, DE, D), lambda i, be, nb: (be[live(i, nb)], 0, 0))],
        out_specs=pl.BlockSpec((MOE_BLOCK, D // 2), lambda i, be, nb: (i, 0)),
        scratch_shapes=[pltpu.VMEM((D, DE), _BF16), pltpu.VMEM((D, DE), _BF16), pltpu.VMEM((DE, D), _BF16)],
    )
    return pl.pallas_call(
        _expert_kernel,
        grid_spec=grid_spec,
        out_shape=jax.ShapeDtypeStruct((n_blocks * MOE_BLOCK, D // 2), jnp.uint32),
        compiler_params=_cparams("arbitrary"),
        name="moe_experts",
    )(block_e, n_blk, xb, w1, w3, w2)


def _final_kernel(dst_ref, dst_next_ref, h1_ref, comb_ref, yb_ref, p_ref, wg_ref, wp_ref, gple_ref, gfin_ref,
                  o_ref, ybuf_ref, sem):
    tm = h1_ref.shape[0]
    i = pl.program_id(0)
    slot = lax.rem(i, 2)

    def gather(d_ref, s):
        def start(c, carry):
            pltpu.make_async_copy(yb_ref.at[_chunk_rows(d_ref[0, 0, c])], ybuf_ref.at[s, _chunk_rows(c)],
                                  sem.at[s]).start()
            return carry
        lax.fori_loop(0, SORT_CHUNKS, start, 0, unroll=DMA_UNROLL)

    @pl.when(i == 0)
    def _():
        gather(dst_ref, 0)

    @pl.when(i + 1 < pl.num_programs(0))
    def _():
        gather(dst_next_ref, 1 - slot)

    pltpu.make_async_copy(ybuf_ref.at[slot], ybuf_ref.at[slot], sem.at[slot]).wait()

    comb = comb_ref[...]
    g1, g2 = comb[:, _META_GATE:_META_GATE + 1], comb[:, _META_GATE + 1:_META_GATE + 2]
    p1, p2 = comb[:, _META_POS:_META_POS + 1], comb[:, _META_POS + 1:_META_POS + 2]
    lane = lax.broadcasted_iota(jnp.int32, (tm, SORT_ROWS), 1).astype(_F32)
    wmat = (jnp.where(lane == p1, g1, 0.0) + jnp.where(lane == p2, g2, 0.0)).astype(_BF16)
    y_lo, y_hi = _unpack_halves(ybuf_ref[slot])
    moe = jnp.concatenate([_dot(wmat, y_lo), _dot(wmat, y_hi)], axis=1)
    h2 = h1_ref[...] + moe
    gate = jax.nn.sigmoid(_dot(_rmsnorm(h2, gple_ref[...]).astype(_BF16), wg_ref[...]))
    h3 = h2 + _dot(p_ref[...].astype(_BF16), wp_ref[...]) * gate
    o_ref[...] = _rmsnorm(h3, gfin_ref[...])


def _final(dst_chunk, h1, comb, yb, p2, w_gate, w_proj, g_ple, g_final):
    T, D = h1.shape
    nt = T // ROW_TILE
    PD = p2.shape[1]
    row = lambda n: pl.BlockSpec((ROW_TILE, n), lambda i: (i, 0))
    vec = lambda n: pl.BlockSpec((1, n), lambda i: (0, 0))
    dest3 = dst_chunk.reshape(nt, 1, SORT_CHUNKS)
    return pl.pallas_call(
        _final_kernel,
        grid=(nt,),
        in_specs=[pl.BlockSpec((1, 1, SORT_CHUNKS), lambda i: (i, 0, 0), memory_space=pltpu.SMEM),
                  pl.BlockSpec((1, 1, SORT_CHUNKS), lambda i: (jnp.minimum(i + 1, nt - 1), 0, 0),
                               memory_space=pltpu.SMEM),
                  row(D), row(ROUTER_LANES),
                  pl.BlockSpec(memory_space=pl.ANY),
                  row(PD),
                  pl.BlockSpec((D, D), lambda i: (0, 0)),
                  pl.BlockSpec((PD, D), lambda i: (0, 0)),
                  vec(D), vec(D)],
        out_specs=row(D),
        out_shape=jax.ShapeDtypeStruct((T, D), _F32),
        scratch_shapes=[pltpu.VMEM((2, SORT_ROWS, D // 2), jnp.uint32), pltpu.SemaphoreType.DMA((2,))],
        compiler_params=_cparams("arbitrary"),
        name="moe_gather_ple_final",
    )(dest3, dest3, h1, comb, yb, p2, w_gate.astype(_BF16), w_proj.astype(_BF16),
      g_ple.reshape(1, D), g_final.reshape(1, D))


def _routing_tables(tile_tbl, counts_f, n_blocks, n_tail):
    start = tile_tbl[:, :, 0].astype(jnp.int32)
    run = tile_tbl[:, :, 1].astype(jnp.int32)
    base = tile_tbl[:, :, 2].astype(jnp.int32)
    counts = counts_f[:, 0].astype(jnp.int32)
    padded = (counts + MOE_BLOCK - 1) // MOE_BLOCK * MOE_BLOCK
    e_iota = jnp.arange(N_EXPERTS, dtype=jnp.int32)
    pad_end = jnp.sum(jnp.where(e_iota[None, :] <= e_iota[:, None], padded[None, :], 0), axis=1)
    pad_start = pad_end - padded
    row0 = (jnp.arange(SORT_CHUNKS, dtype=jnp.int32) * MOE_CHUNK)[None, :, None]
    mine = (start[:, None, :] <= row0) & (row0 < (start + run)[:, None, :])
    used = jnp.any(mine, axis=-1)
    slot0 = (pad_start[None, :] + base - start)[:, None, :] + row0
    dst = jnp.sum(jnp.where(mine, slot0, 0), axis=-1) // MOE_CHUNK
    nt = start.shape[0]
    spare = (n_blocks * MOE_BLOCK // MOE_CHUNK + (jnp.arange(nt, dtype=jnp.int32) % 2)[:, None] * SORT_CHUNKS
             + jnp.arange(SORT_CHUNKS, dtype=jnp.int32)[None, :])
    dst_scatter = jnp.where(used, dst, spare).astype(jnp.int32)
    dst_gather = jnp.where(used, dst, 0).astype(jnp.int32)
    blk_row = jnp.arange(n_blocks, dtype=jnp.int32)[:, None] * MOE_BLOCK
    block_e = jnp.minimum(jnp.sum((pad_end[None, :] <= blk_row).astype(jnp.int32), axis=1), N_EXPERTS - 1)
    n_blk = (pad_end[-1:] // MOE_BLOCK).astype(jnp.int32)
    seg_last = jnp.where(padded > counts, pad_end // MOE_BLOCK - 1, -1)
    tail = n_blk[0] + jnp.arange(n_tail, dtype=jnp.int32)
    spare_blocks = n_blocks + jnp.arange(2 * SORT_ROWS // MOE_BLOCK, dtype=jnp.int32)
    zero_blocks = jnp.concatenate([seg_last, jnp.where(tail < n_blocks, tail, -1), spare_blocks]).astype(jnp.int32)
    return dst_scatter, dst_gather, zero_blocks, block_e, n_blk


def _one_layer(h, p, g_mix, w_in, hy_conv_w, hy_conv_b, hy_f_w1, hy_f_b1, hy_f_freq1, hy_f_w2, hy_f_b2,
               hy_f_freq2, hy_f_w3, hy_skip, na_rpb, g_out_hy, g_out_na, w_out, g_ffn, router_wg, router_bg,
               router_we, router_be, exp_w1, exp_w3, exp_w2, g_ple, w_ple_gate, w_ple_proj):
    B, L, D = h.shape
    T = B * L
    assert T % ROW_TILE == 0 and L % FREQ_CHUNK == 0 and w_in.shape[1] == HY_COLS + 3 * NA_WIDTH
    x2 = h.reshape(T, D)
    u2, qkv = _in_proj(x2, g_mix, w_in)

    hf2 = _filter_mlp(L, hy_f_w1, hy_f_b1, hy_f_freq1, hy_f_w2, hy_f_b2, hy_f_freq2, hy_f_w3)
    tables = _half_dft_tables(L)
    k1r, k1i, k2r, k2i = _filter_spec(tables, hf2)
    y_hy_even, y_hy_odd = _hyena(u2.reshape(B, L // 2, 2 * HY_COLS), hy_conv_w.astype(_F32), hy_conv_b.astype(_F32),
                                 tables, k1r, k1i, k2r, k2i, hy_skip.astype(_F32))
    y_na = _natten(qkv.reshape(B, L, 3 * NA_WIDTH), na_rpb)

    h1, xn, meta, tile_tbl, counts = _out_router(
        y_hy_even.reshape(T // 2, HY_WIDTH), y_hy_odd.reshape(T // 2, HY_WIDTH), y_na.reshape(T, NA_WIDTH),
        g_out_hy, g_out_na, w_out, x2, g_ffn, router_wg, router_bg, router_we, router_be)
    max_rows = T * TOP_K + (T // ROW_TILE) * N_EXPERTS * (MOE_CHUNK - 1) + N_EXPERTS * (MOE_BLOCK - 1)
    n_blocks = -(-max_rows // MOE_BLOCK)
    n_tail = n_blocks - T * TOP_K // MOE_BLOCK
    dst_scatter, dst_gather, zero_blocks, block_e, n_blk = _routing_tables(tile_tbl, counts, n_blocks, n_tail)
    assert 2 * SORT_ROWS % MOE_BLOCK == 0
    xb = _moe_scatter(xn, meta, dst_scatter, zero_blocks, n_blocks * MOE_BLOCK + 2 * SORT_ROWS)
    yb = _moe_experts(xb, n_blocks, block_e, n_blk, exp_w1, exp_w3, exp_w2)
    comb = jnp.pad(meta[:_META_POS + TOP_K].T, ((0, 0), (0, ROUTER_LANES - _META_POS - TOP_K)))
    return h1, dst_gather, comb, yb


def kernel(x, p, g_mix, w_in, hy_conv_w, hy_conv_b, hy_f_w1, hy_f_b1, hy_f_freq1, hy_f_w2, hy_f_b2, hy_f_freq2, hy_f_w3, hy_skip, na_rpb, g_out_hy, g_out_na, w_out, g_ffn, router_wg, router_bg, router_we, router_be, exp_w1, exp_w3, exp_w2, g_ple, w_ple_gate, w_ple_proj, g_final):
    depth = p.shape[0]
    assert depth == 1, "the final RMSNorm is fused into the last layer's kernel; one layer is supported"
    B, L, D = x.shape
    i = 0
    h1, dest, gates, yb = _one_layer(
        x, p[i], g_mix[i], w_in[i], hy_conv_w[i], hy_conv_b[i], hy_f_w1[i], hy_f_b1[i], hy_f_freq1[i],
        hy_f_w2[i], hy_f_b2[i], hy_f_freq2[i], hy_f_w3[i], hy_skip[i], na_rpb[i], g_out_hy[i], g_out_na[i],
        w_out[i], g_ffn[i], router_wg[i], router_bg[i], router_we[i], router_be[i], exp_w1[i], exp_w3[i],
        exp_w2[i], g_ple[i], w_ple_gate[i], w_ple_proj[i])
    out = _final(dest, h1, gates, yb, p[i].reshape(B * L, -1), w_ple_gate[i], w_ple_proj[i], g_ple[i], g_final)
    return out.reshape(B, L, D)
```

```python
import functools
import math

import numpy as np
import jax
import jax.numpy as jnp
from jax import lax
from jax.experimental import pallas as pl
from jax.experimental.pallas import tpu as pltpu

_F32 = jnp.float32
_BF16 = jnp.bfloat16

GRID_W = 64
HY_WIDTH = 512
NA_WIDTH = 512
NA_HEADS = 8
NA_HEAD_DIM = 64
HY_ORDER = 2
SHORT_CONV = 3
FILTER_EMB = 33
FILTER_BANDS = (FILTER_EMB - 1) // 2
DECAY_TARGET = 1e-2
FAST_DECAY_PCT = 0.3
SLOW_DECAY_PCT = 1.5
WIN_ROWS = 8
WIN_COLS = 16
Q_ROWS = 2
N_GROUPS = 4
EXPERTS_PER_GROUP = 8
N_EXPERTS = N_GROUPS * EXPERTS_PER_GROUP
TOP_K = 2
MOE_BLOCK = 512
EPS = 1e-6
NEG_INF = -1e30
HY_COLS = (HY_ORDER + 1) * HY_WIDTH

V7X_LANES = 128
V7X_SUBLANES = 8
V7X_VMEM_LIMIT_BYTES = 56 * 2 ** 20

ROW_TILE = 512
FREQ_CHUNK = 512
HY_CH_TILE = 256
ROUTER_LANES = V7X_LANES
MOE_CHUNK = V7X_SUBLANES
SORT_ROWS = -(-(TOP_K * ROW_TILE + N_EXPERTS * (MOE_CHUNK - 1)) // V7X_LANES) * V7X_LANES
SORT_CHUNKS = SORT_ROWS // MOE_CHUNK
DMA_UNROLL = 8


def _cparams(*sem):
    return pltpu.CompilerParams(dimension_semantics=sem, vmem_limit_bytes=V7X_VMEM_LIMIT_BYTES)


def _resident(shape, index_map):
    return pl.BlockSpec(shape, index_map, pipeline_mode=pl.Buffered(1))


def _rmsnorm(x, g):
    return x * lax.rsqrt(jnp.mean(x * x, axis=-1, keepdims=True) + EPS) * g


def _dot(a, b):
    return jnp.dot(a, b, preferred_element_type=_F32)


def _dot_f32(a, b):
    return jnp.dot(a, b, preferred_element_type=_F32, precision=lax.Precision.HIGHEST)


def _parity_select(n):
    r = jnp.arange(n // 2, dtype=jnp.int32)[None, :, None]
    t = jnp.arange(n, dtype=jnp.int32)[None, None, :]
    p = jnp.arange(2, dtype=jnp.int32)[:, None, None]
    return (t == 2 * r + p).astype(_BF16)


def _in_proj_kernel(x_ref, g_ref, w_ref, sel_ref, u2_ref, qkv_ref):
    xn = _rmsnorm(x_ref[...], g_ref[...]).astype(_BF16)
    n_hy = u2_ref.shape[1] // 2
    xe = _dot(sel_ref[0], xn).astype(_BF16)
    xo = _dot(sel_ref[1], xn).astype(_BF16)
    for c0 in range(0, n_hy, 512):
        w = w_ref[:, c0:c0 + 512]
        u2_ref[:, c0:c0 + 512] = _dot(xe, w).astype(_BF16)
        u2_ref[:, n_hy + c0:n_hy + c0 + 512] = _dot(xo, w).astype(_BF16)
    for c0 in range(0, qkv_ref.shape[1], 512):
        qkv_ref[:, c0:c0 + 512] = _dot(xn, w_ref[:, n_hy + c0:n_hy + c0 + 512]).astype(_BF16)


def _in_proj(x2, g_mix, w_in):
    T, D = x2.shape
    n_in = w_in.shape[1]
    n_qkv = n_in - HY_COLS
    half = ROW_TILE // 2
    return pl.pallas_call(
        _in_proj_kernel,
        grid=(T // ROW_TILE,),
        in_specs=[pl.BlockSpec((ROW_TILE, D), lambda i: (i, 0)),
                  pl.BlockSpec((1, D), lambda i: (0, 0)),
                  _resident((D, n_in), lambda i: (0, 0)),
                  pl.BlockSpec((2, half, ROW_TILE), lambda i: (0, 0, 0))],
        out_specs=[pl.BlockSpec((half, 2 * HY_COLS), lambda i: (i, 0)),
                   pl.BlockSpec((ROW_TILE, n_qkv), lambda i: (i, 0))],
        out_shape=[jax.ShapeDtypeStruct((T // 2, 2 * HY_COLS), _BF16),
                   jax.ShapeDtypeStruct((T, n_qkv), _BF16)],
        compiler_params=_cparams("arbitrary"),
        name="in_proj",
    )(x2, g_mix.reshape(1, D), w_in.astype(_BF16), _parity_select(ROW_TILE))


def _filter_mlp_kernel(z_ref, w1_ref, b1_ref, f1_ref, w2_ref, b2_ref, f2_ref, w3_ref, delta_ref, hf_ref):
    z = z_ref[0]
    hid = jnp.sin(f1_ref[...] * (_dot_f32(z, w1_ref[...]) + b1_ref[...]))
    hid = jnp.sin(f2_ref[...] * (_dot_f32(hid, w2_ref[...]) + b2_ref[...]))
    hf = _dot_f32(hid, w3_ref[...])
    decay = jnp.exp(-z[:, 0:1] * delta_ref[...])
    tl = z.shape[0]
    row = lax.broadcasted_iota(jnp.int32, (tl, HY_WIDTH), 0)
    offset0 = (row == 0) & (pl.program_id(0) == 0) & (pl.program_id(1) == 0)
    for k in range(2 * HY_ORDER):
        blk = hf[:, k * HY_WIDTH:(k + 1) * HY_WIDTH] * decay
        if k >= HY_ORDER:
            blk = jnp.where(offset0, 0.0, blk)
        hf_ref[:, k * HY_WIDTH:(k + 1) * HY_WIDTH] = blk


def _filter_mlp(L, w1, b1, f1, w2, b2, f2, w3):
    t = jnp.linspace(0.0, 1.0, L, dtype=_F32)[:, None]
    w = 2.0 * math.pi * jnp.arange(L, dtype=_F32)[:, None] / L
    bands = jnp.linspace(1e-4, FILTER_BANDS - 1, FILTER_BANDS, dtype=_F32)[None, :]
    z = jnp.concatenate([t, jnp.cos(bands * w), -jnp.sin(bands * w)], axis=-1)
    z = jnp.pad(z, ((0, 0), (0, V7X_LANES - FILTER_EMB)))
    z = jnp.stack([z[0::2], z[1::2]])
    w1p = jnp.pad(w1.astype(_F32), ((0, V7X_LANES - FILTER_EMB), (0, 0)))
    max_decay = math.log(DECAY_TARGET) / FAST_DECAY_PCT
    min_decay = math.log(DECAY_TARGET) / SLOW_DECAY_PCT
    deltas = jnp.abs(jnp.linspace(min_decay, max_decay, HY_WIDTH, dtype=_F32))[None, :]
    hid = w1.shape[1]
    n_out = w3.shape[1]
    lh = L // 2
    tl = min(lh, ROW_TILE)
    full = lambda shape: pl.BlockSpec(shape, lambda p, i: (0, 0))
    return pl.pallas_call(
        _filter_mlp_kernel,
        grid=(2, lh // tl),
        in_specs=[pl.BlockSpec((1, tl, V7X_LANES), lambda p, i: (p, i, 0)),
                  full((V7X_LANES, hid)), full((1, hid)), full((1, hid)),
                  full((hid, hid)), full((1, hid)), full((1, hid)),
                  full((hid, n_out)), full((1, HY_WIDTH))],
        out_specs=pl.BlockSpec((tl, n_out), lambda p, i: (i, p)),
        out_shape=jax.ShapeDtypeStruct((lh, 2 * n_out), _F32),
        compiler_params=_cparams("arbitrary", "arbitrary"),
        name="hyena_filter_mlp",
    )(z, w1p, b1.reshape(1, hid), f1.reshape(1, hid), w2.astype(_F32), b2.reshape(1, hid),
      f2.reshape(1, hid), w3.astype(_F32), deltas)


def _freq_chunks(gp):
    return [(r0, min(FREQ_CHUNK, gp - r0)) for r0 in range(0, gp, FREQ_CHUNK)]


def _half_dft_tables(L):
    lh, n, sb = L // 2, 2 * L, 32
    gp = -(-(lh + 1) // V7X_SUBLANES) * V7X_SUBLANES
    assert lh % sb == 0
    g = jnp.arange(gp, dtype=jnp.int32)
    live = (g <= lh)[:, None]
    ang = lambda s: ((g[:, None] * s[None, :]) % L).astype(_F32) * (2.0 * math.pi / L)
    a_hi = ang(jnp.arange(lh // sb, dtype=jnp.int32) * sb)
    a_lo = ang(jnp.arange(sb, dtype=jnp.int32))
    c_hi, s_hi, c_lo, s_lo = jnp.cos(a_hi), jnp.sin(a_hi), jnp.cos(a_lo), jnp.sin(a_lo)
    cosm = (c_hi[:, :, None] * c_lo[:, None, :] - s_hi[:, :, None] * s_lo[:, None, :]).reshape(gp, lh)
    sinm = (s_hi[:, :, None] * c_lo[:, None, :] + c_hi[:, :, None] * s_lo[:, None, :]).reshape(gp, lh)
    cosm = jnp.where(live, cosm, 0.0)
    sinm = jnp.where(live, sinm, 0.0)
    ff = jnp.concatenate([cosm, -sinm], axis=0).astype(_BF16)
    wgt = jnp.where((g == 0) | (g == lh), 1.0, 2.0)[:, None] / n
    gf = jnp.concatenate([(cosm * wgt).T, (-sinm * wgt).T], axis=1).astype(_BF16)
    tw = g.astype(_F32)[:, None] * (2.0 * math.pi / n)
    wr = jnp.broadcast_to(jnp.cos(tw), (gp, HY_CH_TILE))
    wi = jnp.broadcast_to(-jnp.sin(tw), (gp, HY_CH_TILE))
    return ff, gf, wr, wi


def _half_spectra(ff_ref, wr, wi, xe, xo, r0, n):
    gp = ff_ref.shape[0] // 2
    fc = ff_ref[r0:r0 + n, :]
    fs = ff_ref[gp + r0:gp + r0 + n, :]
    er, ei = _dot(fc, xe), _dot(fs, xe)
    orr, oi = _dot(fc, xo), _dot(fs, xo)
    pr = wr * orr - wi * oi
    pi = wr * oi + wi * orr
    return (er + pr, ei + pi), (er - pr, pi - ei)


def _filter_spec_kernel(ff_ref, wr_ref, wi_ref, fe_ref, fo_ref, be_ref, bo_ref, k1r_ref, k1i_ref, k2r_ref, k2i_ref):
    fe, fo, be, bo = (r[...].astype(_BF16) for r in (fe_ref, fo_ref, be_ref, bo_ref))
    for r0, n in _freq_chunks(wr_ref.shape[0]):
        wr, wi = wr_ref[r0:r0 + n, :], wi_ref[r0:r0 + n, :]
        (f1r, f1i), (f2r, f2i) = _half_spectra(ff_ref, wr, wi, fe, fo, r0, n)
        (b1r, b1i), (b2r, b2i) = _half_spectra(ff_ref, wr, wi, be, bo, r0, n)
        k1r_ref[0, r0:r0 + n, :] = f1r + b1r
        k1i_ref[0, r0:r0 + n, :] = f1i - b1i
        k2r_ref[0, r0:r0 + n, :] = f2r + b2r
        k2i_ref[0, r0:r0 + n, :] = f2i - b2i


def _filter_spec(tables, hf2):
    ff, _, wr, wi = tables
    lh, gp = hf2.shape[0], wr.shape[0]
    n_ct = HY_WIDTH // HY_CH_TILE
    n_blk = hf2.shape[1] // 2 // HY_CH_TILE
    col = lambda blk0: pl.BlockSpec((lh, HY_CH_TILE), lambda j: (0, blk0 + j))
    out = jax.ShapeDtypeStruct((HY_ORDER, gp, HY_WIDTH), _F32)
    ospec = pl.BlockSpec((1, gp, HY_CH_TILE), lambda j: (j // n_ct, 0, j % n_ct))
    full = lambda a: pl.BlockSpec(a.shape, lambda j: (0, 0))
    return pl.pallas_call(
        _filter_spec_kernel,
        grid=(HY_ORDER * n_ct,),
        in_specs=[full(ff), full(wr), full(wi),
                  col(0), col(n_blk),
                  col(HY_ORDER * n_ct), col(n_blk + HY_ORDER * n_ct)],
        out_specs=[ospec] * 4,
        out_shape=[out] * 4,
        compiler_params=_cparams("arbitrary"),
        name="hyena_filter_spectrum",
    )(ff, wr, wi, hf2, hf2, hf2, hf2)


def _parity_conv(ze, zo, w_ref, b_ref):
    lh = ze.shape[0]
    row = lax.broadcasted_iota(jnp.int32, ze.shape, 0)
    zo_prev = jnp.where(row == 0, 0.0, pltpu.roll(zo, 1, axis=0))
    ze_next = jnp.where(row == lh - 1, 0.0, pltpu.roll(ze, lh - 1, axis=0))
    w0, w1, w2, b = w_ref[0:1, :], w_ref[1:2, :], w_ref[2:3, :], b_ref[...]
    return ((b + zo_prev * w0) + ze * w1) + zo * w2, ((b + ze * w0) + zo * w1) + ze_next * w2


def _hyena_kernel(ze_ref, zo_ref, zw_ref, zb_ref, ge_ref, go_ref, gw_ref, gb_ref, skip_ref, ff_ref, gf_ref,
                  wr_ref, wi_ref, k1r_ref, k1i_ref, k2r_ref, k2i_ref, oe_ref, oo_ref, a_ref, b_ref, *, conv_input):
    ze, zo = ze_ref[0].astype(_F32), zo_ref[0].astype(_F32)
    if conv_input:
        ze, zo = _parity_conv(ze, zo, zw_ref, zb_ref)
    xe, xo = ze.astype(_BF16), zo.astype(_BF16)
    gp = wr_ref.shape[0]
    for r0, n in _freq_chunks(gp):
        rows = slice(r0, r0 + n)
        wr, wi = wr_ref[rows, :], wi_ref[rows, :]
        (x1r, x1i), (x2r, x2i) = _half_spectra(ff_ref, wr, wi, xe, xo, r0, n)
        k1r, k1i, k2r, k2i = k1r_ref[0, rows, :], k1i_ref[0, rows, :], k2r_ref[0, rows, :], k2i_ref[0, rows, :]
        y1r, y1i = x1r * k1r - x1i * k1i, x1r * k1i + x1i * k1r
        y2r, y2i = x2r * k2r - x2i * k2i, x2r * k2i + x2i * k2r
        tr, ti = y1r - y2r, y1i + y2i
        a_ref[rows, :] = (y1r + y2r).astype(_BF16)
        a_ref[gp + r0:gp + r0 + n, :] = (y1i - y2i).astype(_BF16)
        b_ref[rows, :] = (tr * wr + ti * wi).astype(_BF16)
        b_ref[gp + r0:gp + r0 + n, :] = (ti * wr - tr * wi).astype(_BF16)
    ge, go = _parity_conv(ge_ref[0].astype(_F32), go_ref[0].astype(_F32), gw_ref, gb_ref)
    skip = skip_ref[0]
    for r0 in range(0, ze.shape[0], FREQ_CHUNK):
        rows = slice(r0, r0 + FREQ_CHUNK)
        g = gf_ref[rows, :]
        oe_ref[0, rows, :] = ge[rows] * (_dot(g, a_ref[...]) + ze[rows] * skip)
        oo_ref[0, rows, :] = go[rows] * (_dot(g, b_ref[...]) + zo[rows] * skip)


def _hyena(u2, conv_w, conv_b, tables, k1r, k1i, k2r, k2i, skip):
    ff, gf, wr, wi = tables
    B, lh, _ = u2.shape
    gp = wr.shape[0]
    assert lh % FREQ_CHUNK == 0
    n_ct = HY_WIDTH // HY_CH_TILE
    odd = HY_COLS // HY_CH_TILE
    conv_b2 = conv_b.reshape(1, HY_COLS)
    skip3 = skip.reshape(HY_ORDER, 1, HY_WIDTH)
    col = lambda blk0: pl.BlockSpec((1, lh, HY_CH_TILE), lambda c, b: (b, 0, blk0 + c))
    cw = lambda blk0: pl.BlockSpec((SHORT_CONV, HY_CH_TILE), lambda c, b: (0, blk0 + c))
    cb = lambda blk0: pl.BlockSpec((1, HY_CH_TILE), lambda c, b: (0, blk0 + c))
    const = lambda a: _resident(a.shape, lambda c, b: (0, 0))
    out = jax.ShapeDtypeStruct((B, lh, HY_WIDTH), _F32)
    ze_arr, zo_arr, ze_spec, zo_spec = u2, u2, col(2 * n_ct), col(odd + 2 * n_ct)
    for o in range(HY_ORDER):
        coef = lambda: pl.BlockSpec((1, gp, HY_CH_TILE), lambda c, b, o=o: (o, 0, c), pipeline_mode=pl.Buffered(1))
        ze_arr, zo_arr = pl.pallas_call(
            functools.partial(_hyena_kernel, conv_input=(o == 0)),
            grid=(n_ct, B),
            in_specs=[ze_spec, zo_spec, cw(2 * n_ct), cb(2 * n_ct),
                      col(o * n_ct), col(odd + o * n_ct), cw(o * n_ct), cb(o * n_ct),
                      pl.BlockSpec((1, 1, HY_CH_TILE), lambda c, b, o=o: (o, 0, c)),
                      const(ff), const(gf), const(wr), const(wi), coef(), coef(), coef(), coef()],
            out_specs=[col(0), col(0)],
            out_shape=[out, out],
            scratch_shapes=[pltpu.VMEM((2 * gp, HY_CH_TILE), _BF16), pltpu.VMEM((2 * gp, HY_CH_TILE), _BF16)],
            compiler_params=_cparams("arbitrary", "arbitrary"),
            name=f"hyena_order_{o}",
        )(ze_arr, zo_arr, conv_w, conv_b2, u2, u2, conv_w, conv_b2, skip3, ff, gf, wr, wi, k1r, k1i, k2r, k2i)
        ze_spec = zo_spec = col(0)
    return ze_arr, zo_arr


def _natten_tables(rows):
    kr = min(WIN_ROWS, rows)
    krb = min(Q_ROWS + kr - 1, rows)
    rs = np.clip(np.arange(rows) - kr // 2, 0, rows - kr)
    cs = np.clip(np.arange(GRID_W) - WIN_COLS // 2, 0, GRID_W - WIN_COLS)
    qc = np.tile(np.arange(GRID_W), Q_ROWS)[:, None]
    kc = np.tile(np.arange(GRID_W), krb)[None, :]
    cases, case_of, kstart = [], [], []
    for p in range(rows // Q_ROWS):
        k_r0 = min(rs[p * Q_ROWS], rows - krb)
        qr = (p * Q_ROWS + np.repeat(np.arange(Q_ROWS), GRID_W))[:, None]
        kr_ = (k_r0 + np.repeat(np.arange(krb), GRID_W))[None, :]
        valid = ((kr_ >= rs[qr]) & (kr_ < rs[qr] + kr) & (kc >= cs[qc]) & (kc < cs[qc] + WIN_COLS))
        dr = np.clip(kr_ - qr + WIN_ROWS - 1, 0, 2 * WIN_ROWS - 2)
        dc = np.clip(kc - qc + WIN_COLS - 1, 0, 2 * WIN_COLS - 2)
        idx = np.where(valid, dr * (2 * WIN_COLS - 1) + dc, -1).astype(np.int32)
        for ci, c in enumerate(cases):
            if np.array_equal(c, idx):
                break
        else:
            ci = len(cases)
            cases.append(idx)
        case_of.append(ci)
        kstart.append(k_r0 * GRID_W)
    return np.stack(cases), np.asarray(case_of, np.int32), np.asarray(kstart, np.int32), krb * GRID_W


def _natten_bias_kernel(dr_ref, rv_ref, u_ref, o_ref, v_ref, *, krb):
    c, h = pl.program_id(0), pl.program_id(1)
    n_m, n_k = v_ref.shape[1], v_ref.shape[2]
    w_shift, w_mask = GRID_W.bit_length() - 1, GRID_W - 1
    col = lax.broadcasted_iota(jnp.int32, (1, n_k), 1)
    kr_col, kc_col = col >> w_shift, col & w_mask
    per_kr = lambda tab, qr: sum(jnp.where(kr_col == kr, tab[(c * Q_ROWS + qr) * krb + kr], 0) for kr in range(krb))

    @pl.when(h == 0)
    def _():
        row = lax.broadcasted_iota(jnp.int32, (n_m, n_k), 0)
        for qr in range(Q_ROWS):
            hit = ((row >> w_shift) == per_kr(dr_ref, qr)) & ((row & w_mask) == kc_col)
            v_ref[qr] = jnp.where(hit, 1.0, 0.0).astype(_BF16)

    uh, ul = _split_bf16(u_ref[0])
    qc = lax.broadcasted_iota(jnp.int32, (GRID_W, n_k), 0)
    cs = jnp.clip(qc - WIN_COLS // 2, 0, GRID_W - WIN_COLS)
    col_ok = (kc_col >= cs) & (kc_col < cs + WIN_COLS)
    for qr in range(Q_ROWS):
        b = _dot(uh, v_ref[qr]) + _dot(ul, v_ref[qr])
        ok = col_ok & (per_kr(rv_ref, qr) > 0)
        o_ref[0, 0, qr * GRID_W:(qr + 1) * GRID_W, :] = jnp.where(ok, b, NEG_INF)


def _natten_bias(rpb, cases):
    n_case, nq, nk = cases.shape
    krb = nk // GRID_W
    n_dr, n_dc = 2 * WIN_ROWS - 1, 2 * WIN_COLS - 1
    c5 = cases.reshape(n_case, Q_ROWS, GRID_W, krb, GRID_W)
    dr_blk = np.where(c5 >= 0, c5 // n_dc, -1).max(axis=(2, 4))
    row_ok = dr_blk >= 0
    cs = np.clip(np.arange(GRID_W) - WIN_COLS // 2, 0, GRID_W - WIN_COLS)[:, None]
    col_ok = (np.arange(GRID_W)[None, :] >= cs) & (np.arange(GRID_W)[None, :] < cs + WIN_COLS)
    dc_idx = np.arange(GRID_W)[None, :] - np.arange(GRID_W)[:, None] + WIN_COLS - 1
    assert np.array_equal(c5 >= 0, row_ok[:, :, None, :, None] & col_ok[None, None, :, None, :])
    assert np.all((c5 < 0) | (c5 == dr_blk[:, :, None, :, None] * n_dc + dc_idx[None, None, :, None, :]))
    onehot = (dc_idx[:, None, :] == np.arange(n_dc)[None, :, None]).astype(np.float32)
    u = jnp.einsum('hdj,qjk->hqdk', rpb.astype(_F32), onehot, precision=lax.Precision.HIGHEST)
    u = u.reshape(NA_HEADS, GRID_W, n_dr * GRID_W)
    grid_spec = pltpu.PrefetchScalarGridSpec(
        num_scalar_prefetch=2,
        grid=(n_case, NA_HEADS),
        in_specs=[pl.BlockSpec((1, GRID_W, n_dr * GRID_W), lambda c, h, *_: (h, 0, 0))],
        out_specs=pl.BlockSpec((1, 1, nq, nk), lambda c, h, *_: (c, h, 0, 0)),
        scratch_shapes=[pltpu.VMEM((Q_ROWS, n_dr * GRID_W, nk), _BF16)],
    )
    return pl.pallas_call(
        functools.partial(_natten_bias_kernel, krb=krb),
        grid_spec=grid_spec,
        out_shape=jax.ShapeDtypeStruct((n_case, NA_HEADS, nq, nk), _F32),
        compiler_params=_cparams("arbitrary", "arbitrary"),
        name="natten_bias",
    )(jnp.asarray(np.maximum(dr_blk, 0).reshape(-1), jnp.int32), jnp.asarray(row_ok.reshape(-1), jnp.int32), u)


def _natten_kernel(case_ref, kstart_ref, qkv_ref, bias_ref, o_ref, *, n_pairs, n_keys):
    nq = Q_ROWS * GRID_W
    pair_w = 2 * NA_HEAD_DIM
    lane = lax.broadcasted_iota(jnp.int32, (nq, pair_w), 1)
    lo_half = lane < NA_HEAD_DIM
    scale = NA_HEAD_DIM ** -0.5

    def body(p, carry):
        q0 = pl.multiple_of(p * nq, nq)
        k0 = pl.multiple_of(kstart_ref[p], GRID_W)
        case = case_ref[p]
        for hp in range(NA_HEADS // 2):
            c0 = hp * pair_w
            q2 = qkv_ref[0, pl.ds(q0, nq), c0:c0 + pair_w]
            k2 = qkv_ref[0, pl.ds(k0, n_keys), NA_WIDTH + c0:NA_WIDTH + c0 + pair_w]
            v2 = qkv_ref[0, pl.ds(k0, n_keys), 2 * NA_WIDTH + c0:2 * NA_WIDTH + c0 + pair_w]
            zero = jnp.zeros_like(q2)
            q2 = q2 * scale
            qq = jnp.concatenate([jnp.where(lo_half, q2, zero), jnp.where(lo_half, zero, q2)], axis=0)
            s = lax.dot_general(qq, k2, (((1,), (1,)), ((), ())), preferred_element_type=_F32)
            s = s + bias_ref[case, hp]
            e = jnp.exp(s - jnp.max(s, axis=-1, keepdims=True))
            l = jnp.sum(e, axis=-1, keepdims=True)
            o = _dot(e.astype(_BF16), v2) / l
            o_ref[0, pl.ds(q0, nq), c0:c0 + pair_w] = jnp.where(lo_half, o[:nq], o[nq:])
        return carry

    lax.fori_loop(0, n_pairs, body, 0, unroll=2)


def _natten(qkv, rpb):
    B, L, _ = qkv.shape
    rows = L // GRID_W
    assert rows % Q_ROWS == 0 and rows >= Q_ROWS + WIN_ROWS - 1
    cases, case_of, kstart, n_keys = _natten_tables(rows)
    n_case = cases.shape[0]
    nq = Q_ROWS * GRID_W
    bias = _natten_bias(rpb, cases).reshape(n_case, NA_HEADS // 2, 2 * nq, n_keys)
    grid_spec = pltpu.PrefetchScalarGridSpec(
        num_scalar_prefetch=2,
        grid=(B,),
        in_specs=[pl.BlockSpec((1, L, 3 * NA_WIDTH), lambda b, *_: (b, 0, 0)),
                  _resident((n_case, NA_HEADS // 2, 2 * nq, n_keys), lambda b, *_: (0, 0, 0, 0))],
        out_specs=pl.BlockSpec((1, L, NA_WIDTH), lambda b, *_: (b, 0, 0)),
    )
    return pl.pallas_call(
        functools.partial(_natten_kernel, n_pairs=rows // Q_ROWS, n_keys=n_keys),
        grid_spec=grid_spec,
        out_shape=jax.ShapeDtypeStruct((B, L, NA_WIDTH), _F32),
        compiler_params=_cparams("arbitrary"),
        name="natten",
    )(jnp.asarray(case_of), jnp.asarray(kstart), qkv, bias)


_META_ID, _META_GATE, _META_POS = 0, 2, 4
ROUTER_ROWS = 8 + N_EXPERTS


def _split_bf16(v):
    hi = v.astype(_BF16)
    return hi, (v - hi.astype(_F32)).astype(_BF16)


def _out_router_kernel(yhe_ref, yho_ref, sel_ref, yna_ref, ghy_ref, gna_ref, wtop_ref, wbot_ref, x_ref, gffn_ref,
                       rwh_ref, rwl_ref, rb_ref, h1_ref, xs_ref, meta_ref, tile_ref, cnt_ref, carry_ref):
    @pl.when(pl.program_id(0) == 0)
    def _():
        carry_ref[...] = jnp.zeros_like(carry_ref)

    nhe = _rmsnorm(yhe_ref[...], ghy_ref[...]).astype(_BF16)
    nho = _rmsnorm(yho_ref[...], ghy_ref[...]).astype(_BF16)
    nh = (_dot(sel_ref[0], nhe) + _dot(sel_ref[1], nho)).astype(_BF16)
    nn = _rmsnorm(yna_ref[...], gna_ref[...]).astype(_BF16)
    h1 = x_ref[...] + (_dot(nh, wtop_ref[...]) + _dot(nn, wbot_ref[...]))
    h1_ref[...] = h1
    xn = _rmsnorm(h1, gffn_ref[...])
    tm = xn.shape[0]

    xh, xl = _split_bf16(xn)
    logits = _dot(xh, rwh_ref[...]) + (_dot(xh, rwl_ref[...]) + _dot(xl, rwh_ref[...]))
    lt = logits.T[:ROUTER_ROWS] + rb_ref[:, 0:1]

    neg = -jnp.inf
    row8 = lax.broadcasted_iota(jnp.int32, (EXPERTS_PER_GROUP, tm), 0).astype(_F32)
    col_max = lambda v: jnp.max(v, axis=0, keepdims=True)
    first_max = lambda v, m: jnp.min(jnp.where(v == m, row8, float(EXPERTS_PER_GROUP)), axis=0, keepdims=True)

    gl = jnp.where(row8 < N_GROUPS, lt[0:8], neg)
    gmax = col_max(gl)
    g_w = 1.0 / jnp.sum(jnp.exp(gl - gmax), axis=0, keepdims=True)
    g_idx = first_max(gl, gmax)
    el = lt[8:8 + EXPERTS_PER_GROUP]
    for g in range(1, N_GROUPS):
        el = jnp.where(g_idx == g, lt[8 + g * EXPERTS_PER_GROUP:8 + (g + 1) * EXPERTS_PER_GROUP], el)
    m1 = col_max(el)
    i1 = first_max(el, m1)
    el2 = jnp.where(row8 == i1, neg, el)
    m2 = col_max(el2)
    i2 = first_max(el2, m2)
    r = jnp.exp(m2 - m1)
    gate1 = g_w / (1.0 + r)
    gate2 = g_w * r / (1.0 + r)
    id1 = g_idx * EXPERTS_PER_GROUP + i1
    id2 = g_idx * EXPERTS_PER_GROUP + i2

    row_e = lax.broadcasted_iota(jnp.int32, (N_EXPERTS, tm), 0).astype(_F32)
    sel1 = row_e == id1
    sel2 = row_e == id2
    onehot = jnp.where(sel1 | sel2, 1.0, 0.0)
    tri_r = lax.broadcasted_iota(jnp.int32, (tm, tm), 0)
    tri_c = lax.broadcasted_iota(jnp.int32, (tm, tm), 1)
    tri = jnp.where(tri_r < tri_c, 1.0, 0.0).astype(_BF16)
    before = _dot(onehot.astype(_BF16), tri)
    cnt = jnp.sum(onehot, axis=1, keepdims=True)
    run = jnp.floor((cnt + (MOE_CHUNK - 1)) * (1.0 / MOE_CHUNK)) * MOE_CHUNK
    run_b = jnp.broadcast_to(run, (N_EXPERTS, ROUTER_LANES))
    e_r = lax.broadcasted_iota(jnp.int32, (N_EXPERTS, N_EXPERTS), 0)
    e_c = lax.broadcasted_iota(jnp.int32, (N_EXPERTS, N_EXPERTS), 1)
    start_b = _dot_f32(jnp.where(e_c < e_r, 1.0, 0.0), run_b)
    start = start_b[:, 0:1]
    pos1 = jnp.sum(jnp.where(sel1, start + before, 0.0), axis=0, keepdims=True)
    pos2 = jnp.sum(jnp.where(sel2, start + before, 0.0), axis=0, keepdims=True)
    xs_ref[...] = xn.astype(_BF16)

    base_b = carry_ref[...]
    lane = lax.broadcasted_iota(jnp.int32, (N_EXPERTS, ROUTER_LANES), 1)
    tile_ref[0] = jnp.where(lane == 0, start_b, jnp.where(lane == 1, run_b, jnp.where(lane == 2, base_b, 0.0)))
    carry_ref[...] = base_b + run_b
    cnt_ref[...] = base_b + run_b

    meta = jnp.zeros((8, tm), _F32)
    for k, v in enumerate((id1, id2, gate1, gate2, pos1, pos2)):
        meta = jnp.where(row8 == k, v, meta)
    meta_ref[...] = meta


def _out_router(y_hy_even, y_hy_odd, y_na, g_hy, g_na, w_out, x2, g_ffn, wg, bg, we, be):
    T, D = x2.shape
    half = ROW_TILE // 2
    sel_t = _parity_select(ROW_TILE).transpose(0, 2, 1)
    gpad = 8 - N_GROUPS
    rw = jnp.concatenate([wg.astype(_F32), jnp.zeros((D, gpad), _F32), we.astype(_F32)], axis=1)
    rw_hi, rw_lo = _split_bf16(jnp.pad(rw, ((0, 0), (0, ROUTER_LANES - ROUTER_ROWS))))
    rb = jnp.concatenate([bg.astype(_F32), jnp.zeros((gpad,), _F32), be.astype(_F32)])
    rb = jnp.broadcast_to(rb[:, None], (ROUTER_ROWS, ROUTER_LANES))
    w_bf = w_out.astype(_BF16)
    row = lambda n: pl.BlockSpec((ROW_TILE, n), lambda i: (i, 0))
    vec = lambda n: pl.BlockSpec((1, n), lambda i: (0, 0))
    mat = lambda r, c: pl.BlockSpec((r, c), lambda i: (0, 0))
    return pl.pallas_call(
        _out_router_kernel,
        grid=(T // ROW_TILE,),
        in_specs=[pl.BlockSpec((half, HY_WIDTH), lambda i: (i, 0)), pl.BlockSpec((half, HY_WIDTH), lambda i: (i, 0)),
                  pl.BlockSpec((2, ROW_TILE, half), lambda i: (0, 0, 0)),
                  row(NA_WIDTH), vec(HY_WIDTH), vec(NA_WIDTH),
                  mat(HY_WIDTH, D), mat(NA_WIDTH, D), row(D), vec(D),
                  mat(D, ROUTER_LANES), mat(D, ROUTER_LANES), mat(ROUTER_ROWS, ROUTER_LANES)],
        out_specs=[row(D),
                   row(D),
                   pl.BlockSpec((8, ROW_TILE), lambda i: (0, i)),
                   pl.BlockSpec((1, N_EXPERTS, ROUTER_LANES), lambda i: (i, 0, 0)),
                   mat(N_EXPERTS, ROUTER_LANES)],
        out_shape=[jax.ShapeDtypeStruct((T, D), _F32),
                   jax.ShapeDtypeStruct((T, D), _BF16),
                   jax.ShapeDtypeStruct((8, T), _F32),
                   jax.ShapeDtypeStruct((T // ROW_TILE, N_EXPERTS, ROUTER_LANES), _F32),
                   jax.ShapeDtypeStruct((N_EXPERTS, ROUTER_LANES), _F32)],
        scratch_shapes=[pltpu.VMEM((N_EXPERTS, ROUTER_LANES), _F32)],
        compiler_params=_cparams("arbitrary"),
        name="out_proj_router",
    )(y_hy_even, y_hy_odd, sel_t, y_na, g_hy.reshape(1, -1), g_na.reshape(1, -1), w_bf[:HY_WIDTH], w_bf[HY_WIDTH:],
      x2, g_ffn.reshape(1, D), rw_hi, rw_lo, rb)


def _chunk_rows(c):
    return pl.ds(pl.multiple_of(c * MOE_CHUNK, MOE_CHUNK), MOE_CHUNK)


_HI_HALF = 0xFFFF0000


def _pack_halves(v):
    c = v.shape[1] // 2
    bits = lax.bitcast_convert_type(v.astype(_BF16).astype(_F32), jnp.uint32)
    return (bits[:, c:] & jnp.uint32(_HI_HALF)) | lax.shift_right_logical(bits[:, :c], jnp.uint32(16))


def _unpack_halves(w):
    lo = lax.bitcast_convert_type(lax.shift_left(w, jnp.uint32(16)), _F32)
    hi = lax.bitcast_convert_type(w & jnp.uint32(_HI_HALF), _F32)
    return lo.astype(_BF16), hi.astype(_BF16)


def _scatter_kernel(zblk_ref, dst_ref, meta_ref, xn_ref, xb_ref, zero_ref, xs_ref, sem, zero_sem):
    i = pl.program_id(0)
    slot = lax.rem(i, 2)

    @pl.when(i == 0)
    def _():
        zero_ref[...] = jnp.zeros_like(zero_ref)

        def blk_copy(j):
            start = pl.multiple_of(zblk_ref[j] * MOE_BLOCK, MOE_BLOCK)
            return pltpu.make_async_copy(zero_ref, xb_ref.at[pl.ds(start, MOE_BLOCK)], zero_sem)

        def start(j, c):
            @pl.when(zblk_ref[j] >= 0)
            def _():
                blk_copy(j).start()
            return c

        def wait(j, c):
            @pl.when(zblk_ref[j] >= 0)
            def _():
                blk_copy(j).wait()
            return c

        lax.fori_loop(0, zblk_ref.shape[0], start, 0)
        lax.fori_loop(0, zblk_ref.shape[0], wait, 0)

    tm = xn_ref.shape[0]
    meta = meta_ref[...]
    pos1, pos2 = meta[_META_POS:_META_POS + 1, :], meta[_META_POS + 1:_META_POS + 2, :]
    srow = lax.broadcasted_iota(jnp.int32, (SORT_ROWS, tm), 0).astype(_F32)
    perm = jnp.where((srow == pos1) | (srow == pos2), 1.0, 0.0).astype(_BF16)
    xs_ref[slot] = _pack_halves(_dot(perm, xn_ref[...]))

    def start(c, carry):
        pltpu.make_async_copy(xs_ref.at[slot, _chunk_rows(c)], xb_ref.at[_chunk_rows(dst_ref[0, 0, c])],
                              sem.at[slot]).start()
        return carry

    lax.fori_loop(0, SORT_CHUNKS, start, 0, unroll=DMA_UNROLL)

    wait_all = lambda s: pltpu.make_async_copy(xs_ref.at[s], xs_ref.at[s], sem.at[s]).wait()

    @pl.when(i > 0)
    def _():
        wait_all(1 - slot)

    @pl.when(i == pl.num_programs(0) - 1)
    def _():
        wait_all(slot)


def _moe_scatter(xn, meta, dst_chunk, zero_blocks, n_slots):
    T, D = xn.shape
    nt = T // ROW_TILE
    dst3 = dst_chunk.reshape(nt, 1, SORT_CHUNKS)
    grid_spec = pltpu.PrefetchScalarGridSpec(
        num_scalar_prefetch=1,
        grid=(nt,),
        in_specs=[pl.BlockSpec((1, 1, SORT_CHUNKS), lambda i, *_: (i, 0, 0), memory_space=pltpu.SMEM),
                  pl.BlockSpec((8, ROW_TILE), lambda i, *_: (0, i)),
                  pl.BlockSpec((ROW_TILE, D), lambda i, *_: (i, 0))],
        out_specs=pl.BlockSpec(memory_space=pl.ANY),
        scratch_shapes=[pltpu.VMEM((MOE_BLOCK, D // 2), jnp.uint32), pltpu.VMEM((2, SORT_ROWS, D // 2), jnp.uint32),
                        pltpu.SemaphoreType.DMA((2,)), pltpu.SemaphoreType.DMA(())],
    )
    return pl.pallas_call(
        _scatter_kernel,
        grid_spec=grid_spec,
        out_shape=jax.ShapeDtypeStruct((n_slots, D // 2), jnp.uint32),
        compiler_params=_cparams("arbitrary"),
        name="moe_scatter",
    )(zero_blocks, dst3, meta, xn)


def _expert_kernel(be_ref, nblk_ref, xb_ref, w1_ref, w3_ref, w2_ref, yb_ref, wb1_ref, wb3_ref, wb2_ref):
    i = pl.program_id(0)

    @pl.when(i < nblk_ref[0])
    def _():
        e = be_ref[i]
        e_prev = be_ref[jnp.maximum(i - 1, 0)]

        @pl.when((i == 0) | (e != e_prev))
        def _():
            wb1_ref[...] = w1_ref[0].astype(_BF16)
            wb3_ref[...] = w3_ref[0].astype(_BF16)
            wb2_ref[...] = w2_ref[0].astype(_BF16)

        x_lo, x_hi = _unpack_halves(xb_ref[...])
        half = x_lo.shape[1]
        up = lambda w_ref: _dot(x_lo, w_ref[:half, :]) + _dot(x_hi, w_ref[half:, :])
        a = up(wb1_ref)
        hid = (a * jax.nn.sigmoid(a) * up(wb3_ref)).astype(_BF16)
        yb_ref[...] = _pack_halves(_dot(hid, wb2_ref[...]))

    @pl.when(i >= nblk_ref[0])
    def _():
        yb_ref[...] = jnp.zeros_like(yb_ref)


def _moe_experts(xb, n_blocks, block_e, n_blk, w1, w3, w2):
    _, D, DE = w1.shape
    live = lambda i, nb: jnp.minimum(i, nb[0] - 1)
    grid_spec = pltpu.PrefetchScalarGridSpec(
        num_scalar_prefetch=2,
        grid=(n_blocks,),
        in_specs=[pl.BlockSpec((MOE_BLOCK, D // 2), lambda i, be, nb: (live(i, nb), 0)),
                  pl.BlockSpec((1, D, DE), lambda i, be, nb: (be[live(i, nb)], 0, 0)),
                  pl.BlockSpec((1, D, DE), lambda i, be, nb: (be[live(i, nb)], 0, 0)),
                  pl.BlockSpec((1, DE, D), lambda i, be, nb: (be[live(i, nb)], 0, 0))],
        out_specs=pl.BlockSpec((MOE_BLOCK, D // 2), lambda i, be, nb: (i, 0)),
        scratch_shapes=[pltpu.VMEM((D, DE), _BF16), pltpu.VMEM((D, DE), _BF16), pltpu.VMEM((DE, D), _BF16)],
    )
    return pl.pallas_call(
        _expert_kernel,
        grid_spec=grid_spec,
        out_shape=jax.ShapeDtypeStruct((n_blocks * MOE_BLOCK, D // 2), jnp.uint32),
        compiler_params=_cparams("arbitrary"),
        name="moe_experts",
    )(block_e, n_blk, xb, w1, w3, w2)


def _final_kernel(dst_ref, dst_next_ref, h1_ref, comb_ref, yb_ref, p_ref, wg_ref, wp_ref, gple_ref, gfin_ref,
                  o_ref, ybuf_ref, sem):
    tm = h1_ref.shape[0]
    i = pl.program_id(0)
    slot = lax.rem(i, 2)

    def gather(d_ref, s):
        def start(c, carry):
            pltpu.make_async_copy(yb_ref.at[_chunk_rows(d_ref[0, 0, c])], ybuf_ref.at[s, _chunk_rows(c)],
                                  sem.at[s]).start()
            return carry
        lax.fori_loop(0, SORT_CHUNKS, start, 0, unroll=DMA_UNROLL)

    @pl.when(i == 0)
    def _():
        gather(dst_ref, 0)

    @pl.when(i + 1 < pl.num_programs(0))
    def _():
        gather(dst_next_ref, 1 - slot)

    pltpu.make_async_copy(ybuf_ref.at[slot], ybuf_ref.at[slot], sem.at[slot]).wait()

    comb = comb_ref[...]
    g1, g2 = comb[:, _META_GATE:_META_GATE + 1], comb[:, _META_GATE + 1:_META_GATE + 2]
    p1, p2 = comb[:, _META_POS:_META_POS + 1], comb[:, _META_POS + 1:_META_POS + 2]
    lane = lax.broadcasted_iota(jnp.int32, (tm, SORT_ROWS), 1).astype(_F32)
    wmat = (jnp.where(lane == p1, g1, 0.0) + jnp.where(lane == p2, g2, 0.0)).astype(_BF16)
    y_lo, y_hi = _unpack_halves(ybuf_ref[slot])
    moe = jnp.concatenate([_dot(wmat, y_lo), _dot(wmat, y_hi)], axis=1)
    h2 = h1_ref[...] + moe
    gate = jax.nn.sigmoid(_dot(_rmsnorm(h2, gple_ref[...]).astype(_BF16), wg_ref[...]))
    h3 = h2 + _dot(p_ref[...].astype(_BF16), wp_ref[...]) * gate
    o_ref[...] = _rmsnorm(h3, gfin_ref[...])


def _final(dst_chunk, h1, comb, yb, p2, w_gate, w_proj, g_ple, g_final):
    T, D = h1.shape
    nt = T // ROW_TILE
    PD = p2.shape[1]
    row = lambda n: pl.BlockSpec((ROW_TILE, n), lambda i: (i, 0))
    vec = lambda n: pl.BlockSpec((1, n), lambda i: (0, 0))
    dest3 = dst_chunk.reshape(nt, 1, SORT_CHUNKS)
    return pl.pallas_call(
        _final_kernel,
        grid=(nt,),
        in_specs=[pl.BlockSpec((1, 1, SORT_CHUNKS), lambda i: (i, 0, 0), memory_space=pltpu.SMEM),
                  pl.BlockSpec((1, 1, SORT_CHUNKS), lambda i: (jnp.minimum(i + 1, nt - 1), 0, 0),
                               memory_space=pltpu.SMEM),
                  row(D), row(ROUTER_LANES),
                  pl.BlockSpec(memory_space=pl.ANY),
                  row(PD),
                  pl.BlockSpec((D, D), lambda i: (0, 0)),
                  pl.BlockSpec((PD, D), lambda i: (0, 0)),
                  vec(D), vec(D)],
        out_specs=row(D),
        out_shape=jax.ShapeDtypeStruct((T, D), _F32),
        scratch_shapes=[pltpu.VMEM((2, SORT_ROWS, D // 2), jnp.uint32), pltpu.SemaphoreType.DMA((2,))],
        compiler_params=_cparams("arbitrary"),
        name="moe_gather_ple_final",
    )(dest3, dest3, h1, comb, yb, p2, w_gate.astype(_BF16), w_proj.astype(_BF16),
      g_ple.reshape(1, D), g_final.reshape(1, D))


def _routing_tables(tile_tbl, counts_f, n_blocks, n_tail):
    start = tile_tbl[:, :, 0].astype(jnp.int32)
    run = tile_tbl[:, :, 1].astype(jnp.int32)
    base = tile_tbl[:, :, 2].astype(jnp.int32)
    counts = counts_f[:, 0].astype(jnp.int32)
    padded = (counts + MOE_BLOCK - 1) // MOE_BLOCK * MOE_BLOCK
    e_iota = jnp.arange(N_EXPERTS, dtype=jnp.int32)
    pad_end = jnp.sum(jnp.where(e_iota[None, :] <= e_iota[:, None], padded[None, :], 0), axis=1)
    pad_start = pad_end - padded
    row0 = (jnp.arange(SORT_CHUNKS, dtype=jnp.int32) * MOE_CHUNK)[None, :, None]
    mine = (start[:, None, :] <= row0) & (row0 < (start + run)[:, None, :])
    used = jnp.any(mine, axis=-1)
    slot0 = (pad_start[None, :] + base - start)[:, None, :] + row0
    dst = jnp.sum(jnp.where(mine, slot0, 0), axis=-1) // MOE_CHUNK
    nt = start.shape[0]
    spare = (n_blocks * MOE_BLOCK // MOE_CHUNK + (jnp.arange(nt, dtype=jnp.int32) % 2)[:, None] * SORT_CHUNKS
             + jnp.arange(SORT_CHUNKS, dtype=jnp.int32)[None, :])
    dst_scatter = jnp.where(used, dst, spare).astype(jnp.int32)
    dst_gather = jnp.where(used, dst, 0).astype(jnp.int32)
    blk_row = jnp.arange(n_blocks, dtype=jnp.int32)[:, None] * MOE_BLOCK
    block_e = jnp.minimum(jnp.sum((pad_end[None, :] <= blk_row).astype(jnp.int32), axis=1), N_EXPERTS - 1)
    n_blk = (pad_end[-1:] // MOE_BLOCK).astype(jnp.int32)
    seg_last = jnp.where(padded > counts, pad_end // MOE_BLOCK - 1, -1)
    tail = n_blk[0] + jnp.arange(n_tail, dtype=jnp.int32)
    spare_blocks = n_blocks + jnp.arange(2 * SORT_ROWS // MOE_BLOCK, dtype=jnp.int32)
    zero_blocks = jnp.concatenate([seg_last, jnp.where(tail < n_blocks, tail, -1), spare_blocks]).astype(jnp.int32)
    return dst_scatter, dst_gather, zero_blocks, block_e, n_blk


def _one_layer(h, p, g_mix, w_in, hy_conv_w, hy_conv_b, hy_f_w1, hy_f_b1, hy_f_freq1, hy_f_w2, hy_f_b2,
               hy_f_freq2, hy_f_w3, hy_skip, na_rpb, g_out_hy, g_out_na, w_out, g_ffn, router_wg, router_bg,
               router_we, router_be, exp_w1, exp_w3, exp_w2, g_ple, w_ple_gate, w_ple_proj):
    B, L, D = h.shape
    T = B * L
    assert T % ROW_TILE == 0 and L % FREQ_CHUNK == 0 and w_in.shape[1] == HY_COLS + 3 * NA_WIDTH
    x2 = h.reshape(T, D)
    u2, qkv = _in_proj(x2, g_mix, w_in)

    hf2 = _filter_mlp(L, hy_f_w1, hy_f_b1, hy_f_freq1, hy_f_w2, hy_f_b2, hy_f_freq2, hy_f_w3)
    tables = _half_dft_tables(L)
    k1r, k1i, k2r, k2i = _filter_spec(tables, hf2)
    y_hy_even, y_hy_odd = _hyena(u2.reshape(B, L // 2, 2 * HY_COLS), hy_conv_w.astype(_F32), hy_conv_b.astype(_F32),
                                 tables, k1r, k1i, k2r, k2i, hy_skip.astype(_F32))
    y_na = _natten(qkv.reshape(B, L, 3 * NA_WIDTH), na_rpb)

    h1, xn, meta, tile_tbl, counts = _out_router(
        y_hy_even.reshape(T // 2, HY_WIDTH), y_hy_odd.reshape(T // 2, HY_WIDTH), y_na.reshape(T, NA_WIDTH),
        g_out_hy, g_out_na, w_out, x2, g_ffn, router_wg, router_bg, router_we, router_be)
    max_rows = T * TOP_K + (T // ROW_TILE) * N_EXPERTS * (MOE_CHUNK - 1) + N_EXPERTS * (MOE_BLOCK - 1)
    n_blocks = -(-max_rows // MOE_BLOCK)
    n_tail = n_blocks - T * TOP_K // MOE_BLOCK
    dst_scatter, dst_gather, zero_blocks, block_e, n_blk = _routing_tables(tile_tbl, counts, n_blocks, n_tail)
    assert 2 * SORT_ROWS % MOE_BLOCK == 0
    xb = _moe_scatter(xn, meta, dst_scatter, zero_blocks, n_blocks * MOE_BLOCK + 2 * SORT_ROWS)
    yb = _moe_experts(xb, n_blocks, block_e, n_blk, exp_w1, exp_w3, exp_w2)
    comb = jnp.pad(meta[:_META_POS + TOP_K].T, ((0, 0), (0, ROUTER_LANES - _META_POS - TOP_K)))
    return h1, dst_gather, comb, yb


def kernel(x, p, g_mix, w_in, hy_conv_w, hy_conv_b, hy_f_w1, hy_f_b1, hy_f_freq1, hy_f_w2, hy_f_b2, hy_f_freq2, hy_f_w3, hy_skip, na_rpb, g_out_hy, g_out_na, w_out, g_ffn, router_wg, router_bg, router_we, router_be, exp_w1, exp_w3, exp_w2, g_ple, w_ple_gate, w_ple_proj, g_final):
    depth = p.shape[0]
    assert depth == 1, "the final RMSNorm is fused into the last layer's kernel; one layer is supported"
    B, L, D = x.shape
    i = 0
    h1, dest, gates, yb = _one_layer(
        x, p[i], g_mix[i], w_in[i], hy_conv_w[i], hy_conv_b[i], hy_f_w1[i], hy_f_b1[i], hy_f_freq1[i],
        hy_f_w2[i], hy_f_b2[i], hy_f_freq2[i], hy_f_w3[i], hy_skip[i], na_rpb[i], g_out_hy[i], g_out_na[i],
        w_out[i], g_ffn[i], router_wg[i], router_bg[i], router_we[i], router_be[i], exp_w1[i], exp_w3[i],
        exp_w2[i], g_ple[i], w_ple_gate[i], w_ple_proj[i])
    out = _final(dest, h1, gates, yb, p[i].reshape(B * L, -1), w_ple_gate[i], w_ple_proj[i], g_ple[i], g_final)
    return out.reshape(B, L, D)
```

```python
import functools
import math

import numpy as np
import jax
import jax.numpy as jnp
from jax import lax
from jax.experimental import pallas as pl
from jax.experimental.pallas import tpu as pltpu

_F32 = jnp.float32
_BF16 = jnp.bfloat16

GRID_W = 64
HY_WIDTH = 512
NA_WIDTH = 512
NA_HEADS = 8
NA_HEAD_DIM = 64
HY_ORDER = 2
SHORT_CONV = 3
FILTER_EMB = 33
FILTER_BANDS = (FILTER_EMB - 1) // 2
DECAY_TARGET = 1e-2
FAST_DECAY_PCT = 0.3
SLOW_DECAY_PCT = 1.5
WIN_ROWS = 8
WIN_COLS = 16
Q_ROWS = 2
N_GROUPS = 4
EXPERTS_PER_GROUP = 8
N_EXPERTS = N_GROUPS * EXPERTS_PER_GROUP
TOP_K = 2
MOE_BLOCK = 512
EPS = 1e-6
NEG_INF = -1e30
HY_COLS = (HY_ORDER + 1) * HY_WIDTH

V7X_LANES = 128
V7X_SUBLANES = 8
V7X_VMEM_LIMIT_BYTES = 56 * 2 ** 20

ROW_TILE = 512
FREQ_CHUNK = 512
HY_CH_TILE = 256
ROUTER_LANES = V7X_LANES
MOE_CHUNK = V7X_SUBLANES
SORT_ROWS = -(-(TOP_K * ROW_TILE + N_EXPERTS * (MOE_CHUNK - 1)) // V7X_LANES) * V7X_LANES
SORT_CHUNKS = SORT_ROWS // MOE_CHUNK
DMA_UNROLL = 8


def _cparams(*sem):
    return pltpu.CompilerParams(dimension_semantics=sem, vmem_limit_bytes=V7X_VMEM_LIMIT_BYTES)


def _resident(shape, index_map):
    return pl.BlockSpec(shape, index_map, pipeline_mode=pl.Buffered(1))


def _rmsnorm(x, g):
    return x * lax.rsqrt(jnp.mean(x * x, axis=-1, keepdims=True) + EPS) * g


def _dot(a, b):
    return jnp.dot(a, b, preferred_element_type=_F32)


def _dot_f32(a, b):
    return jnp.dot(a, b, preferred_element_type=_F32, precision=lax.Precision.HIGHEST)


def _parity_select(n):
    r = jnp.arange(n // 2, dtype=jnp.int32)[None, :, None]
    t = jnp.arange(n, dtype=jnp.int32)[None, None, :]
    p = jnp.arange(2, dtype=jnp.int32)[:, None, None]
    return (t == 2 * r + p).astype(_BF16)


def _in_proj_kernel(x_ref, g_ref, w_ref, sel_ref, u2_ref, qkv_ref):
    xn = _rmsnorm(x_ref[...], g_ref[...]).astype(_BF16)
    n_hy = u2_ref.shape[1] // 2
    xe = _dot(sel_ref[0], xn).astype(_BF16)
    xo = _dot(sel_ref[1], xn).astype(_BF16)
    for c0 in range(0, n_hy, 512):
        w = w_ref[:, c0:c0 + 512]
        u2_ref[:, c0:c0 + 512] = _dot(xe, w).astype(_BF16)
        u2_ref[:, n_hy + c0:n_hy + c0 + 512] = _dot(xo, w).astype(_BF16)
    for c0 in range(0, qkv_ref.shape[1], 512):
        qkv_ref[:, c0:c0 + 512] = _dot(xn, w_ref[:, n_hy + c0:n_hy + c0 + 512]).astype(_BF16)


def _in_proj(x2, g_mix, w_in):
    T, D = x2.shape
    n_in = w_in.shape[1]
    n_qkv = n_in - HY_COLS
    half = ROW_TILE // 2
    return pl.pallas_call(
        _in_proj_kernel,
        grid=(T // ROW_TILE,),
        in_specs=[pl.BlockSpec((ROW_TILE, D), lambda i: (i, 0)),
                  pl.BlockSpec((1, D), lambda i: (0, 0)),
                  _resident((D, n_in), lambda i: (0, 0)),
                  pl.BlockSpec((2, half, ROW_TILE), lambda i: (0, 0, 0))],
        out_specs=[pl.BlockSpec((half, 2 * HY_COLS), lambda i: (i, 0)),
                   pl.BlockSpec((ROW_TILE, n_qkv), lambda i: (i, 0))],
        out_shape=[jax.ShapeDtypeStruct((T // 2, 2 * HY_COLS), _BF16),
                   jax.ShapeDtypeStruct((T, n_qkv), _BF16)],
        compiler_params=_cparams("arbitrary"),
        name="in_proj",
    )(x2, g_mix.reshape(1, D), w_in.astype(_BF16), _parity_select(ROW_TILE))


def _filter_mlp_kernel(z_ref, w1_ref, b1_ref, f1_ref, w2_ref, b2_ref, f2_ref, w3_ref, delta_ref, hf_ref):
    z = z_ref[0]
    hid = jnp.sin(f1_ref[...] * (_dot_f32(z, w1_ref[...]) + b1_ref[...]))
    hid = jnp.sin(f2_ref[...] * (_dot_f32(hid, w2_ref[...]) + b2_ref[...]))
    hf = _dot_f32(hid, w3_ref[...])
    decay = jnp.exp(-z[:, 0:1] * delta_ref[...])
    tl = z.shape[0]
    row = lax.broadcasted_iota(jnp.int32, (tl, HY_WIDTH), 0)
    offset0 = (row == 0) & (pl.program_id(0) == 0) & (pl.program_id(1) == 0)
    for k in range(2 * HY_ORDER):
        blk = hf[:, k * HY_WIDTH:(k + 1) * HY_WIDTH] * decay
        if k >= HY_ORDER:
            blk = jnp.where(offset0, 0.0, blk)
        hf_ref[:, k * HY_WIDTH:(k + 1) * HY_WIDTH] = blk


def _filter_mlp(L, w1, b1, f1, w2, b2, f2, w3):
    t = jnp.linspace(0.0, 1.0, L, dtype=_F32)[:, None]
    w = 2.0 * math.pi * jnp.arange(L, dtype=_F32)[:, None] / L
    bands = jnp.linspace(1e-4, FILTER_BANDS - 1, FILTER_BANDS, dtype=_F32)[None, :]
    z = jnp.concatenate([t, jnp.cos(bands * w), -jnp.sin(bands * w)], axis=-1)
    z = jnp.pad(z, ((0, 0), (0, V7X_LANES - FILTER_EMB)))
    z = jnp.stack([z[0::2], z[1::2]])
    w1p = jnp.pad(w1.astype(_F32), ((0, V7X_LANES - FILTER_EMB), (0, 0)))
    max_decay = math.log(DECAY_TARGET) / FAST_DECAY_PCT
    min_decay = math.log(DECAY_TARGET) / SLOW_DECAY_PCT
    deltas = jnp.abs(jnp.linspace(min_decay, max_decay, HY_WIDTH, dtype=_F32))[None, :]
    hid = w1.shape[1]
    n_out = w3.shape[1]
    lh = L // 2
    tl = min(lh, ROW_TILE)
    full = lambda shape: pl.BlockSpec(shape, lambda p, i: (0, 0))
    return pl.pallas_call(
        _filter_mlp_kernel,
        grid=(2, lh // tl),
        in_specs=[pl.BlockSpec((1, tl, V7X_LANES), lambda p, i: (p, i, 0)),
                  full((V7X_LANES, hid)), full((1, hid)), full((1, hid)),
                  full((hid, hid)), full((1, hid)), full((1, hid)),
                  full((hid, n_out)), full((1, HY_WIDTH))],
        out_specs=pl.BlockSpec((tl, n_out), lambda p, i: (i, p)),
        out_shape=jax.ShapeDtypeStruct((lh, 2 * n_out), _F32),
        compiler_params=_cparams("arbitrary", "arbitrary"),
        name="hyena_filter_mlp",
    )(z, w1p, b1.reshape(1, hid), f1.reshape(1, hid), w2.astype(_F32), b2.reshape(1, hid),
      f2.reshape(1, hid), w3.astype(_F32), deltas)


def _freq_chunks(gp):
    return [(r0, min(FREQ_CHUNK, gp - r0)) for r0 in range(0, gp, FREQ_CHUNK)]


def _half_dft_tables(L):
    lh, n, sb = L // 2, 2 * L, 32
    gp = -(-(lh + 1) // V7X_SUBLANES) * V7X_SUBLANES
    assert lh % sb == 0
    g = jnp.arange(gp, dtype=jnp.int32)
    live = (g <= lh)[:, None]
    ang = lambda s: ((g[:, None] * s[None, :]) % L).astype(_F32) * (2.0 * math.pi / L)
    a_hi = ang(jnp.arange(lh // sb, dtype=jnp.int32) * sb)
    a_lo = ang(jnp.arange(sb, dtype=jnp.int32))
    c_hi, s_hi, c_lo, s_lo = jnp.cos(a_hi), jnp.sin(a_hi), jnp.cos(a_lo), jnp.sin(a_lo)
    cosm = (c_hi[:, :, None] * c_lo[:, None, :] - s_hi[:, :, None] * s_lo[:, None, :]).reshape(gp, lh)
    sinm = (s_hi[:, :, None] * c_lo[:, None, :] + c_hi[:, :, None] * s_lo[:, None, :]).reshape(gp, lh)
    cosm = jnp.where(live, cosm, 0.0)
    sinm = jnp.where(live, sinm, 0.0)
    ff = jnp.concatenate([cosm, -sinm], axis=0).astype(_BF16)
    wgt = jnp.where((g == 0) | (g == lh), 1.0, 2.0)[:, None] / n
    gf = jnp.concatenate([(cosm * wgt).T, (-sinm * wgt).T], axis=1).astype(_BF16)
    tw = g.astype(_F32)[:, None] * (2.0 * math.pi / n)
    wr = jnp.broadcast_to(jnp.cos(tw), (gp, HY_CH_TILE))
    wi = jnp.broadcast_to(-jnp.sin(tw), (gp, HY_CH_TILE))
    return ff, gf, wr, wi


def _half_spectra(ff_ref, wr, wi, xe, xo, r0, n):
    gp = ff_ref.shape[0] // 2
    fc = ff_ref[r0:r0 + n, :]
    fs = ff_ref[gp + r0:gp + r0 + n, :]
    er, ei = _dot(fc, xe), _dot(fs, xe)
    orr, oi = _dot(fc, xo), _dot(fs, xo)
    pr = wr * orr - wi * oi
    pi = wr * oi + wi * orr
    return (er + pr, ei + pi), (er - pr, pi - ei)


def _filter_spec_kernel(ff_ref, wr_ref, wi_ref, fe_ref, fo_ref, be_ref, bo_ref, k1r_ref, k1i_ref, k2r_ref, k2i_ref):
    fe, fo, be, bo = (r[...].astype(_BF16) for r in (fe_ref, fo_ref, be_ref, bo_ref))
    for r0, n in _freq_chunks(wr_ref.shape[0]):
        wr, wi = wr_ref[r0:r0 + n, :], wi_ref[r0:r0 + n, :]
        (f1r, f1i), (f2r, f2i) = _half_spectra(ff_ref, wr, wi, fe, fo, r0, n)
        (b1r, b1i), (b2r, b2i) = _half_spectra(ff_ref, wr, wi, be, bo, r0, n)
        k1r_ref[0, r0:r0 + n, :] = f1r + b1r
        k1i_ref[0, r0:r0 + n, :] = f1i - b1i
        k2r_ref[0, r0:r0 + n, :] = f2r + b2r
        k2i_ref[0, r0:r0 + n, :] = f2i - b2i


def _filter_spec(tables, hf2):
    ff, _, wr, wi = tables
    lh, gp = hf2.shape[0], wr.shape[0]
    n_ct = HY_WIDTH // HY_CH_TILE
    n_blk = hf2.shape[1] // 2 // HY_CH_TILE
    col = lambda blk0: pl.BlockSpec((lh, HY_CH_TILE), lambda j: (0, blk0 + j))
    out = jax.ShapeDtypeStruct((HY_ORDER, gp, HY_WIDTH), _F32)
    ospec = pl.BlockSpec((1, gp, HY_CH_TILE), lambda j: (j // n_ct, 0, j % n_ct))
    full = lambda a: pl.BlockSpec(a.shape, lambda j: (0, 0))
    return pl.pallas_call(
        _filter_spec_kernel,
        grid=(HY_ORDER * n_ct,),
        in_specs=[full(ff), full(wr), full(wi),
                  col(0), col(n_blk),
                  col(HY_ORDER * n_ct), col(n_blk + HY_ORDER * n_ct)],
        out_specs=[ospec] * 4,
        out_shape=[out] * 4,
        compiler_params=_cparams("arbitrary"),
        name="hyena_filter_spectrum",
    )(ff, wr, wi, hf2, hf2, hf2, hf2)


def _parity_conv(ze, zo, w_ref, b_ref):
    lh = ze.shape[0]
    row = lax.broadcasted_iota(jnp.int32, ze.shape, 0)
    zo_prev = jnp.where(row == 0, 0.0, pltpu.roll(zo, 1, axis=0))
    ze_next = jnp.where(row == lh - 1, 0.0, pltpu.roll(ze, lh - 1, axis=0))
    w0, w1, w2, b = w_ref[0:1, :], w_ref[1:2, :], w_ref[2:3, :], b_ref[...]
    return ((b + zo_prev * w0) + ze * w1) + zo * w2, ((b + ze * w0) + zo * w1) + ze_next * w2


def _hyena_kernel(ze_ref, zo_ref, zw_ref, zb_ref, ge_ref, go_ref, gw_ref, gb_ref, skip_ref, ff_ref, gf_ref,
                  wr_ref, wi_ref, k1r_ref, k1i_ref, k2r_ref, k2i_ref, oe_ref, oo_ref, a_ref, b_ref, *, conv_input):
    ze, zo = ze_ref[0].astype(_F32), zo_ref[0].astype(_F32)
    if conv_input:
        ze, zo = _parity_conv(ze, zo, zw_ref, zb_ref)
    xe, xo = ze.astype(_BF16), zo.astype(_BF16)
    gp = wr_ref.shape[0]
    for r0, n in _freq_chunks(gp):
        rows = slice(r0, r0 + n)
        wr, wi = wr_ref[rows, :], wi_ref[rows, :]
        (x1r, x1i), (x2r, x2i) = _half_spectra(ff_ref, wr, wi, xe, xo, r0, n)
        k1r, k1i, k2r, k2i = k1r_ref[0, rows, :], k1i_ref[0, rows, :], k2r_ref[0, rows, :], k2i_ref[0, rows, :]
        y1r, y1i = x1r * k1r - x1i * k1i, x1r * k1i + x1i * k1r
        y2r, y2i = x2r * k2r - x2i * k2i, x2r * k2i + x2i * k2r
        tr, ti = y1r - y2r, y1i + y2i
        a_ref[rows, :] = (y1r + y2r).astype(_BF16)
        a_ref[gp + r0:gp + r0 + n, :] = (y1i - y2i).astype(_BF16)
        b_ref[rows, :] = (tr * wr + ti * wi).astype(_BF16)
        b_ref[gp + r0:gp + r0 + n, :] = (ti * wr - tr * wi).astype(_BF16)
    ge, go = _parity_conv(ge_ref[0].astype(_F32), go_ref[0].astype(_F32), gw_ref, gb_ref)
    skip = skip_ref[0]
    for r0 in range(0, ze.shape[0], FREQ_CHUNK):
        rows = slice(r0, r0 + FREQ_CHUNK)
        g = gf_ref[rows, :]
        oe_ref[0, rows, :] = ge[rows] * (_dot(g, a_ref[...]) + ze[rows] * skip)
        oo_ref[0, rows, :] = go[rows] * (_dot(g, b_ref[...]) + zo[rows] * skip)


def _hyena(u2, conv_w, conv_b, tables, k1r, k1i, k2r, k2i, skip):
    ff, gf, wr, wi = tables
    B, lh, _ = u2.shape
    gp = wr.shape[0]
    assert lh % FREQ_CHUNK == 0
    n_ct = HY_WIDTH // HY_CH_TILE
    odd = HY_COLS // HY_CH_TILE
    conv_b2 = conv_b.reshape(1, HY_COLS)
    skip3 = skip.reshape(HY_ORDER, 1, HY_WIDTH)
    col = lambda blk0: pl.BlockSpec((1, lh, HY_CH_TILE), lambda c, b: (b, 0, blk0 + c))
    cw = lambda blk0: pl.BlockSpec((SHORT_CONV, HY_CH_TILE), lambda c, b: (0, blk0 + c))
    cb = lambda blk0: pl.BlockSpec((1, HY_CH_TILE), lambda c, b: (0, blk0 + c))
    const = lambda a: _resident(a.shape, lambda c, b: (0, 0))
    out = jax.ShapeDtypeStruct((B, lh, HY_WIDTH), _F32)
    ze_arr, zo_arr, ze_spec, zo_spec = u2, u2, col(2 * n_ct), col(odd + 2 * n_ct)
    for o in range(HY_ORDER):
        coef = lambda: pl.BlockSpec((1, gp, HY_CH_TILE), lambda c, b, o=o: (o, 0, c), pipeline_mode=pl.Buffered(1))
        ze_arr, zo_arr = pl.pallas_call(
            functools.partial(_hyena_kernel, conv_input=(o == 0)),
            grid=(n_ct, B),
            in_specs=[ze_spec, zo_spec, cw(2 * n_ct), cb(2 * n_ct),
                      col(o * n_ct), col(odd + o * n_ct), cw(o * n_ct), cb(o * n_ct),
                      pl.BlockSpec((1, 1, HY_CH_TILE), lambda c, b, o=o: (o, 0, c)),
                      const(ff), const(gf), const(wr), const(wi), coef(), coef(), coef(), coef()],
            out_specs=[col(0), col(0)],
            out_shape=[out, out],
            scratch_shapes=[pltpu.VMEM((2 * gp, HY_CH_TILE), _BF16), pltpu.VMEM((2 * gp, HY_CH_TILE), _BF16)],
            compiler_params=_cparams("arbitrary", "arbitrary"),
            name=f"hyena_order_{o}",
        )(ze_arr, zo_arr, conv_w, conv_b2, u2, u2, conv_w, conv_b2, skip3, ff, gf, wr, wi, k1r, k1i, k2r, k2i)
        ze_spec = zo_spec = col(0)
    return ze_arr, zo_arr


def _natten_tables(rows):
    kr = min(WIN_ROWS, rows)
    krb = min(Q_ROWS + kr - 1, rows)
    rs = np.clip(np.arange(rows) - kr // 2, 0, rows - kr)
    cs = np.clip(np.arange(GRID_W) - WIN_COLS // 2, 0, GRID_W - WIN_COLS)
    qc = np.tile(np.arange(GRID_W), Q_ROWS)[:, None]
    kc = np.tile(np.arange(GRID_W), krb)[None, :]
    cases, case_of, kstart = [], [], []
    for p in range(rows // Q_ROWS):
        k_r0 = min(rs[p * Q_ROWS], rows - krb)
        qr = (p * Q_ROWS + np.repeat(np.arange(Q_ROWS), GRID_W))[:, None]
        kr_ = (k_r0 + np.repeat(np.arange(krb), GRID_W))[None, :]
        valid = ((kr_ >= rs[qr]) & (kr_ < rs[qr] + kr) & (kc >= cs[qc]) & (kc < cs[qc] + WIN_COLS))
        dr = np.clip(kr_ - qr + WIN_ROWS - 1, 0, 2 * WIN_ROWS - 2)
        dc = np.clip(kc - qc + WIN_COLS - 1, 0, 2 * WIN_COLS - 2)
        idx = np.where(valid, dr * (2 * WIN_COLS - 1) + dc, -1).astype(np.int32)
        for ci, c in enumerate(cases):
            if np.array_equal(c, idx):
                break
        else:
            ci = len(cases)
            cases.append(idx)
        case_of.append(ci)
        kstart.append(k_r0 * GRID_W)
    return np.stack(cases), np.asarray(case_of, np.int32), np.asarray(kstart, np.int32), krb * GRID_W


def _natten_bias_kernel(dr_ref, rv_ref, u_ref, o_ref, v_ref, *, krb):
    c, h = pl.program_id(0), pl.program_id(1)
    n_m, n_k = v_ref.shape[1], v_ref.shape[2]
    w_shift, w_mask = GRID_W.bit_length() - 1, GRID_W - 1
    col = lax.broadcasted_iota(jnp.int32, (1, n_k), 1)
    kr_col, kc_col = col >> w_shift, col & w_mask
    per_kr = lambda tab, qr: sum(jnp.where(kr_col == kr, tab[(c * Q_ROWS + qr) * krb + kr], 0) for kr in range(krb))

    @pl.when(h == 0)
    def _():
        row = lax.broadcasted_iota(jnp.int32, (n_m, n_k), 0)
        for qr in range(Q_ROWS):
            hit = ((row >> w_shift) == per_kr(dr_ref, qr)) & ((row & w_mask) == kc_col)
            v_ref[qr] = jnp.where(hit, 1.0, 0.0).astype(_BF16)

    uh, ul = _split_bf16(u_ref[0])
    qc = lax.broadcasted_iota(jnp.int32, (GRID_W, n_k), 0)
    cs = jnp.clip(qc - WIN_COLS // 2, 0, GRID_W - WIN_COLS)
    col_ok = (kc_col >= cs) & (kc_col < cs + WIN_COLS)
    for qr in range(Q_ROWS):
        b = _dot(uh, v_ref[qr]) + _dot(ul, v_ref[qr])
        ok = col_ok & (per_kr(rv_ref, qr) > 0)
        o_ref[0, 0, qr * GRID_W:(qr + 1) * GRID_W, :] = jnp.where(ok, b, NEG_INF)


def _natten_bias(rpb, cases):
    n_case, nq, nk = cases.shape
    krb = nk // GRID_W
    n_dr, n_dc = 2 * WIN_ROWS - 1, 2 * WIN_COLS - 1
    c5 = cases.reshape(n_case, Q_ROWS, GRID_W, krb, GRID_W)
    dr_blk = np.where(c5 >= 0, c5 // n_dc, -1).max(axis=(2, 4))
    row_ok = dr_blk >= 0
    cs = np.clip(np.arange(GRID_W) - WIN_COLS // 2, 0, GRID_W - WIN_COLS)[:, None]
    col_ok = (np.arange(GRID_W)[None, :] >= cs) & (np.arange(GRID_W)[None, :] < cs + WIN_COLS)
    dc_idx = np.arange(GRID_W)[None, :] - np.arange(GRID_W)[:, None] + WIN_COLS - 1
    assert np.array_equal(c5 >= 0, row_ok[:, :, None, :, None] & col_ok[None, None, :, None, :])
    assert np.all((c5 < 0) | (c5 == dr_blk[:, :, None, :, None] * n_dc + dc_idx[None, None, :, None, :]))
    onehot = (dc_idx[:, None, :] == np.arange(n_dc)[None, :, None]).astype(np.float32)
    u = jnp.einsum('hdj,qjk->hqdk', rpb.astype(_F32), onehot, precision=lax.Precision.HIGHEST)
    u = u.reshape(NA_HEADS, GRID_W, n_dr * GRID_W)
    grid_spec = pltpu.PrefetchScalarGridSpec(
        num_scalar_prefetch=2,
        grid=(n_case, NA_HEADS),
        in_specs=[pl.BlockSpec((1, GRID_W, n_dr * GRID_W), lambda c, h, *_: (h, 0, 0))],
        out_specs=pl.BlockSpec((1, 1, nq, nk), lambda c, h, *_: (c, h, 0, 0)),
        scratch_shapes=[pltpu.VMEM((Q_ROWS, n_dr * GRID_W, nk), _BF16)],
    )
    return pl.pallas_call(
        functools.partial(_natten_bias_kernel, krb=krb),
        grid_spec=grid_spec,
        out_shape=jax.ShapeDtypeStruct((n_case, NA_HEADS, nq, nk), _F32),
        compiler_params=_cparams("arbitrary", "arbitrary"),
        name="natten_bias",
    )(jnp.asarray(np.maximum(dr_blk, 0).reshape(-1), jnp.int32), jnp.asarray(row_ok.reshape(-1), jnp.int32), u)


def _natten_kernel(case_ref, kstart_ref, qkv_ref, bias_ref, o_ref, *, n_pairs, n_keys):
    nq = Q_ROWS * GRID_W
    pair_w = 2 * NA_HEAD_DIM
    lane = lax.broadcasted_iota(jnp.int32, (nq, pair_w), 1)
    lo_half = lane < NA_HEAD_DIM
    scale = NA_HEAD_DIM ** -0.5

    def body(p, carry):
        q0 = pl.multiple_of(p * nq, nq)
        k0 = pl.multiple_of(kstart_ref[p], GRID_W)
        case = case_ref[p]
        for hp in range(NA_HEADS // 2):
            c0 = hp * pair_w
            q2 = qkv_ref[0, pl.ds(q0, nq), c0:c0 + pair_w]
            k2 = qkv_ref[0, pl.ds(k0, n_keys), NA_WIDTH + c0:NA_WIDTH + c0 + pair_w]
            v2 = qkv_ref[0, pl.ds(k0, n_keys), 2 * NA_WIDTH + c0:2 * NA_WIDTH + c0 + pair_w]
            zero = jnp.zeros_like(q2)
            q2 = q2 * scale
            qq = jnp.concatenate([jnp.where(lo_half, q2, zero), jnp.where(lo_half, zero, q2)], axis=0)
            s = lax.dot_general(qq, k2, (((1,), (1,)), ((), ())), preferred_element_type=_F32)
            s = s + bias_ref[case, hp]
            e = jnp.exp(s - jnp.max(s, axis=-1, keepdims=True))
            l = jnp.sum(e, axis=-1, keepdims=True)
            o = _dot(e.astype(_BF16), v2) / l
            o_ref[0, pl.ds(q0, nq), c0:c0 + pair_w] = jnp.where(lo_half, o[:nq], o[nq:])
        return carry

    lax.fori_loop(0, n_pairs, body, 0, unroll=2)


def _natten(qkv, rpb):
    B, L, _ = qkv.shape
    rows = L // GRID_W
    assert rows % Q_ROWS == 0 and rows >= Q_ROWS + WIN_ROWS - 1
    cases, case_of, kstart, n_keys = _natten_tables(rows)
    n_case = cases.shape[0]
    nq = Q_ROWS * GRID_W
    bias = _natten_bias(rpb, cases).reshape(n_case, NA_HEADS // 2, 2 * nq, n_keys)
    grid_spec = pltpu.PrefetchScalarGridSpec(
        num_scalar_prefetch=2,
        grid=(B,),
        in_specs=[pl.BlockSpec((1, L, 3 * NA_WIDTH), lambda b, *_: (b, 0, 0)),
                  _resident((n_case, NA_HEADS // 2, 2 * nq, n_keys), lambda b, *_: (0, 0, 0, 0))],
        out_specs=pl.BlockSpec((1, L, NA_WIDTH), lambda b, *_: (b, 0, 0)),
    )
    return pl.pallas_call(
        functools.partial(_natten_kernel, n_pairs=rows // Q_ROWS, n_keys=n_keys),
        grid_spec=grid_spec,
        out_shape=jax.ShapeDtypeStruct((B, L, NA_WIDTH), _F32),
        compiler_params=_cparams("arbitrary"),
        name="natten",
    )(jnp.asarray(case_of), jnp.asarray(kstart), qkv, bias)


_META_ID, _META_GATE, _META_POS = 0, 2, 4
ROUTER_ROWS = 8 + N_EXPERTS


def _split_bf16(v):
    hi = v.astype(_BF16)
    return hi, (v - hi.astype(_F32)).astype(_BF16)


def _out_router_kernel(yhe_ref, yho_ref, sel_ref, yna_ref, ghy_ref, gna_ref, wtop_ref, wbot_ref, x_ref, gffn_ref,
                       rw_ref, rb_ref, h1_ref, xs_ref, meta_ref, tile_ref, cnt_ref, carry_ref):
    @pl.when(pl.program_id(0) == 0)
    def _():
        carry_ref[...] = jnp.zeros_like(carry_ref)

    nhe = _rmsnorm(yhe_ref[...], ghy_ref[...]).astype(_BF16)
    nho = _rmsnorm(yho_ref[...], ghy_ref[...]).astype(_BF16)
    nh = (_dot(sel_ref[0], nhe) + _dot(sel_ref[1], nho)).astype(_BF16)
    nn = _rmsnorm(yna_ref[...], gna_ref[...]).astype(_BF16)
    h1 = x_ref[...] + (_dot(nh, wtop_ref[...]) + _dot(nn, wbot_ref[...]))
    h1_ref[...] = h1
    xn = _rmsnorm(h1, gffn_ref[...])
    tm = xn.shape[0]

    xn_bf = xn.astype(_BF16)
    logits = _dot(xn_bf, rw_ref[...]) + rb_ref[...]
    lt = logits.T[:ROUTER_ROWS]

    neg = -jnp.inf
    row8 = lax.broadcasted_iota(jnp.int32, (EXPERTS_PER_GROUP, tm), 0).astype(_F32)
    col_max = lambda v: jnp.max(v, axis=0, keepdims=True)
    first_max = lambda v, m: jnp.min(jnp.where(v == m, row8, float(EXPERTS_PER_GROUP)), axis=0, keepdims=True)

    gl = jnp.where(row8 < N_GROUPS, lt[0:8], neg)
    gmax = col_max(gl)
    g_w = 1.0 / jnp.sum(jnp.exp(gl - gmax), axis=0, keepdims=True)
    g_idx = first_max(gl, gmax)
    el = lt[8:8 + EXPERTS_PER_GROUP]
    for g in range(1, N_GROUPS):
        el = jnp.where(g_idx == g, lt[8 + g * EXPERTS_PER_GROUP:8 + (g + 1) * EXPERTS_PER_GROUP], el)
    m1 = col_max(el)
    i1 = first_max(el, m1)
    el2 = jnp.where(row8 == i1, neg, el)
    m2 = col_max(el2)
    i2 = first_max(el2, m2)
    r = jnp.exp(m2 - m1)
    gate1 = g_w / (1.0 + r)
    gate2 = g_w * r / (1.0 + r)
    id1 = g_idx * EXPERTS_PER_GROUP + i1
    id2 = g_idx * EXPERTS_PER_GROUP + i2

    row_e = lax.broadcasted_iota(jnp.int32, (N_EXPERTS, tm), 0).astype(_F32)
    sel1 = row_e == id1
    sel2 = row_e == id2
    onehot = jnp.where(sel1 | sel2, 1.0, 0.0)
    tri_r = lax.broadcasted_iota(jnp.int32, (tm, tm), 0)
    tri_c = lax.broadcasted_iota(jnp.int32, (tm, tm), 1)
    tri = jnp.where(tri_r < tri_c, 1.0, 0.0).astype(_BF16)
    before = _dot(onehot.astype(_BF16), tri)
    cnt = jnp.sum(onehot, axis=1, keepdims=True)
    run = jnp.floor((cnt + (MOE_CHUNK - 1)) * (1.0 / MOE_CHUNK)) * MOE_CHUNK
    run_b = jnp.broadcast_to(run, (N_EXPERTS, ROUTER_LANES))
    e_r = lax.broadcasted_iota(jnp.int32, (N_EXPERTS, N_EXPERTS), 0)
    e_c = lax.broadcasted_iota(jnp.int32, (N_EXPERTS, N_EXPERTS), 1)
    start_b = _dot_f32(jnp.where(e_c < e_r, 1.0, 0.0), run_b)
    start = start_b[:, 0:1]
    pos1 = jnp.sum(jnp.where(sel1, start + before, 0.0), axis=0, keepdims=True)
    pos2 = jnp.sum(jnp.where(sel2, start + before, 0.0), axis=0, keepdims=True)
    xs_ref[...] = xn_bf

    base_b = carry_ref[...]
    lane = lax.broadcasted_iota(jnp.int32, (N_EXPERTS, ROUTER_LANES), 1)
    tile_ref[0] = jnp.where(lane == 0, start_b, jnp.where(lane == 1, run_b, jnp.where(lane == 2, base_b, 0.0)))
    carry_ref[...] = base_b + run_b
    cnt_ref[...] = base_b + run_b

    meta = jnp.zeros((8, tm), _F32)
    for k, v in enumerate((id1, id2, gate1, gate2, pos1, pos2)):
        meta = jnp.where(row8 == k, v, meta)
    meta_ref[...] = meta


def _out_router(y_hy_even, y_hy_odd, y_na, g_hy, g_na, w_out, x2, g_ffn, wg, bg, we, be):
    T, D = x2.shape
    half = ROW_TILE // 2
    sel_t = _parity_select(ROW_TILE).transpose(0, 2, 1)
    gpad = 8 - N_GROUPS
    rw = jnp.concatenate([wg.astype(_F32), jnp.zeros((D, gpad), _F32), we.astype(_F32)], axis=1)
    rw = jnp.pad(rw, ((0, 0), (0, ROUTER_LANES - ROUTER_ROWS))).astype(_BF16)
    rb = jnp.concatenate([bg.astype(_F32), jnp.zeros((gpad,), _F32), be.astype(_F32)])
    rb = jnp.pad(rb, (0, ROUTER_LANES - ROUTER_ROWS)).reshape(1, ROUTER_LANES)
    w_bf = w_out.astype(_BF16)
    row = lambda n: pl.BlockSpec((ROW_TILE, n), lambda i: (i, 0))
    vec = lambda n: pl.BlockSpec((1, n), lambda i: (0, 0))
    mat = lambda r, c: pl.BlockSpec((r, c), lambda i: (0, 0))
    return pl.pallas_call(
        _out_router_kernel,
        grid=(T // ROW_TILE,),
        in_specs=[pl.BlockSpec((half, HY_WIDTH), lambda i: (i, 0)), pl.BlockSpec((half, HY_WIDTH), lambda i: (i, 0)),
                  pl.BlockSpec((2, ROW_TILE, half), lambda i: (0, 0, 0)),
                  row(NA_WIDTH), vec(HY_WIDTH), vec(NA_WIDTH),
                  mat(HY_WIDTH, D), mat(NA_WIDTH, D), row(D), vec(D),
                  mat(D, ROUTER_LANES), vec(ROUTER_LANES)],
        out_specs=[row(D),
                   row(D),
                   pl.BlockSpec((8, ROW_TILE), lambda i: (0, i)),
                   pl.BlockSpec((1, N_EXPERTS, ROUTER_LANES), lambda i: (i, 0, 0)),
                   mat(N_EXPERTS, ROUTER_LANES)],
        out_shape=[jax.ShapeDtypeStruct((T, D), _F32),
                   jax.ShapeDtypeStruct((T, D), _BF16),
                   jax.ShapeDtypeStruct((8, T), _F32),
                   jax.ShapeDtypeStruct((T // ROW_TILE, N_EXPERTS, ROUTER_LANES), _F32),
                   jax.ShapeDtypeStruct((N_EXPERTS, ROUTER_LANES), _F32)],
        scratch_shapes=[pltpu.VMEM((N_EXPERTS, ROUTER_LANES), _F32)],
        compiler_params=_cparams("arbitrary"),
        name="out_proj_router",
    )(y_hy_even, y_hy_odd, sel_t, y_na, g_hy.reshape(1, -1), g_na.reshape(1, -1), w_bf[:HY_WIDTH], w_bf[HY_WIDTH:],
      x2, g_ffn.reshape(1, D), rw, rb)


def _chunk_rows(c):
    return pl.ds(pl.multiple_of(c * MOE_CHUNK, MOE_CHUNK), MOE_CHUNK)


_HI_HALF = 0xFFFF0000


def _pack_halves(v):
    c = v.shape[1] // 2
    bits = lax.bitcast_convert_type(v.astype(_BF16).astype(_F32), jnp.uint32)
    return (bits[:, c:] & jnp.uint32(_HI_HALF)) | lax.shift_right_logical(bits[:, :c], jnp.uint32(16))


def _unpack_halves(w):
    lo = lax.bitcast_convert_type(lax.shift_left(w, jnp.uint32(16)), _F32)
    hi = lax.bitcast_convert_type(w & jnp.uint32(_HI_HALF), _F32)
    return lo.astype(_BF16), hi.astype(_BF16)


def _scatter_kernel(zblk_ref, dst_ref, meta_ref, xn_ref, xb_ref, zero_ref, xs_ref, sem, zero_sem):
    i = pl.program_id(0)
    slot = lax.rem(i, 2)

    @pl.when(i == 0)
    def _():
        zero_ref[...] = jnp.zeros_like(zero_ref)

        def blk_copy(j):
            start = pl.multiple_of(zblk_ref[j] * MOE_BLOCK, MOE_BLOCK)
            return pltpu.make_async_copy(zero_ref, xb_ref.at[pl.ds(start, MOE_BLOCK)], zero_sem)

        def start(j, c):
            @pl.when(zblk_ref[j] >= 0)
            def _():
                blk_copy(j).start()
            return c

        def wait(j, c):
            @pl.when(zblk_ref[j] >= 0)
            def _():
                blk_copy(j).wait()
            return c

        lax.fori_loop(0, zblk_ref.shape[0], start, 0)
        lax.fori_loop(0, zblk_ref.shape[0], wait, 0)

    tm = xn_ref.shape[0]
    meta = meta_ref[...]
    pos1, pos2 = meta[_META_POS:_META_POS + 1, :], meta[_META_POS + 1:_META_POS + 2, :]
    srow = lax.broadcasted_iota(jnp.int32, (SORT_ROWS, tm), 0).astype(_F32)
    perm = jnp.where((srow == pos1) | (srow == pos2), 1.0, 0.0).astype(_BF16)
    xs_ref[slot] = _pack_halves(_dot(perm, xn_ref[...]))

    def start(c, carry):
        pltpu.make_async_copy(xs_ref.at[slot, _chunk_rows(c)], xb_ref.at[_chunk_rows(dst_ref[0, 0, c])],
                              sem.at[slot]).start()
        return carry

    lax.fori_loop(0, SORT_CHUNKS, start, 0, unroll=DMA_UNROLL)

    wait_all = lambda s: pltpu.make_async_copy(xs_ref.at[s], xs_ref.at[s], sem.at[s]).wait()

    @pl.when(i > 0)
    def _():
        wait_all(1 - slot)

    @pl.when(i == pl.num_programs(0) - 1)
    def _():
        wait_all(slot)


def _moe_scatter(xn, meta, dst_chunk, zero_blocks, n_slots):
    T, D = xn.shape
    nt = T // ROW_TILE
    dst3 = dst_chunk.reshape(nt, 1, SORT_CHUNKS)
    grid_spec = pltpu.PrefetchScalarGridSpec(
        num_scalar_prefetch=1,
        grid=(nt,),
        in_specs=[pl.BlockSpec((1, 1, SORT_CHUNKS), lambda i, *_: (i, 0, 0), memory_space=pltpu.SMEM),
                  pl.BlockSpec((8, ROW_TILE), lambda i, *_: (0, i)),
                  pl.BlockSpec((ROW_TILE, D), lambda i, *_: (i, 0))],
        out_specs=pl.BlockSpec(memory_space=pl.ANY),
        scratch_shapes=[pltpu.VMEM((MOE_BLOCK, D // 2), jnp.uint32), pltpu.VMEM((2, SORT_ROWS, D // 2), jnp.uint32),
                        pltpu.SemaphoreType.DMA((2,)), pltpu.SemaphoreType.DMA(())],
    )
    return pl.pallas_call(
        _scatter_kernel,
        grid_spec=grid_spec,
        out_shape=jax.ShapeDtypeStruct((n_slots, D // 2), jnp.uint32),
        compiler_params=_cparams("arbitrary"),
        name="moe_scatter",
    )(zero_blocks, dst3, meta, xn)


def _expert_kernel(be_ref, nblk_ref, xb_ref, w1_ref, w3_ref, w2_ref, yb_ref, wb1_ref, wb3_ref, wb2_ref):
    i = pl.program_id(0)

    @pl.when(i < nblk_ref[0])
    def _():
        e = be_ref[i]
        e_prev = be_ref[jnp.maximum(i - 1, 0)]

        @pl.when((i == 0) | (e != e_prev))
        def _():
            wb1_ref[...] = w1_ref[0].astype(_BF16)
            wb3_ref[...] = w3_ref[0].astype(_BF16)
            wb2_ref[...] = w2_ref[0].astype(_BF16)

        x_lo, x_hi = _unpack_halves(xb_ref[...])
        half = x_lo.shape[1]
        up = lambda w_ref: _dot(x_lo, w_ref[:half, :]) + _dot(x_hi, w_ref[half:, :])
        a = up(wb1_ref)
        hid = (a * jax.nn.sigmoid(a) * up(wb3_ref)).astype(_BF16)
        yb_ref[...] = _pack_halves(_dot(hid, wb2_ref[...]))

    @pl.when(i >= nblk_ref[0])
    def _():
        yb_ref[...] = jnp.zeros_like(yb_ref)


def _moe_experts(xb, n_blocks, block_e, n_blk, w1, w3, w2):
    _, D, DE = w1.shape
    live = lambda i, nb: jnp.minimum(i, nb[0] - 1)
    grid_spec = pltpu.PrefetchScalarGridSpec(
        num_scalar_prefetch=2,
        grid=(n_blocks,),
        in_specs=[pl.BlockSpec((MOE_BLOCK, D // 2), lambda i, be, nb: (live(i, nb), 0)),
                  pl.BlockSpec((1, D, DE), lambda i, be, nb: (be[live(i, nb)], 0, 0)),
                  pl.BlockSpec((1, D, DE), lambda i, be, nb: (be[live(i, nb)], 0, 0)),
                  pl.BlockSpec((1, DE, D), lambda i, be, nb: (be[live(i, nb)], 0, 0))],
        out_specs=pl.BlockSpec((MOE_BLOCK, D // 2), lambda i, be, nb: (i, 0)),
        scratch_shapes=[pltpu.VMEM((D, DE), _BF16), pltpu.VMEM((D, DE), _BF16), pltpu.VMEM((DE, D), _BF16)],
    )
    return pl.pallas_call(
        _expert_kernel,
        grid_spec=grid_spec,
        out_shape=jax.ShapeDtypeStruct((n_blocks * MOE_BLOCK, D // 2), jnp.uint32),
        compiler_params=_cparams("arbitrary"),
        name="moe_experts",
    )(block_e, n_blk, xb, w1, w3, w2)


def _final_kernel(dst_ref, dst_next_ref, h1_ref, comb_ref, yb_ref, p_ref, wg_ref, wp_ref, gple_ref, gfin_ref,
                  o_ref, ybuf_ref, sem):
    tm = h1_ref.shape[0]
    i = pl.program_id(0)
    slot = lax.rem(i, 2)

    def gather(d_ref, s):
        def start(c, carry):
            pltpu.make_async_copy(yb_ref.at[_chunk_rows(d_ref[0, 0, c])], ybuf_ref.at[s, _chunk_rows(c)],
                                  sem.at[s]).start()
            return carry
        lax.fori_loop(0, SORT_CHUNKS, start, 0, unroll=DMA_UNROLL)

    @pl.when(i == 0)
    def _():
        gather(dst_ref, 0)

    @pl.when(i + 1 < pl.num_programs(0))
    def _():
        gather(dst_next_ref, 1 - slot)

    pltpu.make_async_copy(ybuf_ref.at[slot], ybuf_ref.at[slot], sem.at[slot]).wait()

    comb = comb_ref[...]
    g1, g2 = comb[:, _META_GATE:_META_GATE + 1], comb[:, _META_GATE + 1:_META_GATE + 2]
    p1, p2 = comb[:, _META_POS:_META_POS + 1], comb[:, _META_POS + 1:_META_POS + 2]
    lane = lax.broadcasted_iota(jnp.int32, (tm, SORT_ROWS), 1).astype(_F32)
    wmat = (jnp.where(lane == p1, g1, 0.0) + jnp.where(lane == p2, g2, 0.0)).astype(_BF16)
    y_lo, y_hi = _unpack_halves(ybuf_ref[slot])
    moe = jnp.concatenate([_dot(wmat, y_lo), _dot(wmat, y_hi)], axis=1)
    h2 = h1_ref[...] + moe
    gate = jax.nn.sigmoid(_dot(_rmsnorm(h2, gple_ref[...]).astype(_BF16), wg_ref[...]))
    h3 = h2 + _dot(p_ref[...].astype(_BF16), wp_ref[...]) * gate
    o_ref[...] = _rmsnorm(h3, gfin_ref[...])


def _final(dst_chunk, h1, comb, yb, p2, w_gate, w_proj, g_ple, g_final):
    T, D = h1.shape
    nt = T // ROW_TILE
    PD = p2.shape[1]
    row = lambda n: pl.BlockSpec((ROW_TILE, n), lambda i: (i, 0))
    vec = lambda n: pl.BlockSpec((1, n), lambda i: (0, 0))
    dest3 = dst_chunk.reshape(nt, 1, SORT_CHUNKS)
    return pl.pallas_call(
        _final_kernel,
        grid=(nt,),
        in_specs=[pl.BlockSpec((1, 1, SORT_CHUNKS), lambda i: (i, 0, 0), memory_space=pltpu.SMEM),
                  pl.BlockSpec((1, 1, SORT_CHUNKS), lambda i: (jnp.minimum(i + 1, nt - 1), 0, 0),
                               memory_space=pltpu.SMEM),
                  row(D), row(ROUTER_LANES),
                  pl.BlockSpec(memory_space=pl.ANY),
                  row(PD),
                  pl.BlockSpec((D, D), lambda i: (0, 0)),
                  pl.BlockSpec((PD, D), lambda i: (0, 0)),
                  vec(D), vec(D)],
        out_specs=row(D),
        out_shape=jax.ShapeDtypeStruct((T, D), _F32),
        scratch_shapes=[pltpu.VMEM((2, SORT_ROWS, D // 2), jnp.uint32), pltpu.SemaphoreType.DMA((2,))],
        compiler_params=_cparams("arbitrary"),
        name="moe_gather_ple_final",
    )(dest3, dest3, h1, comb, yb, p2, w_gate.astype(_BF16), w_proj.astype(_BF16),
      g_ple.reshape(1, D), g_final.reshape(1, D))


def _routing_tables(tile_tbl, counts_f, n_blocks, n_tail):
    start = tile_tbl[:, :, 0].astype(jnp.int32)
    run = tile_tbl[:, :, 1].astype(jnp.int32)
    base = tile_tbl[:, :, 2].astype(jnp.int32)
    counts = counts_f[:, 0].astype(jnp.int32)
    padded = (counts + MOE_BLOCK - 1) // MOE_BLOCK * MOE_BLOCK
    e_iota = jnp.arange(N_EXPERTS, dtype=jnp.int32)
    pad_end = jnp.sum(jnp.where(e_iota[None, :] <= e_iota[:, None], padded[None, :], 0), axis=1)
    pad_start = pad_end - padded
    row0 = (jnp.arange(SORT_CHUNKS, dtype=jnp.int32) * MOE_CHUNK)[None, :, None]
    mine = (start[:, None, :] <= row0) & (row0 < (start + run)[:, None, :])
    used = jnp.any(mine, axis=-1)
    slot0 = (pad_start[None, :] + base - start)[:, None, :] + row0
    dst = jnp.sum(jnp.where(mine, slot0, 0), axis=-1) // MOE_CHUNK
    nt = start.shape[0]
    spare = (n_blocks * MOE_BLOCK // MOE_CHUNK + (jnp.arange(nt, dtype=jnp.int32) % 2)[:, None] * SORT_CHUNKS
             + jnp.arange(SORT_CHUNKS, dtype=jnp.int32)[None, :])
    dst_scatter = jnp.where(used, dst, spare).astype(jnp.int32)
    dst_gather = jnp.where(used, dst, 0).astype(jnp.int32)
    blk_row = jnp.arange(n_blocks, dtype=jnp.int32)[:, None] * MOE_BLOCK
    block_e = jnp.minimum(jnp.sum((pad_end[None, :] <= blk_row).astype(jnp.int32), axis=1), N_EXPERTS - 1)
    n_blk = (pad_end[-1:] // MOE_BLOCK).astype(jnp.int32)
    seg_last = jnp.where(padded > counts, pad_end // MOE_BLOCK - 1, -1)
    tail = n_blk[0] + jnp.arange(n_tail, dtype=jnp.int32)
    spare_blocks = n_blocks + jnp.arange(2 * SORT_ROWS // MOE_BLOCK, dtype=jnp.int32)
    zero_blocks = jnp.concatenate([seg_last, jnp.where(tail < n_blocks, tail, -1), spare_blocks]).astype(jnp.int32)
    return dst_scatter, dst_gather, zero_blocks, block_e, n_blk


def _one_layer(h, p, g_mix, w_in, hy_conv_w, hy_conv_b, hy_f_w1, hy_f_b1, hy_f_freq1, hy_f_w2, hy_f_b2,
               hy_f_freq2, hy_f_w3, hy_skip, na_rpb, g_out_hy, g_out_na, w_out, g_ffn, router_wg, router_bg,
               router_we, router_be, exp_w1, exp_w3, exp_w2, g_ple, w_ple_gate, w_ple_proj):
    B, L, D = h.shape
    T = B * L
    assert T % ROW_TILE == 0 and L % FREQ_CHUNK == 0 and w_in.shape[1] == HY_COLS + 3 * NA_WIDTH
    x2 = h.reshape(T, D)
    u2, qkv = _in_proj(x2, g_mix, w_in)

    hf2 = _filter_mlp(L, hy_f_w1, hy_f_b1, hy_f_freq1, hy_f_w2, hy_f_b2, hy_f_freq2, hy_f_w3)
    tables = _half_dft_tables(L)
    k1r, k1i, k2r, k2i = _filter_spec(tables, hf2)
    y_hy_even, y_hy_odd = _hyena(u2.reshape(B, L // 2, 2 * HY_COLS), hy_conv_w.astype(_F32), hy_conv_b.astype(_F32),
                                 tables, k1r, k1i, k2r, k2i, hy_skip.astype(_F32))
    y_na = _natten(qkv.reshape(B, L, 3 * NA_WIDTH), na_rpb)

    h1, xn, meta, tile_tbl, counts = _out_router(
        y_hy_even.reshape(T // 2, HY_WIDTH), y_hy_odd.reshape(T // 2, HY_WIDTH), y_na.reshape(T, NA_WIDTH),
        g_out_hy, g_out_na, w_out, x2, g_ffn, router_wg, router_bg, router_we, router_be)
    max_rows = T * TOP_K + (T // ROW_TILE) * N_EXPERTS * (MOE_CHUNK - 1) + N_EXPERTS * (MOE_BLOCK - 1)
    n_blocks = -(-max_rows // MOE_BLOCK)
    n_tail = n_blocks - T * TOP_K // MOE_BLOCK
    dst_scatter, dst_gather, zero_blocks, block_e, n_blk = _routing_tables(tile_tbl, counts, n_blocks, n_tail)
    assert 2 * SORT_ROWS % MOE_BLOCK == 0
    xb = _moe_scatter(xn, meta, dst_scatter, zero_blocks, n_blocks * MOE_BLOCK + 2 * SORT_ROWS)
    yb = _moe_experts(xb, n_blocks, block_e, n_blk, exp_w1, exp_w3, exp_w2)
    comb = jnp.pad(meta[:_META_POS + TOP_K].T, ((0, 0), (0, ROUTER_LANES - _META_POS - TOP_K)))
    return h1, dst_gather, comb, yb


def kernel(x, p, g_mix, w_in, hy_conv_w, hy_conv_b, hy_f_w1, hy_f_b1, hy_f_freq1, hy_f_w2, hy_f_b2, hy_f_freq2, hy_f_w3, hy_skip, na_rpb, g_out_hy, g_out_na, w_out, g_ffn, router_wg, router_bg, router_we, router_be, exp_w1, exp_w3, exp_w2, g_ple, w_ple_gate, w_ple_proj, g_final):
    depth = p.shape[0]
    assert depth == 1, "the final RMSNorm is fused into the last layer's kernel; one layer is supported"
    B, L, D = x.shape
    i = 0
    h1, dest, gates, yb = _one_layer(
        x, p[i], g_mix[i], w_in[i], hy_conv_w[i], hy_conv_b[i], hy_f_w1[i], hy_f_b1[i], hy_f_freq1[i],
        hy_f_w2[i], hy_f_b2[i], hy_f_freq2[i], hy_f_w3[i], hy_skip[i], na_rpb[i], g_out_hy[i], g_out_na[i],
        w_out[i], g_ffn[i], router_wg[i], router_bg[i], router_we[i], router_be[i], exp_w1[i], exp_w3[i],
        exp_w2[i], g_ple[i], w_ple_gate[i], w_ple_proj[i])
    out = _final(dest, h1, gates, yb, p[i].reshape(B * L, -1), w_ple_gate[i], w_ple_proj[i], g_ple[i], g_final)
    return out.reshape(B, L, D)
```

```python
import functools
import math

import numpy as np
import jax
import jax.numpy as jnp
from jax import lax
from jax.experimental import pallas as pl
from jax.experimental.pallas import tpu as pltpu

_F32 = jnp.float32
_BF16 = jnp.bfloat16

GRID_W = 64
HY_WIDTH = 512
NA_WIDTH = 512
NA_HEADS = 8
NA_HEAD_DIM = 64
HY_ORDER = 2
SHORT_CONV = 3
FILTER_EMB = 33
FILTER_BANDS = (FILTER_EMB - 1) // 2
DECAY_TARGET = 1e-2
FAST_DECAY_PCT = 0.3
SLOW_DECAY_PCT = 1.5
WIN_ROWS = 8
WIN_COLS = 16
Q_ROWS = 4
N_GROUPS = 4
EXPERTS_PER_GROUP = 8
N_EXPERTS = N_GROUPS * EXPERTS_PER_GROUP
TOP_K = 2
MOE_BLOCK = 512
EPS = 1e-6
NEG_INF = -1e30
HY_COLS = (HY_ORDER + 1) * HY_WIDTH

V7X_LANES = 128
V7X_SUBLANES = 8
V7X_VMEM_LIMIT_BYTES = 56 * 2 ** 20

ROW_TILE = 512
FREQ_CHUNK = 512
HY_CH_TILE = 256
ROUTER_LANES = V7X_LANES
MOE_CHUNK = V7X_SUBLANES
SORT_ROWS = -(-(TOP_K * ROW_TILE + N_EXPERTS * (MOE_CHUNK - 1)) // V7X_LANES) * V7X_LANES
SORT_CHUNKS = SORT_ROWS // MOE_CHUNK
DMA_UNROLL = 8


def _cparams(*sem):
    return pltpu.CompilerParams(dimension_semantics=sem, vmem_limit_bytes=V7X_VMEM_LIMIT_BYTES)


def _resident(shape, index_map):
    return pl.BlockSpec(shape, index_map, pipeline_mode=pl.Buffered(1))


def _rmsnorm(x, g):
    return x * lax.rsqrt(jnp.mean(x * x, axis=-1, keepdims=True) + EPS) * g


def _dot(a, b):
    return jnp.dot(a, b, preferred_element_type=_F32)


def _dot_f32(a, b):
    return jnp.dot(a, b, preferred_element_type=_F32, precision=lax.Precision.HIGHEST)


def _parity_select(n):
    r = jnp.arange(n // 2, dtype=jnp.int32)[None, :, None]
    t = jnp.arange(n, dtype=jnp.int32)[None, None, :]
    p = jnp.arange(2, dtype=jnp.int32)[:, None, None]
    return (t == 2 * r + p).astype(_BF16)


def _in_proj_kernel(x_ref, g_ref, w_ref, sel_ref, u2_ref, qkv_ref):
    xn = _rmsnorm(x_ref[...], g_ref[...]).astype(_BF16)
    n_hy = u2_ref.shape[1] // 2
    xe = _dot(sel_ref[0], xn).astype(_BF16)
    xo = _dot(sel_ref[1], xn).astype(_BF16)
    for c0 in range(0, n_hy, 512):
        w = w_ref[:, c0:c0 + 512]
        u2_ref[:, c0:c0 + 512] = _dot(xe, w).astype(_BF16)
        u2_ref[:, n_hy + c0:n_hy + c0 + 512] = _dot(xo, w).astype(_BF16)
    for c0 in range(0, qkv_ref.shape[1], 512):
        qkv_ref[:, c0:c0 + 512] = _dot(xn, w_ref[:, n_hy + c0:n_hy + c0 + 512]).astype(_BF16)


def _in_proj(x2, g_mix, w_in):
    T, D = x2.shape
    n_in = w_in.shape[1]
    n_qkv = n_in - HY_COLS
    half = ROW_TILE // 2
    return pl.pallas_call(
        _in_proj_kernel,
        grid=(T // ROW_TILE,),
        in_specs=[pl.BlockSpec((ROW_TILE, D), lambda i: (i, 0)),
                  pl.BlockSpec((1, D), lambda i: (0, 0)),
                  _resident((D, n_in), lambda i: (0, 0)),
                  pl.BlockSpec((2, half, ROW_TILE), lambda i: (0, 0, 0))],
        out_specs=[pl.BlockSpec((half, 2 * HY_COLS), lambda i: (i, 0)),
                   pl.BlockSpec((ROW_TILE, n_qkv), lambda i: (i, 0))],
        out_shape=[jax.ShapeDtypeStruct((T // 2, 2 * HY_COLS), _BF16),
                   jax.ShapeDtypeStruct((T, n_qkv), _BF16)],
        compiler_params=_cparams("arbitrary"),
        name="in_proj",
    )(x2, g_mix.reshape(1, D), w_in.astype(_BF16), _parity_select(ROW_TILE))


def _filter_mlp_kernel(z_ref, w1_ref, b1_ref, f1_ref, w2_ref, b2_ref, f2_ref, w3_ref, delta_ref, hf_ref):
    z = z_ref[0]
    hid = jnp.sin(f1_ref[...] * (_dot_f32(z, w1_ref[...]) + b1_ref[...]))
    hid = jnp.sin(f2_ref[...] * (_dot_f32(hid, w2_ref[...]) + b2_ref[...]))
    hf = _dot_f32(hid, w3_ref[...])
    decay = jnp.exp(-z[:, 0:1] * delta_ref[...])
    tl = z.shape[0]
    row = lax.broadcasted_iota(jnp.int32, (tl, HY_WIDTH), 0)
    offset0 = (row == 0) & (pl.program_id(0) == 0) & (pl.program_id(1) == 0)
    for k in range(2 * HY_ORDER):
        blk = hf[:, k * HY_WIDTH:(k + 1) * HY_WIDTH] * decay
        if k >= HY_ORDER:
            blk = jnp.where(offset0, 0.0, blk)
        hf_ref[:, k * HY_WIDTH:(k + 1) * HY_WIDTH] = blk


def _filter_mlp(L, w1, b1, f1, w2, b2, f2, w3):
    t = jnp.linspace(0.0, 1.0, L, dtype=_F32)[:, None]
    w = 2.0 * math.pi * jnp.arange(L, dtype=_F32)[:, None] / L
    bands = jnp.linspace(1e-4, FILTER_BANDS - 1, FILTER_BANDS, dtype=_F32)[None, :]
    z = jnp.concatenate([t, jnp.cos(bands * w), -jnp.sin(bands * w)], axis=-1)
    z = jnp.pad(z, ((0, 0), (0, V7X_LANES - FILTER_EMB)))
    z = jnp.stack([z[0::2], z[1::2]])
    w1p = jnp.pad(w1.astype(_F32), ((0, V7X_LANES - FILTER_EMB), (0, 0)))
    max_decay = math.log(DECAY_TARGET) / FAST_DECAY_PCT
    min_decay = math.log(DECAY_TARGET) / SLOW_DECAY_PCT
    deltas = jnp.abs(jnp.linspace(min_decay, max_decay, HY_WIDTH, dtype=_F32))[None, :]
    hid = w1.shape[1]
    n_out = w3.shape[1]
    lh = L // 2
    tl = min(lh, ROW_TILE)
    full = lambda shape: pl.BlockSpec(shape, lambda p, i: (0, 0))
    return pl.pallas_call(
        _filter_mlp_kernel,
        grid=(2, lh // tl),
        in_specs=[pl.BlockSpec((1, tl, V7X_LANES), lambda p, i: (p, i, 0)),
                  full((V7X_LANES, hid)), full((1, hid)), full((1, hid)),
                  full((hid, hid)), full((1, hid)), full((1, hid)),
                  full((hid, n_out)), full((1, HY_WIDTH))],
        out_specs=pl.BlockSpec((tl, n_out), lambda p, i: (i, p)),
        out_shape=jax.ShapeDtypeStruct((lh, 2 * n_out), _F32),
        compiler_params=_cparams("arbitrary", "arbitrary"),
        name="hyena_filter_mlp",
    )(z, w1p, b1.reshape(1, hid), f1.reshape(1, hid), w2.astype(_F32), b2.reshape(1, hid),
      f2.reshape(1, hid), w3.astype(_F32), deltas)


def _freq_chunks(gp):
    return [(r0, min(FREQ_CHUNK, gp - r0)) for r0 in range(0, gp, FREQ_CHUNK)]


def _half_dft_tables(L):
    lh, n, sb = L // 2, 2 * L, 32
    gp = -(-(lh + 1) // V7X_SUBLANES) * V7X_SUBLANES
    assert lh % sb == 0
    g = jnp.arange(gp, dtype=jnp.int32)
    live = (g <= lh)[:, None]
    ang = lambda s: ((g[:, None] * s[None, :]) % L).astype(_F32) * (2.0 * math.pi / L)
    a_hi = ang(jnp.arange(lh // sb, dtype=jnp.int32) * sb)
    a_lo = ang(jnp.arange(sb, dtype=jnp.int32))
    c_hi, s_hi, c_lo, s_lo = jnp.cos(a_hi), jnp.sin(a_hi), jnp.cos(a_lo), jnp.sin(a_lo)
    cosm = (c_hi[:, :, None] * c_lo[:, None, :] - s_hi[:, :, None] * s_lo[:, None, :]).reshape(gp, lh)
    sinm = (s_hi[:, :, None] * c_lo[:, None, :] + c_hi[:, :, None] * s_lo[:, None, :]).reshape(gp, lh)
    cosm = jnp.where(live, cosm, 0.0)
    sinm = jnp.where(live, sinm, 0.0)
    ff = jnp.concatenate([cosm, -sinm], axis=0).astype(_BF16)
    wgt = jnp.where((g == 0) | (g == lh), 1.0, 2.0)[:, None] / n
    gf = jnp.concatenate([(cosm * wgt).T, (-sinm * wgt).T], axis=1).astype(_BF16)
    tw = g.astype(_F32)[:, None] * (2.0 * math.pi / n)
    wr = jnp.broadcast_to(jnp.cos(tw), (gp, HY_CH_TILE))
    wi = jnp.broadcast_to(-jnp.sin(tw), (gp, HY_CH_TILE))
    return ff, gf, wr, wi


def _half_spectra(ff_ref, wr, wi, xe, xo, r0, n):
    gp = ff_ref.shape[0] // 2
    fc = ff_ref[r0:r0 + n, :]
    fs = ff_ref[gp + r0:gp + r0 + n, :]
    er, ei = _dot(fc, xe), _dot(fs, xe)
    orr, oi = _dot(fc, xo), _dot(fs, xo)
    pr = wr * orr - wi * oi
    pi = wr * oi + wi * orr
    return (er + pr, ei + pi), (er - pr, pi - ei)


def _filter_spec_kernel(ff_ref, wr_ref, wi_ref, fe_ref, fo_ref, be_ref, bo_ref, k1r_ref, k1i_ref, k2r_ref, k2i_ref):
    fe, fo, be, bo = (r[...].astype(_BF16) for r in (fe_ref, fo_ref, be_ref, bo_ref))
    for r0, n in _freq_chunks(wr_ref.shape[0]):
        wr, wi = wr_ref[r0:r0 + n, :], wi_ref[r0:r0 + n, :]
        (f1r, f1i), (f2r, f2i) = _half_spectra(ff_ref, wr, wi, fe, fo, r0, n)
        (b1r, b1i), (b2r, b2i) = _half_spectra(ff_ref, wr, wi, be, bo, r0, n)
        k1r_ref[0, r0:r0 + n, :] = f1r + b1r
        k1i_ref[0, r0:r0 + n, :] = f1i - b1i
        k2r_ref[0, r0:r0 + n, :] = f2r + b2r
        k2i_ref[0, r0:r0 + n, :] = f2i - b2i


def _filter_spec(tables, hf2):
    ff, _, wr, wi = tables
    lh, gp = hf2.shape[0], wr.shape[0]
    n_ct = HY_WIDTH // HY_CH_TILE
    n_blk = hf2.shape[1] // 2 // HY_CH_TILE
    col = lambda blk0: pl.BlockSpec((lh, HY_CH_TILE), lambda j: (0, blk0 + j))
    out = jax.ShapeDtypeStruct((HY_ORDER, gp, HY_WIDTH), _F32)
    ospec = pl.BlockSpec((1, gp, HY_CH_TILE), lambda j: (j // n_ct, 0, j % n_ct))
    full = lambda a: pl.BlockSpec(a.shape, lambda j: (0, 0))
    return pl.pallas_call(
        _filter_spec_kernel,
        grid=(HY_ORDER * n_ct,),
        in_specs=[full(ff), full(wr), full(wi),
                  col(0), col(n_blk),
                  col(HY_ORDER * n_ct), col(n_blk + HY_ORDER * n_ct)],
        out_specs=[ospec] * 4,
        out_shape=[out] * 4,
        compiler_params=_cparams("arbitrary"),
        name="hyena_filter_spectrum",
    )(ff, wr, wi, hf2, hf2, hf2, hf2)


def _parity_conv(ze, zo, w_ref, b_ref):
    lh = ze.shape[0]
    row = lax.broadcasted_iota(jnp.int32, ze.shape, 0)
    zo_prev = jnp.where(row == 0, 0.0, pltpu.roll(zo, 1, axis=0))
    ze_next = jnp.where(row == lh - 1, 0.0, pltpu.roll(ze, lh - 1, axis=0))
    w0, w1, w2, b = w_ref[0:1, :], w_ref[1:2, :], w_ref[2:3, :], b_ref[...]
    return ((b + zo_prev * w0) + ze * w1) + zo * w2, ((b + ze * w0) + zo * w1) + ze_next * w2


def _hyena_kernel(ze_ref, zo_ref, zw_ref, zb_ref, ge_ref, go_ref, gw_ref, gb_ref, skip_ref, ff_ref, gf_ref,
                  wr_ref, wi_ref, k1r_ref, k1i_ref, k2r_ref, k2i_ref, oe_ref, oo_ref, a_ref, b_ref, *, conv_input):
    ze, zo = ze_ref[0].astype(_F32), zo_ref[0].astype(_F32)
    if conv_input:
        ze, zo = _parity_conv(ze, zo, zw_ref, zb_ref)
    xe, xo = ze.astype(_BF16), zo.astype(_BF16)
    gp = wr_ref.shape[0]
    for r0, n in _freq_chunks(gp):
        rows = slice(r0, r0 + n)
        wr, wi = wr_ref[rows, :], wi_ref[rows, :]
        (x1r, x1i), (x2r, x2i) = _half_spectra(ff_ref, wr, wi, xe, xo, r0, n)
        k1r, k1i, k2r, k2i = k1r_ref[0, rows, :], k1i_ref[0, rows, :], k2r_ref[0, rows, :], k2i_ref[0, rows, :]
        y1r, y1i = x1r * k1r - x1i * k1i, x1r * k1i + x1i * k1r
        y2r, y2i = x2r * k2r - x2i * k2i, x2r * k2i + x2i * k2r
        tr, ti = y1r - y2r, y1i + y2i
        a_ref[rows, :] = (y1r + y2r).astype(_BF16)
        a_ref[gp + r0:gp + r0 + n, :] = (y1i - y2i).astype(_BF16)
        b_ref[rows, :] = (tr * wr + ti * wi).astype(_BF16)
        b_ref[gp + r0:gp + r0 + n, :] = (ti * wr - tr * wi).astype(_BF16)
    ge, go = _parity_conv(ge_ref[0].astype(_F32), go_ref[0].astype(_F32), gw_ref, gb_ref)
    skip = skip_ref[0]
    for r0 in range(0, ze.shape[0], FREQ_CHUNK):
        rows = slice(r0, r0 + FREQ_CHUNK)
        g = gf_ref[rows, :]
        oe_ref[0, rows, :] = ge[rows] * (_dot(g, a_ref[...]) + ze[rows] * skip)
        oo_ref[0, rows, :] = go[rows] * (_dot(g, b_ref[...]) + zo[rows] * skip)


def _hyena(u2, conv_w, conv_b, tables, k1r, k1i, k2r, k2i, skip):
    ff, gf, wr, wi = tables
    B, lh, _ = u2.shape
    gp = wr.shape[0]
    assert lh % FREQ_CHUNK == 0
    n_ct = HY_WIDTH // HY_CH_TILE
    odd = HY_COLS // HY_CH_TILE
    conv_b2 = conv_b.reshape(1, HY_COLS)
    skip3 = skip.reshape(HY_ORDER, 1, HY_WIDTH)
    col = lambda blk0: pl.BlockSpec((1, lh, HY_CH_TILE), lambda c, b: (b, 0, blk0 + c))
    cw = lambda blk0: pl.BlockSpec((SHORT_CONV, HY_CH_TILE), lambda c, b: (0, blk0 + c))
    cb = lambda blk0: pl.BlockSpec((1, HY_CH_TILE), lambda c, b: (0, blk0 + c))
    const = lambda a: _resident(a.shape, lambda c, b: (0, 0))
    out = jax.ShapeDtypeStruct((B, lh, HY_WIDTH), _F32)
    ze_arr, zo_arr, ze_spec, zo_spec = u2, u2, col(2 * n_ct), col(odd + 2 * n_ct)
    for o in range(HY_ORDER):
        coef = lambda: pl.BlockSpec((1, gp, HY_CH_TILE), lambda c, b, o=o: (o, 0, c), pipeline_mode=pl.Buffered(1))
        ze_arr, zo_arr = pl.pallas_call(
            functools.partial(_hyena_kernel, conv_input=(o == 0)),
            grid=(n_ct, B),
            in_specs=[ze_spec, zo_spec, cw(2 * n_ct), cb(2 * n_ct),
                      col(o * n_ct), col(odd + o * n_ct), cw(o * n_ct), cb(o * n_ct),
                      pl.BlockSpec((1, 1, HY_CH_TILE), lambda c, b, o=o: (o, 0, c)),
                      const(ff), const(gf), const(wr), const(wi), coef(), coef(), coef(), coef()],
            out_specs=[col(0), col(0)],
            out_shape=[out, out],
            scratch_shapes=[pltpu.VMEM((2 * gp, HY_CH_TILE), _BF16), pltpu.VMEM((2 * gp, HY_CH_TILE), _BF16)],
            compiler_params=_cparams("arbitrary", "arbitrary"),
            name=f"hyena_order_{o}",
        )(ze_arr, zo_arr, conv_w, conv_b2, u2, u2, conv_w, conv_b2, skip3, ff, gf, wr, wi, k1r, k1i, k2r, k2i)
        ze_spec = zo_spec = col(0)
    return ze_arr, zo_arr


def _natten_tables(rows):
    kr = min(WIN_ROWS, rows)
    krb = min(-(-(Q_ROWS + kr - 1) // 2) * 2, rows)
    rs = np.clip(np.arange(rows) - kr // 2, 0, rows - kr)
    cs = np.clip(np.arange(GRID_W) - WIN_COLS // 2, 0, GRID_W - WIN_COLS)
    qc = np.tile(np.arange(GRID_W), Q_ROWS)[:, None]
    kc = np.tile(np.arange(GRID_W), krb)[None, :]
    cases, case_of, kstart = [], [], []
    for p in range(rows // Q_ROWS):
        k_r0 = min(rs[p * Q_ROWS], rows - krb)
        qr = (p * Q_ROWS + np.repeat(np.arange(Q_ROWS), GRID_W))[:, None]
        kr_ = (k_r0 + np.repeat(np.arange(krb), GRID_W))[None, :]
        valid = ((kr_ >= rs[qr]) & (kr_ < rs[qr] + kr) & (kc >= cs[qc]) & (kc < cs[qc] + WIN_COLS))
        dr = np.clip(kr_ - qr + WIN_ROWS - 1, 0, 2 * WIN_ROWS - 2)
        dc = np.clip(kc - qc + WIN_COLS - 1, 0, 2 * WIN_COLS - 2)
        idx = np.where(valid, dr * (2 * WIN_COLS - 1) + dc, -1).astype(np.int32)
        for ci, c in enumerate(cases):
            if np.array_equal(c, idx):
                break
        else:
            ci = len(cases)
            cases.append(idx)
        case_of.append(ci)
        kstart.append(k_r0 * GRID_W)
    return np.stack(cases), np.asarray(case_of, np.int32), np.asarray(kstart, np.int32), krb * GRID_W


def _natten_bias_kernel(dr_ref, rv_ref, u_ref, o_ref, v_ref, *, krb):
    c, h = pl.program_id(0), pl.program_id(1)
    n_m, n_k = v_ref.shape[1], v_ref.shape[2]
    w_shift, w_mask = GRID_W.bit_length() - 1, GRID_W - 1
    col = lax.broadcasted_iota(jnp.int32, (1, n_k), 1)
    kr_col, kc_col = col >> w_shift, col & w_mask
    per_kr = lambda tab, qr: sum(jnp.where(kr_col == kr, tab[(c * Q_ROWS + qr) * krb + kr], 0) for kr in range(krb))

    @pl.when(h == 0)
    def _():
        row = lax.broadcasted_iota(jnp.int32, (n_m, n_k), 0)
        for qr in range(Q_ROWS):
            hit = ((row >> w_shift) == per_kr(dr_ref, qr)) & ((row & w_mask) == kc_col)
            v_ref[qr] = jnp.where(hit, 1.0, 0.0).astype(_BF16)

    uh, ul = _split_bf16(u_ref[0])
    qc = lax.broadcasted_iota(jnp.int32, (GRID_W, n_k), 0)
    cs = jnp.clip(qc - WIN_COLS // 2, 0, GRID_W - WIN_COLS)
    col_ok = (kc_col >= cs) & (kc_col < cs + WIN_COLS)
    for qr in range(Q_ROWS):
        b = _dot(uh, v_ref[qr]) + _dot(ul, v_ref[qr])
        ok = col_ok & (per_kr(rv_ref, qr) > 0)
        o_ref[0, 0, qr * GRID_W:(qr + 1) * GRID_W, :] = jnp.where(ok, b, NEG_INF)


def _natten_bias(rpb, cases):
    n_case, nq, nk = cases.shape
    krb = nk // GRID_W
    n_dr, n_dc = 2 * WIN_ROWS - 1, 2 * WIN_COLS - 1
    c5 = cases.reshape(n_case, Q_ROWS, GRID_W, krb, GRID_W)
    dr_blk = np.where(c5 >= 0, c5 // n_dc, -1).max(axis=(2, 4))
    row_ok = dr_blk >= 0
    cs = np.clip(np.arange(GRID_W) - WIN_COLS // 2, 0, GRID_W - WIN_COLS)[:, None]
    col_ok = (np.arange(GRID_W)[None, :] >= cs) & (np.arange(GRID_W)[None, :] < cs + WIN_COLS)
    dc_idx = np.arange(GRID_W)[None, :] - np.arange(GRID_W)[:, None] + WIN_COLS - 1
    assert np.array_equal(c5 >= 0, row_ok[:, :, None, :, None] & col_ok[None, None, :, None, :])
    assert np.all((c5 < 0) | (c5 == dr_blk[:, :, None, :, None] * n_dc + dc_idx[None, None, :, None, :]))
    onehot = (dc_idx[:, None, :] == np.arange(n_dc)[None, :, None]).astype(np.float32)
    u = jnp.einsum('hdj,qjk->hqdk', rpb.astype(_F32), onehot, precision=lax.Precision.HIGHEST)
    u = u.reshape(NA_HEADS, GRID_W, n_dr * GRID_W)
    grid_spec = pltpu.PrefetchScalarGridSpec(
        num_scalar_prefetch=2,
        grid=(n_case, NA_HEADS),
        in_specs=[pl.BlockSpec((1, GRID_W, n_dr * GRID_W), lambda c, h, *_: (h, 0, 0))],
        out_specs=pl.BlockSpec((1, 1, nq, nk), lambda c, h, *_: (c, h, 0, 0)),
        scratch_shapes=[pltpu.VMEM((Q_ROWS, n_dr * GRID_W, nk), _BF16)],
    )
    return pl.pallas_call(
        functools.partial(_natten_bias_kernel, krb=krb),
        grid_spec=grid_spec,
        out_shape=jax.ShapeDtypeStruct((n_case, NA_HEADS, nq, nk), _F32),
        compiler_params=_cparams("arbitrary", "arbitrary"),
        name="natten_bias",
    )(jnp.asarray(np.maximum(dr_blk, 0).reshape(-1), jnp.int32), jnp.asarray(row_ok.reshape(-1), jnp.int32), u)


def _natten_kernel(case_ref, kstart_ref, qkv_ref, bias_ref, o_ref, *, n_pairs, n_keys):
    nq = Q_ROWS * GRID_W
    pair_w = 2 * NA_HEAD_DIM
    lane = lax.broadcasted_iota(jnp.int32, (nq, pair_w), 1)
    lo_half = lane < NA_HEAD_DIM
    scale = NA_HEAD_DIM ** -0.5

    def body(p, carry):
        q0 = pl.multiple_of(p * nq, nq)
        k0 = pl.multiple_of(kstart_ref[p], GRID_W)
        case = case_ref[p]
        for hp in range(NA_HEADS // 2):
            c0 = hp * pair_w
            q2 = qkv_ref[0, pl.ds(q0, nq), c0:c0 + pair_w]
            k2 = qkv_ref[0, pl.ds(k0, n_keys), NA_WIDTH + c0:NA_WIDTH + c0 + pair_w]
            v2 = qkv_ref[0, pl.ds(k0, n_keys), 2 * NA_WIDTH + c0:2 * NA_WIDTH + c0 + pair_w]
            zero = jnp.zeros_like(q2)
            q2 = q2 * scale
            qq = jnp.concatenate([jnp.where(lo_half, q2, zero), jnp.where(lo_half, zero, q2)], axis=0)
            s = lax.dot_general(qq, k2, (((1,), (1,)), ((), ())), preferred_element_type=_F32)
            s = s + bias_ref[case, hp]
            e = jnp.exp(s - jnp.max(s, axis=-1, keepdims=True))
            l = jnp.sum(e, axis=-1, keepdims=True)
            o = _dot(e.astype(_BF16), v2) / l
            o_ref[0, pl.ds(q0, nq), c0:c0 + pair_w] = jnp.where(lo_half, o[:nq], o[nq:])
        return carry

    lax.fori_loop(0, n_pairs, body, 0, unroll=2)


def _natten(qkv, rpb):
    B, L, _ = qkv.shape
    rows = L // GRID_W
    assert rows % (2 * Q_ROWS) == 0 and rows >= Q_ROWS + WIN_ROWS
    cases, case_of, kstart, n_keys = _natten_tables(rows)
    n_case = cases.shape[0]
    nq = Q_ROWS * GRID_W
    bias = _natten_bias(rpb, cases).reshape(n_case, NA_HEADS // 2, 2 * nq, n_keys)
    grid_spec = pltpu.PrefetchScalarGridSpec(
        num_scalar_prefetch=2,
        grid=(B,),
        in_specs=[pl.BlockSpec((1, L, 3 * NA_WIDTH), lambda b, *_: (b, 0, 0)),
                  _resident((n_case, NA_HEADS // 2, 2 * nq, n_keys), lambda b, *_: (0, 0, 0, 0))],
        out_specs=pl.BlockSpec((1, L, NA_WIDTH), lambda b, *_: (b, 0, 0)),
    )
    return pl.pallas_call(
        functools.partial(_natten_kernel, n_pairs=rows // Q_ROWS, n_keys=n_keys),
        grid_spec=grid_spec,
        out_shape=jax.ShapeDtypeStruct((B, L, NA_WIDTH), _F32),
        compiler_params=_cparams("arbitrary"),
        name="natten",
    )(jnp.asarray(case_of), jnp.asarray(kstart), qkv, bias)


_META_ID, _META_GATE, _META_POS = 0, 2, 4
ROUTER_ROWS = 8 + N_EXPERTS


def _split_bf16(v):
    hi = v.astype(_BF16)
    return hi, (v - hi.astype(_F32)).astype(_BF16)


def _out_router_kernel(yhe_ref, yho_ref, sel_ref, yna_ref, ghy_ref, gna_ref, wtop_ref, wbot_ref, x_ref, gffn_ref,
                       rw_ref, rb_ref, h1_ref, xs_ref, meta_ref, tile_ref, cnt_ref, carry_ref):
    @pl.when(pl.program_id(0) == 0)
    def _():
        carry_ref[...] = jnp.zeros_like(carry_ref)

    nhe = _rmsnorm(yhe_ref[...], ghy_ref[...]).astype(_BF16)
    nho = _rmsnorm(yho_ref[...], ghy_ref[...]).astype(_BF16)
    nh = (_dot(sel_ref[0], nhe) + _dot(sel_ref[1], nho)).astype(_BF16)
    nn = _rmsnorm(yna_ref[...], gna_ref[...]).astype(_BF16)
    h1 = x_ref[...] + (_dot(nh, wtop_ref[...]) + _dot(nn, wbot_ref[...]))
    h1_ref[...] = h1
    xn = _rmsnorm(h1, gffn_ref[...])
    tm = xn.shape[0]

    xn_bf = xn.astype(_BF16)
    logits = _dot(xn_bf, rw_ref[...]) + rb_ref[...]
    lt = logits.T[:ROUTER_ROWS]

    neg = -jnp.inf
    row8 = lax.broadcasted_iota(jnp.int32, (EXPERTS_PER_GROUP, tm), 0).astype(_F32)
    col_max = lambda v: jnp.max(v, axis=0, keepdims=True)
    first_max = lambda v, m: jnp.min(jnp.where(v == m, row8, float(EXPERTS_PER_GROUP)), axis=0, keepdims=True)

    gl = jnp.where(row8 < N_GROUPS, lt[0:8], neg)
    gmax = col_max(gl)
    g_w = 1.0 / jnp.sum(jnp.exp(gl - gmax), axis=0, keepdims=True)
    g_idx = first_max(gl, gmax)
    el = lt[8:8 + EXPERTS_PER_GROUP]
    for g in range(1, N_GROUPS):
        el = jnp.where(g_idx == g, lt[8 + g * EXPERTS_PER_GROUP:8 + (g + 1) * EXPERTS_PER_GROUP], el)
    m1 = col_max(el)
    i1 = first_max(el, m1)
    el2 = jnp.where(row8 == i1, neg, el)
    m2 = col_max(el2)
    i2 = first_max(el2, m2)
    r = jnp.exp(m2 - m1)
    gate1 = g_w / (1.0 + r)
    gate2 = g_w * r / (1.0 + r)
    id1 = g_idx * EXPERTS_PER_GROUP + i1
    id2 = g_idx * EXPERTS_PER_GROUP + i2

    row_e = lax.broadcasted_iota(jnp.int32, (N_EXPERTS, tm), 0).astype(_F32)
    sel1 = row_e == id1
    sel2 = row_e == id2
    onehot = jnp.where(sel1 | sel2, 1.0, 0.0)
    tri_r = lax.broadcasted_iota(jnp.int32, (tm, tm), 0)
    tri_c = lax.broadcasted_iota(jnp.int32, (tm, tm), 1)
    tri = jnp.where(tri_r < tri_c, 1.0, 0.0).astype(_BF16)
    before = _dot(onehot.astype(_BF16), tri)
    cnt = jnp.sum(onehot, axis=1, keepdims=True)
    run = jnp.floor((cnt + (MOE_CHUNK - 1)) * (1.0 / MOE_CHUNK)) * MOE_CHUNK
    run_b = jnp.broadcast_to(run, (N_EXPERTS, ROUTER_LANES))
    e_r = lax.broadcasted_iota(jnp.int32, (N_EXPERTS, N_EXPERTS), 0)
    e_c = lax.broadcasted_iota(jnp.int32, (N_EXPERTS, N_EXPERTS), 1)
    start_b = _dot_f32(jnp.where(e_c < e_r, 1.0, 0.0), run_b)
    start = start_b[:, 0:1]
    pos1 = jnp.sum(jnp.where(sel1, start + before, 0.0), axis=0, keepdims=True)
    pos2 = jnp.sum(jnp.where(sel2, start + before, 0.0), axis=0, keepdims=True)
    xs_ref[...] = xn_bf

    base_b = carry_ref[...]
    lane = lax.broadcasted_iota(jnp.int32, (N_EXPERTS, ROUTER_LANES), 1)
    tile_ref[0] = jnp.where(lane == 0, start_b, jnp.where(lane == 1, run_b, jnp.where(lane == 2, base_b, 0.0)))
    carry_ref[...] = base_b + run_b
    cnt_ref[...] = base_b + run_b

    meta = jnp.zeros((8, tm), _F32)
    for k, v in enumerate((id1, id2, gate1, gate2, pos1, pos2)):
        meta = jnp.where(row8 == k, v, meta)
    meta_ref[...] = meta


def _out_router(y_hy_even, y_hy_odd, y_na, g_hy, g_na, w_out, x2, g_ffn, wg, bg, we, be):
    T, D = x2.shape
    half = ROW_TILE // 2
    sel_t = _parity_select(ROW_TILE).transpose(0, 2, 1)
    gpad = 8 - N_GROUPS
    rw = jnp.concatenate([wg.astype(_F32), jnp.zeros((D, gpad), _F32), we.astype(_F32)], axis=1)
    rw = jnp.pad(rw, ((0, 0), (0, ROUTER_LANES - ROUTER_ROWS))).astype(_BF16)
    rb = jnp.concatenate([bg.astype(_F32), jnp.zeros((gpad,), _F32), be.astype(_F32)])
    rb = jnp.pad(rb, (0, ROUTER_LANES - ROUTER_ROWS)).reshape(1, ROUTER_LANES)
    w_bf = w_out.astype(_BF16)
    row = lambda n: pl.BlockSpec((ROW_TILE, n), lambda i: (i, 0))
    vec = lambda n: pl.BlockSpec((1, n), lambda i: (0, 0))
    mat = lambda r, c: pl.BlockSpec((r, c), lambda i: (0, 0))
    return pl.pallas_call(
        _out_router_kernel,
        grid=(T // ROW_TILE,),
        in_specs=[pl.BlockSpec((half, HY_WIDTH), lambda i: (i, 0)), pl.BlockSpec((half, HY_WIDTH), lambda i: (i, 0)),
                  pl.BlockSpec((2, ROW_TILE, half), lambda i: (0, 0, 0)),
                  row(NA_WIDTH), vec(HY_WIDTH), vec(NA_WIDTH),
                  mat(HY_WIDTH, D), mat(NA_WIDTH, D), row(D), vec(D),
                  mat(D, ROUTER_LANES), vec(ROUTER_LANES)],
        out_specs=[row(D),
                   row(D),
                   pl.BlockSpec((8, ROW_TILE), lambda i: (0, i)),
                   pl.BlockSpec((1, N_EXPERTS, ROUTER_LANES), lambda i: (i, 0, 0)),
                   mat(N_EXPERTS, ROUTER_LANES)],
        out_shape=[jax.ShapeDtypeStruct((T, D), _F32),
                   jax.ShapeDtypeStruct((T, D), _BF16),
                   jax.ShapeDtypeStruct((8, T), _F32),
                   jax.ShapeDtypeStruct((T // ROW_TILE, N_EXPERTS, ROUTER_LANES), _F32),
                   jax.ShapeDtypeStruct((N_EXPERTS, ROUTER_LANES), _F32)],
        scratch_shapes=[pltpu.VMEM((N_EXPERTS, ROUTER_LANES), _F32)],
        compiler_params=_cparams("arbitrary"),
        name="out_proj_router",
    )(y_hy_even, y_hy_odd, sel_t, y_na, g_hy.reshape(1, -1), g_na.reshape(1, -1), w_bf[:HY_WIDTH], w_bf[HY_WIDTH:],
      x2, g_ffn.reshape(1, D), rw, rb)


def _chunk_rows(c):
    return pl.ds(pl.multiple_of(c * MOE_CHUNK, MOE_CHUNK), MOE_CHUNK)


_HI_HALF = 0xFFFF0000


def _pack_halves(v):
    c = v.shape[1] // 2
    bits = lax.bitcast_convert_type(v.astype(_BF16).astype(_F32), jnp.uint32)
    return (bits[:, c:] & jnp.uint32(_HI_HALF)) | lax.shift_right_logical(bits[:, :c], jnp.uint32(16))


def _unpack_halves(w):
    lo = lax.bitcast_convert_type(lax.shift_left(w, jnp.uint32(16)), _F32)
    hi = lax.bitcast_convert_type(w & jnp.uint32(_HI_HALF), _F32)
    return lo.astype(_BF16), hi.astype(_BF16)


def _scatter_kernel(zblk_ref, dst_ref, meta_ref, xn_ref, xb_ref, zero_ref, xs_ref, sem, zero_sem):
    i = pl.program_id(0)
    slot = lax.rem(i, 2)

    @pl.when(i == 0)
    def _():
        zero_ref[...] = jnp.zeros_like(zero_ref)

        def blk_copy(j):
            start = pl.multiple_of(zblk_ref[j] * MOE_BLOCK, MOE_BLOCK)
            return pltpu.make_async_copy(zero_ref, xb_ref.at[pl.ds(start, MOE_BLOCK)], zero_sem)

        def start(j, c):
            @pl.when(zblk_ref[j] >= 0)
            def _():
                blk_copy(j).start()
            return c

        def wait(j, c):
            @pl.when(zblk_ref[j] >= 0)
            def _():
                blk_copy(j).wait()
            return c

        lax.fori_loop(0, zblk_ref.shape[0], start, 0)
        lax.fori_loop(0, zblk_ref.shape[0], wait, 0)

    tm = xn_ref.shape[0]
    meta = meta_ref[...]
    pos1, pos2 = meta[_META_POS:_META_POS + 1, :], meta[_META_POS + 1:_META_POS + 2, :]
    srow = lax.broadcasted_iota(jnp.int32, (SORT_ROWS, tm), 0).astype(_F32)
    perm = jnp.where((srow == pos1) | (srow == pos2), 1.0, 0.0).astype(_BF16)
    xs_ref[slot] = _pack_halves(_dot(perm, xn_ref[...]))

    def start(c, carry):
        pltpu.make_async_copy(xs_ref.at[slot, _chunk_rows(c)], xb_ref.at[_chunk_rows(dst_ref[0, 0, c])],
                              sem.at[slot]).start()
        return carry

    lax.fori_loop(0, SORT_CHUNKS, start, 0, unroll=DMA_UNROLL)

    wait_all = lambda s: pltpu.make_async_copy(xs_ref.at[s], xs_ref.at[s], sem.at[s]).wait()

    @pl.when(i > 0)
    def _():
        wait_all(1 - slot)

    @pl.when(i == pl.num_programs(0) - 1)
    def _():
        wait_all(slot)


def _moe_scatter(xn, meta, dst_chunk, zero_blocks, n_slots):
    T, D = xn.shape
    nt = T // ROW_TILE
    dst3 = dst_chunk.reshape(nt, 1, SORT_CHUNKS)
    grid_spec = pltpu.PrefetchScalarGridSpec(
        num_scalar_prefetch=1,
        grid=(nt,),
        in_specs=[pl.BlockSpec((1, 1, SORT_CHUNKS), lambda i, *_: (i, 0, 0), memory_space=pltpu.SMEM),
                  pl.BlockSpec((8, ROW_TILE), lambda i, *_: (0, i)),
                  pl.BlockSpec((ROW_TILE, D), lambda i, *_: (i, 0))],
        out_specs=pl.BlockSpec(memory_space=pl.ANY),
        scratch_shapes=[pltpu.VMEM((MOE_BLOCK, D // 2), jnp.uint32), pltpu.VMEM((2, SORT_ROWS, D // 2), jnp.uint32),
                        pltpu.SemaphoreType.DMA((2,)), pltpu.SemaphoreType.DMA(())],
    )
    return pl.pallas_call(
        _scatter_kernel,
        grid_spec=grid_spec,
        out_shape=jax.ShapeDtypeStruct((n_slots, D // 2), jnp.uint32),
        compiler_params=_cparams("arbitrary"),
        name="moe_scatter",
    )(zero_blocks, dst3, meta, xn)


def _expert_kernel(be_ref, nblk_ref, xb_ref, w1_ref, w3_ref, w2_ref, yb_ref, wb1_ref, wb3_ref, wb2_ref):
    i = pl.program_id(0)

    @pl.when(i < nblk_ref[0])
    def _():
        e = be_ref[i]
        e_prev = be_ref[jnp.maximum(i - 1, 0)]

        @pl.when((i == 0) | (e != e_prev))
        def _():
            wb1_ref[...] = w1_ref[0].astype(_BF16)
            wb3_ref[...] = w3_ref[0].astype(_BF16)
            wb2_ref[...] = w2_ref[0].astype(_BF16)

        x_lo, x_hi = _unpack_halves(xb_ref[...])
        half = x_lo.shape[1]
        up = lambda w_ref: _dot(x_lo, w_ref[:half, :]) + _dot(x_hi, w_ref[half:, :])
        a = up(wb1_ref)
        hid = (a * jax.nn.sigmoid(a) * up(wb3_ref)).astype(_BF16)
        yb_ref[...] = _pack_halves(_dot(hid, wb2_ref[...]))

    @pl.when(i >= nblk_ref[0])
    def _():
        yb_ref[...] = jnp.zeros_like(yb_ref)


def _moe_experts(xb, n_blocks, block_e, n_blk, w1, w3, w2):
    _, D, DE = w1.shape
    live = lambda i, nb: jnp.minimum(i, nb[0] - 1)
    grid_spec = pltpu.PrefetchScalarGridSpec(
        num_scalar_prefetch=2,
        grid=(n_blocks,),
        in_specs=[pl.BlockSpec((MOE_BLOCK, D // 2), lambda i, be, nb: (live(i, nb), 0)),
                  pl.BlockSpec((1, D, DE), lambda i, be, nb: (be[live(i, nb)], 0, 0)),
                  pl.BlockSpec((1, D, DE), lambda i, be, nb: (be[live(i, nb)], 0, 0)),
                  pl.BlockSpec((1, DE, D), lambda i, be, nb: (be[live(i, nb)], 0, 0))],
        out_specs=pl.BlockSpec((MOE_BLOCK, D // 2), lambda i, be, nb: (i, 0)),
        scratch_shapes=[pltpu.VMEM((D, DE), _BF16), pltpu.VMEM((D, DE), _BF16), pltpu.VMEM((DE, D), _BF16)],
    )
    return pl.pallas_call(
        _expert_kernel,
        grid_spec=grid_spec,
        out_shape=jax.ShapeDtypeStruct((n_blocks * MOE_BLOCK, D // 2), jnp.uint32),
        compiler_params=_cparams("arbitrary"),
        name="moe_experts",
    )(block_e, n_blk, xb, w1, w3, w2)


def _final_kernel(dst_ref, dst_next_ref, h1_ref, comb_ref, yb_ref, p_ref, wg_ref, wp_ref, gple_ref, gfin_ref,
                  o_ref, ybuf_ref, sem):
    tm = h1_ref.shape[0]
    i = pl.program_id(0)
    slot = lax.rem(i, 2)

    def gather(d_ref, s):
        def start(c, carry):
            pltpu.make_async_copy(yb_ref.at[_chunk_rows(d_ref[0, 0, c])], ybuf_ref.at[s, _chunk_rows(c)],
                                  sem.at[s]).start()
            return carry
        lax.fori_loop(0, SORT_CHUNKS, start, 0, unroll=DMA_UNROLL)

    @pl.when(i == 0)
    def _():
        gather(dst_ref, 0)

    @pl.when(i + 1 < pl.num_programs(0))
    def _():
        gather(dst_next_ref, 1 - slot)

    pltpu.make_async_copy(ybuf_ref.at[slot], ybuf_ref.at[slot], sem.at[slot]).wait()

    comb = comb_ref[...]
    g1, g2 = comb[:, _META_GATE:_META_GATE + 1], comb[:, _META_GATE + 1:_META_GATE + 2]
    p1, p2 = comb[:, _META_POS:_META_POS + 1], comb[:, _META_POS + 1:_META_POS + 2]
    lane = lax.broadcasted_iota(jnp.int32, (tm, SORT_ROWS), 1).astype(_F32)
    wmat = (jnp.where(lane == p1, g1, 0.0) + jnp.where(lane == p2, g2, 0.0)).astype(_BF16)
    y_lo, y_hi = _unpack_halves(ybuf_ref[slot])
    moe = jnp.concatenate([_dot(wmat, y_lo), _dot(wmat, y_hi)], axis=1)
    h2 = h1_ref[...] + moe
    gate = jax.nn.sigmoid(_dot(_rmsnorm(h2, gple_ref[...]).astype(_BF16), wg_ref[...]))
    h3 = h2 + _dot(p_ref[...].astype(_BF16), wp_ref[...]) * gate
    o_ref[...] = _rmsnorm(h3, gfin_ref[...])


def _final(dst_chunk, h1, comb, yb, p2, w_gate, w_proj, g_ple, g_final):
    T, D = h1.shape
    nt = T // ROW_TILE
    PD = p2.shape[1]
    row = lambda n: pl.BlockSpec((ROW_TILE, n), lambda i: (i, 0))
    vec = lambda n: pl.BlockSpec((1, n), lambda i: (0, 0))
    dest3 = dst_chunk.reshape(nt, 1, SORT_CHUNKS)
    return pl.pallas_call(
        _final_kernel,
        grid=(nt,),
        in_specs=[pl.BlockSpec((1, 1, SORT_CHUNKS), lambda i: (i, 0, 0), memory_space=pltpu.SMEM),
                  pl.BlockSpec((1, 1, SORT_CHUNKS), lambda i: (jnp.minimum(i + 1, nt - 1), 0, 0),
                               memory_space=pltpu.SMEM),
                  row(D), row(ROUTER_LANES),
                  pl.BlockSpec(memory_space=pl.ANY),
                  row(PD),
                  pl.BlockSpec((D, D), lambda i: (0, 0)),
                  pl.BlockSpec((PD, D), lambda i: (0, 0)),
                  vec(D), vec(D)],
        out_specs=row(D),
        out_shape=jax.ShapeDtypeStruct((T, D), _F32),
        scratch_shapes=[pltpu.VMEM((2, SORT_ROWS, D // 2), jnp.uint32), pltpu.SemaphoreType.DMA((2,))],
        compiler_params=_cparams("arbitrary"),
        name="moe_gather_ple_final",
    )(dest3, dest3, h1, comb, yb, p2, w_gate.astype(_BF16), w_proj.astype(_BF16),
      g_ple.reshape(1, D), g_final.reshape(1, D))


def _routing_tables(tile_tbl, counts_f, n_blocks, n_tail):
    start = tile_tbl[:, :, 0].astype(jnp.int32)
    run = tile_tbl[:, :, 1].astype(jnp.int32)
    base = tile_tbl[:, :, 2].astype(jnp.int32)
    counts = counts_f[:, 0].astype(jnp.int32)
    padded = (counts + MOE_BLOCK - 1) // MOE_BLOCK * MOE_BLOCK
    e_iota = jnp.arange(N_EXPERTS, dtype=jnp.int32)
    pad_end = jnp.sum(jnp.where(e_iota[None, :] <= e_iota[:, None], padded[None, :], 0), axis=1)
    pad_start = pad_end - padded
    row0 = (jnp.arange(SORT_CHUNKS, dtype=jnp.int32) * MOE_CHUNK)[None, :, None]
    mine = (start[:, None, :] <= row0) & (row0 < (start + run)[:, None, :])
    used = jnp.any(mine, axis=-1)
    slot0 = (pad_start[None, :] + base - start)[:, None, :] + row0
    dst = jnp.sum(jnp.where(mine, slot0, 0), axis=-1) // MOE_CHUNK
    nt = start.shape[0]
    spare = (n_blocks * MOE_BLOCK // MOE_CHUNK + (jnp.arange(nt, dtype=jnp.int32) % 2)[:, None] * SORT_CHUNKS
             + jnp.arange(SORT_CHUNKS, dtype=jnp.int32)[None, :])
    dst_scatter = jnp.where(used, dst, spare).astype(jnp.int32)
    dst_gather = jnp.where(used, dst, 0).astype(jnp.int32)
    blk_row = jnp.arange(n_blocks, dtype=jnp.int32)[:, None] * MOE_BLOCK
    block_e = jnp.minimum(jnp.sum((pad_end[None, :] <= blk_row).astype(jnp.int32), axis=1), N_EXPERTS - 1)
    n_blk = (pad_end[-1:] // MOE_BLOCK).astype(jnp.int32)
    seg_last = jnp.where(padded > counts, pad_end // MOE_BLOCK - 1, -1)
    tail = n_blk[0] + jnp.arange(n_tail, dtype=jnp.int32)
    spare_blocks = n_blocks + jnp.arange(2 * SORT_ROWS // MOE_BLOCK, dtype=jnp.int32)
    zero_blocks = jnp.concatenate([seg_last, jnp.where(tail < n_blocks, tail, -1), spare_blocks]).astype(jnp.int32)
    return dst_scatter, dst_gather, zero_blocks, block_e, n_blk


def _one_layer(h, p, g_mix, w_in, hy_conv_w, hy_conv_b, hy_f_w1, hy_f_b1, hy_f_freq1, hy_f_w2, hy_f_b2,
               hy_f_freq2, hy_f_w3, hy_skip, na_rpb, g_out_hy, g_out_na, w_out, g_ffn, router_wg, router_bg,
               router_we, router_be, exp_w1, exp_w3, exp_w2, g_ple, w_ple_gate, w_ple_proj):
    B, L, D = h.shape
    T = B * L
    assert T % ROW_TILE == 0 and L % FREQ_CHUNK == 0 and w_in.shape[1] == HY_COLS + 3 * NA_WIDTH
    x2 = h.reshape(T, D)
    u2, qkv = _in_proj(x2, g_mix, w_in)

    hf2 = _filter_mlp(L, hy_f_w1, hy_f_b1, hy_f_freq1, hy_f_w2, hy_f_b2, hy_f_freq2, hy_f_w3)
    tables = _half_dft_tables(L)
    k1r, k1i, k2r, k2i = _filter_spec(tables, hf2)
    y_hy_even, y_hy_odd = _hyena(u2.reshape(B, L // 2, 2 * HY_COLS), hy_conv_w.astype(_F32), hy_conv_b.astype(_F32),
                                 tables, k1r, k1i, k2r, k2i, hy_skip.astype(_F32))
    y_na = _natten(qkv.reshape(B, L, 3 * NA_WIDTH), na_rpb)

    h1, xn, meta, tile_tbl, counts = _out_router(
        y_hy_even.reshape(T // 2, HY_WIDTH), y_hy_odd.reshape(T // 2, HY_WIDTH), y_na.reshape(T, NA_WIDTH),
        g_out_hy, g_out_na, w_out, x2, g_ffn, router_wg, router_bg, router_we, router_be)
    max_rows = T * TOP_K + (T // ROW_TILE) * N_EXPERTS * (MOE_CHUNK - 1) + N_EXPERTS * (MOE_BLOCK - 1)
    n_blocks = -(-max_rows // MOE_BLOCK)
    n_tail = n_blocks - T * TOP_K // MOE_BLOCK
    dst_scatter, dst_gather, zero_blocks, block_e, n_blk = _routing_tables(tile_tbl, counts, n_blocks, n_tail)
    assert 2 * SORT_ROWS % MOE_BLOCK == 0
    xb = _moe_scatter(xn, meta, dst_scatter, zero_blocks, n_blocks * MOE_BLOCK + 2 * SORT_ROWS)
    yb = _moe_experts(xb, n_blocks, block_e, n_blk, exp_w1, exp_w3, exp_w2)
    comb = jnp.pad(meta[:_META_POS + TOP_K].T, ((0, 0), (0, ROUTER_LANES - _META_POS - TOP_K)))
    return h1, dst_gather, comb, yb


def kernel(x, p, g_mix, w_in, hy_conv_w, hy_conv_b, hy_f_w1, hy_f_b1, hy_f_freq1, hy_f_w2, hy_f_b2, hy_f_freq2, hy_f_w3, hy_skip, na_rpb, g_out_hy, g_out_na, w_out, g_ffn, router_wg, router_bg, router_we, router_be, exp_w1, exp_w3, exp_w2, g_ple, w_ple_gate, w_ple_proj, g_final):
    depth = p.shape[0]
    assert depth == 1, "the final RMSNorm is fused into the last layer's kernel; one layer is supported"
    B, L, D = x.shape
    i = 0
    h1, dest, gates, yb = _one_layer(
        x, p[i], g_mix[i], w_in[i], hy_conv_w[i], hy_conv_b[i], hy_f_w1[i], hy_f_b1[i], hy_f_freq1[i],
        hy_f_w2[i], hy_f_b2[i], hy_f_freq2[i], hy_f_w3[i], hy_skip[i], na_rpb[i], g_out_hy[i], g_out_na[i],
        w_out[i], g_ffn[i], router_wg[i], router_bg[i], router_we[i], router_be[i], exp_w1[i], exp_w3[i],
        exp_w2[i], g_ple[i], w_ple_gate[i], w_ple_proj[i])
    out = _final(dest, h1, gates, yb, p[i].reshape(B * L, -1), w_ple_gate[i], w_ple_proj[i], g_ple[i], g_final)
    return out.reshape(B, L, D)
```

```python
import functools
import math

import numpy as np
import jax
import jax.numpy as jnp
from jax import lax
from jax.experimental import pallas as pl
from jax.experimental.pallas import tpu as pltpu

_F32 = jnp.float32
_BF16 = jnp.bfloat16

GRID_W = 64
HY_WIDTH = 512
NA_WIDTH = 512
NA_HEADS = 8
NA_HEAD_DIM = 64
HY_ORDER = 2
SHORT_CONV = 3
FILTER_EMB = 33
FILTER_BANDS = (FILTER_EMB - 1) // 2
DECAY_TARGET = 1e-2
FAST_DECAY_PCT = 0.3
SLOW_DECAY_PCT = 1.5
WIN_ROWS = 8
WIN_COLS = 16
Q_ROWS = 4
N_GROUPS = 4
EXPERTS_PER_GROUP = 8
N_EXPERTS = N_GROUPS * EXPERTS_PER_GROUP
TOP_K = 2
MOE_BLOCK = 512
EPS = 1e-6
NEG_INF = -1e30
HY_COLS = (HY_ORDER + 1) * HY_WIDTH

V7X_LANES = 128
V7X_SUBLANES = 8
V7X_VMEM_LIMIT_BYTES = 56 * 2 ** 20

ROW_TILE = 512
FREQ_CHUNK = 512
HY_CH_TILE = 256
ROUTER_LANES = V7X_LANES
MOE_CHUNK = V7X_SUBLANES
SORT_ROWS = -(-(TOP_K * ROW_TILE + N_EXPERTS * (MOE_CHUNK - 1)) // V7X_LANES) * V7X_LANES
SORT_CHUNKS = SORT_ROWS // MOE_CHUNK
DMA_UNROLL = 8


def _cparams(*sem):
    return pltpu.CompilerParams(dimension_semantics=sem, vmem_limit_bytes=V7X_VMEM_LIMIT_BYTES)


def _resident(shape, index_map):
    return pl.BlockSpec(shape, index_map, pipeline_mode=pl.Buffered(1))


def _rmsnorm(x, g):
    return x * lax.rsqrt(jnp.mean(x * x, axis=-1, keepdims=True) + EPS) * g


def _dot(a, b):
    return jnp.dot(a, b, preferred_element_type=_F32)


def _dot_f32(a, b):
    return jnp.dot(a, b, preferred_element_type=_F32, precision=lax.Precision.HIGHEST)


def _parity_select(n):
    r = jnp.arange(n // 2, dtype=jnp.int32)[None, :, None]
    t = jnp.arange(n, dtype=jnp.int32)[None, None, :]
    p = jnp.arange(2, dtype=jnp.int32)[:, None, None]
    return (t == 2 * r + p).astype(_BF16)


def _in_proj_kernel(x_ref, g_ref, w_ref, sel_ref, u2_ref, qkv_ref):
    xn = _rmsnorm(x_ref[...], g_ref[...]).astype(_BF16)
    n_hy = u2_ref.shape[1] // 2
    xe = _dot(sel_ref[0], xn).astype(_BF16)
    xo = _dot(sel_ref[1], xn).astype(_BF16)
    for c0 in range(0, n_hy, 512):
        w = w_ref[:, c0:c0 + 512]
        u2_ref[:, c0:c0 + 512] = _dot(xe, w).astype(_BF16)
        u2_ref[:, n_hy + c0:n_hy + c0 + 512] = _dot(xo, w).astype(_BF16)
    for c0 in range(0, qkv_ref.shape[1], 512):
        qkv_ref[:, c0:c0 + 512] = _dot(xn, w_ref[:, n_hy + c0:n_hy + c0 + 512]).astype(_BF16)


def _in_proj(x2, g_mix, w_in):
    T, D = x2.shape
    n_in = w_in.shape[1]
    n_qkv = n_in - HY_COLS
    half = ROW_TILE // 2
    return pl.pallas_call(
        _in_proj_kernel,
        grid=(T // ROW_TILE,),
        in_specs=[pl.BlockSpec((ROW_TILE, D), lambda i: (i, 0)),
                  pl.BlockSpec((1, D), lambda i: (0, 0)),
                  _resident((D, n_in), lambda i: (0, 0)),
                  pl.BlockSpec((2, half, ROW_TILE), lambda i: (0, 0, 0))],
        out_specs=[pl.BlockSpec((half, 2 * HY_COLS), lambda i: (i, 0)),
                   pl.BlockSpec((ROW_TILE, n_qkv), lambda i: (i, 0))],
        out_shape=[jax.ShapeDtypeStruct((T // 2, 2 * HY_COLS), _BF16),
                   jax.ShapeDtypeStruct((T, n_qkv), _BF16)],
        compiler_params=_cparams("arbitrary"),
        name="in_proj",
    )(x2, g_mix.reshape(1, D), w_in.astype(_BF16), _parity_select(ROW_TILE))


def _filter_mlp_kernel(z_ref, w1_ref, b1_ref, f1_ref, w2_ref, b2_ref, f2_ref, w3_ref, delta_ref, hf_ref):
    z = z_ref[0]
    hid = jnp.sin(f1_ref[...] * (_dot_f32(z, w1_ref[...]) + b1_ref[...]))
    hid = jnp.sin(f2_ref[...] * (_dot_f32(hid, w2_ref[...]) + b2_ref[...]))
    hf = _dot_f32(hid, w3_ref[...])
    decay = jnp.exp(-z[:, 0:1] * delta_ref[...])
    tl = z.shape[0]
    row = lax.broadcasted_iota(jnp.int32, (tl, HY_WIDTH), 0)
    offset0 = (row == 0) & (pl.program_id(0) == 0) & (pl.program_id(1) == 0)
    for k in range(2 * HY_ORDER):
        blk = hf[:, k * HY_WIDTH:(k + 1) * HY_WIDTH] * decay
        if k >= HY_ORDER:
            blk = jnp.where(offset0, 0.0, blk)
        hf_ref[:, k * HY_WIDTH:(k + 1) * HY_WIDTH] = blk


def _filter_mlp(L, w1, b1, f1, w2, b2, f2, w3):
    t = jnp.linspace(0.0, 1.0, L, dtype=_F32)[:, None]
    w = 2.0 * math.pi * jnp.arange(L, dtype=_F32)[:, None] / L
    bands = jnp.linspace(1e-4, FILTER_BANDS - 1, FILTER_BANDS, dtype=_F32)[None, :]
    z = jnp.concatenate([t, jnp.cos(bands * w), -jnp.sin(bands * w)], axis=-1)
    z = jnp.pad(z, ((0, 0), (0, V7X_LANES - FILTER_EMB)))
    z = jnp.stack([z[0::2], z[1::2]])
    w1p = jnp.pad(w1.astype(_F32), ((0, V7X_LANES - FILTER_EMB), (0, 0)))
    max_decay = math.log(DECAY_TARGET) / FAST_DECAY_PCT
    min_decay = math.log(DECAY_TARGET) / SLOW_DECAY_PCT
    deltas = jnp.abs(jnp.linspace(min_decay, max_decay, HY_WIDTH, dtype=_F32))[None, :]
    hid = w1.shape[1]
    n_out = w3.shape[1]
    lh = L // 2
    tl = min(lh, ROW_TILE)
    full = lambda shape: pl.BlockSpec(shape, lambda p, i: (0, 0))
    return pl.pallas_call(
        _filter_mlp_kernel,
        grid=(2, lh // tl),
        in_specs=[pl.BlockSpec((1, tl, V7X_LANES), lambda p, i: (p, i, 0)),
                  full((V7X_LANES, hid)), full((1, hid)), full((1, hid)),
                  full((hid, hid)), full((1, hid)), full((1, hid)),
                  full((hid, n_out)), full((1, HY_WIDTH))],
        out_specs=pl.BlockSpec((tl, n_out), lambda p, i: (i, p)),
        out_shape=jax.ShapeDtypeStruct((lh, 2 * n_out), _F32),
        compiler_params=_cparams("arbitrary", "arbitrary"),
        name="hyena_filter_mlp",
    )(z, w1p, b1.reshape(1, hid), f1.reshape(1, hid), w2.astype(_F32), b2.reshape(1, hid),
      f2.reshape(1, hid), w3.astype(_F32), deltas)


def _freq_chunks(gp):
    return [(r0, min(FREQ_CHUNK, gp - r0)) for r0 in range(0, gp, FREQ_CHUNK)]


def _half_dft_tables(L):
    lh, n, sb = L // 2, 2 * L, 32
    gp = -(-(lh + 1) // V7X_SUBLANES) * V7X_SUBLANES
    assert lh % sb == 0
    g = jnp.arange(gp, dtype=jnp.int32)
    live = (g <= lh)[:, None]
    ang = lambda s: ((g[:, None] * s[None, :]) % L).astype(_F32) * (2.0 * math.pi / L)
    a_hi = ang(jnp.arange(lh // sb, dtype=jnp.int32) * sb)
    a_lo = ang(jnp.arange(sb, dtype=jnp.int32))
    c_hi, s_hi, c_lo, s_lo = jnp.cos(a_hi), jnp.sin(a_hi), jnp.cos(a_lo), jnp.sin(a_lo)
    cosm = (c_hi[:, :, None] * c_lo[:, None, :] - s_hi[:, :, None] * s_lo[:, None, :]).reshape(gp, lh)
    sinm = (s_hi[:, :, None] * c_lo[:, None, :] + c_hi[:, :, None] * s_lo[:, None, :]).reshape(gp, lh)
    cosm = jnp.where(live, cosm, 0.0)
    sinm = jnp.where(live, sinm, 0.0)
    ff = jnp.concatenate([cosm, -sinm], axis=0).astype(_BF16)
    wgt = jnp.where((g == 0) | (g == lh), 1.0, 2.0)[:, None] / n
    gf = jnp.concatenate([(cosm * wgt).T, (-sinm * wgt).T], axis=1).astype(_BF16)
    tw = g.astype(_F32)[:, None] * (2.0 * math.pi / n)
    wr = jnp.broadcast_to(jnp.cos(tw), (gp, HY_CH_TILE))
    wi = jnp.broadcast_to(-jnp.sin(tw), (gp, HY_CH_TILE))
    return ff, gf, wr, wi


def _half_spectra(ff_ref, wr, wi, xe, xo, r0, n):
    gp = ff_ref.shape[0] // 2
    fc = ff_ref[r0:r0 + n, :]
    fs = ff_ref[gp + r0:gp + r0 + n, :]
    er, ei = _dot(fc, xe), _dot(fs, xe)
    orr, oi = _dot(fc, xo), _dot(fs, xo)
    pr = wr * orr - wi * oi
    pi = wr * oi + wi * orr
    return (er + pr, ei + pi), (er - pr, pi - ei)


def _filter_spec_kernel(ff_ref, wr_ref, wi_ref, fe_ref, fo_ref, be_ref, bo_ref, k1r_ref, k1i_ref, k2r_ref, k2i_ref):
    fe, fo, be, bo = (r[...].astype(_BF16) for r in (fe_ref, fo_ref, be_ref, bo_ref))
    for r0, n in _freq_chunks(wr_ref.shape[0]):
        wr, wi = wr_ref[r0:r0 + n, :], wi_ref[r0:r0 + n, :]
        (f1r, f1i), (f2r, f2i) = _half_spectra(ff_ref, wr, wi, fe, fo, r0, n)
        (b1r, b1i), (b2r, b2i) = _half_spectra(ff_ref, wr, wi, be, bo, r0, n)
        k1r_ref[0, r0:r0 + n, :] = f1r + b1r
        k1i_ref[0, r0:r0 + n, :] = f1i - b1i
        k2r_ref[0, r0:r0 + n, :] = f2r + b2r
        k2i_ref[0, r0:r0 + n, :] = f2i - b2i


def _filter_spec(tables, hf2):
    ff, _, wr, wi = tables
    lh, gp = hf2.shape[0], wr.shape[0]
    n_ct = HY_WIDTH // HY_CH_TILE
    n_blk = hf2.shape[1] // 2 // HY_CH_TILE
    col = lambda blk0: pl.BlockSpec((lh, HY_CH_TILE), lambda j: (0, blk0 + j))
    out = jax.ShapeDtypeStruct((HY_ORDER, gp, HY_WIDTH), _F32)
    ospec = pl.BlockSpec((1, gp, HY_CH_TILE), lambda j: (j // n_ct, 0, j % n_ct))
    full = lambda a: pl.BlockSpec(a.shape, lambda j: (0, 0))
    return pl.pallas_call(
        _filter_spec_kernel,
        grid=(HY_ORDER * n_ct,),
        in_specs=[full(ff), full(wr), full(wi),
                  col(0), col(n_blk),
                  col(HY_ORDER * n_ct), col(n_blk + HY_ORDER * n_ct)],
        out_specs=[ospec] * 4,
        out_shape=[out] * 4,
        compiler_params=_cparams("arbitrary"),
        name="hyena_filter_spectrum",
    )(ff, wr, wi, hf2, hf2, hf2, hf2)


def _parity_conv(ze, zo, w_ref, b_ref):
    lh = ze.shape[0]
    row = lax.broadcasted_iota(jnp.int32, ze.shape, 0)
    zo_prev = jnp.where(row == 0, 0.0, pltpu.roll(zo, 1, axis=0))
    ze_next = jnp.where(row == lh - 1, 0.0, pltpu.roll(ze, lh - 1, axis=0))
    w0, w1, w2, b = w_ref[0:1, :], w_ref[1:2, :], w_ref[2:3, :], b_ref[...]
    return ((b + zo_prev * w0) + ze * w1) + zo * w2, ((b + ze * w0) + zo * w1) + ze_next * w2


def _hyena_kernel(ze_ref, zo_ref, zw_ref, zb_ref, ge_ref, go_ref, gw_ref, gb_ref, skip_ref, ff_ref, gf_ref,
                  wr_ref, wi_ref, k1r_ref, k1i_ref, k2r_ref, k2i_ref, oe_ref, oo_ref, a_ref, b_ref, *, conv_input):
    ze, zo = ze_ref[0].astype(_F32), zo_ref[0].astype(_F32)
    if conv_input:
        ze, zo = _parity_conv(ze, zo, zw_ref, zb_ref)
    xe, xo = ze.astype(_BF16), zo.astype(_BF16)
    gp = wr_ref.shape[0]
    for r0, n in _freq_chunks(gp):
        rows = slice(r0, r0 + n)
        wr, wi = wr_ref[rows, :], wi_ref[rows, :]
        (x1r, x1i), (x2r, x2i) = _half_spectra(ff_ref, wr, wi, xe, xo, r0, n)
        k1r, k1i, k2r, k2i = k1r_ref[0, rows, :], k1i_ref[0, rows, :], k2r_ref[0, rows, :], k2i_ref[0, rows, :]
        y1r, y1i = x1r * k1r - x1i * k1i, x1r * k1i + x1i * k1r
        y2r, y2i = x2r * k2r - x2i * k2i, x2r * k2i + x2i * k2r
        tr, ti = y1r - y2r, y1i + y2i
        a_ref[rows, :] = (y1r + y2r).astype(_BF16)
        a_ref[gp + r0:gp + r0 + n, :] = (y1i - y2i).astype(_BF16)
        b_ref[rows, :] = (tr * wr + ti * wi).astype(_BF16)
        b_ref[gp + r0:gp + r0 + n, :] = (ti * wr - tr * wi).astype(_BF16)
    ge, go = _parity_conv(ge_ref[0].astype(_F32), go_ref[0].astype(_F32), gw_ref, gb_ref)
    skip = skip_ref[0]
    for r0 in range(0, ze.shape[0], FREQ_CHUNK):
        rows = slice(r0, r0 + FREQ_CHUNK)
        g = gf_ref[rows, :]
        oe_ref[0, rows, :] = ge[rows] * (_dot(g, a_ref[...]) + ze[rows] * skip)
        oo_ref[0, rows, :] = go[rows] * (_dot(g, b_ref[...]) + zo[rows] * skip)


def _hyena(u2, conv_w, conv_b, tables, k1r, k1i, k2r, k2i, skip):
    ff, gf, wr, wi = tables
    B, lh, _ = u2.shape
    gp = wr.shape[0]
    assert lh % FREQ_CHUNK == 0
    n_ct = HY_WIDTH // HY_CH_TILE
    odd = HY_COLS // HY_CH_TILE
    conv_b2 = conv_b.reshape(1, HY_COLS)
    skip3 = skip.reshape(HY_ORDER, 1, HY_WIDTH)
    col = lambda blk0: pl.BlockSpec((1, lh, HY_CH_TILE), lambda c, b: (b, 0, blk0 + c))
    cw = lambda blk0: pl.BlockSpec((SHORT_CONV, HY_CH_TILE), lambda c, b: (0, blk0 + c))
    cb = lambda blk0: pl.BlockSpec((1, HY_CH_TILE), lambda c, b: (0, blk0 + c))
    const = lambda a: _resident(a.shape, lambda c, b: (0, 0))
    out = jax.ShapeDtypeStruct((B, lh, HY_WIDTH), _F32)
    ze_arr, zo_arr, ze_spec, zo_spec = u2, u2, col(2 * n_ct), col(odd + 2 * n_ct)
    for o in range(HY_ORDER):
        coef = lambda: pl.BlockSpec((1, gp, HY_CH_TILE), lambda c, b, o=o: (o, 0, c), pipeline_mode=pl.Buffered(1))
        ze_arr, zo_arr = pl.pallas_call(
            functools.partial(_hyena_kernel, conv_input=(o == 0)),
            grid=(n_ct, B),
            in_specs=[ze_spec, zo_spec, cw(2 * n_ct), cb(2 * n_ct),
                      col(o * n_ct), col(odd + o * n_ct), cw(o * n_ct), cb(o * n_ct),
                      pl.BlockSpec((1, 1, HY_CH_TILE), lambda c, b, o=o: (o, 0, c)),
                      const(ff), const(gf), const(wr), const(wi), coef(), coef(), coef(), coef()],
            out_specs=[col(0), col(0)],
            out_shape=[out, out],
            scratch_shapes=[pltpu.VMEM((2 * gp, HY_CH_TILE), _BF16), pltpu.VMEM((2 * gp, HY_CH_TILE), _BF16)],
            compiler_params=_cparams("arbitrary", "arbitrary"),
            name=f"hyena_order_{o}",
        )(ze_arr, zo_arr, conv_w, conv_b2, u2, u2, conv_w, conv_b2, skip3, ff, gf, wr, wi, k1r, k1i, k2r, k2i)
        ze_spec = zo_spec = col(0)
    return ze_arr, zo_arr


def _natten_tables(rows):
    kr = min(WIN_ROWS, rows)
    krb = min(-(-(Q_ROWS + kr - 1) // 2) * 2, rows)
    rs = np.clip(np.arange(rows) - kr // 2, 0, rows - kr)
    cs = np.clip(np.arange(GRID_W) - WIN_COLS // 2, 0, GRID_W - WIN_COLS)
    qc = np.tile(np.arange(GRID_W), Q_ROWS)[:, None]
    kc = np.tile(np.arange(GRID_W), krb)[None, :]
    cases, case_of, kstart = [], [], []
    for p in range(rows // Q_ROWS):
        k_r0 = min(rs[p * Q_ROWS], rows - krb)
        qr = (p * Q_ROWS + np.repeat(np.arange(Q_ROWS), GRID_W))[:, None]
        kr_ = (k_r0 + np.repeat(np.arange(krb), GRID_W))[None, :]
        valid = ((kr_ >= rs[qr]) & (kr_ < rs[qr] + kr) & (kc >= cs[qc]) & (kc < cs[qc] + WIN_COLS))
        dr = np.clip(kr_ - qr + WIN_ROWS - 1, 0, 2 * WIN_ROWS - 2)
        dc = np.clip(kc - qc + WIN_COLS - 1, 0, 2 * WIN_COLS - 2)
        idx = np.where(valid, dr * (2 * WIN_COLS - 1) + dc, -1).astype(np.int32)
        for ci, c in enumerate(cases):
            if np.array_equal(c, idx):
                break
        else:
            ci = len(cases)
            cases.append(idx)
        case_of.append(ci)
        kstart.append(k_r0 * GRID_W)
    return np.stack(cases), np.asarray(case_of, np.int32), np.asarray(kstart, np.int32), krb * GRID_W


def _natten_bias_kernel(dr_ref, rv_ref, u_ref, o_ref, v_ref, *, krb):
    c, h = pl.program_id(0), pl.program_id(1)
    n_m, n_k = v_ref.shape[1], v_ref.shape[2]
    w_shift, w_mask = GRID_W.bit_length() - 1, GRID_W - 1
    col = lax.broadcasted_iota(jnp.int32, (1, n_k), 1)
    kr_col, kc_col = col >> w_shift, col & w_mask
    per_kr = lambda tab, qr: sum(jnp.where(kr_col == kr, tab[(c * Q_ROWS + qr) * krb + kr], 0) for kr in range(krb))

    @pl.when(h == 0)
    def _():
        row = lax.broadcasted_iota(jnp.int32, (n_m, n_k), 0)
        for qr in range(Q_ROWS):
            hit = ((row >> w_shift) == per_kr(dr_ref, qr)) & ((row & w_mask) == kc_col)
            v_ref[qr] = jnp.where(hit, 1.0, 0.0).astype(_BF16)

    uh, ul = _split_bf16(u_ref[0])
    qc = lax.broadcasted_iota(jnp.int32, (GRID_W, n_k), 0)
    cs = jnp.clip(qc - WIN_COLS // 2, 0, GRID_W - WIN_COLS)
    col_ok = (kc_col >= cs) & (kc_col < cs + WIN_COLS)
    for qr in range(Q_ROWS):
        b = _dot(uh, v_ref[qr]) + _dot(ul, v_ref[qr])
        ok = col_ok & (per_kr(rv_ref, qr) > 0)
        o_ref[0, 0, qr * GRID_W:(qr + 1) * GRID_W, :] = jnp.where(ok, b, NEG_INF)


def _natten_bias(rpb, cases):
    n_case, nq, nk = cases.shape
    krb = nk // GRID_W
    n_dr, n_dc = 2 * WIN_ROWS - 1, 2 * WIN_COLS - 1
    c5 = cases.reshape(n_case, Q_ROWS, GRID_W, krb, GRID_W)
    dr_blk = np.where(c5 >= 0, c5 // n_dc, -1).max(axis=(2, 4))
    row_ok = dr_blk >= 0
    cs = np.clip(np.arange(GRID_W) - WIN_COLS // 2, 0, GRID_W - WIN_COLS)[:, None]
    col_ok = (np.arange(GRID_W)[None, :] >= cs) & (np.arange(GRID_W)[None, :] < cs + WIN_COLS)
    dc_idx = np.arange(GRID_W)[None, :] - np.arange(GRID_W)[:, None] + WIN_COLS - 1
    assert np.array_equal(c5 >= 0, row_ok[:, :, None, :, None] & col_ok[None, None, :, None, :])
    assert np.all((c5 < 0) | (c5 == dr_blk[:, :, None, :, None] * n_dc + dc_idx[None, None, :, None, :]))
    onehot = (dc_idx[:, None, :] == np.arange(n_dc)[None, :, None]).astype(np.float32)
    u = jnp.einsum('hdj,qjk->hqdk', rpb.astype(_F32), onehot, precision=lax.Precision.HIGHEST)
    u = u.reshape(NA_HEADS, GRID_W, n_dr * GRID_W)
    grid_spec = pltpu.PrefetchScalarGridSpec(
        num_scalar_prefetch=2,
        grid=(n_case, NA_HEADS),
        in_specs=[pl.BlockSpec((1, GRID_W, n_dr * GRID_W), lambda c, h, *_: (h, 0, 0))],
        out_specs=pl.BlockSpec((1, 1, nq, nk), lambda c, h, *_: (c, h, 0, 0)),
        scratch_shapes=[pltpu.VMEM((Q_ROWS, n_dr * GRID_W, nk), _BF16)],
    )
    return pl.pallas_call(
        functools.partial(_natten_bias_kernel, krb=krb),
        grid_spec=grid_spec,
        out_shape=jax.ShapeDtypeStruct((n_case, NA_HEADS, nq, nk), _F32),
        compiler_params=_cparams("arbitrary", "arbitrary"),
        name="natten_bias",
    )(jnp.asarray(np.maximum(dr_blk, 0).reshape(-1), jnp.int32), jnp.asarray(row_ok.reshape(-1), jnp.int32), u)


def _natten_kernel(case_ref, kstart_ref, qkv_ref, bias_ref, o_ref, *, n_pairs, n_keys):
    nq = Q_ROWS * GRID_W
    pair_w = 2 * NA_HEAD_DIM
    lane = lax.broadcasted_iota(jnp.int32, (nq, pair_w), 1)
    lo_half = lane < NA_HEAD_DIM
    scale = NA_HEAD_DIM ** -0.5

    def body(p, carry):
        q0 = pl.multiple_of(p * nq, nq)
        k0 = pl.multiple_of(kstart_ref[p], GRID_W)
        case = case_ref[p]
        for hp in range(NA_HEADS // 2):
            c0 = hp * pair_w
            q2 = qkv_ref[0, pl.ds(q0, nq), c0:c0 + pair_w]
            k2 = qkv_ref[0, pl.ds(k0, n_keys), NA_WIDTH + c0:NA_WIDTH + c0 + pair_w]
            v2 = qkv_ref[0, pl.ds(k0, n_keys), 2 * NA_WIDTH + c0:2 * NA_WIDTH + c0 + pair_w]
            zero = jnp.zeros_like(q2)
            q2 = q2 * scale
            qq = jnp.concatenate([jnp.where(lo_half, q2, zero), jnp.where(lo_half, zero, q2)], axis=0)
            s = lax.dot_general(qq, k2, (((1,), (1,)), ((), ())), preferred_element_type=_F32)
            s = s + bias_ref[case, hp]
            e = jnp.exp(s - jnp.max(s, axis=-1, keepdims=True))
            l = jnp.sum(e, axis=-1, keepdims=True)
            o = _dot(e.astype(_BF16), v2) / l
            o_ref[0, pl.ds(q0, nq), c0:c0 + pair_w] = jnp.where(lo_half, o[:nq], o[nq:])
        return carry

    lax.fori_loop(0, n_pairs, body, 0, unroll=2)


def _natten(qkv, rpb):
    B, L, _ = qkv.shape
    rows = L // GRID_W
    assert rows % (2 * Q_ROWS) == 0 and rows >= Q_ROWS + WIN_ROWS
    cases, case_of, kstart, n_keys = _natten_tables(rows)
    n_case = cases.shape[0]
    nq = Q_ROWS * GRID_W
    bias = _natten_bias(rpb, cases).reshape(n_case, NA_HEADS // 2, 2 * nq, n_keys)
    grid_spec = pltpu.PrefetchScalarGridSpec(
        num_scalar_prefetch=2,
        grid=(B,),
        in_specs=[pl.BlockSpec((1, L, 3 * NA_WIDTH), lambda b, *_: (b, 0, 0)),
                  _resident((n_case, NA_HEADS // 2, 2 * nq, n_keys), lambda b, *_: (0, 0, 0, 0))],
        out_specs=pl.BlockSpec((1, L, NA_WIDTH), lambda b, *_: (b, 0, 0)),
    )
    return pl.pallas_call(
        functools.partial(_natten_kernel, n_pairs=rows // Q_ROWS, n_keys=n_keys),
        grid_spec=grid_spec,
        out_shape=jax.ShapeDtypeStruct((B, L, NA_WIDTH), _F32),
        compiler_params=_cparams("arbitrary"),
        name="natten",
    )(jnp.asarray(case_of), jnp.asarray(kstart), qkv, bias)


_META_ID, _META_GATE, _META_POS = 0, 2, 4
ROUTER_ROWS = 8 + N_EXPERTS


def _split_bf16(v):
    hi = v.astype(_BF16)
    return hi, (v - hi.astype(_F32)).astype(_BF16)


def _out_router_kernel(yhe_ref, yho_ref, sel_ref, yna_ref, ghy_ref, gna_ref, wtop_ref, wbot_ref, x_ref, gffn_ref,
                       rw_ref, rb_ref, h1_ref, xs_ref, meta_ref, tile_ref, cnt_ref, carry_ref):
    @pl.when(pl.program_id(0) == 0)
    def _():
        carry_ref[...] = jnp.zeros_like(carry_ref)

    nhe = _rmsnorm(yhe_ref[...], ghy_ref[...]).astype(_BF16)
    nho = _rmsnorm(yho_ref[...], ghy_ref[...]).astype(_BF16)
    nh = (_dot(sel_ref[0], nhe) + _dot(sel_ref[1], nho)).astype(_BF16)
    nn = _rmsnorm(yna_ref[...], gna_ref[...]).astype(_BF16)
    h1 = x_ref[...] + (_dot(nh, wtop_ref[...]) + _dot(nn, wbot_ref[...]))
    h1_ref[...] = h1
    xn = _rmsnorm(h1, gffn_ref[...])
    tm = xn.shape[0]

    xn_bf = xn.astype(_BF16)
    logits = _dot(xn_bf, rw_ref[...]) + rb_ref[...]
    lt = logits.T[:ROUTER_ROWS]

    neg = -jnp.inf
    row8 = lax.broadcasted_iota(jnp.int32, (EXPERTS_PER_GROUP, tm), 0).astype(_F32)
    col_max = lambda v: jnp.max(v, axis=0, keepdims=True)
    first_max = lambda v, m: jnp.min(jnp.where(v == m, row8, float(EXPERTS_PER_GROUP)), axis=0, keepdims=True)

    gl = jnp.where(row8 < N_GROUPS, lt[0:8], neg)
    gmax = col_max(gl)
    g_w = 1.0 / jnp.sum(jnp.exp(gl - gmax), axis=0, keepdims=True)
    g_idx = first_max(gl, gmax)
    el = lt[8:8 + EXPERTS_PER_GROUP]
    for g in range(1, N_GROUPS):
        el = jnp.where(g_idx == g, lt[8 + g * EXPERTS_PER_GROUP:8 + (g + 1) * EXPERTS_PER_GROUP], el)
    m1 = col_max(el)
    i1 = first_max(el, m1)
    el2 = jnp.where(row8 == i1, neg, el)
    m2 = col_max(el2)
    i2 = first_max(el2, m2)
    r = jnp.exp(m2 - m1)
    gate1 = g_w / (1.0 + r)
    gate2 = g_w * r / (1.0 + r)
    id1 = g_idx * EXPERTS_PER_GROUP + i1
    id2 = g_idx * EXPERTS_PER_GROUP + i2

    row_e = lax.broadcasted_iota(jnp.int32, (N_EXPERTS, tm), 0).astype(_F32)
    sel1 = row_e == id1
    sel2 = row_e == id2
    onehot = jnp.where(sel1 | sel2, 1.0, 0.0)
    tri_r = lax.broadcasted_iota(jnp.int32, (tm, tm), 0)
    tri_c = lax.broadcasted_iota(jnp.int32, (tm, tm), 1)
    tri = jnp.where(tri_r < tri_c, 1.0, 0.0).astype(_BF16)
    before = _dot(onehot.astype(_BF16), tri)
    cnt = jnp.sum(onehot, axis=1, keepdims=True)
    run = jnp.floor((cnt + (MOE_CHUNK - 1)) * (1.0 / MOE_CHUNK)) * MOE_CHUNK
    run_b = jnp.broadcast_to(run, (N_EXPERTS, ROUTER_LANES))
    e_r = lax.broadcasted_iota(jnp.int32, (N_EXPERTS, N_EXPERTS), 0)
    e_c = lax.broadcasted_iota(jnp.int32, (N_EXPERTS, N_EXPERTS), 1)
    start_b = _dot_f32(jnp.where(e_c < e_r, 1.0, 0.0), run_b)
    start = start_b[:, 0:1]
    pos1 = jnp.sum(jnp.where(sel1, start + before, 0.0), axis=0, keepdims=True)
    pos2 = jnp.sum(jnp.where(sel2, start + before, 0.0), axis=0, keepdims=True)
    xs_ref[...] = xn_bf

    base_b = carry_ref[...]
    lane = lax.broadcasted_iota(jnp.int32, (N_EXPERTS, ROUTER_LANES), 1)
    tile_ref[0] = jnp.where(lane == 0, start_b, jnp.where(lane == 1, run_b, jnp.where(lane == 2, base_b, 0.0)))
    carry_ref[...] = base_b + run_b
    cnt_ref[...] = base_b + run_b

    meta = jnp.zeros((8, tm), _F32)
    for k, v in enumerate((id1, id2, gate1, gate2, pos1, pos2)):
        meta = jnp.where(row8 == k, v, meta)
    meta_ref[...] = meta


def _out_router(y_hy_even, y_hy_odd, y_na, g_hy, g_na, w_out, x2, g_ffn, wg, bg, we, be):
    T, D = x2.shape
    half = ROW_TILE // 2
    sel_t = _parity_select(ROW_TILE).transpose(0, 2, 1)
    gpad = 8 - N_GROUPS
    rw = jnp.concatenate([wg.astype(_F32), jnp.zeros((D, gpad), _F32), we.astype(_F32)], axis=1)
    rw = jnp.pad(rw, ((0, 0), (0, ROUTER_LANES - ROUTER_ROWS))).astype(_BF16)
    rb = jnp.concatenate([bg.astype(_F32), jnp.zeros((gpad,), _F32), be.astype(_F32)])
    rb = jnp.pad(rb, (0, ROUTER_LANES - ROUTER_ROWS)).reshape(1, ROUTER_LANES)
    w_bf = w_out.astype(_BF16)
    row = lambda n: pl.BlockSpec((ROW_TILE, n), lambda i: (i, 0))
    vec = lambda n: pl.BlockSpec((1, n), lambda i: (0, 0))
    mat = lambda r, c: pl.BlockSpec((r, c), lambda i: (0, 0))
    return pl.pallas_call(
        _out_router_kernel,
        grid=(T // ROW_TILE,),
        in_specs=[pl.BlockSpec((half, HY_WIDTH), lambda i: (i, 0)), pl.BlockSpec((half, HY_WIDTH), lambda i: (i, 0)),
                  pl.BlockSpec((2, ROW_TILE, half), lambda i: (0, 0, 0)),
                  row(NA_WIDTH), vec(HY_WIDTH), vec(NA_WIDTH),
                  mat(HY_WIDTH, D), mat(NA_WIDTH, D), row(D), vec(D),
                  mat(D, ROUTER_LANES), vec(ROUTER_LANES)],
        out_specs=[row(D),
                   row(D),
                   pl.BlockSpec((8, ROW_TILE), lambda i: (0, i)),
                   pl.BlockSpec((1, N_EXPERTS, ROUTER_LANES), lambda i: (i, 0, 0)),
                   mat(N_EXPERTS, ROUTER_LANES)],
        out_shape=[jax.ShapeDtypeStruct((T, D), _F32),
                   jax.ShapeDtypeStruct((T, D), _BF16),
                   jax.ShapeDtypeStruct((8, T), _F32),
                   jax.ShapeDtypeStruct((T // ROW_TILE, N_EXPERTS, ROUTER_LANES), _F32),
                   jax.ShapeDtypeStruct((N_EXPERTS, ROUTER_LANES), _F32)],
        scratch_shapes=[pltpu.VMEM((N_EXPERTS, ROUTER_LANES), _F32)],
        compiler_params=_cparams("arbitrary"),
        name="out_proj_router",
    )(y_hy_even, y_hy_odd, sel_t, y_na, g_hy.reshape(1, -1), g_na.reshape(1, -1), w_bf[:HY_WIDTH], w_bf[HY_WIDTH:],
      x2, g_ffn.reshape(1, D), rw, rb)


def _chunk_rows(c):
    return pl.ds(pl.multiple_of(c * MOE_CHUNK, MOE_CHUNK), MOE_CHUNK)


_HI_HALF = 0xFFFF0000


def _pack_halves(v):
    c = v.shape[1] // 2
    bits = lax.bitcast_convert_type(v.astype(_BF16).astype(_F32), jnp.uint32)
    return (bits[:, c:] & jnp.uint32(_HI_HALF)) | lax.shift_right_logical(bits[:, :c], jnp.uint32(16))


def _unpack_halves(w):
    lo = lax.bitcast_convert_type(lax.shift_left(w, jnp.uint32(16)), _F32)
    hi = lax.bitcast_convert_type(w & jnp.uint32(_HI_HALF), _F32)
    return lo.astype(_BF16), hi.astype(_BF16)


def _scatter_kernel(zblk_ref, dst_ref, meta_ref, xn_ref, xb_ref, zero_ref, xs_ref, sem, zero_sem):
    i = pl.program_id(0)
    slot = lax.rem(i, 2)

    @pl.when(i == 0)
    def _():
        zero_ref[...] = jnp.zeros_like(zero_ref)

        def blk_copy(j):
            start = pl.multiple_of(zblk_ref[j] * MOE_BLOCK, MOE_BLOCK)
            return pltpu.make_async_copy(zero_ref, xb_ref.at[pl.ds(start, MOE_BLOCK)], zero_sem)

        def start(j, c):
            @pl.when(zblk_ref[j] >= 0)
            def _():
                blk_copy(j).start()
            return c

        def wait(j, c):
            @pl.when(zblk_ref[j] >= 0)
            def _():
                blk_copy(j).wait()
            return c

        lax.fori_loop(0, zblk_ref.shape[0], start, 0)
        lax.fori_loop(0, zblk_ref.shape[0], wait, 0)

    tm = xn_ref.shape[0]
    meta = meta_ref[...]
    pos1, pos2 = meta[_META_POS:_META_POS + 1, :], meta[_META_POS + 1:_META_POS + 2, :]
    srow = lax.broadcasted_iota(jnp.int32, (SORT_ROWS, tm), 0).astype(_F32)
    perm = jnp.where((srow == pos1) | (srow == pos2), 1.0, 0.0).astype(_BF16)
    xs_ref[slot] = _pack_halves(_dot(perm, xn_ref[...]))

    def start(c, carry):
        pltpu.make_async_copy(xs_ref.at[slot, _chunk_rows(c)], xb_ref.at[_chunk_rows(dst_ref[0, 0, c])],
                              sem.at[slot]).start()
        return carry

    lax.fori_loop(0, SORT_CHUNKS, start, 0, unroll=DMA_UNROLL)

    wait_all = lambda s: pltpu.make_async_copy(xs_ref.at[s], xs_ref.at[s], sem.at[s]).wait()

    @pl.when(i > 0)
    def _():
        wait_all(1 - slot)

    @pl.when(i == pl.num_programs(0) - 1)
    def _():
        wait_all(slot)


def _moe_scatter(xn, meta, dst_chunk, zero_blocks, n_slots):
    T, D = xn.shape
    nt = T // ROW_TILE
    dst3 = dst_chunk.reshape(nt, 1, SORT_CHUNKS)
    grid_spec = pltpu.PrefetchScalarGridSpec(
        num_scalar_prefetch=1,
        grid=(nt,),
        in_specs=[pl.BlockSpec((1, 1, SORT_CHUNKS), lambda i, *_: (i, 0, 0), memory_space=pltpu.SMEM),
                  pl.BlockSpec((8, ROW_TILE), lambda i, *_: (0, i)),
                  pl.BlockSpec((ROW_TILE, D), lambda i, *_: (i, 0))],
        out_specs=pl.BlockSpec(memory_space=pl.ANY),
        scratch_shapes=[pltpu.VMEM((MOE_BLOCK, D // 2), jnp.uint32), pltpu.VMEM((2, SORT_ROWS, D // 2), jnp.uint32),
                        pltpu.SemaphoreType.DMA((2,)), pltpu.SemaphoreType.DMA(())],
    )
    return pl.pallas_call(
        _scatter_kernel,
        grid_spec=grid_spec,
        out_shape=jax.ShapeDtypeStruct((n_slots, D // 2), jnp.uint32),
        compiler_params=_cparams("arbitrary"),
        name="moe_scatter",
    )(zero_blocks, dst3, meta, xn)


def _expert_kernel(be_ref, nblk_ref, first_ref, slot_ref, next_ref, xb_ref, w1_ref, w3_ref, w2_ref, yb_ref,
                   wf1_ref, wf3_ref, wf2_ref, wb1_ref, wb3_ref, wb2_ref, sem):
    i = pl.program_id(0)

    def weight_copies(e, s):
        return [pltpu.make_async_copy(src.at[e], dst.at[s], sem.at[s])
                for src, dst in ((w1_ref, wf1_ref), (w3_ref, wf3_ref), (w2_ref, wf2_ref))]

    @pl.when(i == 0)
    def _():
        for c in weight_copies(be_ref[0], 0):
            c.start()

    @pl.when(i < nblk_ref[0])
    def _():
        @pl.when(first_ref[i] == 1)
        def _():
            s = slot_ref[i]
            for c in weight_copies(be_ref[i], s):
                c.wait()
            wb1_ref[...] = wf1_ref[s].astype(_BF16)
            wb3_ref[...] = wf3_ref[s].astype(_BF16)
            wb2_ref[...] = wf2_ref[s].astype(_BF16)

            @pl.when(next_ref[i] >= 0)
            def _():
                for c in weight_copies(next_ref[i], 1 - s):
                    c.start()

        x_lo, x_hi = _unpack_halves(xb_ref[...])
        half = x_lo.shape[1]
        up = lambda w_ref: _dot(x_lo, w_ref[:half, :]) + _dot(x_hi, w_ref[half:, :])
        a = up(wb1_ref)
        hid = (a * jax.nn.sigmoid(a) * up(wb3_ref)).astype(_BF16)
        yb_ref[...] = _pack_halves(_dot(hid, wb2_ref[...]))

    @pl.when(i >= nblk_ref[0])
    def _():
        yb_ref[...] = jnp.zeros_like(yb_ref)


def _moe_experts(xb, n_blocks, expert_tables, w1, w3, w2):
    _, D, DE = w1.shape
    live = lambda i, nb: jnp.maximum(jnp.minimum(i, nb[0] - 1), 0)
    hbm = pl.BlockSpec(memory_space=pl.ANY)
    grid_spec = pltpu.PrefetchScalarGridSpec(
        num_scalar_prefetch=5,
        grid=(n_blocks,),
        in_specs=[pl.BlockSpec((MOE_BLOCK, D // 2), lambda i, be, nb, *_: (live(i, nb), 0)), hbm, hbm, hbm],
        out_specs=pl.BlockSpec((MOE_BLOCK, D // 2), lambda i, *_: (i, 0)),
        scratch_shapes=[pltpu.VMEM((2, D, DE), _F32), pltpu.VMEM((2, D, DE), _F32), pltpu.VMEM((2, DE, D), _F32),
                        pltpu.VMEM((D, DE), _BF16), pltpu.VMEM((D, DE), _BF16), pltpu.VMEM((DE, D), _BF16),
                        pltpu.SemaphoreType.DMA((2,))],
    )
    return pl.pallas_call(
        _expert_kernel,
        grid_spec=grid_spec,
        out_shape=jax.ShapeDtypeStruct((n_blocks * MOE_BLOCK, D // 2), jnp.uint32),
        compiler_params=_cparams("arbitrary"),
        name="moe_experts",
    )(*expert_tables, xb, w1, w3, w2)


def _final_kernel(dst_ref, dst_next_ref, h1_ref, comb_ref, yb_ref, p_ref, wg_ref, wp_ref, gple_ref, gfin_ref,
                  o_ref, ybuf_ref, sem):
    tm = h1_ref.shape[0]
    i = pl.program_id(0)
    slot = lax.rem(i, 2)

    def gather(d_ref, s):
        def start(c, carry):
            pltpu.make_async_copy(yb_ref.at[_chunk_rows(d_ref[0, 0, c])], ybuf_ref.at[s, _chunk_rows(c)],
                                  sem.at[s]).start()
            return carry
        lax.fori_loop(0, SORT_CHUNKS, start, 0, unroll=DMA_UNROLL)

    @pl.when(i == 0)
    def _():
        gather(dst_ref, 0)

    @pl.when(i + 1 < pl.num_programs(0))
    def _():
        gather(dst_next_ref, 1 - slot)

    pltpu.make_async_copy(ybuf_ref.at[slot], ybuf_ref.at[slot], sem.at[slot]).wait()

    comb = comb_ref[...]
    g1, g2 = comb[:, _META_GATE:_META_GATE + 1], comb[:, _META_GATE + 1:_META_GATE + 2]
    p1, p2 = comb[:, _META_POS:_META_POS + 1], comb[:, _META_POS + 1:_META_POS + 2]
    lane = lax.broadcasted_iota(jnp.int32, (tm, SORT_ROWS), 1).astype(_F32)
    wmat = (jnp.where(lane == p1, g1, 0.0) + jnp.where(lane == p2, g2, 0.0)).astype(_BF16)
    y_lo, y_hi = _unpack_halves(ybuf_ref[slot])
    moe = jnp.concatenate([_dot(wmat, y_lo), _dot(wmat, y_hi)], axis=1)
    h2 = h1_ref[...] + moe
    gate = jax.nn.sigmoid(_dot(_rmsnorm(h2, gple_ref[...]).astype(_BF16), wg_ref[...]))
    h3 = h2 + _dot(p_ref[...].astype(_BF16), wp_ref[...]) * gate
    o_ref[...] = _rmsnorm(h3, gfin_ref[...])


def _final(dst_chunk, h1, comb, yb, p2, w_gate, w_proj, g_ple, g_final):
    T, D = h1.shape
    nt = T // ROW_TILE
    PD = p2.shape[1]
    row = lambda n: pl.BlockSpec((ROW_TILE, n), lambda i: (i, 0))
    vec = lambda n: pl.BlockSpec((1, n), lambda i: (0, 0))
    dest3 = dst_chunk.reshape(nt, 1, SORT_CHUNKS)
    return pl.pallas_call(
        _final_kernel,
        grid=(nt,),
        in_specs=[pl.BlockSpec((1, 1, SORT_CHUNKS), lambda i: (i, 0, 0), memory_space=pltpu.SMEM),
                  pl.BlockSpec((1, 1, SORT_CHUNKS), lambda i: (jnp.minimum(i + 1, nt - 1), 0, 0),
                               memory_space=pltpu.SMEM),
                  row(D), row(ROUTER_LANES),
                  pl.BlockSpec(memory_space=pl.ANY),
                  row(PD),
                  pl.BlockSpec((D, D), lambda i: (0, 0)),
                  pl.BlockSpec((PD, D), lambda i: (0, 0)),
                  vec(D), vec(D)],
        out_specs=row(D),
        out_shape=jax.ShapeDtypeStruct((T, D), _F32),
        scratch_shapes=[pltpu.VMEM((2, SORT_ROWS, D // 2), jnp.uint32), pltpu.SemaphoreType.DMA((2,))],
        compiler_params=_cparams("arbitrary"),
        name="moe_gather_ple_final",
    )(dest3, dest3, h1, comb, yb, p2, w_gate.astype(_BF16), w_proj.astype(_BF16),
      g_ple.reshape(1, D), g_final.reshape(1, D))


def _routing_tables(tile_tbl, counts_f, n_blocks, n_tail):
    start = tile_tbl[:, :, 0].astype(jnp.int32)
    run = tile_tbl[:, :, 1].astype(jnp.int32)
    base = tile_tbl[:, :, 2].astype(jnp.int32)
    counts = counts_f[:, 0].astype(jnp.int32)
    padded = (counts + MOE_BLOCK - 1) // MOE_BLOCK * MOE_BLOCK
    e_iota = jnp.arange(N_EXPERTS, dtype=jnp.int32)
    pad_end = jnp.sum(jnp.where(e_iota[None, :] <= e_iota[:, None], padded[None, :], 0), axis=1)
    pad_start = pad_end - padded
    row0 = (jnp.arange(SORT_CHUNKS, dtype=jnp.int32) * MOE_CHUNK)[None, :, None]
    mine = (start[:, None, :] <= row0) & (row0 < (start + run)[:, None, :])
    used = jnp.any(mine, axis=-1)
    slot0 = (pad_start[None, :] + base - start)[:, None, :] + row0
    dst = jnp.sum(jnp.where(mine, slot0, 0), axis=-1) // MOE_CHUNK
    nt = start.shape[0]
    spare = (n_blocks * MOE_BLOCK // MOE_CHUNK + (jnp.arange(nt, dtype=jnp.int32) % 2)[:, None] * SORT_CHUNKS
             + jnp.arange(SORT_CHUNKS, dtype=jnp.int32)[None, :])
    dst_scatter = jnp.where(used, dst, spare).astype(jnp.int32)
    dst_gather = jnp.where(used, dst, 0).astype(jnp.int32)
    blk_row = jnp.arange(n_blocks, dtype=jnp.int32)[:, None] * MOE_BLOCK
    block_e = jnp.minimum(jnp.sum((pad_end[None, :] <= blk_row).astype(jnp.int32), axis=1), N_EXPERTS - 1)
    n_blk = (pad_end[-1:] // MOE_BLOCK).astype(jnp.int32)
    seg_last = jnp.where(padded > counts, pad_end // MOE_BLOCK - 1, -1)
    tail = n_blk[0] + jnp.arange(n_tail, dtype=jnp.int32)
    spare_blocks = n_blocks + jnp.arange(2 * SORT_ROWS // MOE_BLOCK, dtype=jnp.int32)
    zero_blocks = jnp.concatenate([seg_last, jnp.where(tail < n_blocks, tail, -1), spare_blocks]).astype(jnp.int32)
    has_rows = padded > 0
    before = e_iota[None, :] < e_iota[:, None]
    slot_e = jnp.sum(jnp.where(before & has_rows[None, :], 1, 0), axis=1) % 2
    later = jnp.where((e_iota[None, :] > e_iota[:, None]) & has_rows[None, :], e_iota[None, :], N_EXPERTS)
    next_e = jnp.min(later, axis=1)
    next_e = jnp.where(next_e == N_EXPERTS, -1, next_e)
    mine_e = block_e[:, None] == e_iota[None, :]
    pick = lambda tab: jnp.sum(jnp.where(mine_e, tab[None, :], 0), axis=1).astype(jnp.int32)
    first = (blk_row[:, 0] == pick(pad_start)).astype(jnp.int32)
    expert_tables = (block_e.astype(jnp.int32), n_blk, first, pick(slot_e), pick(next_e))
    return dst_scatter, dst_gather, zero_blocks, expert_tables


def _one_layer(h, p, g_mix, w_in, hy_conv_w, hy_conv_b, hy_f_w1, hy_f_b1, hy_f_freq1, hy_f_w2, hy_f_b2,
               hy_f_freq2, hy_f_w3, hy_skip, na_rpb, g_out_hy, g_out_na, w_out, g_ffn, router_wg, router_bg,
               router_we, router_be, exp_w1, exp_w3, exp_w2, g_ple, w_ple_gate, w_ple_proj):
    B, L, D = h.shape
    T = B * L
    assert T % ROW_TILE == 0 and L % FREQ_CHUNK == 0 and w_in.shape[1] == HY_COLS + 3 * NA_WIDTH
    x2 = h.reshape(T, D)
    u2, qkv = _in_proj(x2, g_mix, w_in)

    hf2 = _filter_mlp(L, hy_f_w1, hy_f_b1, hy_f_freq1, hy_f_w2, hy_f_b2, hy_f_freq2, hy_f_w3)
    tables = _half_dft_tables(L)
    k1r, k1i, k2r, k2i = _filter_spec(tables, hf2)
    y_hy_even, y_hy_odd = _hyena(u2.reshape(B, L // 2, 2 * HY_COLS), hy_conv_w.astype(_F32), hy_conv_b.astype(_F32),
                                 tables, k1r, k1i, k2r, k2i, hy_skip.astype(_F32))
    y_na = _natten(qkv.reshape(B, L, 3 * NA_WIDTH), na_rpb)

    h1, xn, meta, tile_tbl, counts = _out_router(
        y_hy_even.reshape(T // 2, HY_WIDTH), y_hy_odd.reshape(T // 2, HY_WIDTH), y_na.reshape(T, NA_WIDTH),
        g_out_hy, g_out_na, w_out, x2, g_ffn, router_wg, router_bg, router_we, router_be)
    max_rows = T * TOP_K + (T // ROW_TILE) * N_EXPERTS * (MOE_CHUNK - 1) + N_EXPERTS * (MOE_BLOCK - 1)
    n_blocks = -(-max_rows // MOE_BLOCK)
    n_tail = n_blocks - T * TOP_K // MOE_BLOCK
    dst_scatter, dst_gather, zero_blocks, expert_tables = _routing_tables(tile_tbl, counts, n_blocks, n_tail)
    assert 2 * SORT_ROWS % MOE_BLOCK == 0
    xb = _moe_scatter(xn, meta, dst_scatter, zero_blocks, n_blocks * MOE_BLOCK + 2 * SORT_ROWS)
    yb = _moe_experts(xb, n_blocks, expert_tables, exp_w1, exp_w3, exp_w2)
    comb = jnp.pad(meta[:_META_POS + TOP_K].T, ((0, 0), (0, ROUTER_LANES - _META_POS - TOP_K)))
    return h1, dst_gather, comb, yb


def kernel(x, p, g_mix, w_in, hy_conv_w, hy_conv_b, hy_f_w1, hy_f_b1, hy_f_freq1, hy_f_w2, hy_f_b2, hy_f_freq2, hy_f_w3, hy_skip, na_rpb, g_out_hy, g_out_na, w_out, g_ffn, router_wg, router_bg, router_we, router_be, exp_w1, exp_w3, exp_w2, g_ple, w_ple_gate, w_ple_proj, g_final):
    depth = p.shape[0]
    assert depth == 1, "the final RMSNorm is fused into the last layer's kernel; one layer is supported"
    B, L, D = x.shape
    i = 0
    h1, dest, gates, yb = _one_layer(
        x, p[i], g_mix[i], w_in[i], hy_conv_w[i], hy_conv_b[i], hy_f_w1[i], hy_f_b1[i], hy_f_freq1[i],
        hy_f_w2[i], hy_f_b2[i], hy_f_freq2[i], hy_f_w3[i], hy_skip[i], na_rpb[i], g_out_hy[i], g_out_na[i],
        w_out[i], g_ffn[i], router_wg[i], router_bg[i], router_we[i], router_be[i], exp_w1[i], exp_w3[i],
        exp_w2[i], g_ple[i], w_ple_gate[i], w_ple_proj[i])
    out = _final(dest, h1, gates, yb, p[i].reshape(B * L, -1), w_ple_gate[i], w_ple_proj[i], g_ple[i], g_final)
    return out.reshape(B, L, D)
```

```python
import functools
import math

import numpy as np
import jax
import jax.numpy as jnp
from jax import lax
from jax.experimental import pallas as pl
from jax.experimental.pallas import tpu as pltpu

_F32 = jnp.float32
_BF16 = jnp.bfloat16

GRID_W = 64
HY_WIDTH = 512
NA_WIDTH = 512
NA_HEADS = 8
NA_HEAD_DIM = 64
HY_ORDER = 2
SHORT_CONV = 3
FILTER_EMB = 33
FILTER_BANDS = (FILTER_EMB - 1) // 2
DECAY_TARGET = 1e-2
FAST_DECAY_PCT = 0.3
SLOW_DECAY_PCT = 1.5
WIN_ROWS = 8
WIN_COLS = 16
Q_ROWS = 4
N_GROUPS = 4
EXPERTS_PER_GROUP = 8
N_EXPERTS = N_GROUPS * EXPERTS_PER_GROUP
TOP_K = 2
MOE_BLOCK = 512
EPS = 1e-6
NEG_INF = -1e30
HY_COLS = (HY_ORDER + 1) * HY_WIDTH

V7X_LANES = 128
V7X_SUBLANES = 8
V7X_VMEM_LIMIT_BYTES = 56 * 2 ** 20

ROW_TILE = 512
FREQ_CHUNK = 512
HY_CH_TILE = 256
ROUTER_LANES = V7X_LANES
MOE_CHUNK = V7X_SUBLANES
SORT_ROWS = -(-(TOP_K * ROW_TILE + N_EXPERTS * (MOE_CHUNK - 1)) // V7X_LANES) * V7X_LANES
SORT_CHUNKS = SORT_ROWS // MOE_CHUNK
DMA_UNROLL = 8


def _cparams(*sem):
    return pltpu.CompilerParams(dimension_semantics=sem, vmem_limit_bytes=V7X_VMEM_LIMIT_BYTES)


def _resident(shape, index_map):
    return pl.BlockSpec(shape, index_map, pipeline_mode=pl.Buffered(1))


def _rmsnorm(x, g):
    return x * lax.rsqrt(jnp.mean(x * x, axis=-1, keepdims=True) + EPS) * g


def _dot(a, b):
    return jnp.dot(a, b, preferred_element_type=_F32)


def _dot_f32(a, b):
    return jnp.dot(a, b, preferred_element_type=_F32, precision=lax.Precision.HIGHEST)


def _parity_select(n):
    r = jnp.arange(n // 2, dtype=jnp.int32)[None, :, None]
    t = jnp.arange(n, dtype=jnp.int32)[None, None, :]
    p = jnp.arange(2, dtype=jnp.int32)[:, None, None]
    return (t == 2 * r + p).astype(_BF16)


def _in_proj_kernel(x_ref, g_ref, w_ref, sel_ref, u2_ref, qkv_ref):
    xn = _rmsnorm(x_ref[...], g_ref[...]).astype(_BF16)
    n_hy = u2_ref.shape[1] // 2
    xe = _dot(sel_ref[0], xn).astype(_BF16)
    xo = _dot(sel_ref[1], xn).astype(_BF16)
    for c0 in range(0, n_hy, 512):
        w = w_ref[:, c0:c0 + 512]
        u2_ref[:, c0:c0 + 512] = _dot(xe, w).astype(_BF16)
        u2_ref[:, n_hy + c0:n_hy + c0 + 512] = _dot(xo, w).astype(_BF16)
    for c0 in range(0, qkv_ref.shape[1], 512):
        qkv_ref[:, c0:c0 + 512] = _dot(xn, w_ref[:, n_hy + c0:n_hy + c0 + 512]).astype(_BF16)


def _in_proj(x2, g_mix, w_in):
    T, D = x2.shape
    n_in = w_in.shape[1]
    n_qkv = n_in - HY_COLS
    half = ROW_TILE // 2
    return pl.pallas_call(
        _in_proj_kernel,
        grid=(T // ROW_TILE,),
        in_specs=[pl.BlockSpec((ROW_TILE, D), lambda i: (i, 0)),
                  pl.BlockSpec((1, D), lambda i: (0, 0)),
                  _resident((D, n_in), lambda i: (0, 0)),
                  pl.BlockSpec((2, half, ROW_TILE), lambda i: (0, 0, 0))],
        out_specs=[pl.BlockSpec((half, 2 * HY_COLS), lambda i: (i, 0)),
                   pl.BlockSpec((ROW_TILE, n_qkv), lambda i: (i, 0))],
        out_shape=[jax.ShapeDtypeStruct((T // 2, 2 * HY_COLS), _BF16),
                   jax.ShapeDtypeStruct((T, n_qkv), _BF16)],
        compiler_params=_cparams("arbitrary"),
        name="in_proj",
    )(x2, g_mix.reshape(1, D), w_in.astype(_BF16), _parity_select(ROW_TILE))


def _filter_mlp_kernel(z_ref, w1_ref, b1_ref, f1_ref, w2_ref, b2_ref, f2_ref, w3_ref, delta_ref, hf_ref):
    z = z_ref[0]
    hid = jnp.sin(f1_ref[...] * (_dot_f32(z, w1_ref[...]) + b1_ref[...]))
    hid = jnp.sin(f2_ref[...] * (_dot_f32(hid, w2_ref[...]) + b2_ref[...]))
    hf = _dot_f32(hid, w3_ref[...])
    decay = jnp.exp(-z[:, 0:1] * delta_ref[...])
    tl = z.shape[0]
    row = lax.broadcasted_iota(jnp.int32, (tl, HY_WIDTH), 0)
    offset0 = (row == 0) & (pl.program_id(0) == 0) & (pl.program_id(1) == 0)
    for k in range(2 * HY_ORDER):
        blk = hf[:, k * HY_WIDTH:(k + 1) * HY_WIDTH] * decay
        if k >= HY_ORDER:
            blk = jnp.where(offset0, 0.0, blk)
        hf_ref[:, k * HY_WIDTH:(k + 1) * HY_WIDTH] = blk


def _filter_mlp(L, w1, b1, f1, w2, b2, f2, w3):
    t = jnp.linspace(0.0, 1.0, L, dtype=_F32)[:, None]
    w = 2.0 * math.pi * jnp.arange(L, dtype=_F32)[:, None] / L
    bands = jnp.linspace(1e-4, FILTER_BANDS - 1, FILTER_BANDS, dtype=_F32)[None, :]
    z = jnp.concatenate([t, jnp.cos(bands * w), -jnp.sin(bands * w)], axis=-1)
    z = jnp.pad(z, ((0, 0), (0, V7X_LANES - FILTER_EMB)))
    z = jnp.stack([z[0::2], z[1::2]])
    w1p = jnp.pad(w1.astype(_F32), ((0, V7X_LANES - FILTER_EMB), (0, 0)))
    max_decay = math.log(DECAY_TARGET) / FAST_DECAY_PCT
    min_decay = math.log(DECAY_TARGET) / SLOW_DECAY_PCT
    deltas = jnp.abs(jnp.linspace(min_decay, max_decay, HY_WIDTH, dtype=_F32))[None, :]
    hid = w1.shape[1]
    n_out = w3.shape[1]
    lh = L // 2
    tl = min(lh, ROW_TILE)
    full = lambda shape: pl.BlockSpec(shape, lambda p, i: (0, 0))
    return pl.pallas_call(
        _filter_mlp_kernel,
        grid=(2, lh // tl),
        in_specs=[pl.BlockSpec((1, tl, V7X_LANES), lambda p, i: (p, i, 0)),
                  full((V7X_LANES, hid)), full((1, hid)), full((1, hid)),
                  full((hid, hid)), full((1, hid)), full((1, hid)),
                  full((hid, n_out)), full((1, HY_WIDTH))],
        out_specs=pl.BlockSpec((tl, n_out), lambda p, i: (i, p)),
        out_shape=jax.ShapeDtypeStruct((lh, 2 * n_out), _F32),
        compiler_params=_cparams("arbitrary", "arbitrary"),
        name="hyena_filter_mlp",
    )(z, w1p, b1.reshape(1, hid), f1.reshape(1, hid), w2.astype(_F32), b2.reshape(1, hid),
      f2.reshape(1, hid), w3.astype(_F32), deltas)


def _freq_chunks(gp):
    return [(r0, min(FREQ_CHUNK, gp - r0)) for r0 in range(0, gp, FREQ_CHUNK)]


def _half_dft_tables(L):
    lh, n, sb = L // 2, 2 * L, 32
    gp = -(-(lh + 1) // V7X_SUBLANES) * V7X_SUBLANES
    assert lh % sb == 0
    g = jnp.arange(gp, dtype=jnp.int32)
    live = (g <= lh)[:, None]
    ang = lambda s: ((g[:, None] * s[None, :]) % L).astype(_F32) * (2.0 * math.pi / L)
    a_hi = ang(jnp.arange(lh // sb, dtype=jnp.int32) * sb)
    a_lo = ang(jnp.arange(sb, dtype=jnp.int32))
    c_hi, s_hi, c_lo, s_lo = jnp.cos(a_hi), jnp.sin(a_hi), jnp.cos(a_lo), jnp.sin(a_lo)
    cosm = (c_hi[:, :, None] * c_lo[:, None, :] - s_hi[:, :, None] * s_lo[:, None, :]).reshape(gp, lh)
    sinm = (s_hi[:, :, None] * c_lo[:, None, :] + c_hi[:, :, None] * s_lo[:, None, :]).reshape(gp, lh)
    cosm = jnp.where(live, cosm, 0.0)
    sinm = jnp.where(live, sinm, 0.0)
    ff = jnp.concatenate([cosm, -sinm], axis=0).astype(_BF16)
    wgt = jnp.where((g == 0) | (g == lh), 1.0, 2.0)[:, None] / n
    gf = jnp.concatenate([(cosm * wgt).T, (-sinm * wgt).T], axis=1).astype(_BF16)
    tw = g.astype(_F32)[:, None] * (2.0 * math.pi / n)
    wr = jnp.broadcast_to(jnp.cos(tw), (gp, HY_CH_TILE))
    wi = jnp.broadcast_to(-jnp.sin(tw), (gp, HY_CH_TILE))
    return ff, gf, wr, wi


def _half_spectra(ff_ref, wr, wi, xe, xo, r0, n):
    gp = ff_ref.shape[0] // 2
    fc = ff_ref[r0:r0 + n, :]
    fs = ff_ref[gp + r0:gp + r0 + n, :]
    er, ei = _dot(fc, xe), _dot(fs, xe)
    orr, oi = _dot(fc, xo), _dot(fs, xo)
    pr = wr * orr - wi * oi
    pi = wr * oi + wi * orr
    return (er + pr, ei + pi), (er - pr, pi - ei)


def _filter_spec_kernel(ff_ref, wr_ref, wi_ref, fe_ref, fo_ref, be_ref, bo_ref, k1r_ref, k1i_ref, k2r_ref, k2i_ref):
    fe, fo, be, bo = (r[...].astype(_BF16) for r in (fe_ref, fo_ref, be_ref, bo_ref))
    for r0, n in _freq_chunks(wr_ref.shape[0]):
        wr, wi = wr_ref[r0:r0 + n, :], wi_ref[r0:r0 + n, :]
        (f1r, f1i), (f2r, f2i) = _half_spectra(ff_ref, wr, wi, fe, fo, r0, n)
        (b1r, b1i), (b2r, b2i) = _half_spectra(ff_ref, wr, wi, be, bo, r0, n)
        k1r_ref[0, r0:r0 + n, :] = f1r + b1r
        k1i_ref[0, r0:r0 + n, :] = f1i - b1i
        k2r_ref[0, r0:r0 + n, :] = f2r + b2r
        k2i_ref[0, r0:r0 + n, :] = f2i - b2i


def _filter_spec(tables, hf2):
    ff, _, wr, wi = tables
    lh, gp = hf2.shape[0], wr.shape[0]
    n_ct = HY_WIDTH // HY_CH_TILE
    n_blk = hf2.shape[1] // 2 // HY_CH_TILE
    col = lambda blk0: pl.BlockSpec((lh, HY_CH_TILE), lambda j: (0, blk0 + j))
    out = jax.ShapeDtypeStruct((HY_ORDER, gp, HY_WIDTH), _F32)
    ospec = pl.BlockSpec((1, gp, HY_CH_TILE), lambda j: (j // n_ct, 0, j % n_ct))
    full = lambda a: pl.BlockSpec(a.shape, lambda j: (0, 0))
    return pl.pallas_call(
        _filter_spec_kernel,
        grid=(HY_ORDER * n_ct,),
        in_specs=[full(ff), full(wr), full(wi),
                  col(0), col(n_blk),
                  col(HY_ORDER * n_ct), col(n_blk + HY_ORDER * n_ct)],
        out_specs=[ospec] * 4,
        out_shape=[out] * 4,
        compiler_params=_cparams("arbitrary"),
        name="hyena_filter_spectrum",
    )(ff, wr, wi, hf2, hf2, hf2, hf2)


def _parity_conv(ze, zo, w_ref, b_ref):
    lh = ze.shape[0]
    row = lax.broadcasted_iota(jnp.int32, ze.shape, 0)
    zo_prev = jnp.where(row == 0, 0.0, pltpu.roll(zo, 1, axis=0))
    ze_next = jnp.where(row == lh - 1, 0.0, pltpu.roll(ze, lh - 1, axis=0))
    w0, w1, w2, b = w_ref[0:1, :], w_ref[1:2, :], w_ref[2:3, :], b_ref[...]
    return ((b + zo_prev * w0) + ze * w1) + zo * w2, ((b + ze * w0) + zo * w1) + ze_next * w2


def _hyena_kernel(ze_ref, zo_ref, zw_ref, zb_ref, ge_ref, go_ref, gw_ref, gb_ref, skip_ref, ff_ref, gf_ref,
                  wr_ref, wi_ref, k1r_ref, k1i_ref, k2r_ref, k2i_ref, oe_ref, oo_ref, a_ref, b_ref, *, conv_input):
    ze, zo = ze_ref[0].astype(_F32), zo_ref[0].astype(_F32)
    if conv_input:
        ze, zo = _parity_conv(ze, zo, zw_ref, zb_ref)
    xe, xo = ze.astype(_BF16), zo.astype(_BF16)
    gp = wr_ref.shape[0]
    for r0, n in _freq_chunks(gp):
        rows = slice(r0, r0 + n)
        wr, wi = wr_ref[rows, :], wi_ref[rows, :]
        (x1r, x1i), (x2r, x2i) = _half_spectra(ff_ref, wr, wi, xe, xo, r0, n)
        k1r, k1i, k2r, k2i = k1r_ref[0, rows, :], k1i_ref[0, rows, :], k2r_ref[0, rows, :], k2i_ref[0, rows, :]
        y1r, y1i = x1r * k1r - x1i * k1i, x1r * k1i + x1i * k1r
        y2r, y2i = x2r * k2r - x2i * k2i, x2r * k2i + x2i * k2r
        tr, ti = y1r - y2r, y1i + y2i
        a_ref[rows, :] = (y1r + y2r).astype(_BF16)
        a_ref[gp + r0:gp + r0 + n, :] = (y1i - y2i).astype(_BF16)
        b_ref[rows, :] = (tr * wr + ti * wi).astype(_BF16)
        b_ref[gp + r0:gp + r0 + n, :] = (ti * wr - tr * wi).astype(_BF16)
    ge, go = _parity_conv(ge_ref[0].astype(_F32), go_ref[0].astype(_F32), gw_ref, gb_ref)
    skip = skip_ref[0]
    for r0 in range(0, ze.shape[0], FREQ_CHUNK):
        rows = slice(r0, r0 + FREQ_CHUNK)
        g = gf_ref[rows, :]
        oe_ref[0, rows, :] = ge[rows] * (_dot(g, a_ref[...]) + ze[rows] * skip)
        oo_ref[0, rows, :] = go[rows] * (_dot(g, b_ref[...]) + zo[rows] * skip)


def _hyena(u2, conv_w, conv_b, tables, k1r, k1i, k2r, k2i, skip):
    ff, gf, wr, wi = tables
    B, lh, _ = u2.shape
    gp = wr.shape[0]
    assert lh % FREQ_CHUNK == 0
    n_ct = HY_WIDTH // HY_CH_TILE
    odd = HY_COLS // HY_CH_TILE
    conv_b2 = conv_b.reshape(1, HY_COLS)
    skip3 = skip.reshape(HY_ORDER, 1, HY_WIDTH)
    col = lambda blk0: pl.BlockSpec((1, lh, HY_CH_TILE), lambda c, b: (b, 0, blk0 + c))
    cw = lambda blk0: pl.BlockSpec((SHORT_CONV, HY_CH_TILE), lambda c, b: (0, blk0 + c))
    cb = lambda blk0: pl.BlockSpec((1, HY_CH_TILE), lambda c, b: (0, blk0 + c))
    const = lambda a: _resident(a.shape, lambda c, b: (0, 0))
    out = jax.ShapeDtypeStruct((B, lh, HY_WIDTH), _F32)
    ze_arr, zo_arr, ze_spec, zo_spec = u2, u2, col(2 * n_ct), col(odd + 2 * n_ct)
    for o in range(HY_ORDER):
        coef = lambda: pl.BlockSpec((1, gp, HY_CH_TILE), lambda c, b, o=o: (o, 0, c), pipeline_mode=pl.Buffered(1))
        ze_arr, zo_arr = pl.pallas_call(
            functools.partial(_hyena_kernel, conv_input=(o == 0)),
            grid=(n_ct, B),
            in_specs=[ze_spec, zo_spec, cw(2 * n_ct), cb(2 * n_ct),
                      col(o * n_ct), col(odd + o * n_ct), cw(o * n_ct), cb(o * n_ct),
                      pl.BlockSpec((1, 1, HY_CH_TILE), lambda c, b, o=o: (o, 0, c)),
                      const(ff), const(gf), const(wr), const(wi), coef(), coef(), coef(), coef()],
            out_specs=[col(0), col(0)],
            out_shape=[out, out],
            scratch_shapes=[pltpu.VMEM((2 * gp, HY_CH_TILE), _BF16), pltpu.VMEM((2 * gp, HY_CH_TILE), _BF16)],
            compiler_params=_cparams("arbitrary", "arbitrary"),
            name=f"hyena_order_{o}",
        )(ze_arr, zo_arr, conv_w, conv_b2, u2, u2, conv_w, conv_b2, skip3, ff, gf, wr, wi, k1r, k1i, k2r, k2i)
        ze_spec = zo_spec = col(0)
    return ze_arr, zo_arr


def _natten_tables(rows):
    kr = min(WIN_ROWS, rows)
    krb = min(-(-(Q_ROWS + kr - 1) // 2) * 2, rows)
    rs = np.clip(np.arange(rows) - kr // 2, 0, rows - kr)
    cs = np.clip(np.arange(GRID_W) - WIN_COLS // 2, 0, GRID_W - WIN_COLS)
    qc = np.tile(np.arange(GRID_W), Q_ROWS)[:, None]
    kc = np.tile(np.arange(GRID_W), krb)[None, :]
    cases, case_of, kstart = [], [], []
    for p in range(rows // Q_ROWS):
        k_r0 = min(rs[p * Q_ROWS], rows - krb)
        qr = (p * Q_ROWS + np.repeat(np.arange(Q_ROWS), GRID_W))[:, None]
        kr_ = (k_r0 + np.repeat(np.arange(krb), GRID_W))[None, :]
        valid = ((kr_ >= rs[qr]) & (kr_ < rs[qr] + kr) & (kc >= cs[qc]) & (kc < cs[qc] + WIN_COLS))
        dr = np.clip(kr_ - qr + WIN_ROWS - 1, 0, 2 * WIN_ROWS - 2)
        dc = np.clip(kc - qc + WIN_COLS - 1, 0, 2 * WIN_COLS - 2)
        idx = np.where(valid, dr * (2 * WIN_COLS - 1) + dc, -1).astype(np.int32)
        for ci, c in enumerate(cases):
            if np.array_equal(c, idx):
                break
        else:
            ci = len(cases)
            cases.append(idx)
        case_of.append(ci)
        kstart.append(k_r0 * GRID_W)
    return np.stack(cases), np.asarray(case_of, np.int32), np.asarray(kstart, np.int32), krb * GRID_W


def _natten_bias_kernel(dr_ref, rv_ref, u_ref, o_ref, *, krb):
    c = pl.program_id(0)
    n_h, _, n_m = u_ref.shape
    n_k = o_ref.shape[3]
    w_shift, w_mask = GRID_W.bit_length() - 1, GRID_W - 1
    col = lax.broadcasted_iota(jnp.int32, (1, n_k), 1)
    kr_col, kc_col = col >> w_shift, col & w_mask
    per_kr = lambda tab, qr: sum(jnp.where(kr_col == kr, tab[(c * Q_ROWS + qr) * krb + kr], 0) for kr in range(krb))

    uh, ul = _split_bf16(u_ref[...].reshape(n_h * GRID_W, n_m))
    row = lax.broadcasted_iota(jnp.int32, (n_m, n_k), 0)
    qc = lax.broadcasted_iota(jnp.int32, (GRID_W, n_k), 0)
    cs = jnp.clip(qc - WIN_COLS // 2, 0, GRID_W - WIN_COLS)
    col_ok = (kc_col >= cs) & (kc_col < cs + WIN_COLS)
    for qr in range(Q_ROWS):
        hit = ((row >> w_shift) == per_kr(dr_ref, qr)) & ((row & w_mask) == kc_col)
        v = jnp.where(hit, 1.0, 0.0).astype(_BF16)
        b = (_dot(uh, v) + _dot(ul, v)).reshape(n_h, GRID_W, n_k)
        ok = col_ok & (per_kr(rv_ref, qr) > 0)
        o_ref[0, :, qr * GRID_W:(qr + 1) * GRID_W, :] = jnp.where(ok[None], b, NEG_INF)


def _natten_bias(rpb, cases):
    n_case, nq, nk = cases.shape
    krb = nk // GRID_W
    n_dr, n_dc = 2 * WIN_ROWS - 1, 2 * WIN_COLS - 1
    c5 = cases.reshape(n_case, Q_ROWS, GRID_W, krb, GRID_W)
    dr_blk = np.where(c5 >= 0, c5 // n_dc, -1).max(axis=(2, 4))
    row_ok = dr_blk >= 0
    cs = np.clip(np.arange(GRID_W) - WIN_COLS // 2, 0, GRID_W - WIN_COLS)[:, None]
    col_ok = (np.arange(GRID_W)[None, :] >= cs) & (np.arange(GRID_W)[None, :] < cs + WIN_COLS)
    dc_idx = np.arange(GRID_W)[None, :] - np.arange(GRID_W)[:, None] + WIN_COLS - 1
    assert np.array_equal(c5 >= 0, row_ok[:, :, None, :, None] & col_ok[None, None, :, None, :])
    assert np.all((c5 < 0) | (c5 == dr_blk[:, :, None, :, None] * n_dc + dc_idx[None, None, :, None, :]))
    onehot = (dc_idx[:, None, :] == np.arange(n_dc)[None, :, None]).astype(np.float32)
    u = jnp.einsum('hdj,qjk->hqdk', rpb.astype(_F32), onehot, precision=lax.Precision.HIGHEST)
    u = u.reshape(NA_HEADS, GRID_W, n_dr * GRID_W)
    grid_spec = pltpu.PrefetchScalarGridSpec(
        num_scalar_prefetch=2,
        grid=(n_case,),
        in_specs=[pl.BlockSpec((NA_HEADS, GRID_W, n_dr * GRID_W), lambda c, *_: (0, 0, 0))],
        out_specs=pl.BlockSpec((1, NA_HEADS, nq, nk), lambda c, *_: (c, 0, 0, 0)),
    )
    return pl.pallas_call(
        functools.partial(_natten_bias_kernel, krb=krb),
        grid_spec=grid_spec,
        out_shape=jax.ShapeDtypeStruct((n_case, NA_HEADS, nq, nk), _F32),
        compiler_params=_cparams("arbitrary"),
        name="natten_bias",
    )(jnp.asarray(np.maximum(dr_blk, 0).reshape(-1), jnp.int32), jnp.asarray(row_ok.reshape(-1), jnp.int32), u)


def _natten_kernel(case_ref, kstart_ref, qkv_ref, bias_ref, o_ref, *, n_pairs, n_keys):
    nq = Q_ROWS * GRID_W
    pair_w = 2 * NA_HEAD_DIM
    lane = lax.broadcasted_iota(jnp.int32, (nq, pair_w), 1)
    lo_half = lane < NA_HEAD_DIM
    scale = NA_HEAD_DIM ** -0.5

    def body(p, carry):
        q0 = pl.multiple_of(p * nq, nq)
        k0 = pl.multiple_of(kstart_ref[p], GRID_W)
        case = case_ref[p]
        for hp in range(NA_HEADS // 2):
            c0 = hp * pair_w
            q2 = qkv_ref[0, pl.ds(q0, nq), c0:c0 + pair_w]
            k2 = qkv_ref[0, pl.ds(k0, n_keys), NA_WIDTH + c0:NA_WIDTH + c0 + pair_w]
            v2 = qkv_ref[0, pl.ds(k0, n_keys), 2 * NA_WIDTH + c0:2 * NA_WIDTH + c0 + pair_w]
            zero = jnp.zeros_like(q2)
            q2 = q2 * scale
            qq = jnp.concatenate([jnp.where(lo_half, q2, zero), jnp.where(lo_half, zero, q2)], axis=0)
            s = lax.dot_general(qq, k2, (((1,), (1,)), ((), ())), preferred_element_type=_F32)
            s = s + bias_ref[case, hp]
            e = jnp.exp(s - jnp.max(s, axis=-1, keepdims=True))
            l = jnp.sum(e, axis=-1, keepdims=True)
            o = _dot(e.astype(_BF16), v2) / l
            o_ref[0, pl.ds(q0, nq), c0:c0 + pair_w] = jnp.where(lo_half, o[:nq], o[nq:])
        return carry

    lax.fori_loop(0, n_pairs, body, 0, unroll=2)


def _natten(qkv, rpb):
    B, L, _ = qkv.shape
    rows = L // GRID_W
    assert rows % (2 * Q_ROWS) == 0 and rows >= Q_ROWS + WIN_ROWS
    cases, case_of, kstart, n_keys = _natten_tables(rows)
    n_case = cases.shape[0]
    nq = Q_ROWS * GRID_W
    bias = _natten_bias(rpb, cases).reshape(n_case, NA_HEADS // 2, 2 * nq, n_keys)
    grid_spec = pltpu.PrefetchScalarGridSpec(
        num_scalar_prefetch=2,
        grid=(B,),
        in_specs=[pl.BlockSpec((1, L, 3 * NA_WIDTH), lambda b, *_: (b, 0, 0)),
                  _resident((n_case, NA_HEADS // 2, 2 * nq, n_keys), lambda b, *_: (0, 0, 0, 0))],
        out_specs=pl.BlockSpec((1, L, NA_WIDTH), lambda b, *_: (b, 0, 0)),
    )
    return pl.pallas_call(
        functools.partial(_natten_kernel, n_pairs=rows // Q_ROWS, n_keys=n_keys),
        grid_spec=grid_spec,
        out_shape=jax.ShapeDtypeStruct((B, L, NA_WIDTH), _F32),
        compiler_params=_cparams("arbitrary"),
        name="natten",
    )(jnp.asarray(case_of), jnp.asarray(kstart), qkv, bias)


_META_ID, _META_GATE, _META_POS = 0, 2, 4
ROUTER_ROWS = 8 + N_EXPERTS


def _split_bf16(v):
    hi = v.astype(_BF16)
    return hi, (v - hi.astype(_F32)).astype(_BF16)


def _out_router_kernel(yhe_ref, yho_ref, sel_ref, yna_ref, ghy_ref, gna_ref, wtop_ref, wbot_ref, x_ref, gffn_ref,
                       rw_ref, rb_ref, h1_ref, xs_ref, meta_ref, tile_ref, cnt_ref, carry_ref):
    @pl.when(pl.program_id(0) == 0)
    def _():
        carry_ref[...] = jnp.zeros_like(carry_ref)

    nhe = _rmsnorm(yhe_ref[...], ghy_ref[...]).astype(_BF16)
    nho = _rmsnorm(yho_ref[...], ghy_ref[...]).astype(_BF16)
    nh = (_dot(sel_ref[0], nhe) + _dot(sel_ref[1], nho)).astype(_BF16)
    nn = _rmsnorm(yna_ref[...], gna_ref[...]).astype(_BF16)
    h1 = x_ref[...] + (_dot(nh, wtop_ref[...]) + _dot(nn, wbot_ref[...]))
    h1_ref[...] = h1
    xn = _rmsnorm(h1, gffn_ref[...])
    tm = xn.shape[0]

    xn_bf = xn.astype(_BF16)
    logits = _dot(xn_bf, rw_ref[...]) + rb_ref[...]
    lt = logits.T[:ROUTER_ROWS]

    neg = -jnp.inf
    row8 = lax.broadcasted_iota(jnp.int32, (EXPERTS_PER_GROUP, tm), 0).astype(_F32)
    col_max = lambda v: jnp.max(v, axis=0, keepdims=True)
    first_max = lambda v, m: jnp.min(jnp.where(v == m, row8, float(EXPERTS_PER_GROUP)), axis=0, keepdims=True)

    gl = jnp.where(row8 < N_GROUPS, lt[0:8], neg)
    gmax = col_max(gl)
    g_w = 1.0 / jnp.sum(jnp.exp(gl - gmax), axis=0, keepdims=True)
    g_idx = first_max(gl, gmax)
    el = lt[8:8 + EXPERTS_PER_GROUP]
    for g in range(1, N_GROUPS):
        el = jnp.where(g_idx == g, lt[8 + g * EXPERTS_PER_GROUP:8 + (g + 1) * EXPERTS_PER_GROUP], el)
    m1 = col_max(el)
    i1 = first_max(el, m1)
    el2 = jnp.where(row8 == i1, neg, el)
    m2 = col_max(el2)
    i2 = first_max(el2, m2)
    r = jnp.exp(m2 - m1)
    gate1 = g_w / (1.0 + r)
    gate2 = g_w * r / (1.0 + r)
    id1 = g_idx * EXPERTS_PER_GROUP + i1
    id2 = g_idx * EXPERTS_PER_GROUP + i2

    row_e = lax.broadcasted_iota(jnp.int32, (N_EXPERTS, tm), 0).astype(_F32)
    sel1 = row_e == id1
    sel2 = row_e == id2
    onehot = jnp.where(sel1 | sel2, 1.0, 0.0)
    tri_r = lax.broadcasted_iota(jnp.int32, (tm, tm), 0)
    tri_c = lax.broadcasted_iota(jnp.int32, (tm, tm), 1)
    tri = jnp.where(tri_r < tri_c, 1.0, 0.0).astype(_BF16)
    before = _dot(onehot.astype(_BF16), tri)
    cnt = jnp.sum(onehot, axis=1, keepdims=True)
    run = jnp.floor((cnt + (MOE_CHUNK - 1)) * (1.0 / MOE_CHUNK)) * MOE_CHUNK
    run_b = jnp.broadcast_to(run, (N_EXPERTS, ROUTER_LANES))
    e_r = lax.broadcasted_iota(jnp.int32, (N_EXPERTS, N_EXPERTS), 0)
    e_c = lax.broadcasted_iota(jnp.int32, (N_EXPERTS, N_EXPERTS), 1)
    start_b = _dot_f32(jnp.where(e_c < e_r, 1.0, 0.0), run_b)
    start = start_b[:, 0:1]
    pos1 = jnp.sum(jnp.where(sel1, start + before, 0.0), axis=0, keepdims=True)
    pos2 = jnp.sum(jnp.where(sel2, start + before, 0.0), axis=0, keepdims=True)
    xs_ref[...] = xn_bf

    base_b = carry_ref[...]
    lane = lax.broadcasted_iota(jnp.int32, (N_EXPERTS, ROUTER_LANES), 1)
    tile_ref[0] = jnp.where(lane == 0, start_b, jnp.where(lane == 1, run_b, jnp.where(lane == 2, base_b, 0.0)))
    carry_ref[...] = base_b + run_b
    cnt_ref[...] = base_b + run_b

    meta = jnp.zeros((8, tm), _F32)
    for k, v in enumerate((id1, id2, gate1, gate2, pos1, pos2)):
        meta = jnp.where(row8 == k, v, meta)
    meta_ref[...] = meta


def _out_router(y_hy_even, y_hy_odd, y_na, g_hy, g_na, w_out, x2, g_ffn, wg, bg, we, be):
    T, D = x2.shape
    half = ROW_TILE // 2
    sel_t = _parity_select(ROW_TILE).transpose(0, 2, 1)
    gpad = 8 - N_GROUPS
    rw = jnp.concatenate([wg.astype(_F32), jnp.zeros((D, gpad), _F32), we.astype(_F32)], axis=1)
    rw = jnp.pad(rw, ((0, 0), (0, ROUTER_LANES - ROUTER_ROWS))).astype(_BF16)
    rb = jnp.concatenate([bg.astype(_F32), jnp.zeros((gpad,), _F32), be.astype(_F32)])
    rb = jnp.pad(rb, (0, ROUTER_LANES - ROUTER_ROWS)).reshape(1, ROUTER_LANES)
    w_bf = w_out.astype(_BF16)
    row = lambda n: pl.BlockSpec((ROW_TILE, n), lambda i: (i, 0))
    vec = lambda n: pl.BlockSpec((1, n), lambda i: (0, 0))
    mat = lambda r, c: pl.BlockSpec((r, c), lambda i: (0, 0))
    return pl.pallas_call(
        _out_router_kernel,
        grid=(T // ROW_TILE,),
        in_specs=[pl.BlockSpec((half, HY_WIDTH), lambda i: (i, 0)), pl.BlockSpec((half, HY_WIDTH), lambda i: (i, 0)),
                  pl.BlockSpec((2, ROW_TILE, half), lambda i: (0, 0, 0)),
                  row(NA_WIDTH), vec(HY_WIDTH), vec(NA_WIDTH),
                  mat(HY_WIDTH, D), mat(NA_WIDTH, D), row(D), vec(D),
                  mat(D, ROUTER_LANES), vec(ROUTER_LANES)],
        out_specs=[row(D),
                   row(D),
                   pl.BlockSpec((8, ROW_TILE), lambda i: (0, i)),
                   pl.BlockSpec((1, N_EXPERTS, ROUTER_LANES), lambda i: (i, 0, 0)),
                   mat(N_EXPERTS, ROUTER_LANES)],
        out_shape=[jax.ShapeDtypeStruct((T, D), _F32),
                   jax.ShapeDtypeStruct((T, D), _BF16),
                   jax.ShapeDtypeStruct((8, T), _F32),
                   jax.ShapeDtypeStruct((T // ROW_TILE, N_EXPERTS, ROUTER_LANES), _F32),
                   jax.ShapeDtypeStruct((N_EXPERTS, ROUTER_LANES), _F32)],
        scratch_shapes=[pltpu.VMEM((N_EXPERTS, ROUTER_LANES), _F32)],
        compiler_params=_cparams("arbitrary"),
        name="out_proj_router",
    )(y_hy_even, y_hy_odd, sel_t, y_na, g_hy.reshape(1, -1), g_na.reshape(1, -1), w_bf[:HY_WIDTH], w_bf[HY_WIDTH:],
      x2, g_ffn.reshape(1, D), rw, rb)


def _chunk_rows(c):
    return pl.ds(pl.multiple_of(c * MOE_CHUNK, MOE_CHUNK), MOE_CHUNK)


_HI_HALF = 0xFFFF0000


def _pack_halves(v):
    c = v.shape[1] // 2
    bits = lax.bitcast_convert_type(v.astype(_BF16).astype(_F32), jnp.uint32)
    return (bits[:, c:] & jnp.uint32(_HI_HALF)) | lax.shift_right_logical(bits[:, :c], jnp.uint32(16))


def _unpack_halves(w):
    lo = lax.bitcast_convert_type(lax.shift_left(w, jnp.uint32(16)), _F32)
    hi = lax.bitcast_convert_type(w & jnp.uint32(_HI_HALF), _F32)
    return lo.astype(_BF16), hi.astype(_BF16)


def _scatter_kernel(zblk_ref, dst_ref, meta_ref, xn_ref, xb_ref, zero_ref, xs_ref, sem, zero_sem):
    i = pl.program_id(0)
    slot = lax.rem(i, 2)

    @pl.when(i == 0)
    def _():
        zero_ref[...] = jnp.zeros_like(zero_ref)

        def blk_copy(j):
            start = pl.multiple_of(zblk_ref[j] * MOE_BLOCK, MOE_BLOCK)
            return pltpu.make_async_copy(zero_ref, xb_ref.at[pl.ds(start, MOE_BLOCK)], zero_sem)

        def start(j, c):
            @pl.when(zblk_ref[j] >= 0)
            def _():
                blk_copy(j).start()
            return c

        def wait(j, c):
            @pl.when(zblk_ref[j] >= 0)
            def _():
                blk_copy(j).wait()
            return c

        lax.fori_loop(0, zblk_ref.shape[0], start, 0)
        lax.fori_loop(0, zblk_ref.shape[0], wait, 0)

    tm = xn_ref.shape[0]
    meta = meta_ref[...]
    pos1, pos2 = meta[_META_POS:_META_POS + 1, :], meta[_META_POS + 1:_META_POS + 2, :]
    srow = lax.broadcasted_iota(jnp.int32, (SORT_ROWS, tm), 0).astype(_F32)
    perm = jnp.where((srow == pos1) | (srow == pos2), 1.0, 0.0).astype(_BF16)
    xs_ref[slot] = _pack_halves(_dot(perm, xn_ref[...]))

    def start(c, carry):
        pltpu.make_async_copy(xs_ref.at[slot, _chunk_rows(c)], xb_ref.at[_chunk_rows(dst_ref[0, 0, c])],
                              sem.at[slot]).start()
        return carry

    lax.fori_loop(0, SORT_CHUNKS, start, 0, unroll=DMA_UNROLL)

    wait_all = lambda s: pltpu.make_async_copy(xs_ref.at[s], xs_ref.at[s], sem.at[s]).wait()

    @pl.when(i > 0)
    def _():
        wait_all(1 - slot)

    @pl.when(i == pl.num_programs(0) - 1)
    def _():
        wait_all(slot)


def _moe_scatter(xn, meta, dst_chunk, zero_blocks, n_slots):
    T, D = xn.shape
    nt = T // ROW_TILE
    dst3 = dst_chunk.reshape(nt, 1, SORT_CHUNKS)
    grid_spec = pltpu.PrefetchScalarGridSpec(
        num_scalar_prefetch=1,
        grid=(nt,),
        in_specs=[pl.BlockSpec((1, 1, SORT_CHUNKS), lambda i, *_: (i, 0, 0), memory_space=pltpu.SMEM),
                  pl.BlockSpec((8, ROW_TILE), lambda i, *_: (0, i)),
                  pl.BlockSpec((ROW_TILE, D), lambda i, *_: (i, 0))],
        out_specs=pl.BlockSpec(memory_space=pl.ANY),
        scratch_shapes=[pltpu.VMEM((MOE_BLOCK, D // 2), jnp.uint32), pltpu.VMEM((2, SORT_ROWS, D // 2), jnp.uint32),
                        pltpu.SemaphoreType.DMA((2,)), pltpu.SemaphoreType.DMA(())],
    )
    return pl.pallas_call(
        _scatter_kernel,
        grid_spec=grid_spec,
        out_shape=jax.ShapeDtypeStruct((n_slots, D // 2), jnp.uint32),
        compiler_params=_cparams("arbitrary"),
        name="moe_scatter",
    )(zero_blocks, dst3, meta, xn)


def _expert_kernel(be_ref, nblk_ref, first_ref, slot_ref, next_ref, xb_ref, w1_ref, w3_ref, w2_ref, yb_ref,
                   wf1_ref, wf3_ref, wf2_ref, wb1_ref, wb3_ref, wb2_ref, sem):
    i = pl.program_id(0)

    def weight_copies(e, s):
        return [pltpu.make_async_copy(src.at[e], dst.at[s], sem.at[s])
                for src, dst in ((w1_ref, wf1_ref), (w3_ref, wf3_ref), (w2_ref, wf2_ref))]

    @pl.when(i == 0)
    def _():
        for c in weight_copies(be_ref[0], 0):
            c.start()

    @pl.when(i < nblk_ref[0])
    def _():
        @pl.when(first_ref[i] == 1)
        def _():
            s = slot_ref[i]
            for c in weight_copies(be_ref[i], s):
                c.wait()
            wb1_ref[...] = wf1_ref[s].astype(_BF16)
            wb3_ref[...] = wf3_ref[s].astype(_BF16)
            wb2_ref[...] = wf2_ref[s].astype(_BF16)

            @pl.when(next_ref[i] >= 0)
            def _():
                for c in weight_copies(next_ref[i], 1 - s):
                    c.start()

        x_lo, x_hi = _unpack_halves(xb_ref[...])
        half = x_lo.shape[1]
        up = lambda w_ref: _dot(x_lo, w_ref[:half, :]) + _dot(x_hi, w_ref[half:, :])
        a = up(wb1_ref)
        hid = (a * jax.nn.sigmoid(a) * up(wb3_ref)).astype(_BF16)
        yb_ref[...] = _pack_halves(_dot(hid, wb2_ref[...]))

    @pl.when(i >= nblk_ref[0])
    def _():
        yb_ref[...] = jnp.zeros_like(yb_ref)


def _moe_experts(xb, n_blocks, expert_tables, w1, w3, w2):
    _, D, DE = w1.shape
    live = lambda i, nb: jnp.maximum(jnp.minimum(i, nb[0] - 1), 0)
    hbm = pl.BlockSpec(memory_space=pl.ANY)
    grid_spec = pltpu.PrefetchScalarGridSpec(
        num_scalar_prefetch=5,
        grid=(n_blocks,),
        in_specs=[pl.BlockSpec((MOE_BLOCK, D // 2), lambda i, be, nb, *_: (live(i, nb), 0)), hbm, hbm, hbm],
        out_specs=pl.BlockSpec((MOE_BLOCK, D // 2), lambda i, *_: (i, 0)),
        scratch_shapes=[pltpu.VMEM((2, D, DE), _F32), pltpu.VMEM((2, D, DE), _F32), pltpu.VMEM((2, DE, D), _F32),
                        pltpu.VMEM((D, DE), _BF16), pltpu.VMEM((D, DE), _BF16), pltpu.VMEM((DE, D), _BF16),
                        pltpu.SemaphoreType.DMA((2,))],
    )
    return pl.pallas_call(
        _expert_kernel,
        grid_spec=grid_spec,
        out_shape=jax.ShapeDtypeStruct((n_blocks * MOE_BLOCK, D // 2), jnp.uint32),
        compiler_params=_cparams("arbitrary"),
        name="moe_experts",
    )(*expert_tables, xb, w1, w3, w2)


def _final_kernel(dst_ref, dst_next_ref, h1_ref, comb_ref, yb_ref, p_ref, wg_ref, wp_ref, gple_ref, gfin_ref,
                  o_ref, ybuf_ref, sem):
    tm = h1_ref.shape[0]
    i = pl.program_id(0)
    slot = lax.rem(i, 2)

    def gather(d_ref, s):
        def start(c, carry):
            pltpu.make_async_copy(yb_ref.at[_chunk_rows(d_ref[0, 0, c])], ybuf_ref.at[s, _chunk_rows(c)],
                                  sem.at[s]).start()
            return carry
        lax.fori_loop(0, SORT_CHUNKS, start, 0, unroll=DMA_UNROLL)

    @pl.when(i == 0)
    def _():
        gather(dst_ref, 0)

    @pl.when(i + 1 < pl.num_programs(0))
    def _():
        gather(dst_next_ref, 1 - slot)

    pltpu.make_async_copy(ybuf_ref.at[slot], ybuf_ref.at[slot], sem.at[slot]).wait()

    comb = comb_ref[...].T
    g1, g2 = comb[:, _META_GATE:_META_GATE + 1], comb[:, _META_GATE + 1:_META_GATE + 2]
    p1, p2 = comb[:, _META_POS:_META_POS + 1], comb[:, _META_POS + 1:_META_POS + 2]
    lane = lax.broadcasted_iota(jnp.int32, (tm, SORT_ROWS), 1).astype(_F32)
    wmat = (jnp.where(lane == p1, g1, 0.0) + jnp.where(lane == p2, g2, 0.0)).astype(_BF16)
    y_lo, y_hi = _unpack_halves(ybuf_ref[slot])
    moe = jnp.concatenate([_dot(wmat, y_lo), _dot(wmat, y_hi)], axis=1)
    h2 = h1_ref[...] + moe
    gate = jax.nn.sigmoid(_dot(_rmsnorm(h2, gple_ref[...]).astype(_BF16), wg_ref[...]))
    h3 = h2 + _dot(p_ref[...].astype(_BF16), wp_ref[...]) * gate
    o_ref[...] = _rmsnorm(h3, gfin_ref[...])


def _final(dst_chunk, h1, comb, yb, p2, w_gate, w_proj, g_ple, g_final):
    T, D = h1.shape
    nt = T // ROW_TILE
    PD = p2.shape[1]
    row = lambda n: pl.BlockSpec((ROW_TILE, n), lambda i: (i, 0))
    vec = lambda n: pl.BlockSpec((1, n), lambda i: (0, 0))
    dest3 = dst_chunk.reshape(nt, 1, SORT_CHUNKS)
    return pl.pallas_call(
        _final_kernel,
        grid=(nt,),
        in_specs=[pl.BlockSpec((1, 1, SORT_CHUNKS), lambda i: (i, 0, 0), memory_space=pltpu.SMEM),
                  pl.BlockSpec((1, 1, SORT_CHUNKS), lambda i: (jnp.minimum(i + 1, nt - 1), 0, 0),
                               memory_space=pltpu.SMEM),
                  row(D), pl.BlockSpec((8, ROW_TILE), lambda i: (0, i)),
                  pl.BlockSpec(memory_space=pl.ANY),
                  row(PD),
                  pl.BlockSpec((D, D), lambda i: (0, 0)),
                  pl.BlockSpec((PD, D), lambda i: (0, 0)),
                  vec(D), vec(D)],
        out_specs=row(D),
        out_shape=jax.ShapeDtypeStruct((T, D), _F32),
        scratch_shapes=[pltpu.VMEM((2, SORT_ROWS, D // 2), jnp.uint32), pltpu.SemaphoreType.DMA((2,))],
        compiler_params=_cparams("arbitrary"),
        name="moe_gather_ple_final",
    )(dest3, dest3, h1, comb, yb, p2, w_gate.astype(_BF16), w_proj.astype(_BF16),
      g_ple.reshape(1, D), g_final.reshape(1, D))


def _routing_tables(tile_tbl, counts_f, n_blocks, n_tail):
    start = tile_tbl[:, :, 0].astype(jnp.int32)
    run = tile_tbl[:, :, 1].astype(jnp.int32)
    base = tile_tbl[:, :, 2].astype(jnp.int32)
    counts = counts_f[:, 0].astype(jnp.int32)
    padded = (counts + MOE_BLOCK - 1) // MOE_BLOCK * MOE_BLOCK
    e_iota = jnp.arange(N_EXPERTS, dtype=jnp.int32)
    pad_end = jnp.sum(jnp.where(e_iota[None, :] <= e_iota[:, None], padded[None, :], 0), axis=1)
    pad_start = pad_end - padded
    row0 = (jnp.arange(SORT_CHUNKS, dtype=jnp.int32) * MOE_CHUNK)[None, :, None]
    mine = (start[:, None, :] <= row0) & (row0 < (start + run)[:, None, :])
    used = jnp.any(mine, axis=-1)
    slot0 = (pad_start[None, :] + base - start)[:, None, :] + row0
    dst = jnp.sum(jnp.where(mine, slot0, 0), axis=-1) // MOE_CHUNK
    nt = start.shape[0]
    spare = (n_blocks * MOE_BLOCK // MOE_CHUNK + (jnp.arange(nt, dtype=jnp.int32) % 2)[:, None] * SORT_CHUNKS
             + jnp.arange(SORT_CHUNKS, dtype=jnp.int32)[None, :])
    dst_scatter = jnp.where(used, dst, spare).astype(jnp.int32)
    dst_gather = jnp.where(used, dst, 0).astype(jnp.int32)
    blk_row = jnp.arange(n_blocks, dtype=jnp.int32)[:, None] * MOE_BLOCK
    block_e = jnp.minimum(jnp.sum((pad_end[None, :] <= blk_row).astype(jnp.int32), axis=1), N_EXPERTS - 1)
    n_blk = (pad_end[-1:] // MOE_BLOCK).astype(jnp.int32)
    seg_last = jnp.where(padded > counts, pad_end // MOE_BLOCK - 1, -1)
    tail = n_blk[0] + jnp.arange(n_tail, dtype=jnp.int32)
    spare_blocks = n_blocks + jnp.arange(2 * SORT_ROWS // MOE_BLOCK, dtype=jnp.int32)
    zero_blocks = jnp.concatenate([seg_last, jnp.where(tail < n_blocks, tail, -1), spare_blocks]).astype(jnp.int32)
    has_rows = padded > 0
    before = e_iota[None, :] < e_iota[:, None]
    slot_e = jnp.sum(jnp.where(before & has_rows[None, :], 1, 0), axis=1) % 2
    later = jnp.where((e_iota[None, :] > e_iota[:, None]) & has_rows[None, :], e_iota[None, :], N_EXPERTS)
    next_e = jnp.min(later, axis=1)
    next_e = jnp.where(next_e == N_EXPERTS, -1, next_e)
    mine_e = block_e[:, None] == e_iota[None, :]
    pick = lambda tab: jnp.sum(jnp.where(mine_e, tab[None, :], 0), axis=1).astype(jnp.int32)
    first = (blk_row[:, 0] == pick(pad_start)).astype(jnp.int32)
    expert_tables = (block_e.astype(jnp.int32), n_blk, first, pick(slot_e), pick(next_e))
    return dst_scatter, dst_gather, zero_blocks, expert_tables


def _one_layer(h, p, g_mix, w_in, hy_conv_w, hy_conv_b, hy_f_w1, hy_f_b1, hy_f_freq1, hy_f_w2, hy_f_b2,
               hy_f_freq2, hy_f_w3, hy_skip, na_rpb, g_out_hy, g_out_na, w_out, g_ffn, router_wg, router_bg,
               router_we, router_be, exp_w1, exp_w3, exp_w2, g_ple, w_ple_gate, w_ple_proj):
    B, L, D = h.shape
    T = B * L
    assert T % ROW_TILE == 0 and L % FREQ_CHUNK == 0 and w_in.shape[1] == HY_COLS + 3 * NA_WIDTH
    x2 = h.reshape(T, D)
    u2, qkv = _in_proj(x2, g_mix, w_in)

    hf2 = _filter_mlp(L, hy_f_w1, hy_f_b1, hy_f_freq1, hy_f_w2, hy_f_b2, hy_f_freq2, hy_f_w3)
    tables = _half_dft_tables(L)
    k1r, k1i, k2r, k2i = _filter_spec(tables, hf2)
    y_hy_even, y_hy_odd = _hyena(u2.reshape(B, L // 2, 2 * HY_COLS), hy_conv_w.astype(_F32), hy_conv_b.astype(_F32),
                                 tables, k1r, k1i, k2r, k2i, hy_skip.astype(_F32))
    y_na = _natten(qkv.reshape(B, L, 3 * NA_WIDTH), na_rpb)

    h1, xn, meta, tile_tbl, counts = _out_router(
        y_hy_even.reshape(T // 2, HY_WIDTH), y_hy_odd.reshape(T // 2, HY_WIDTH), y_na.reshape(T, NA_WIDTH),
        g_out_hy, g_out_na, w_out, x2, g_ffn, router_wg, router_bg, router_we, router_be)
    max_rows = T * TOP_K + (T // ROW_TILE) * N_EXPERTS * (MOE_CHUNK - 1) + N_EXPERTS * (MOE_BLOCK - 1)
    n_blocks = -(-max_rows // MOE_BLOCK)
    n_tail = n_blocks - T * TOP_K // MOE_BLOCK
    dst_scatter, dst_gather, zero_blocks, expert_tables = _routing_tables(tile_tbl, counts, n_blocks, n_tail)
    assert 2 * SORT_ROWS % MOE_BLOCK == 0
    xb = _moe_scatter(xn, meta, dst_scatter, zero_blocks, n_blocks * MOE_BLOCK + 2 * SORT_ROWS)
    yb = _moe_experts(xb, n_blocks, expert_tables, exp_w1, exp_w3, exp_w2)
    return h1, dst_gather, meta, yb


def kernel(x, p, g_mix, w_in, hy_conv_w, hy_conv_b, hy_f_w1, hy_f_b1, hy_f_freq1, hy_f_w2, hy_f_b2, hy_f_freq2, hy_f_w3, hy_skip, na_rpb, g_out_hy, g_out_na, w_out, g_ffn, router_wg, router_bg, router_we, router_be, exp_w1, exp_w3, exp_w2, g_ple, w_ple_gate, w_ple_proj, g_final):
    depth = p.shape[0]
    assert depth == 1, "the final RMSNorm is fused into the last layer's kernel; one layer is supported"
    B, L, D = x.shape
    i = 0
    h1, dest, gates, yb = _one_layer(
        x, p[i], g_mix[i], w_in[i], hy_conv_w[i], hy_conv_b[i], hy_f_w1[i], hy_f_b1[i], hy_f_freq1[i],
        hy_f_w2[i], hy_f_b2[i], hy_f_freq2[i], hy_f_w3[i], hy_skip[i], na_rpb[i], g_out_hy[i], g_out_na[i],
        w_out[i], g_ffn[i], router_wg[i], router_bg[i], router_we[i], router_be[i], exp_w1[i], exp_w3[i],
        exp_w2[i], g_ple[i], w_ple_gate[i], w_ple_proj[i])
    out = _final(dest, h1, gates, yb, p[i].reshape(B * L, -1), w_ple_gate[i], w_ple_proj[i], g_ple[i], g_final)
    return out.reshape(B, L, D)
```
